```python
import jax, jax.numpy as jnp
from jax import lax
import numpy as np

D_MODEL = 1024
BATCH = 16
SEQ = 4096
DEPTH = 1

MLA_HEADS = 8
MLA_NOPE = 64
MLA_ROPE = 32
MLA_V = 64
Q_LORA = 256
KV_LORA = 128
ATTN_QBLOCK = 128
RET_HEADS = 4
RET_DK = 64
RET_DV = 128
RET_CHUNK = 128
ROPE_BASE = 10000.0
NORM_EPS = 1e-6
MIX_WIDTH = MLA_HEADS * MLA_V + RET_HEADS * RET_DV
IN_WIDTH = Q_LORA + KV_LORA + MLA_ROPE + 2 * RET_HEADS * RET_DK + 2 * RET_HEADS * RET_DV
N_GROUPS = 4
EXPERTS_PER_GROUP = 8
N_EXPERTS = N_GROUPS * EXPERTS_PER_GROUP
TOP_K = 2
D_EXPERT = 256
MOE_BLOCK = 256

kernel_name = "hybrid_mla_retention_hmoe_adaln"


def rmsnorm(x, g):
    xf = x.astype(jnp.float32)
    y = xf * lax.rsqrt(jnp.mean(xf * xf, axis=-1, keepdims=True) + NORM_EPS)
    return (y * g.astype(jnp.float32)).astype(x.dtype)


def rope(x, positions):
    half = x.shape[-1] // 2
    inv_freq = ROPE_BASE ** (-(jnp.arange(half, dtype=jnp.float32) / half))
    ang = positions.astype(jnp.float32)[..., None] * inv_freq
    cos = jnp.cos(ang)[:, :, None, :]
    sin = jnp.sin(ang)[:, :, None, :]
    x1 = x[..., :half].astype(jnp.float32)
    x2 = x[..., half:].astype(jnp.float32)
    return jnp.concatenate([x1 * cos - x2 * sin, x1 * sin + x2 * cos], axis=-1).astype(x.dtype)


def mla_causal_attention(q, k, v):
    B, S, H, dqk = q.shape
    nqb = S // ATTN_QBLOCK
    scale = dqk ** -0.5
    q_blocks = q.reshape(B, nqb, ATTN_QBLOCK, H, dqk).transpose(1, 0, 2, 3, 4)
    key_idx = jnp.arange(S)

    def block(args):
        qb, i = args
        s = jnp.einsum('bqhd,bkhd->bhqk', qb, k).astype(jnp.float32) * scale
        q_idx = i * ATTN_QBLOCK + jnp.arange(ATTN_QBLOCK)
        mask = key_idx[None, :] <= q_idx[:, None]
        s = jnp.where(mask[None, None], s, jnp.finfo(jnp.float32).min)
        p = jax.nn.softmax(s, axis=-1).astype(v.dtype)
        return jnp.einsum('bhqk,bkhe->bqhe', p, v)

    o = lax.map(block, (q_blocks, jnp.arange(nqb)))
    return o.transpose(1, 0, 2, 3, 4).reshape(B, S, H * v.shape[-1])


def chunkwise_retention(q, k, v):
    B, S, H, dk = q.shape
    dv = v.shape[-1]
    C = RET_CHUNK
    n = S // C
    gamma = 1.0 - jnp.power(2.0, -5.0 - jnp.arange(H, dtype=jnp.float32))
    log_g = jnp.log(gamma)
    idx = jnp.arange(C, dtype=jnp.float32)
    diff = idx[:, None] - idx[None, :]
    dmask = jnp.where(diff[None] >= 0, jnp.exp(jnp.maximum(diff, 0.0)[None] * log_g[:, None, None]), 0.0)
    zeta = jnp.exp((C - 1.0 - idx)[None, :] * log_g[:, None])
    xi = jnp.exp((idx + 1.0)[None, :] * log_g[:, None])
    chunk_decay = jnp.exp(C * log_g)

    qc = q.reshape(B, n, C, H, dk)
    kc = k.reshape(B, n, C, H, dk)
    vc = v.reshape(B, n, C, H, dv)
    s = jnp.einsum('bnqhd,bnkhd->bnhqk', qc, kc) * dmask[None, None]
    o_intra = jnp.einsum('bnhqk,bnkhe->bnqhe', s, vc)
    u = jnp.einsum('bnkhd,hk,bnkhe->nbhde', kc, zeta, vc)

    def step(state, u_n):
        return state * chunk_decay[None, :, None, None] + u_n, state

    _, prev_states = lax.scan(step, jnp.zeros((B, H, dk, dv), u.dtype), u)
    o_cross = jnp.einsum('bnqhd,hq,nbhde->bnqhe', qc, xi, prev_states)
    return (o_intra + o_cross).reshape(B, S, H, dv)


def head_groupnorm(o):
    of = o.astype(jnp.float32)
    mu = jnp.mean(of, axis=-1, keepdims=True)
    var = jnp.mean(jnp.square(of - mu), axis=-1, keepdims=True)
    return ((of - mu) * lax.rsqrt(var + NORM_EPS)).astype(o.dtype)


def hybrid_mixer(h, positions, w_in, q_norm_g, w_uq, kv_norm_g, w_ukv, w_o):
    B, S, _ = h.shape
    proj = h @ w_in
    sizes = [Q_LORA, KV_LORA, MLA_ROPE, RET_HEADS * RET_DK, RET_HEADS * RET_DK,
             RET_HEADS * RET_DV, RET_HEADS * RET_DV]
    splits = [int(s) for s in np.cumsum(sizes)[:-1]]
    c_q, c_kv, k_rope, r_q, r_k, r_v, r_g = jnp.split(proj, splits, axis=-1)

    q = (rmsnorm(c_q, q_norm_g) @ w_uq).reshape(B, S, MLA_HEADS, MLA_NOPE + MLA_ROPE)
    q = jnp.concatenate([q[..., :MLA_NOPE], rope(q[..., MLA_NOPE:], positions)], axis=-1)
    kv = (rmsnorm(c_kv, kv_norm_g) @ w_ukv).reshape(B, S, MLA_HEADS, MLA_NOPE + MLA_V)
    k_nope, v = kv[..., :MLA_NOPE], kv[..., MLA_NOPE:]
    k_pe = rope(k_rope[:, :, None, :], positions)
    k = jnp.concatenate([k_nope, jnp.broadcast_to(k_pe, (B, S, MLA_HEADS, MLA_ROPE))], axis=-1)
    o_mla = mla_causal_attention(q, k, v)

    rq = rope(r_q.reshape(B, S, RET_HEADS, RET_DK), positions)
    rk = rope(r_k.reshape(B, S, RET_HEADS, RET_DK), positions) * (RET_DK ** -0.5)
    rv = r_v.reshape(B, S, RET_HEADS, RET_DV)
    o_ret = head_groupnorm(chunkwise_retention(rq, rk, rv)).reshape(B, S, RET_HEADS * RET_DV)
    o_ret = jax.nn.silu(r_g) * o_ret

    return jnp.concatenate([o_mla, o_ret], axis=-1) @ w_o


def hierarchical_moe(h, w_gr, b_gr, w_er, b_er, w1, w3, w2):
    B, S, D = h.shape
    T = B * S
    xf = h.reshape(T, D)
    p_group = jax.nn.softmax((xf @ w_gr).astype(jnp.float32) + b_gr, axis=-1)
    p_top, g_top = lax.top_k(p_group, 1)
    le = jnp.einsum('td,dge->tge', xf, w_er).astype(jnp.float32) + b_er
    le_sel = jnp.take_along_axis(le, g_top[:, :, None], axis=1)[:, 0]
    e_val, e_idx = lax.top_k(le_sel, TOP_K)
    gate = jax.nn.softmax(e_val, axis=-1) * p_top

    expert_id = (g_top * EXPERTS_PER_GROUP + e_idx).reshape(-1).astype(jnp.int32)
    token_id = jnp.repeat(jnp.arange(T, dtype=jnp.int32), TOP_K)
    weight = gate.reshape(-1)
    A = T * TOP_K

    order = jnp.argsort(expert_id)
    e_sorted, t_sorted, w_sorted = expert_id[order], token_id[order], weight[order]
    counts = jnp.bincount(expert_id, length=N_EXPERTS)
    start = jnp.cumsum(counts) - counts
    padded = ((counts + MOE_BLOCK - 1) // MOE_BLOCK) * MOE_BLOCK
    pend = jnp.cumsum(padded)
    pstart = pend - padded
    dest = pstart[e_sorted] + (jnp.arange(A) - start[e_sorted])
    P = A + N_EXPERTS * MOE_BLOCK
    nb = P // MOE_BLOCK
    tok_buf = jnp.zeros((P,), jnp.int32).at[dest].set(t_sorted)
    w_buf = jnp.zeros((P,), jnp.float32).at[dest].set(w_sorted)
    blk_expert = jnp.minimum(jnp.searchsorted(pend, jnp.arange(nb) * MOE_BLOCK, side='right'), N_EXPERTS - 1)
    xin = xf[tok_buf].reshape(nb, MOE_BLOCK, D)

    def expert_ffn(args):
        xb, e = args
        return (jax.nn.silu(xb @ w1[e]) * (xb @ w3[e])) @ w2[e]

    y = lax.map(expert_ffn, (xin, blk_expert)).reshape(P, D)
    out = jnp.zeros((T, D), h.dtype).at[tok_buf].add((y * w_buf[:, None]).astype(h.dtype))
    return out.reshape(B, S, D)


def setup_inputs(seed: int = 0) -> dict:
    key = jax.random.key(seed)
    ks = jax.random.split(key, 24)
    D, L = D_MODEL, DEPTH
    nrm = lambda k, shape, fan: jax.random.normal(k, shape, jnp.float32) * (fan ** -0.5)
    offset = jax.random.randint(ks[2], (BATCH, 1), 0, 1024, dtype=jnp.int32)
    return {
        "x": jax.random.normal(ks[0], (BATCH, SEQ, D), jnp.float32),
        "c": jax.random.normal(ks[1], (BATCH, D), jnp.float32),
        "positions": offset + jnp.arange(SEQ, dtype=jnp.int32)[None, :],
        "w_ada": nrm(ks[3], (L, D, 6 * D), D) * 0.5,
        "b_ada": jax.random.normal(ks[4], (L, 6 * D), jnp.float32) * 0.01,
        "norm1_g": 1.0 + 0.02 * jax.random.normal(ks[5], (L, D), jnp.float32),
        "w_in": nrm(ks[6], (L, D, IN_WIDTH), D),
        "q_norm_g": 1.0 + 0.02 * jax.random.normal(ks[7], (L, Q_LORA), jnp.float32),
        "w_uq": nrm(ks[8], (L, Q_LORA, MLA_HEADS * (MLA_NOPE + MLA_ROPE)), Q_LORA),
        "kv_norm_g": 1.0 + 0.02 * jax.random.normal(ks[9], (L, KV_LORA), jnp.float32),
        "w_ukv": nrm(ks[10], (L, KV_LORA, MLA_HEADS * (MLA_NOPE + MLA_V)), KV_LORA),
        "w_o": nrm(ks[11], (L, MIX_WIDTH, D), MIX_WIDTH),
        "norm2_g": 1.0 + 0.02 * jax.random.normal(ks[12], (L, D), jnp.float32),
        "w_gr": nrm(ks[13], (L, D, N_GROUPS), D),
        "b_gr": jax.random.normal(ks[14], (L, N_GROUPS), jnp.float32) * 0.01,
        "w_er": nrm(ks[15], (L, D, N_GROUPS, EXPERTS_PER_GROUP), D),
        "b_er": jax.random.normal(ks[16], (L, N_GROUPS, EXPERTS_PER_GROUP), jnp.float32) * 0.01,
        "w1": nrm(ks[17], (L, N_EXPERTS, D, D_EXPERT), D),
        "w3": nrm(ks[18], (L, N_EXPERTS, D, D_EXPERT), D),
        "w2": nrm(ks[19], (L, N_EXPERTS, D_EXPERT, D), D_EXPERT),
        "final_g": 1.0 + 0.02 * jax.random.normal(ks[20], (D,), jnp.float32),
    }


def reference(x, c, positions, w_ada, b_ada, norm1_g, w_in, q_norm_g, w_uq, kv_norm_g, w_ukv, w_o,
              norm2_g, w_gr, b_gr, w_er, b_er, w1, w3, w2, final_g):
    for l in range(DEPTH):
        mod = jax.nn.silu(c) @ w_ada[l] + b_ada[l]
        sh1, sc1, g1, sh2, sc2, g2 = [m[:, None, :] for m in jnp.split(mod, 6, axis=-1)]
        h = rmsnorm(x, norm1_g[l]) * (1.0 + sc1) + sh1
        x = x + g1 * hybrid_mixer(h, positions, w_in[l], q_norm_g[l], w_uq[l], kv_norm_g[l], w_ukv[l], w_o[l])
        h = rmsnorm(x, norm2_g[l]) * (1.0 + sc2) + sh2
        x = x + g2 * hierarchical_moe(h, w_gr[l], b_gr[l], w_er[l], b_er[l], w1[l], w3[l], w2[l])
    return rmsnorm(x, final_g)
```

```python
import functools

import jax
import jax.numpy as jnp
from jax import lax
from jax.experimental import pallas as pl
from jax.experimental.pallas import tpu as pltpu

MLA_HEADS = 8
MLA_NOPE = 64
MLA_ROPE = 32
MLA_V = 64
Q_LORA = 256
KV_LORA = 128
RET_HEADS = 4
RET_DK = 64
RET_DV = 128
RET_CHUNK = 128
ROPE_BASE = 10000.0
NORM_EPS = 1e-6
N_GROUPS = 4
EXPERTS_PER_GROUP = 8
N_EXPERTS = N_GROUPS * EXPERTS_PER_GROUP
TOP_K = 2
D_EXPERT = 256
MOE_BLOCK = 256

LANES = 128
VMEM_LIMIT = 56 * 1024 * 1024

F32 = jnp.float32
BF16 = jnp.bfloat16
NEG = float(jnp.finfo(jnp.float32).min)

_C_Q = 0
_C_KV = _C_Q + Q_LORA
_C_KPE = _C_KV + KV_LORA
_C_KPE_ROT = _C_KPE + LANES
_C_RQ = _C_KPE_ROT + LANES
_C_RQ_ROT = _C_RQ + RET_HEADS * RET_DK
_C_RK = _C_RQ_ROT + RET_HEADS * RET_DK
_C_RK_ROT = _C_RK + RET_HEADS * RET_DK
_C_RV = _C_RK_ROT + RET_HEADS * RET_DK
_C_RG = _C_RV + RET_HEADS * RET_DV
_IN_PERM = _C_RG + RET_HEADS * RET_DV


def _silu(v):
    return v / (1.0 + jnp.exp(-v))


def _mm(a, b):
    return jnp.dot(a, b, preferred_element_type=F32)


def _mm_nt(a, b):
    return lax.dot_general(a, b, (((1,), (1,)), ((), ())), preferred_element_type=F32)


def _params(*sem):
    return pltpu.CompilerParams(dimension_semantics=sem, vmem_limit_bytes=VMEM_LIMIT)


def _adaln_kernel(c_ref, w_ref, b_ref, o_ref):
    a = _silu(c_ref[...]).astype(BF16)
    o_ref[...] = _mm(a, w_ref[...].astype(BF16)) + b_ref[...]


def _adaln(c, w_ada, b_ada):
    bsz, d = c.shape
    n = w_ada.shape[1]
    tn = d
    return pl.pallas_call(
        _adaln_kernel,
        grid=(n // tn,),
        in_specs=[pl.BlockSpec((bsz, d), lambda j: (0, 0)),
                  pl.BlockSpec((d, tn), lambda j: (0, j)),
                  pl.BlockSpec((1, tn), lambda j: (0, j))],
        out_specs=pl.BlockSpec((bsz, tn), lambda j: (0, j)),
        out_shape=jax.ShapeDtypeStruct((bsz, n), F32),
        compiler_params=_params("arbitrary"),
        name="adaln",
    )(c, w_ada, b_ada.reshape(1, n))


def _rms(v, g):
    return v * lax.rsqrt(jnp.mean(v * v, axis=-1, keepdims=True) + NORM_EPS) * g


def _pre_kernel(x_ref, mod_ref, pos_ref, g1_ref, win_ref, qg_ref, wuq_ref, wuqr_ref, kvg_ref, wuk_ref,
                wuv_ref, invf_ref, q_ref, k_ref, v_ref, rq_ref, rk_ref, rv_ref, rg_ref):
    x = x_ref[0]
    sh1 = mod_ref[0, 0:1, :]
    sc1 = mod_ref[0, 1:2, :]
    h = _rms(x, g1_ref[...]) * (1.0 + sc1) + sh1
    proj = _mm(h.astype(BF16), win_ref[...])

    ang = pos_ref[0] * invf_ref[...]
    cs = jnp.cos(ang)
    sn = jnp.sin(ang)
    lane = lax.broadcasted_iota(jnp.int32, cs.shape, 1)
    hi = lane >= RET_DK
    c_mla = jnp.where(hi, cs, 1.0)
    s_mla = jnp.where(hi, sn, 0.0)
    c_ret = jnp.where(hi, pltpu.roll(cs, RET_DK, 1), cs)
    s_ret = jnp.where(hi, pltpu.roll(sn, RET_DK, 1), sn)

    cq = _rms(proj[:, _C_Q:_C_Q + Q_LORA], qg_ref[...]).astype(BF16)
    qa = _mm(cq, wuq_ref[...])
    qb = _mm(cq, wuqr_ref[...])
    scale = (MLA_NOPE + MLA_ROPE) ** -0.5
    for hd in range(MLA_HEADS):
        sl = slice(hd * LANES, (hd + 1) * LANES)
        q_ref[0, :, sl] = ((qa[:, sl] * c_mla + qb[:, sl] * s_mla) * scale).astype(BF16)

    ckv = _rms(proj[:, _C_KV:_C_KV + KV_LORA], kvg_ref[...]).astype(BF16)
    kn = _mm(ckv, wuk_ref[...])
    kpe = proj[:, _C_KPE:_C_KPE + LANES] * c_mla + proj[:, _C_KPE_ROT:_C_KPE_ROT + LANES] * s_mla
    for hd in range(MLA_HEADS):
        sl = slice(hd * LANES, (hd + 1) * LANES)
        k_ref[0, :, sl] = (kn[:, sl] + kpe).astype(BF16)
    v_ref[0] = _mm(ckv, wuv_ref[...]).astype(BF16)

    for i in range(RET_HEADS * RET_DK // LANES):
        o = i * LANES
        rq = proj[:, _C_RQ + o:_C_RQ + o + LANES] * c_ret + proj[:, _C_RQ_ROT + o:_C_RQ_ROT + o + LANES] * s_ret
        rk = proj[:, _C_RK + o:_C_RK + o + LANES] * c_ret + proj[:, _C_RK_ROT + o:_C_RK_ROT + o + LANES] * s_ret
        rq_ref[0, :, o:o + LANES] = rq.astype(BF16)
        rk_ref[0, :, o:o + LANES] = (rk * (RET_DK ** -0.5)).astype(BF16)
    rv_ref[0] = proj[:, _C_RV:_C_RV + RET_HEADS * RET_DV].astype(BF16)
    rg_ref[0] = proj[:, _C_RG:_C_RG + RET_HEADS * RET_DV]


def _rot_cols(w, half):
    k = w.shape[0]
    w4 = w.reshape(k, -1, 2, half)
    return jnp.stack([-w4[:, :, 1], w4[:, :, 0]], axis=2).reshape(k, -1)


def _pad_heads(w, width, left):
    k = w.shape[0]
    w3 = w.reshape(k, -1, width)
    w3 = jnp.pad(w3, ((0, 0), (0, 0), (left, LANES - left - width)))
    return w3.reshape(k, -1)


def _pre(x, mod3, pos3, norm1_g, w_in, q_norm_g, w_uq, kv_norm_g, w_ukv, tm):
    bsz, s, d = x.shape
    o = 0
    parts = {}
    for name, width in (("cq", Q_LORA), ("ckv", KV_LORA), ("kr", MLA_ROPE), ("rq", RET_HEADS * RET_DK),
                        ("rk", RET_HEADS * RET_DK), ("rv", RET_HEADS * RET_DV), ("rg", RET_HEADS * RET_DV)):
        parts[name] = w_in[:, o:o + width]
        o += width
    w_in_p = jnp.concatenate([
        parts["cq"], parts["ckv"],
        _pad_heads(parts["kr"], MLA_ROPE, MLA_NOPE), _pad_heads(_rot_cols(parts["kr"], MLA_ROPE // 2), MLA_ROPE, MLA_NOPE),
        parts["rq"], _rot_cols(parts["rq"], RET_DK // 2),
        parts["rk"], _rot_cols(parts["rk"], RET_DK // 2),
        parts["rv"], parts["rg"]], axis=1).astype(BF16)
    assert w_in_p.shape[1] == _IN_PERM
    wq3 = w_uq.reshape(Q_LORA, MLA_HEADS, MLA_NOPE + MLA_ROPE)
    wq_rope = wq3[:, :, MLA_NOPE:].reshape(Q_LORA, -1)
    w_uq_p = _pad_heads(wq3.reshape(Q_LORA, -1), MLA_NOPE + MLA_ROPE, 0).astype(BF16)
    w_uq_r = _pad_heads(_rot_cols(wq_rope, MLA_ROPE // 2), MLA_ROPE, MLA_NOPE).astype(BF16)
    wkv3 = w_ukv.reshape(KV_LORA, MLA_HEADS, MLA_NOPE + MLA_V)
    w_uk_p = _pad_heads(wkv3[:, :, :MLA_NOPE].reshape(KV_LORA, -1), MLA_NOPE, 0).astype(BF16)
    w_uv = wkv3[:, :, MLA_NOPE:].reshape(KV_LORA, -1).astype(BF16)
    half_r, half_m = RET_DK // 2, MLA_ROPE // 2
    f_r = ROPE_BASE ** (-(jnp.arange(half_r, dtype=F32) / half_r))
    f_m = ROPE_BASE ** (-(jnp.arange(half_m, dtype=F32) / half_m))
    invf = jnp.concatenate([f_r, f_r, f_m, f_m, jnp.zeros((LANES - 2 * half_r - 2 * half_m,), F32)]).reshape(1, LANES)

    hq = MLA_HEADS * LANES
    const = lambda shape: pl.BlockSpec(shape, lambda b, i: (0,) * len(shape))
    tile = lambda w: pl.BlockSpec((1, tm, w), lambda b, i: (b, i, 0))
    return pl.pallas_call(
        _pre_kernel,
        grid=(bsz, s // tm),
        in_specs=[tile(d), pl.BlockSpec((1, 6, d), lambda b, i: (b, 0, 0)), tile(1), const((1, d)),
                  const((d, _IN_PERM)), const((1, Q_LORA)), const((Q_LORA, hq)), const((Q_LORA, hq)),
                  const((1, KV_LORA)), const((KV_LORA, hq)), const((KV_LORA, MLA_HEADS * MLA_V)), const((1, LANES))],
        out_specs=[tile(hq), tile(hq), tile(MLA_HEADS * MLA_V), tile(RET_HEADS * RET_DK), tile(RET_HEADS * RET_DK),
                   tile(RET_HEADS * RET_DV), tile(RET_HEADS * RET_DV)],
        out_shape=[jax.ShapeDtypeStruct((bsz, s, hq), BF16), jax.ShapeDtypeStruct((bsz, s, hq), BF16),
                   jax.ShapeDtypeStruct((bsz, s, MLA_HEADS * MLA_V), BF16),
                   jax.ShapeDtypeStruct((bsz, s, RET_HEADS * RET_DK), BF16),
                   jax.ShapeDtypeStruct((bsz, s, RET_HEADS * RET_DK), BF16),
                   jax.ShapeDtypeStruct((bsz, s, RET_HEADS * RET_DV), BF16),
                   jax.ShapeDtypeStruct((bsz, s, RET_HEADS * RET_DV), F32)],
        compiler_params=_params("parallel", "arbitrary"),
        name="pre_mixer",
    )(x, mod3, pos3, norm1_g.reshape(1, d), w_in_p, q_norm_g.reshape(1, -1), w_uq_p, w_uq_r,
      kv_norm_g.reshape(1, -1), w_uk_p, w_uv, invf)


def _attn_kernel(q_ref, k_ref, v_ref, o_ref, *, tq):
    qi = pl.program_id(2)
    row = lax.broadcasted_iota(jnp.int32, (tq, tq), 0)
    col = lax.broadcasted_iota(jnp.int32, (tq, tq), 1)
    outs = []
    for hh in range(2):
        hs = slice(hh * LANES, (hh + 1) * LANES)
        q = q_ref[0, :, hs]

        def step(j, carry, masked):
            m, l, acc = carry
            start = pl.multiple_of(j * tq, tq)
            s = _mm_nt(q, k_ref[0, pl.ds(start, tq), hs])
            if masked:
                s = jnp.where(col <= row, s, NEG)
            m_new = jnp.maximum(m, jnp.max(s, axis=-1, keepdims=True))
            p = jnp.exp(s - m_new)
            alpha = jnp.exp(m - m_new)
            l = alpha * l + jnp.sum(p, axis=-1, keepdims=True)
            acc = alpha * acc + _mm(p.astype(BF16), v_ref[0, pl.ds(start, tq), :])
            return m_new, l, acc

        carry = (jnp.full((tq, 1), NEG, F32), jnp.zeros((tq, 1), F32), jnp.zeros((tq, LANES), F32))
        carry = lax.fori_loop(0, qi, functools.partial(step, masked=False), carry)
        _, l, acc = step(qi, carry, True)
        outs.append(acc / l)
    lane = lax.broadcasted_iota(jnp.int32, (tq, LANES), 1)
    o_ref[0] = jnp.where(lane < MLA_V, outs[0], outs[1]).astype(BF16)


def _attention(q, k, v, tq):
    bsz, s, _ = q.shape
    pairs = MLA_HEADS // 2
    return pl.pallas_call(
        functools.partial(_attn_kernel, tq=tq),
        grid=(bsz, pairs, s // tq),
        in_specs=[pl.BlockSpec((1, tq, 2 * LANES), lambda b, p, i: (b, i, p)),
                  pl.BlockSpec((1, s, 2 * LANES), lambda b, p, i: (b, 0, p)),
                  pl.BlockSpec((1, s, 2 * MLA_V), lambda b, p, i: (b, 0, p))],
        out_specs=pl.BlockSpec((1, tq, 2 * MLA_V), lambda b, p, i: (b, i, p)),
        out_shape=jax.ShapeDtypeStruct((bsz, s, MLA_HEADS * MLA_V), BF16),
        compiler_params=_params("parallel", "parallel", "arbitrary"),
        name="mla_attention",
    )(q, k, v)


def _ret_kernel(rq_ref, rk_ref, rv_ref, rg_ref, dm_ref, xi_ref, zt_ref, dc_ref, o_ref, st_ref, *, ts):
    @pl.when(pl.program_id(1) == 0)
    def _():
        st_ref[...] = jnp.zeros_like(st_ref)

    lane = lax.broadcasted_iota(jnp.int32, (RET_CHUNK, LANES), 1)

    def chunk(c, carry):
        rows = pl.ds(pl.multiple_of(c * RET_CHUNK, RET_CHUNK), RET_CHUNK)
        for hd in range(RET_HEADS):
            pair, sub = divmod(hd, LANES // RET_DK)
            mine = (lane >= sub * RET_DK) & (lane < (sub + 1) * RET_DK)
            ps = slice(pair * LANES, (pair + 1) * LANES)
            vs = slice(hd * RET_DV, (hd + 1) * RET_DV)
            qh = jnp.where(mine, rq_ref[0, rows, ps], 0.0).astype(BF16)
            kh = jnp.where(mine, rk_ref[0, rows, ps], 0.0).astype(BF16)
            vh = rv_ref[0, rows, vs]
            st = st_ref[hd]
            sc = _mm_nt(qh, kh) * dm_ref[hd]
            o = _mm(sc.astype(BF16), vh)
            o = o + _mm((qh.astype(F32) * xi_ref[hd]).astype(BF16), st.astype(BF16))
            kz = (kh.astype(F32) * zt_ref[hd]).astype(BF16)
            st_ref[hd] = st * dc_ref[hd] + _mm(kz.T, vh)
            mu = jnp.mean(o, axis=-1, keepdims=True)
            oc = o - mu
            on = oc * lax.rsqrt(jnp.mean(oc * oc, axis=-1, keepdims=True) + NORM_EPS)
            o_ref[0, rows, vs] = (_silu(rg_ref[0, rows, vs]) * on).astype(BF16)
        return carry

    lax.fori_loop(0, ts // RET_CHUNK, chunk, 0)


def _retention(rq, rk, rv, rg, ts):
    bsz, s, _ = rq.shape
    c = RET_CHUNK
    gamma = 1.0 - jnp.power(2.0, -5.0 - jnp.arange(RET_HEADS, dtype=F32))
    log_g = jnp.log(gamma)
    idx = jnp.arange(c, dtype=F32)
    diff = idx[:, None] - idx[None, :]
    dmask = jnp.where(diff[None] >= 0, jnp.exp(jnp.maximum(diff, 0.0)[None] * log_g[:, None, None]), 0.0)
    zeta = jnp.exp((c - 1.0 - idx)[None, :] * log_g[:, None])
    xi = jnp.exp((idx + 1.0)[None, :] * log_g[:, None])
    decay = jnp.exp(c * log_g)
    xi_b = jnp.broadcast_to(xi[:, :, None], (RET_HEADS, c, LANES))
    zt_b = jnp.broadcast_to(zeta[:, :, None], (RET_HEADS, c, LANES))
    dc_b = jnp.broadcast_to(decay[:, None, None], (RET_HEADS, LANES, RET_DV))

    tile = lambda w: pl.BlockSpec((1, ts, w), lambda b, i: (b, i, 0))
    const = lambda shape: pl.BlockSpec(shape, lambda b, i: (0,) * len(shape))
    return pl.pallas_call(
        functools.partial(_ret_kernel, ts=ts),
        grid=(bsz, s // ts),
        in_specs=[tile(RET_HEADS * RET_DK), tile(RET_HEADS * RET_DK), tile(RET_HEADS * RET_DV), tile(RET_HEADS * RET_DV),
                  const((RET_HEADS, c, c)), const((RET_HEADS, c, LANES)), const((RET_HEADS, c, LANES)),
                  const((RET_HEADS, LANES, RET_DV))],
        out_specs=tile(RET_HEADS * RET_DV),
        out_shape=jax.ShapeDtypeStruct((bsz, s, RET_HEADS * RET_DV), BF16),
        scratch_shapes=[pltpu.VMEM((RET_HEADS, LANES, RET_DV), F32)],
        compiler_params=_params("parallel", "arbitrary"),
        name="retention",
    )(rq, rk, rv, rg, dmask, xi_b, zt_b, dc_b)


def _post_kernel(x_ref, om_ref, or_ref, mod_ref, wo_ref, g2_ref, wrh_ref, wrl_ref, br_ref,
                 x1_ref, h2_ref, ri_ref):
    half = om_ref.shape[-1]
    mix = _mm(om_ref[0], wo_ref[0:half, :]) + _mm(or_ref[0], wo_ref[half:, :])
    x1 = x_ref[0] + mod_ref[0, 2:3, :] * mix
    x1_ref[0] = x1
    h2 = _rms(x1, g2_ref[...]) * (1.0 + mod_ref[0, 4:5, :]) + mod_ref[0, 3:4, :]
    h2_ref[0] = h2

    hi = h2.astype(BF16)
    lo = (h2 - hi.astype(F32)).astype(BF16)
    lg = _mm(hi, wrh_ref[...]) + _mm(lo, wrh_ref[...]) + _mm(hi, wrl_ref[...]) + br_ref[...]
    lane = lax.broadcasted_iota(jnp.int32, lg.shape, 1)
    big = jnp.int32(1 << 20)

    gmask = lane < N_GROUPS
    gmax = jnp.max(jnp.where(gmask, lg, NEG), axis=-1, keepdims=True)
    ge = jnp.where(gmask, jnp.exp(lg - gmax), 0.0)
    pg = ge / jnp.sum(ge, axis=-1, keepdims=True)
    p_top = jnp.max(pg, axis=-1, keepdims=True)
    g_top = jnp.min(jnp.where(gmask & (pg == p_top), lane, big), axis=-1, keepdims=True)

    el = lane - N_GROUPS
    assert EXPERTS_PER_GROUP == 8
    emask = (el >= 0) & (el < N_EXPERTS) & (lax.shift_right_arithmetic(el, 3) == g_top)
    ev = jnp.where(emask, lg, NEG)
    v1 = jnp.max(ev, axis=-1, keepdims=True)
    i1 = jnp.min(jnp.where(emask & (ev == v1), lane, big), axis=-1, keepdims=True)
    emask2 = emask & (lane != i1)
    ev2 = jnp.where(emask2, lg, NEG)
    v2 = jnp.max(ev2, axis=-1, keepdims=True)
    i2 = jnp.min(jnp.where(emask2 & (ev2 == v2), lane, big), axis=-1, keepdims=True)
    e = jnp.exp(v2 - v1)
    den = 1.0 + e
    w1 = (1.0 / den) * p_top
    w2 = (e / den) * p_top
    ri = jnp.where(lane == 0, (i1 - N_GROUPS).astype(F32),
                   jnp.where(lane == 1, (i2 - N_GROUPS).astype(F32),
                             jnp.where(lane == 2, w1, jnp.where(lane == 3, w2, 0.0))))
    ri_ref[0] = ri


def _post(x, o_mla, o_ret, mod3, w_o, norm2_g, w_gr, b_gr, w_er, b_er, tm):
    bsz, s, d = x.shape
    w_r = jnp.concatenate([w_gr, w_er.reshape(d, N_EXPERTS), jnp.zeros((d, LANES - N_GROUPS - N_EXPERTS), F32)], axis=1)
    w_rh = w_r.astype(BF16)
    w_rl = (w_r - w_rh.astype(F32)).astype(BF16)
    b_r = jnp.concatenate([b_gr, b_er.reshape(-1), jnp.zeros((LANES - N_GROUPS - N_EXPERTS,), F32)]).reshape(1, LANES)
    tile = lambda w: pl.BlockSpec((1, tm, w), lambda b, i: (b, i, 0))
    const = lambda shape: pl.BlockSpec(shape, lambda b, i: (0,) * len(shape))
    return pl.pallas_call(
        _post_kernel,
        grid=(bsz, s // tm),
        in_specs=[tile(d), tile(o_mla.shape[-1]), tile(o_ret.shape[-1]), pl.BlockSpec((1, 6, d), lambda b, i: (b, 0, 0)),
                  const((d, d)), const((1, d)), const((d, LANES)), const((d, LANES)), const((1, LANES))],
        out_specs=[tile(d), tile(d), tile(LANES)],
        out_shape=[jax.ShapeDtypeStruct((bsz, s, d), F32), jax.ShapeDtypeStruct((bsz, s, d), F32),
                   jax.ShapeDtypeStruct((bsz, s, LANES), F32)],
        compiler_params=_params("parallel", "arbitrary"),
        name="post_mixer",
    )(x, o_mla, o_ret, mod3, w_o.astype(BF16), norm2_g.reshape(1, d), w_rh, w_rl, b_r)


def _expert_kernel(be_ref, tok_ref, nxt_ref, dst_ref, wg_ref, h2_hbm, w1_ref, w3_ref, w2_ref, y_hbm,
                   xbuf, ybuf, gsem, ssem, *, nb):
    i = pl.program_id(0)
    slot = lax.rem(i, 2)
    rows = MOE_BLOCK

    def gather(idx_ref, sl):
        def issue(r, c):
            pltpu.make_async_copy(h2_hbm.at[pl.ds(idx_ref[0, 0, r], 1)], xbuf.at[sl].at[pl.ds(r, 1)],
                                  gsem.at[sl]).start()
            return c
        lax.fori_loop(0, rows, issue, 0, unroll=8)

    def wait_gather(sl):
        pltpu.make_async_copy(h2_hbm.at[pl.ds(0, rows)], xbuf.at[sl], gsem.at[sl]).wait()

    def wait_scatter(sl):
        pltpu.make_async_copy(ybuf.at[sl], y_hbm.at[pl.ds(0, rows)], ssem.at[sl]).wait()

    @pl.when(i == 0)
    def _():
        gather(tok_ref, 0)

    @pl.when(i + 1 < nb)
    def _():
        gather(nxt_ref, 1 - slot)

    wait_gather(slot)

    @pl.when(i >= 2)
    def _():
        wait_scatter(slot)

    xb = xbuf[slot].astype(BF16)
    a = _mm(xb, w1_ref[0])
    b = _mm(xb, w3_ref[0])
    hm = (_silu(a) * b).astype(BF16)
    ybuf[slot] = _mm(hm, w2_ref[0]) * wg_ref[0]

    def issue_out(r, c):
        pltpu.make_async_copy(ybuf.at[slot].at[pl.ds(r, 1)], y_hbm.at[pl.ds(dst_ref[0, 0, r], 1)],
                              ssem.at[slot]).start()
        return c
    lax.fori_loop(0, rows, issue_out, 0, unroll=8)

    @pl.when(i == nb - 1)
    def _():
        wait_scatter(slot)
        if nb >= 2:
            wait_scatter(1 - slot)


def _experts(h2, blk_expert, tok_buf, dst_buf, w_buf, w1, w3, w2, n_out_rows):
    t, d = h2.shape
    nb = blk_expert.shape[0]
    rows = MOE_BLOCK
    idx_spec = lambda f: pl.BlockSpec((1, 1, rows), f, memory_space=pltpu.SMEM)
    grid_spec = pltpu.PrefetchScalarGridSpec(
        num_scalar_prefetch=1,
        grid=(nb,),
        in_specs=[idx_spec(lambda i, be: (i, 0, 0)),
                  idx_spec(lambda i, be: (jnp.minimum(i + 1, nb - 1), 0, 0)),
                  idx_spec(lambda i, be: (i, 0, 0)),
                  pl.BlockSpec((1, rows, 1), lambda i, be: (i, 0, 0)),
                  pl.BlockSpec(memory_space=pl.ANY),
                  pl.BlockSpec((1, d, D_EXPERT), lambda i, be: (be[i], 0, 0)),
                  pl.BlockSpec((1, d, D_EXPERT), lambda i, be: (be[i], 0, 0)),
                  pl.BlockSpec((1, D_EXPERT, d), lambda i, be: (be[i], 0, 0))],
        out_specs=pl.BlockSpec(memory_space=pl.ANY),
        scratch_shapes=[pltpu.VMEM((2, rows, d), F32), pltpu.VMEM((2, rows, d), F32),
                        pltpu.SemaphoreType.DMA((2,)), pltpu.SemaphoreType.DMA((2,))])
    tok3 = tok_buf.reshape(nb, 1, rows)
    return pl.pallas_call(
        functools.partial(_expert_kernel, nb=nb),
        grid_spec=grid_spec,
        out_shape=jax.ShapeDtypeStruct((n_out_rows, d), F32),
        compiler_params=_params("arbitrary"),
        name="moe_experts",
    )(blk_expert, tok3, tok3, dst_buf.reshape(nb, 1, rows), w_buf.reshape(nb, rows, 1), h2,
      w1.astype(BF16), w3.astype(BF16), w2.astype(BF16))


def _dispatch_tables(ri, t):
    e_ids = ri[:, 0:TOP_K].astype(jnp.int32).reshape(-1)
    gates = ri[:, TOP_K:2 * TOP_K].reshape(-1)
    a_n = t * TOP_K
    p_n = a_n + N_EXPERTS * MOE_BLOCK
    nb = p_n // MOE_BLOCK
    onehot = (e_ids[:, None] == jnp.arange(N_EXPERTS, dtype=jnp.int32)[None, :]).astype(jnp.int32)
    csum = jnp.cumsum(onehot, axis=0)
    rank = jnp.sum((csum - 1) * onehot, axis=1)
    counts = csum[-1]
    padded = ((counts + MOE_BLOCK - 1) // MOE_BLOCK) * MOE_BLOCK
    pend = jnp.cumsum(padded)
    pstart = pend - padded
    slot = pstart[e_ids] + rank
    a_idx = jnp.arange(a_n, dtype=jnp.int32)
    tok_buf = jnp.zeros((p_n,), jnp.int32).at[slot].set(a_idx // TOP_K)
    w_buf = jnp.zeros((p_n,), F32).at[slot].set(gates)
    s_idx = jnp.arange(p_n, dtype=jnp.int32)
    seg = jnp.searchsorted(pend, s_idx, side="right").astype(jnp.int32)
    seg_c = jnp.minimum(seg, N_EXPERTS - 1)
    cstart = jnp.cumsum(counts) - counts
    valid_before = jnp.where(seg < N_EXPERTS,
                             cstart[seg_c] + jnp.minimum(s_idx - pstart[seg_c], counts[seg_c]), a_n)
    dump = TOP_K * t + (s_idx - valid_before)
    dst_buf = dump.at[slot].set((a_idx % TOP_K) * t + a_idx // TOP_K)
    blk_expert = jnp.minimum(jnp.searchsorted(pend, jnp.arange(nb, dtype=jnp.int32) * MOE_BLOCK, side="right"),
                             N_EXPERTS - 1).astype(jnp.int32)
    n_out_rows = TOP_K * t + (p_n - a_n)
    return blk_expert, tok_buf, dst_buf, w_buf, n_out_rows


def _final_kernel(x1_ref, y0_ref, y1_ref, mod_ref, g_ref, o_ref):
    x2 = x1_ref[0] + mod_ref[0, 5:6, :] * (y0_ref[...] + y1_ref[...])
    o_ref[0] = _rms(x2, g_ref[...])


def _final(x1, y, mod3, final_g, tm):
    bsz, s, d = x1.shape
    per_b = s // tm
    tpb = bsz * per_b
    return pl.pallas_call(
        _final_kernel,
        grid=(bsz, per_b),
        in_specs=[pl.BlockSpec((1, tm, d), lambda b, i: (b, i, 0)),
                  pl.BlockSpec((tm, d), lambda b, i: (b * per_b + i, 0)),
                  pl.BlockSpec((tm, d), lambda b, i: (tpb + b * per_b + i, 0)),
                  pl.BlockSpec((1, 6, d), lambda b, i: (b, 0, 0)),
                  pl.BlockSpec((1, d), lambda b, i: (0, 0))],
        out_specs=pl.BlockSpec((1, tm, d), lambda b, i: (b, i, 0)),
        out_shape=jax.ShapeDtypeStruct((bsz, s, d), F32),
        compiler_params=_params("parallel", "arbitrary"),
        name="final_norm",
    )(x1, y, y, mod3, final_g.reshape(1, d))


def kernel(x, c, positions, w_ada, b_ada, norm1_g, w_in, q_norm_g, w_uq, kv_norm_g, w_ukv, w_o, norm2_g,
           w_gr, b_gr, w_er, b_er, w1, w3, w2, final_g):
    bsz, s, d = x.shape
    assert w_ada.shape[0] == 1, "one layer"
    tm = min(512, s)
    tq = min(256, s)
    ts = min(1024, s)
    mod3 = _adaln(c, w_ada[0], b_ada[0]).reshape(bsz, 6, d)
    pos3 = positions.astype(F32).reshape(bsz, s, 1)
    q, k, v, rq, rk, rv, rg = _pre(x, mod3, pos3, norm1_g[0], w_in[0], q_norm_g[0], w_uq[0], kv_norm_g[0], w_ukv[0], tm)
    o_mla = _attention(q, k, v, tq)
    o_ret = _retention(rq, rk, rv, rg, ts)
    x1, h2, ri = _post(x, o_mla, o_ret, mod3, w_o[0], norm2_g[0], w_gr[0], b_gr[0], w_er[0], b_er[0], tm)
    t = bsz * s
    blk_expert, tok_buf, dst_buf, w_buf, n_out_rows = _dispatch_tables(ri.reshape(t, LANES), t)
    y = _experts(h2.reshape(t, d), blk_expert, tok_buf, dst_buf, w_buf, w1[0], w3[0], w2[0], n_out_rows)
    return _final(x1, y, mod3, final_g, tm)
```

```python
import functools

import jax
import jax.numpy as jnp
from jax import lax
from jax.experimental import pallas as pl
from jax.experimental.pallas import tpu as pltpu

MLA_HEADS = 8
MLA_NOPE = 64
MLA_ROPE = 32
MLA_V = 64
Q_LORA = 256
KV_LORA = 128
RET_HEADS = 4
RET_DK = 64
RET_DV = 128
RET_CHUNK = 128
ROPE_BASE = 10000.0
NORM_EPS = 1e-6
N_GROUPS = 4
EXPERTS_PER_GROUP = 8
N_EXPERTS = N_GROUPS * EXPERTS_PER_GROUP
TOP_K = 2
D_EXPERT = 256
MOE_BLOCK = 256

LANES = 128
VMEM_LIMIT = 56 * 1024 * 1024

F32 = jnp.float32
BF16 = jnp.bfloat16
NEG = float(jnp.finfo(jnp.float32).min)
LOG2_E = 1.4426950408889634
ATTN_HEADS_PER_STEP = 8

_C_Q = 0
_C_KV = _C_Q + Q_LORA
_C_KPE = _C_KV + KV_LORA
_C_KPE_ROT = _C_KPE + LANES
_C_RQ = _C_KPE_ROT + LANES
_C_RQ_ROT = _C_RQ + RET_HEADS * RET_DK
_C_RK = _C_RQ_ROT + RET_HEADS * RET_DK
_C_RK_ROT = _C_RK + RET_HEADS * RET_DK
_C_RV = _C_RK_ROT + RET_HEADS * RET_DK
_C_RG = _C_RV + RET_HEADS * RET_DV
_IN_PERM = _C_RG + RET_HEADS * RET_DV


def _silu(v):
    return v / (1.0 + jnp.exp(-v))


def _mm(a, b):
    return jnp.dot(a, b, preferred_element_type=F32)


def _mm_nt(a, b):
    return lax.dot_general(a, b, (((1,), (1,)), ((), ())), preferred_element_type=F32)


def _params(*sem):
    return pltpu.CompilerParams(dimension_semantics=sem, vmem_limit_bytes=VMEM_LIMIT)


def _adaln_kernel(c_ref, w_ref, b_ref, o_ref):
    a = _silu(c_ref[...]).astype(BF16)
    o_ref[...] = _mm(a, w_ref[...].astype(BF16)) + b_ref[...]


def _adaln(c, w_ada, b_ada):
    bsz, d = c.shape
    n = w_ada.shape[1]
    tn = d
    return pl.pallas_call(
        _adaln_kernel,
        grid=(n // tn,),
        in_specs=[pl.BlockSpec((bsz, d), lambda j: (0, 0)),
                  pl.BlockSpec((d, tn), lambda j: (0, j)),
                  pl.BlockSpec((1, tn), lambda j: (0, j))],
        out_specs=pl.BlockSpec((bsz, tn), lambda j: (0, j)),
        out_shape=jax.ShapeDtypeStruct((bsz, n), F32),
        compiler_params=_params("arbitrary"),
        name="adaln",
    )(c, w_ada, b_ada.reshape(1, n))


def _rms(v, g):
    return v * lax.rsqrt(jnp.mean(v * v, axis=-1, keepdims=True) + NORM_EPS) * g


def _pre_kernel(x_ref, mod_ref, pos_ref, g1_ref, win_ref, qg_ref, wuq_ref, wuqr_ref, kvg_ref, wuk_ref,
                wuv_ref, invf_ref, q_ref, k_ref, v_ref, rq_ref, rk_ref, rv_ref, rg_ref):
    x = x_ref[0]
    sh1 = mod_ref[0, 0:1, :]
    sc1 = mod_ref[0, 1:2, :]
    h = _rms(x, g1_ref[...]) * (1.0 + sc1) + sh1
    proj = _mm(h.astype(BF16), win_ref[...])

    ang = pos_ref[0] * invf_ref[...]
    cs = jnp.cos(ang)
    sn = jnp.sin(ang)
    lane = lax.broadcasted_iota(jnp.int32, cs.shape, 1)
    hi = lane >= RET_DK
    c_mla = jnp.where(hi, cs, 1.0)
    s_mla = jnp.where(hi, sn, 0.0)
    c_ret = jnp.where(hi, pltpu.roll(cs, RET_DK, 1), cs)
    s_ret = jnp.where(hi, pltpu.roll(sn, RET_DK, 1), sn)

    cq = _rms(proj[:, _C_Q:_C_Q + Q_LORA], qg_ref[...]).astype(BF16)
    qa = _mm(cq, wuq_ref[...])
    qb = _mm(cq, wuqr_ref[...])
    scale = (MLA_NOPE + MLA_ROPE) ** -0.5 * LOG2_E
    for hd in range(MLA_HEADS):
        sl = slice(hd * LANES, (hd + 1) * LANES)
        q_ref[0, :, sl] = ((qa[:, sl] * c_mla + qb[:, sl] * s_mla) * scale).astype(BF16)

    ckv = _rms(proj[:, _C_KV:_C_KV + KV_LORA], kvg_ref[...]).astype(BF16)
    kn = _mm(ckv, wuk_ref[...])
    kpe = proj[:, _C_KPE:_C_KPE + LANES] * c_mla + proj[:, _C_KPE_ROT:_C_KPE_ROT + LANES] * s_mla
    for hd in range(MLA_HEADS):
        sl = slice(hd * LANES, (hd + 1) * LANES)
        k_ref[0, :, sl] = (kn[:, sl] + kpe).astype(BF16)
    vt = _mm(ckv, wuv_ref[...]).T
    tq = v_ref.shape[-1]
    for cb in range(v_ref.shape[1]):
        v_ref[0, cb] = vt[:, cb * tq:(cb + 1) * tq].astype(BF16)

    for i in range(RET_HEADS * RET_DK // LANES):
        o = i * LANES
        rq = proj[:, _C_RQ + o:_C_RQ + o + LANES] * c_ret + proj[:, _C_RQ_ROT + o:_C_RQ_ROT + o + LANES] * s_ret
        rk = proj[:, _C_RK + o:_C_RK + o + LANES] * c_ret + proj[:, _C_RK_ROT + o:_C_RK_ROT + o + LANES] * s_ret
        rq_ref[0, :, o:o + LANES] = rq.astype(BF16)
        rk_ref[0, :, o:o + LANES] = (rk * (RET_DK ** -0.5)).astype(BF16)
    rv_ref[0] = proj[:, _C_RV:_C_RV + RET_HEADS * RET_DV].astype(BF16)
    rg_ref[0] = proj[:, _C_RG:_C_RG + RET_HEADS * RET_DV]


def _rot_cols(w, half):
    k = w.shape[0]
    w4 = w.reshape(k, -1, 2, half)
    return jnp.stack([-w4[:, :, 1], w4[:, :, 0]], axis=2).reshape(k, -1)


def _pad_heads(w, width, left):
    k = w.shape[0]
    w3 = w.reshape(k, -1, width)
    w3 = jnp.pad(w3, ((0, 0), (0, 0), (left, LANES - left - width)))
    return w3.reshape(k, -1)


def _pre(x, mod3, pos3, norm1_g, w_in, q_norm_g, w_uq, kv_norm_g, w_ukv, tm, tq):
    bsz, s, d = x.shape
    o = 0
    parts = {}
    for name, width in (("cq", Q_LORA), ("ckv", KV_LORA), ("kr", MLA_ROPE), ("rq", RET_HEADS * RET_DK),
                        ("rk", RET_HEADS * RET_DK), ("rv", RET_HEADS * RET_DV), ("rg", RET_HEADS * RET_DV)):
        parts[name] = w_in[:, o:o + width]
        o += width
    w_in_p = jnp.concatenate([
        parts["cq"], parts["ckv"],
        _pad_heads(parts["kr"], MLA_ROPE, MLA_NOPE), _pad_heads(_rot_cols(parts["kr"], MLA_ROPE // 2), MLA_ROPE, MLA_NOPE),
        parts["rq"], _rot_cols(parts["rq"], RET_DK // 2),
        parts["rk"], _rot_cols(parts["rk"], RET_DK // 2),
        parts["rv"], parts["rg"]], axis=1).astype(BF16)
    assert w_in_p.shape[1] == _IN_PERM
    wq3 = w_uq.reshape(Q_LORA, MLA_HEADS, MLA_NOPE + MLA_ROPE)
    wq_rope = wq3[:, :, MLA_NOPE:].reshape(Q_LORA, -1)
    w_uq_p = _pad_heads(wq3.reshape(Q_LORA, -1), MLA_NOPE + MLA_ROPE, 0).astype(BF16)
    w_uq_r = _pad_heads(_rot_cols(wq_rope, MLA_ROPE // 2), MLA_ROPE, MLA_NOPE).astype(BF16)
    wkv3 = w_ukv.reshape(KV_LORA, MLA_HEADS, MLA_NOPE + MLA_V)
    w_uk_p = _pad_heads(wkv3[:, :, :MLA_NOPE].reshape(KV_LORA, -1), MLA_NOPE, 0).astype(BF16)
    w_uv = wkv3[:, :, MLA_NOPE:].reshape(KV_LORA, -1).astype(BF16)
    half_r, half_m = RET_DK // 2, MLA_ROPE // 2
    f_r = ROPE_BASE ** (-(jnp.arange(half_r, dtype=F32) / half_r))
    f_m = ROPE_BASE ** (-(jnp.arange(half_m, dtype=F32) / half_m))
    invf = jnp.concatenate([f_r, f_r, f_m, f_m, jnp.zeros((LANES - 2 * half_r - 2 * half_m,), F32)]).reshape(1, LANES)

    hq = MLA_HEADS * LANES
    const = lambda shape: pl.BlockSpec(shape, lambda b, i: (0,) * len(shape))
    tile = lambda w: pl.BlockSpec((1, tm, w), lambda b, i: (b, i, 0))
    return pl.pallas_call(
        _pre_kernel,
        grid=(bsz, s // tm),
        in_specs=[tile(d), pl.BlockSpec((1, 6, d), lambda b, i: (b, 0, 0)), tile(1), const((1, d)),
                  const((d, _IN_PERM)), const((1, Q_LORA)), const((Q_LORA, hq)), const((Q_LORA, hq)),
                  const((1, KV_LORA)), const((KV_LORA, hq)), const((KV_LORA, MLA_HEADS * MLA_V)), const((1, LANES))],
        out_specs=[tile(hq), tile(hq),
                   pl.BlockSpec((1, tm // tq, MLA_HEADS * MLA_V, tq), lambda b, i: (b, i, 0, 0)),
                   tile(RET_HEADS * RET_DK), tile(RET_HEADS * RET_DK),
                   tile(RET_HEADS * RET_DV), tile(RET_HEADS * RET_DV)],
        out_shape=[jax.ShapeDtypeStruct((bsz, s, hq), BF16), jax.ShapeDtypeStruct((bsz, s, hq), BF16),
                   jax.ShapeDtypeStruct((bsz, s // tq, MLA_HEADS * MLA_V, tq), BF16),
                   jax.ShapeDtypeStruct((bsz, s, RET_HEADS * RET_DK), BF16),
                   jax.ShapeDtypeStruct((bsz, s, RET_HEADS * RET_DK), BF16),
                   jax.ShapeDtypeStruct((bsz, s, RET_HEADS * RET_DV), BF16),
                   jax.ShapeDtypeStruct((bsz, s, RET_HEADS * RET_DV), F32)],
        compiler_params=_params("parallel", "arbitrary"),
        name="pre_mixer",
    )(x, mod3, pos3, norm1_g.reshape(1, d), w_in_p, q_norm_g.reshape(1, -1), w_uq_p, w_uq_r,
      kv_norm_g.reshape(1, -1), w_uk_p, w_uv, invf)


def _attn_kernel(q_ref, k_ref, vt_ref, o_ref, *, tq, hps):
    qi = pl.program_id(1)
    key = lax.broadcasted_iota(jnp.int32, (tq, tq), 0)
    qry = lax.broadcasted_iota(jnp.int32, (tq, tq), 1)
    hsl = [slice(hh * LANES, (hh + 1) * LANES) for hh in range(hps)]
    vsl = [slice(hh * MLA_V, (hh + 1) * MLA_V) for hh in range(hps)]
    qs = [q_ref[0, :, hs] for hs in hsl]

    def heads(carry, blk, nblk, masked=False):
        start = pl.multiple_of(blk * tq, tq)
        sts = [_mm_nt(k_ref[0, pl.ds(start, nblk * tq), hsl[hh]], qs[hh]) for hh in range(hps)]
        ps, stats = [], []
        for hh in range(hps):
            m, l, acc = carry[hh]
            st = jnp.where(key <= qry, sts[hh], NEG) if masked else sts[hh]
            m_new = jnp.maximum(m, jnp.max(st, axis=0, keepdims=True))
            p = jnp.exp2(st - m_new)
            alpha = jnp.exp2(m - m_new)
            stats.append((m_new, alpha * l + jnp.sum(p, axis=0, keepdims=True), alpha * acc))
            ps.append(p.astype(BF16))
        out = []
        for hh in range(hps):
            m_new, l, acc = stats[hh]
            for c in range(nblk):
                acc = acc + _mm(vt_ref[0, blk + c, vsl[hh], :], ps[hh][c * tq:(c + 1) * tq, :])
            out.append((m_new, l, acc))
        return tuple(out)

    init = (jnp.full((1, tq), NEG, F32), jnp.zeros((1, tq), F32), jnp.zeros((MLA_V, tq), F32))
    carry = (init,) * hps
    carry = lax.fori_loop(0, qi // 2, lambda j, c: heads(c, 2 * j, 2), carry)
    carry = lax.fori_loop(0, qi % 2, lambda j, c: heads(c, qi - 1, 1), carry)
    carry = heads(carry, qi, 1, masked=True)
    out_t = jnp.concatenate([acc / l for _, l, acc in carry], axis=0)
    o_ref[0] = out_t.T.astype(BF16)


def _attention(q, k, vt, tq, hps):
    bsz, s, _ = q.shape
    groups = MLA_HEADS // hps
    assert vt.shape == (bsz, s // tq, MLA_HEADS * MLA_V, tq)
    return pl.pallas_call(
        functools.partial(_attn_kernel, tq=tq, hps=hps),
        grid=(bsz * groups, s // tq),
        in_specs=[pl.BlockSpec((1, tq, hps * LANES), lambda g, i: (g // groups, i, g % groups)),
                  pl.BlockSpec((1, s, hps * LANES), lambda g, i: (g // groups, 0, g % groups)),
                  pl.BlockSpec((1, s // tq, hps * MLA_V, tq), lambda g, i: (g // groups, 0, g % groups, 0))],
        out_specs=pl.BlockSpec((1, tq, hps * MLA_V), lambda g, i: (g // groups, i, g % groups)),
        out_shape=jax.ShapeDtypeStruct((bsz, s, MLA_HEADS * MLA_V), BF16),
        compiler_params=_params("parallel", "arbitrary"),
        name="mla_attention",
    )(q, k, vt)


def _ret_kernel(rq_ref, rk_ref, rv_ref, rg_ref, dm_ref, xi_ref, zt_ref, dc_ref, o_ref, st_ref, *, ts):
    @pl.when(pl.program_id(1) == 0)
    def _():
        st_ref[...] = jnp.zeros_like(st_ref)

    lane = lax.broadcasted_iota(jnp.int32, (RET_CHUNK, LANES), 1)

    def chunk(c, carry):
        rows = pl.ds(pl.multiple_of(c * RET_CHUNK, RET_CHUNK), RET_CHUNK)
        for hd in range(RET_HEADS):
            pair, sub = divmod(hd, LANES // RET_DK)
            mine = (lane >= sub * RET_DK) & (lane < (sub + 1) * RET_DK)
            ps = slice(pair * LANES, (pair + 1) * LANES)
            vs = slice(hd * RET_DV, (hd + 1) * RET_DV)
            qh = jnp.where(mine, rq_ref[0, rows, ps], 0.0).astype(BF16)
            kh = jnp.where(mine, rk_ref[0, rows, ps], 0.0).astype(BF16)
            vh = rv_ref[0, rows, vs]
            st = st_ref[hd]
            sc = _mm_nt(qh, kh) * dm_ref[hd]
            o = _mm(sc.astype(BF16), vh)
            o = o + _mm((qh.astype(F32) * xi_ref[hd]).astype(BF16), st.astype(BF16))
            kz = (kh.astype(F32) * zt_ref[hd]).astype(BF16)
            st_ref[hd] = st * dc_ref[hd] + _mm(kz.T, vh)
            mu = jnp.mean(o, axis=-1, keepdims=True)
            oc = o - mu
            on = oc * lax.rsqrt(jnp.mean(oc * oc, axis=-1, keepdims=True) + NORM_EPS)
            o_ref[0, rows, vs] = (_silu(rg_ref[0, rows, vs]) * on).astype(BF16)
        return carry

    lax.fori_loop(0, ts // RET_CHUNK, chunk, 0)


def _retention(rq, rk, rv, rg, ts):
    bsz, s, _ = rq.shape
    c = RET_CHUNK
    gamma = 1.0 - jnp.power(2.0, -5.0 - jnp.arange(RET_HEADS, dtype=F32))
    log_g = jnp.log(gamma)
    idx = jnp.arange(c, dtype=F32)
    diff = idx[:, None] - idx[None, :]
    dmask = jnp.where(diff[None] >= 0, jnp.exp(jnp.maximum(diff, 0.0)[None] * log_g[:, None, None]), 0.0)
    zeta = jnp.exp((c - 1.0 - idx)[None, :] * log_g[:, None])
    xi = jnp.exp((idx + 1.0)[None, :] * log_g[:, None])
    decay = jnp.exp(c * log_g)
    xi_b = jnp.broadcast_to(xi[:, :, None], (RET_HEADS, c, LANES))
    zt_b = jnp.broadcast_to(zeta[:, :, None], (RET_HEADS, c, LANES))
    dc_b = jnp.broadcast_to(decay[:, None, None], (RET_HEADS, LANES, RET_DV))

    tile = lambda w: pl.BlockSpec((1, ts, w), lambda b, i: (b, i, 0))
    const = lambda shape: pl.BlockSpec(shape, lambda b, i: (0,) * len(shape))
    return pl.pallas_call(
        functools.partial(_ret_kernel, ts=ts),
        grid=(bsz, s // ts),
        in_specs=[tile(RET_HEADS * RET_DK), tile(RET_HEADS * RET_DK), tile(RET_HEADS * RET_DV), tile(RET_HEADS * RET_DV),
                  const((RET_HEADS, c, c)), const((RET_HEADS, c, LANES)), const((RET_HEADS, c, LANES)),
                  const((RET_HEADS, LANES, RET_DV))],
        out_specs=tile(RET_HEADS * RET_DV),
        out_shape=jax.ShapeDtypeStruct((bsz, s, RET_HEADS * RET_DV), BF16),
        scratch_shapes=[pltpu.VMEM((RET_HEADS, LANES, RET_DV), F32)],
        compiler_params=_params("parallel", "arbitrary"),
        name="retention",
    )(rq, rk, rv, rg, dmask, xi_b, zt_b, dc_b)


def _post_kernel(x_ref, om_ref, or_ref, mod_ref, wo_ref, g2_ref, wrh_ref, wrl_ref, br_ref,
                 x1_ref, h2_ref, ri_ref):
    half = om_ref.shape[-1]
    mix = _mm(om_ref[0], wo_ref[0:half, :]) + _mm(or_ref[0], wo_ref[half:, :])
    x1 = x_ref[0] + mod_ref[0, 2:3, :] * mix
    x1_ref[0] = x1
    h2 = _rms(x1, g2_ref[...]) * (1.0 + mod_ref[0, 4:5, :]) + mod_ref[0, 3:4, :]
    h2_ref[0] = h2

    hi = h2.astype(BF16)
    lo = (h2 - hi.astype(F32)).astype(BF16)
    lg = _mm(hi, wrh_ref[...]) + _mm(lo, wrh_ref[...]) + _mm(hi, wrl_ref[...]) + br_ref[...]
    lane = lax.broadcasted_iota(jnp.int32, lg.shape, 1)
    big = jnp.int32(1 << 20)

    gmask = lane < N_GROUPS
    gmax = jnp.max(jnp.where(gmask, lg, NEG), axis=-1, keepdims=True)
    ge = jnp.where(gmask, jnp.exp(lg - gmax), 0.0)
    pg = ge / jnp.sum(ge, axis=-1, keepdims=True)
    p_top = jnp.max(pg, axis=-1, keepdims=True)
    g_top = jnp.min(jnp.where(gmask & (pg == p_top), lane, big), axis=-1, keepdims=True)

    el = lane - N_GROUPS
    assert EXPERTS_PER_GROUP == 8
    emask = (el >= 0) & (el < N_EXPERTS) & (lax.shift_right_arithmetic(el, 3) == g_top)
    ev = jnp.where(emask, lg, NEG)
    v1 = jnp.max(ev, axis=-1, keepdims=True)
    i1 = jnp.min(jnp.where(emask & (ev == v1), lane, big), axis=-1, keepdims=True)
    emask2 = emask & (lane != i1)
    ev2 = jnp.where(emask2, lg, NEG)
    v2 = jnp.max(ev2, axis=-1, keepdims=True)
    i2 = jnp.min(jnp.where(emask2 & (ev2 == v2), lane, big), axis=-1, keepdims=True)
    e = jnp.exp(v2 - v1)
    den = 1.0 + e
    w1 = (1.0 / den) * p_top
    w2 = (e / den) * p_top
    ri = jnp.where(lane == 0, (i1 - N_GROUPS).astype(F32),
                   jnp.where(lane == 1, (i2 - N_GROUPS).astype(F32),
                             jnp.where(lane == 2, w1, jnp.where(lane == 3, w2, 0.0))))
    ri_ref[0] = ri


def _post(x, o_mla, o_ret, mod3, w_o, norm2_g, w_gr, b_gr, w_er, b_er, tm):
    bsz, s, d = x.shape
    w_r = jnp.concatenate([w_gr, w_er.reshape(d, N_EXPERTS), jnp.zeros((d, LANES - N_GROUPS - N_EXPERTS), F32)], axis=1)
    w_rh = w_r.astype(BF16)
    w_rl = (w_r - w_rh.astype(F32)).astype(BF16)
    b_r = jnp.concatenate([b_gr, b_er.reshape(-1), jnp.zeros((LANES - N_GROUPS - N_EXPERTS,), F32)]).reshape(1, LANES)
    tile = lambda w: pl.BlockSpec((1, tm, w), lambda b, i: (b, i, 0))
    const = lambda shape: pl.BlockSpec(shape, lambda b, i: (0,) * len(shape))
    return pl.pallas_call(
        _post_kernel,
        grid=(bsz, s // tm),
        in_specs=[tile(d), tile(o_mla.shape[-1]), tile(o_ret.shape[-1]), pl.BlockSpec((1, 6, d), lambda b, i: (b, 0, 0)),
                  const((d, d)), const((1, d)), const((d, LANES)), const((d, LANES)), const((1, LANES))],
        out_specs=[tile(d), tile(d), tile(LANES)],
        out_shape=[jax.ShapeDtypeStruct((bsz, s, d), F32), jax.ShapeDtypeStruct((bsz, s, d), F32),
                   jax.ShapeDtypeStruct((bsz, s, LANES), F32)],
        compiler_params=_params("parallel", "arbitrary"),
        name="post_mixer",
    )(x, o_mla, o_ret, mod3, w_o.astype(BF16), norm2_g.reshape(1, d), w_rh, w_rl, b_r)


def _expert_kernel(be_ref, tok_ref, nxt_ref, dst_ref, wg_ref, h2_hbm, w1_ref, w3_ref, w2_ref, y_hbm,
                   xbuf, ybuf, gsem, ssem, *, nb):
    i = pl.program_id(0)
    slot = lax.rem(i, 2)
    rows = MOE_BLOCK

    def gather(idx_ref, sl):
        def issue(r, c):
            pltpu.make_async_copy(h2_hbm.at[pl.ds(idx_ref[0, 0, r], 1)], xbuf.at[sl].at[pl.ds(r, 1)],
                                  gsem.at[sl]).start()
            return c
        lax.fori_loop(0, rows, issue, 0, unroll=8)

    def wait_gather(sl):
        pltpu.make_async_copy(h2_hbm.at[pl.ds(0, rows)], xbuf.at[sl], gsem.at[sl]).wait()

    def wait_scatter(sl):
        pltpu.make_async_copy(ybuf.at[sl], y_hbm.at[pl.ds(0, rows)], ssem.at[sl]).wait()

    @pl.when(i == 0)
    def _():
        gather(tok_ref, 0)

    @pl.when(i + 1 < nb)
    def _():
        gather(nxt_ref, 1 - slot)

    wait_gather(slot)

    @pl.when(i >= 2)
    def _():
        wait_scatter(slot)

    xb = xbuf[slot].astype(BF16)
    a = _mm(xb, w1_ref[0])
    b = _mm(xb, w3_ref[0])
    hm = (_silu(a) * b).astype(BF16)
    ybuf[slot] = _mm(hm, w2_ref[0]) * wg_ref[0]

    def issue_out(r, c):
        pltpu.make_async_copy(ybuf.at[slot].at[pl.ds(r, 1)], y_hbm.at[pl.ds(dst_ref[0, 0, r], 1)],
                              ssem.at[slot]).start()
        return c
    lax.fori_loop(0, rows, issue_out, 0, unroll=8)

    @pl.when(i == nb - 1)
    def _():
        wait_scatter(slot)
        if nb >= 2:
            wait_scatter(1 - slot)


def _experts(h2, blk_expert, tok_buf, dst_buf, w_buf, w1, w3, w2, n_out_rows):
    t, d = h2.shape
    nb = blk_expert.shape[0]
    rows = MOE_BLOCK
    idx_spec = lambda f: pl.BlockSpec((1, 1, rows), f, memory_space=pltpu.SMEM)
    grid_spec = pltpu.PrefetchScalarGridSpec(
        num_scalar_prefetch=1,
        grid=(nb,),
        in_specs=[idx_spec(lambda i, be: (i, 0, 0)),
                  idx_spec(lambda i, be: (jnp.minimum(i + 1, nb - 1), 0, 0)),
                  idx_spec(lambda i, be: (i, 0, 0)),
                  pl.BlockSpec((1, rows, 1), lambda i, be: (i, 0, 0)),
                  pl.BlockSpec(memory_space=pl.ANY),
                  pl.BlockSpec((1, d, D_EXPERT), lambda i, be: (be[i], 0, 0)),
                  pl.BlockSpec((1, d, D_EXPERT), lambda i, be: (be[i], 0, 0)),
                  pl.BlockSpec((1, D_EXPERT, d), lambda i, be: (be[i], 0, 0))],
        out_specs=pl.BlockSpec(memory_space=pl.ANY),
        scratch_shapes=[pltpu.VMEM((2, rows, d), F32), pltpu.VMEM((2, rows, d), F32),
                        pltpu.SemaphoreType.DMA((2,)), pltpu.SemaphoreType.DMA((2,))])
    tok3 = tok_buf.reshape(nb, 1, rows)
    return pl.pallas_call(
        functools.partial(_expert_kernel, nb=nb),
        grid_spec=grid_spec,
        out_shape=jax.ShapeDtypeStruct((n_out_rows, d), F32),
        compiler_params=_params("arbitrary"),
        name="moe_experts",
    )(blk_expert, tok3, tok3, dst_buf.reshape(nb, 1, rows), w_buf.reshape(nb, rows, 1), h2,
      w1.astype(BF16), w3.astype(BF16), w2.astype(BF16))


def _dispatch_tables(ri, t):
    e_ids = ri[:, 0:TOP_K].astype(jnp.int32).reshape(-1)
    gates = ri[:, TOP_K:2 * TOP_K].reshape(-1)
    a_n = t * TOP_K
    p_n = a_n + N_EXPERTS * MOE_BLOCK
    nb = p_n // MOE_BLOCK
    onehot = (e_ids[:, None] == jnp.arange(N_EXPERTS, dtype=jnp.int32)[None, :]).astype(jnp.int32)
    csum = jnp.cumsum(onehot, axis=0)
    rank = jnp.sum((csum - 1) * onehot, axis=1)
    counts = csum[-1]
    padded = ((counts + MOE_BLOCK - 1) // MOE_BLOCK) * MOE_BLOCK
    pend = jnp.cumsum(padded)
    pstart = pend - padded
    slot = pstart[e_ids] + rank
    a_idx = jnp.arange(a_n, dtype=jnp.int32)
    tok_buf = jnp.zeros((p_n,), jnp.int32).at[slot].set(a_idx // TOP_K)
    w_buf = jnp.zeros((p_n,), F32).at[slot].set(gates)
    s_idx = jnp.arange(p_n, dtype=jnp.int32)
    seg = jnp.searchsorted(pend, s_idx, side="right").astype(jnp.int32)
    seg_c = jnp.minimum(seg, N_EXPERTS - 1)
    cstart = jnp.cumsum(counts) - counts
    valid_before = jnp.where(seg < N_EXPERTS,
                             cstart[seg_c] + jnp.minimum(s_idx - pstart[seg_c], counts[seg_c]), a_n)
    dump = TOP_K * t + (s_idx - valid_before)
    dst_buf = dump.at[slot].set((a_idx % TOP_K) * t + a_idx // TOP_K)
    blk_expert = jnp.minimum(jnp.searchsorted(pend, jnp.arange(nb, dtype=jnp.int32) * MOE_BLOCK, side="right"),
                             N_EXPERTS - 1).astype(jnp.int32)
    n_out_rows = TOP_K * t + (p_n - a_n)
    return blk_expert, tok_buf, dst_buf, w_buf, n_out_rows


def _final_kernel(x1_ref, y0_ref, y1_ref, mod_ref, g_ref, o_ref):
    x2 = x1_ref[0] + mod_ref[0, 5:6, :] * (y0_ref[...] + y1_ref[...])
    o_ref[0] = _rms(x2, g_ref[...])


def _final(x1, y, mod3, final_g, tm):
    bsz, s, d = x1.shape
    per_b = s // tm
    tpb = bsz * per_b
    return pl.pallas_call(
        _final_kernel,
        grid=(bsz, per_b),
        in_specs=[pl.BlockSpec((1, tm, d), lambda b, i: (b, i, 0)),
                  pl.BlockSpec((tm, d), lambda b, i: (b * per_b + i, 0)),
                  pl.BlockSpec((tm, d), lambda b, i: (tpb + b * per_b + i, 0)),
                  pl.BlockSpec((1, 6, d), lambda b, i: (b, 0, 0)),
                  pl.BlockSpec((1, d), lambda b, i: (0, 0))],
        out_specs=pl.BlockSpec((1, tm, d), lambda b, i: (b, i, 0)),
        out_shape=jax.ShapeDtypeStruct((bsz, s, d), F32),
        compiler_params=_params("parallel", "arbitrary"),
        name="final_norm",
    )(x1, y, y, mod3, final_g.reshape(1, d))


def kernel(x, c, positions, w_ada, b_ada, norm1_g, w_in, q_norm_g, w_uq, kv_norm_g, w_ukv, w_o, norm2_g,
           w_gr, b_gr, w_er, b_er, w1, w3, w2, final_g):
    bsz, s, d = x.shape
    assert w_ada.shape[0] == 1, "one layer"
    tm = min(512, s)
    tq = min(256, s)
    ts = min(1024, s)
    mod3 = _adaln(c, w_ada[0], b_ada[0]).reshape(bsz, 6, d)
    pos3 = positions.astype(F32).reshape(bsz, s, 1)
    q, k, vt, rq, rk, rv, rg = _pre(x, mod3, pos3, norm1_g[0], w_in[0], q_norm_g[0], w_uq[0], kv_norm_g[0], w_ukv[0],
                                    tm, tq)
    o_mla = _attention(q, k, vt, tq, ATTN_HEADS_PER_STEP)
    o_ret = _retention(rq, rk, rv, rg, ts)
    x1, h2, ri = _post(x, o_mla, o_ret, mod3, w_o[0], norm2_g[0], w_gr[0], b_gr[0], w_er[0], b_er[0], tm)
    t = bsz * s
    blk_expert, tok_buf, dst_buf, w_buf, n_out_rows = _dispatch_tables(ri.reshape(t, LANES), t)
    y = _experts(h2.reshape(t, d), blk_expert, tok_buf, dst_buf, w_buf, w1[0], w3[0], w2[0], n_out_rows)
    return _final(x1, y, mod3, final_g, tm)
```

```python
import functools

import jax
import jax.numpy as jnp
from jax import lax
from jax.experimental import pallas as pl
from jax.experimental.pallas import tpu as pltpu

MLA_HEADS = 8
MLA_NOPE = 64
MLA_ROPE = 32
MLA_V = 64
Q_LORA = 256
KV_LORA = 128
RET_HEADS = 4
RET_DK = 64
RET_DV = 128
RET_CHUNK = 128
ROPE_BASE = 10000.0
NORM_EPS = 1e-6
N_GROUPS = 4
EXPERTS_PER_GROUP = 8
N_EXPERTS = N_GROUPS * EXPERTS_PER_GROUP
TOP_K = 2
D_EXPERT = 256
MOE_TILE = 256
MOE_CHUNK = 16
MOE_SLAB = 1024
MOE_TILE_CHUNKS = MOE_SLAB // MOE_CHUNK
MOE_BLOCK = 512

LANES = 128
VMEM_LIMIT = 56 * 1024 * 1024

F32 = jnp.float32
BF16 = jnp.bfloat16
NEG = float(jnp.finfo(jnp.float32).min)
LOG2_E = 1.4426950408889634
ATTN_HEADS_PER_STEP = 8

_C_Q = 0
_C_KV = _C_Q + Q_LORA
_C_KPE = _C_KV + KV_LORA
_C_KPE_ROT = _C_KPE + LANES
_C_RQ = _C_KPE_ROT + LANES
_C_RQ_ROT = _C_RQ + RET_HEADS * RET_DK
_C_RK = _C_RQ_ROT + RET_HEADS * RET_DK
_C_RK_ROT = _C_RK + RET_HEADS * RET_DK
_C_RV = _C_RK_ROT + RET_HEADS * RET_DK
_C_RG = _C_RV + RET_HEADS * RET_DV
_IN_PERM = _C_RG + RET_HEADS * RET_DV


def _silu(v):
    return v / (1.0 + jnp.exp(-v))


def _mm(a, b):
    return jnp.dot(a, b, preferred_element_type=F32)


def _mm_nt(a, b):
    return lax.dot_general(a, b, (((1,), (1,)), ((), ())), preferred_element_type=F32)


def _params(*sem):
    return pltpu.CompilerParams(dimension_semantics=sem, vmem_limit_bytes=VMEM_LIMIT)


def _adaln_kernel(c_ref, w_ref, b_ref, o_ref):
    a = _silu(c_ref[...]).astype(BF16)
    o_ref[...] = _mm(a, w_ref[...].astype(BF16)) + b_ref[...]


def _adaln(c, w_ada, b_ada):
    bsz, d = c.shape
    n = w_ada.shape[1]
    tn = d
    return pl.pallas_call(
        _adaln_kernel,
        grid=(n // tn,),
        in_specs=[pl.BlockSpec((bsz, d), lambda j: (0, 0)),
                  pl.BlockSpec((d, tn), lambda j: (0, j)),
                  pl.BlockSpec((1, tn), lambda j: (0, j))],
        out_specs=pl.BlockSpec((bsz, tn), lambda j: (0, j)),
        out_shape=jax.ShapeDtypeStruct((bsz, n), F32),
        compiler_params=_params("arbitrary"),
        name="adaln",
    )(c, w_ada, b_ada.reshape(1, n))


def _rms(v, g):
    return v * lax.rsqrt(jnp.mean(v * v, axis=-1, keepdims=True) + NORM_EPS) * g


def _pre_kernel(x_ref, mod_ref, pos_ref, g1_ref, win_ref, qg_ref, wuq_ref, wuqr_ref, kvg_ref, wuk_ref,
                wuv_ref, invf_ref, q_ref, k_ref, v_ref, rq_ref, rk_ref, rv_ref, rg_ref):
    x = x_ref[0]
    sh1 = mod_ref[0, 0:1, :]
    sc1 = mod_ref[0, 1:2, :]
    h = _rms(x, g1_ref[...]) * (1.0 + sc1) + sh1
    proj = _mm(h.astype(BF16), win_ref[...])

    ang = pos_ref[0] * invf_ref[...]
    cs = jnp.cos(ang)
    sn = jnp.sin(ang)
    lane = lax.broadcasted_iota(jnp.int32, cs.shape, 1)
    hi = lane >= RET_DK
    c_mla = jnp.where(hi, cs, 1.0)
    s_mla = jnp.where(hi, sn, 0.0)
    c_ret = jnp.where(hi, pltpu.roll(cs, RET_DK, 1), cs)
    s_ret = jnp.where(hi, pltpu.roll(sn, RET_DK, 1), sn)

    cq = _rms(proj[:, _C_Q:_C_Q + Q_LORA], qg_ref[...]).astype(BF16)
    qa = _mm(cq, wuq_ref[...])
    qb = _mm(cq, wuqr_ref[...])
    scale = (MLA_NOPE + MLA_ROPE) ** -0.5 * LOG2_E
    for hd in range(MLA_HEADS):
        sl = slice(hd * LANES, (hd + 1) * LANES)
        q_ref[0, :, sl] = ((qa[:, sl] * c_mla + qb[:, sl] * s_mla) * scale).astype(BF16)

    ckv = _rms(proj[:, _C_KV:_C_KV + KV_LORA], kvg_ref[...]).astype(BF16)
    kn = _mm(ckv, wuk_ref[...])
    kpe = proj[:, _C_KPE:_C_KPE + LANES] * c_mla + proj[:, _C_KPE_ROT:_C_KPE_ROT + LANES] * s_mla
    for hd in range(MLA_HEADS):
        sl = slice(hd * LANES, (hd + 1) * LANES)
        k_ref[0, :, sl] = (kn[:, sl] + kpe).astype(BF16)
    vt = _mm(ckv, wuv_ref[...]).T
    tq = v_ref.shape[-1]
    for cb in range(v_ref.shape[1]):
        v_ref[0, cb] = vt[:, cb * tq:(cb + 1) * tq].astype(BF16)

    for i in range(RET_HEADS * RET_DK // LANES):
        o = i * LANES
        rq = proj[:, _C_RQ + o:_C_RQ + o + LANES] * c_ret + proj[:, _C_RQ_ROT + o:_C_RQ_ROT + o + LANES] * s_ret
        rk = proj[:, _C_RK + o:_C_RK + o + LANES] * c_ret + proj[:, _C_RK_ROT + o:_C_RK_ROT + o + LANES] * s_ret
        rq_ref[0, :, o:o + LANES] = rq.astype(BF16)
        rk_ref[0, :, o:o + LANES] = (rk * (RET_DK ** -0.5)).astype(BF16)
    rv_ref[0] = proj[:, _C_RV:_C_RV + RET_HEADS * RET_DV].astype(BF16)
    rg_ref[0] = proj[:, _C_RG:_C_RG + RET_HEADS * RET_DV]


def _rot_cols(w, half):
    k = w.shape[0]
    w4 = w.reshape(k, -1, 2, half)
    return jnp.stack([-w4[:, :, 1], w4[:, :, 0]], axis=2).reshape(k, -1)


def _pad_heads(w, width, left):
    k = w.shape[0]
    w3 = w.reshape(k, -1, width)
    w3 = jnp.pad(w3, ((0, 0), (0, 0), (left, LANES - left - width)))
    return w3.reshape(k, -1)


def _pre(x, mod3, pos3, norm1_g, w_in, q_norm_g, w_uq, kv_norm_g, w_ukv, tm, tq):
    bsz, s, d = x.shape
    o = 0
    parts = {}
    for name, width in (("cq", Q_LORA), ("ckv", KV_LORA), ("kr", MLA_ROPE), ("rq", RET_HEADS * RET_DK),
                        ("rk", RET_HEADS * RET_DK), ("rv", RET_HEADS * RET_DV), ("rg", RET_HEADS * RET_DV)):
        parts[name] = w_in[:, o:o + width]
        o += width
    w_in_p = jnp.concatenate([
        parts["cq"], parts["ckv"],
        _pad_heads(parts["kr"], MLA_ROPE, MLA_NOPE), _pad_heads(_rot_cols(parts["kr"], MLA_ROPE // 2), MLA_ROPE, MLA_NOPE),
        parts["rq"], _rot_cols(parts["rq"], RET_DK // 2),
        parts["rk"], _rot_cols(parts["rk"], RET_DK // 2),
        parts["rv"], parts["rg"]], axis=1).astype(BF16)
    assert w_in_p.shape[1] == _IN_PERM
    wq3 = w_uq.reshape(Q_LORA, MLA_HEADS, MLA_NOPE + MLA_ROPE)
    wq_rope = wq3[:, :, MLA_NOPE:].reshape(Q_LORA, -1)
    w_uq_p = _pad_heads(wq3.reshape(Q_LORA, -1), MLA_NOPE + MLA_ROPE, 0).astype(BF16)
    w_uq_r = _pad_heads(_rot_cols(wq_rope, MLA_ROPE // 2), MLA_ROPE, MLA_NOPE).astype(BF16)
    wkv3 = w_ukv.reshape(KV_LORA, MLA_HEADS, MLA_NOPE + MLA_V)
    w_uk_p = _pad_heads(wkv3[:, :, :MLA_NOPE].reshape(KV_LORA, -1), MLA_NOPE, 0).astype(BF16)
    w_uv = wkv3[:, :, MLA_NOPE:].reshape(KV_LORA, -1).astype(BF16)
    half_r, half_m = RET_DK // 2, MLA_ROPE // 2
    f_r = ROPE_BASE ** (-(jnp.arange(half_r, dtype=F32) / half_r))
    f_m = ROPE_BASE ** (-(jnp.arange(half_m, dtype=F32) / half_m))
    invf = jnp.concatenate([f_r, f_r, f_m, f_m, jnp.zeros((LANES - 2 * half_r - 2 * half_m,), F32)]).reshape(1, LANES)

    hq = MLA_HEADS * LANES
    const = lambda shape: pl.BlockSpec(shape, lambda b, i: (0,) * len(shape))
    tile = lambda w: pl.BlockSpec((1, tm, w), lambda b, i: (b, i, 0))
    return pl.pallas_call(
        _pre_kernel,
        grid=(bsz, s // tm),
        in_specs=[tile(d), pl.BlockSpec((1, 6, d), lambda b, i: (b, 0, 0)), tile(1), const((1, d)),
                  const((d, _IN_PERM)), const((1, Q_LORA)), const((Q_LORA, hq)), const((Q_LORA, hq)),
                  const((1, KV_LORA)), const((KV_LORA, hq)), const((KV_LORA, MLA_HEADS * MLA_V)), const((1, LANES))],
        out_specs=[tile(hq), tile(hq),
                   pl.BlockSpec((1, tm // tq, MLA_HEADS * MLA_V, tq), lambda b, i: (b, i, 0, 0)),
                   tile(RET_HEADS * RET_DK), tile(RET_HEADS * RET_DK),
                   tile(RET_HEADS * RET_DV), tile(RET_HEADS * RET_DV)],
        out_shape=[jax.ShapeDtypeStruct((bsz, s, hq), BF16), jax.ShapeDtypeStruct((bsz, s, hq), BF16),
                   jax.ShapeDtypeStruct((bsz, s // tq, MLA_HEADS * MLA_V, tq), BF16),
                   jax.ShapeDtypeStruct((bsz, s, RET_HEADS * RET_DK), BF16),
                   jax.ShapeDtypeStruct((bsz, s, RET_HEADS * RET_DK), BF16),
                   jax.ShapeDtypeStruct((bsz, s, RET_HEADS * RET_DV), BF16),
                   jax.ShapeDtypeStruct((bsz, s, RET_HEADS * RET_DV), F32)],
        compiler_params=_params("parallel", "arbitrary"),
        name="pre_mixer",
    )(x, mod3, pos3, norm1_g.reshape(1, d), w_in_p, q_norm_g.reshape(1, -1), w_uq_p, w_uq_r,
      kv_norm_g.reshape(1, -1), w_uk_p, w_uv, invf)


def _attn_kernel(q_ref, k_ref, vt_ref, o_ref, *, tq, hps):
    qi = pl.program_id(1)
    key = lax.broadcasted_iota(jnp.int32, (tq, tq), 0)
    qry = lax.broadcasted_iota(jnp.int32, (tq, tq), 1)
    hsl = [slice(hh * LANES, (hh + 1) * LANES) for hh in range(hps)]
    vsl = [slice(hh * MLA_V, (hh + 1) * MLA_V) for hh in range(hps)]
    qs = [q_ref[0, :, hs] for hs in hsl]

    def heads(carry, blk, nblk, masked=False):
        start = pl.multiple_of(blk * tq, tq)
        sts = [_mm_nt(k_ref[0, pl.ds(start, nblk * tq), hsl[hh]], qs[hh]) for hh in range(hps)]
        ps, stats = [], []
        for hh in range(hps):
            m, l, acc = carry[hh]
            st = jnp.where(key <= qry, sts[hh], NEG) if masked else sts[hh]
            m_new = jnp.maximum(m, jnp.max(st, axis=0, keepdims=True))
            p = jnp.exp2(st - m_new)
            alpha = jnp.exp2(m - m_new)
            stats.append((m_new, alpha * l + jnp.sum(p, axis=0, keepdims=True), alpha * acc))
            ps.append(p.astype(BF16))
        out = []
        for hh in range(hps):
            m_new, l, acc = stats[hh]
            for c in range(nblk):
                acc = acc + _mm(vt_ref[0, blk + c, vsl[hh], :], ps[hh][c * tq:(c + 1) * tq, :])
            out.append((m_new, l, acc))
        return tuple(out)

    init = (jnp.full((1, tq), NEG, F32), jnp.zeros((1, tq), F32), jnp.zeros((MLA_V, tq), F32))
    carry = (init,) * hps
    carry = lax.fori_loop(0, qi // 2, lambda j, c: heads(c, 2 * j, 2), carry)
    carry = lax.fori_loop(0, qi % 2, lambda j, c: heads(c, qi - 1, 1), carry)
    carry = heads(carry, qi, 1, masked=True)
    out_t = jnp.concatenate([acc / l for _, l, acc in carry], axis=0)
    o_ref[0] = out_t.T.astype(BF16)


def _attention(q, k, vt, tq, hps):
    bsz, s, _ = q.shape
    groups = MLA_HEADS // hps
    assert vt.shape == (bsz, s // tq, MLA_HEADS * MLA_V, tq)
    return pl.pallas_call(
        functools.partial(_attn_kernel, tq=tq, hps=hps),
        grid=(bsz * groups, s // tq),
        in_specs=[pl.BlockSpec((1, tq, hps * LANES), lambda g, i: (g // groups, i, g % groups)),
                  pl.BlockSpec((1, s, hps * LANES), lambda g, i: (g // groups, 0, g % groups)),
                  pl.BlockSpec((1, s // tq, hps * MLA_V, tq), lambda g, i: (g // groups, 0, g % groups, 0))],
        out_specs=pl.BlockSpec((1, tq, hps * MLA_V), lambda g, i: (g // groups, i, g % groups)),
        out_shape=jax.ShapeDtypeStruct((bsz, s, MLA_HEADS * MLA_V), BF16),
        compiler_params=_params("parallel", "arbitrary"),
        name="mla_attention",
    )(q, k, vt)


def _ret_kernel(rq_ref, rk_ref, rv_ref, rg_ref, dm_ref, xi_ref, zt_ref, dc_ref, o_ref, st_ref, *, ts):
    @pl.when(pl.program_id(1) == 0)
    def _():
        st_ref[...] = jnp.zeros_like(st_ref)

    lane = lax.broadcasted_iota(jnp.int32, (RET_CHUNK, LANES), 1)

    def chunk(c, carry):
        rows = pl.ds(pl.multiple_of(c * RET_CHUNK, RET_CHUNK), RET_CHUNK)
        for hd in range(RET_HEADS):
            pair, sub = divmod(hd, LANES // RET_DK)
            mine = (lane >= sub * RET_DK) & (lane < (sub + 1) * RET_DK)
            ps = slice(pair * LANES, (pair + 1) * LANES)
            vs = slice(hd * RET_DV, (hd + 1) * RET_DV)
            qh = jnp.where(mine, rq_ref[0, rows, ps], 0.0).astype(BF16)
            kh = jnp.where(mine, rk_ref[0, rows, ps], 0.0).astype(BF16)
            vh = rv_ref[0, rows, vs]
            st = st_ref[hd]
            sc = _mm_nt(qh, kh) * dm_ref[hd]
            o = _mm(sc.astype(BF16), vh)
            o = o + _mm((qh.astype(F32) * xi_ref[hd]).astype(BF16), st.astype(BF16))
            kz = (kh.astype(F32) * zt_ref[hd]).astype(BF16)
            st_ref[hd] = st * dc_ref[hd] + _mm(kz.T, vh)
            mu = jnp.mean(o, axis=-1, keepdims=True)
            oc = o - mu
            on = oc * lax.rsqrt(jnp.mean(oc * oc, axis=-1, keepdims=True) + NORM_EPS)
            o_ref[0, rows, vs] = (_silu(rg_ref[0, rows, vs]) * on).astype(BF16)
        return carry

    lax.fori_loop(0, ts // RET_CHUNK, chunk, 0)


def _retention(rq, rk, rv, rg, ts):
    bsz, s, _ = rq.shape
    c = RET_CHUNK
    gamma = 1.0 - jnp.power(2.0, -5.0 - jnp.arange(RET_HEADS, dtype=F32))
    log_g = jnp.log(gamma)
    idx = jnp.arange(c, dtype=F32)
    diff = idx[:, None] - idx[None, :]
    dmask = jnp.where(diff[None] >= 0, jnp.exp(jnp.maximum(diff, 0.0)[None] * log_g[:, None, None]), 0.0)
    zeta = jnp.exp((c - 1.0 - idx)[None, :] * log_g[:, None])
    xi = jnp.exp((idx + 1.0)[None, :] * log_g[:, None])
    decay = jnp.exp(c * log_g)
    xi_b = jnp.broadcast_to(xi[:, :, None], (RET_HEADS, c, LANES))
    zt_b = jnp.broadcast_to(zeta[:, :, None], (RET_HEADS, c, LANES))
    dc_b = jnp.broadcast_to(decay[:, None, None], (RET_HEADS, LANES, RET_DV))

    tile = lambda w: pl.BlockSpec((1, ts, w), lambda b, i: (b, i, 0))
    const = lambda shape: pl.BlockSpec(shape, lambda b, i: (0,) * len(shape))
    return pl.pallas_call(
        functools.partial(_ret_kernel, ts=ts),
        grid=(bsz, s // ts),
        in_specs=[tile(RET_HEADS * RET_DK), tile(RET_HEADS * RET_DK), tile(RET_HEADS * RET_DV), tile(RET_HEADS * RET_DV),
                  const((RET_HEADS, c, c)), const((RET_HEADS, c, LANES)), const((RET_HEADS, c, LANES)),
                  const((RET_HEADS, LANES, RET_DV))],
        out_specs=tile(RET_HEADS * RET_DV),
        out_shape=jax.ShapeDtypeStruct((bsz, s, RET_HEADS * RET_DV), BF16),
        scratch_shapes=[pltpu.VMEM((RET_HEADS, LANES, RET_DV), F32)],
        compiler_params=_params("parallel", "arbitrary"),
        name="retention",
    )(rq, rk, rv, rg, dmask, xi_b, zt_b, dc_b)


def _post_kernel(x_ref, om_ref, or_ref, mod_ref, wo_ref, g2_ref, wrh_ref, wrl_ref, br_ref,
                 x1_ref, h2_ref, ri_ref, rit_ref, cnt_ref):
    half = om_ref.shape[-1]
    mix = _mm(om_ref[0], wo_ref[0:half, :]) + _mm(or_ref[0], wo_ref[half:, :])
    x1 = x_ref[0] + mod_ref[0, 2:3, :] * mix
    x1_ref[0] = x1
    h2 = _rms(x1, g2_ref[...]) * (1.0 + mod_ref[0, 4:5, :]) + mod_ref[0, 3:4, :]

    hi = h2.astype(BF16)
    h2_ref[0] = hi
    lo = (h2 - hi.astype(F32)).astype(BF16)
    lg = _mm(hi, wrh_ref[...]) + _mm(lo, wrh_ref[...]) + _mm(hi, wrl_ref[...]) + br_ref[...]
    lane = lax.broadcasted_iota(jnp.int32, lg.shape, 1)
    big = jnp.int32(1 << 20)

    gmask = lane < N_GROUPS
    gmax = jnp.max(jnp.where(gmask, lg, NEG), axis=-1, keepdims=True)
    ge = jnp.where(gmask, jnp.exp(lg - gmax), 0.0)
    pg = ge / jnp.sum(ge, axis=-1, keepdims=True)
    p_top = jnp.max(pg, axis=-1, keepdims=True)
    g_top = jnp.min(jnp.where(gmask & (pg == p_top), lane, big), axis=-1, keepdims=True)

    el = lane - N_GROUPS
    assert EXPERTS_PER_GROUP == 8
    emask = (el >= 0) & (el < N_EXPERTS) & (lax.shift_right_arithmetic(el, 3) == g_top)
    ev = jnp.where(emask, lg, NEG)
    v1 = jnp.max(ev, axis=-1, keepdims=True)
    i1 = jnp.min(jnp.where(emask & (ev == v1), lane, big), axis=-1, keepdims=True)
    emask2 = emask & (lane != i1)
    ev2 = jnp.where(emask2, lg, NEG)
    v2 = jnp.max(ev2, axis=-1, keepdims=True)
    i2 = jnp.min(jnp.where(emask2 & (ev2 == v2), lane, big), axis=-1, keepdims=True)
    e = jnp.exp(v2 - v1)
    den = 1.0 + e
    w1 = (1.0 / den) * p_top
    w2 = (e / den) * p_top

    tm = lg.shape[0]
    r_io = lax.broadcasted_iota(jnp.int32, (MOE_TILE, MOE_TILE), 0)
    c_io = lax.broadcasted_iota(jnp.int32, (MOE_TILE, MOE_TILE), 1)
    earlier_tok = (c_io < r_io).astype(BF16)
    lr_io = lax.broadcasted_iota(jnp.int32, (LANES, LANES), 0)
    lc_io = lax.broadcasted_iota(jnp.int32, (LANES, LANES), 1)
    earlier_lane = (lr_io < lc_io).astype(BF16)
    oh = [lane == i1, lane == i2]
    cnt_all = (oh[0] | oh[1]).astype(BF16)
    pos = []
    for hf in range(tm // MOE_TILE):
        cnt = cnt_all[hf * MOE_TILE:(hf + 1) * MOE_TILE, :]
        excl = _mm(earlier_tok, cnt)
        n = jnp.sum(cnt.astype(F32), axis=0, keepdims=True)
        npad = jnp.floor((n + (MOE_CHUNK - 1)) * (1.0 / MOE_CHUNK)) * MOE_CHUNK
        loff = _mm(jnp.broadcast_to(npad, (8, LANES)).astype(BF16), earlier_lane)
        pos.append(excl + loff[0:1, :])
        cnt_ref[hf] = jnp.broadcast_to(n, (8, LANES))
    pos = jnp.concatenate(pos, axis=0)
    lp0, lp1 = [jnp.sum(jnp.where(o, pos, 0.0), axis=-1, keepdims=True) for o in oh]
    cols = [(i1 - N_GROUPS).astype(F32), (i2 - N_GROUPS).astype(F32), w1, w2, lp0, lp1]
    ri = jnp.zeros(lg.shape, F32)
    for j, col in enumerate(cols):
        ri = jnp.where(lane == j, col, ri)
    ri_ref[0] = ri
    rit_ref[...] = ri.T


def _post(x, o_mla, o_ret, mod3, w_o, norm2_g, w_gr, b_gr, w_er, b_er, tm):
    bsz, s, d = x.shape
    w_r = jnp.concatenate([w_gr, w_er.reshape(d, N_EXPERTS), jnp.zeros((d, LANES - N_GROUPS - N_EXPERTS), F32)], axis=1)
    w_rh = w_r.astype(BF16)
    w_rl = (w_r - w_rh.astype(F32)).astype(BF16)
    b_r = jnp.concatenate([b_gr, b_er.reshape(-1), jnp.zeros((LANES - N_GROUPS - N_EXPERTS,), F32)]).reshape(1, LANES)
    tile = lambda w: pl.BlockSpec((1, tm, w), lambda b, i: (b, i, 0))
    const = lambda shape: pl.BlockSpec(shape, lambda b, i: (0,) * len(shape))
    per_b = s // tm
    sub = tm // MOE_TILE
    return pl.pallas_call(
        _post_kernel,
        grid=(bsz, per_b),
        in_specs=[tile(d), tile(o_mla.shape[-1]), tile(o_ret.shape[-1]), pl.BlockSpec((1, 6, d), lambda b, i: (b, 0, 0)),
                  const((d, d)), const((1, d)), const((d, LANES)), const((d, LANES)), const((1, LANES))],
        out_specs=[tile(d), tile(d), tile(LANES),
                   pl.BlockSpec((LANES, tm), lambda b, i: (0, b * per_b + i)),
                   pl.BlockSpec((sub, 8, LANES), lambda b, i: (b * per_b + i, 0, 0))],
        out_shape=[jax.ShapeDtypeStruct((bsz, s, d), F32), jax.ShapeDtypeStruct((bsz, s, d), BF16),
                   jax.ShapeDtypeStruct((bsz, s, LANES), F32),
                   jax.ShapeDtypeStruct((LANES, bsz * s), F32),
                   jax.ShapeDtypeStruct((bsz * s // MOE_TILE, 8, LANES), F32)],
        compiler_params=_params("parallel", "arbitrary"),
        name="post_mixer",
    )(x, o_mla, o_ret, mod3, w_o.astype(BF16), norm2_g.reshape(1, d), w_rh, w_rl, b_r)


def _chunk_rows(c):
    return pl.ds(pl.multiple_of(c * MOE_CHUNK, MOE_CHUNK), MOE_CHUNK)


def _dispatch_kernel(dmap_ref, nchk_ref, tstart_ref, tn_ref, nbr_ref, h2_ref, ri_ref, rit_ref, xs_hbm,
                     xloc, zblk, sem, zsem, bsem, *, nt, nb):
    i = pl.program_id(0)
    slot = lax.rem(i, 2)
    d = h2_ref.shape[-1]

    def zero_copy(e, c):
        return pltpu.make_async_copy(zblk.at[pl.ds(0, MOE_CHUNK)], xs_hbm.at[_chunk_rows(tstart_ref[e] + c)], zsem)

    def zero_block(j):
        rows = pl.ds(pl.multiple_of(j * MOE_BLOCK, MOE_BLOCK), MOE_BLOCK)
        return pltpu.make_async_copy(zblk, xs_hbm.at[rows], bsem)

    def chunk_copy(t, c, sl):
        return pltpu.make_async_copy(xloc.at[sl].at[_chunk_rows(c)],
                                     xs_hbm.at[_chunk_rows(dmap_ref[t * MOE_TILE_CHUNKS + c])], sem.at[sl])

    def wait_tile(t, sl):
        lax.fori_loop(0, nchk_ref[t], lambda c, z: (chunk_copy(t, c, sl).wait(), z)[1], 0)

    @pl.when(i == 0)
    def _():
        zblk[...] = jnp.zeros_like(zblk)
        lax.fori_loop(nbr_ref[0], nb, lambda j, z: (zero_block(j).start(), z)[1], 0)
        for e in range(N_EXPERTS):
            lax.fori_loop(0, tn_ref[e], lambda c, z, e=e: (zero_copy(e, c).start(), z)[1], 0)
        for e in range(N_EXPERTS):
            lax.fori_loop(0, tn_ref[e], lambda c, z, e=e: (zero_copy(e, c).wait(), z)[1], 0)

    @pl.when(i >= 2)
    def _():
        wait_tile(i - 2, slot)

    s_io = lax.broadcasted_iota(jnp.int32, (MOE_SLAB, MOE_TILE), 0).astype(F32)
    pm = [(s_io == rit_ref[4 + k:5 + k, :]).astype(BF16) for k in range(TOP_K)]
    xloc[slot, :, 0:d] = _mm(pm[0] + pm[1], h2_ref[...]).astype(BF16)
    lane = lax.broadcasted_iota(jnp.int32, (MOE_TILE, LANES), 1)
    wx = jnp.zeros((MOE_SLAB, LANES), F32)
    for k in range(TOP_K):
        w = ri_ref[:, TOP_K + k:TOP_K + k + 1]
        hi = w.astype(BF16).astype(F32)
        wx = wx + _mm(pm[k], jnp.where(lane == 0, hi, jnp.where(lane == 1, w - hi, 0.0)).astype(BF16))
    xloc[slot, :, d:d + LANES] = wx.astype(BF16)

    lax.fori_loop(0, nchk_ref[i], lambda c, z: (chunk_copy(i, c, slot).start(), z)[1], 0)

    @pl.when(i == nt - 1)
    def _():
        wait_tile(i, slot)
        if nt >= 2:
            wait_tile(i - 1, 1 - slot)
        lax.fori_loop(nbr_ref[0], nb, lambda j, z: (zero_block(j).wait(), z)[1], 0)


def _dispatch(h2, ri, rit, tables, n_rows):
    t, d = h2.shape
    nt = t // MOE_TILE
    assert MOE_SLAB >= TOP_K * MOE_TILE + N_EXPERTS * (MOE_CHUNK - 1)
    grid_spec = pltpu.PrefetchScalarGridSpec(
        num_scalar_prefetch=5,
        grid=(nt,),
        in_specs=[pl.BlockSpec((MOE_TILE, d), lambda i, *_: (i, 0)),
                  pl.BlockSpec((MOE_TILE, LANES), lambda i, *_: (i, 0)),
                  pl.BlockSpec((8, MOE_TILE), lambda i, *_: (0, i))],
        out_specs=pl.BlockSpec(memory_space=pl.ANY),
        scratch_shapes=[pltpu.VMEM((2, MOE_SLAB, d + LANES), BF16), pltpu.VMEM((MOE_BLOCK, d + LANES), BF16),
                        pltpu.SemaphoreType.DMA((2,)), pltpu.SemaphoreType.DMA(()), pltpu.SemaphoreType.DMA(())])
    return pl.pallas_call(
        functools.partial(_dispatch_kernel, nt=nt, nb=n_rows // MOE_BLOCK),
        grid_spec=grid_spec,
        out_shape=jax.ShapeDtypeStruct((n_rows, d + LANES), BF16),
        compiler_params=_params("arbitrary"),
        name="moe_dispatch",
    )(tables["dmap"], tables["tile_chunks"], tables["tail_start"], tables["tail_n"], tables["n_blocks"],
      h2, ri, rit)


def _expert_kernel(be_ref, nbr_ref, xs_ref, w1_ref, w3_ref, w2_ref, y_ref):
    used = pl.program_id(0) < nbr_ref[0]

    @pl.when(jnp.logical_not(used))
    def _():
        y_ref[...] = jnp.zeros_like(y_ref)

    @pl.when(used)
    def _():
        d = y_ref.shape[-1]
        xb = xs_ref[:, 0:d]
        gw = xs_ref[:, d:d + LANES].astype(F32)
        gate = gw[:, 0:1] + gw[:, 1:2]
        a = _mm(xb, w1_ref[0])
        b = _mm(xb, w3_ref[0])
        hm = (_silu(a) * b).astype(BF16)
        y_ref[...] = (_mm(hm, w2_ref[0]) * gate).astype(BF16)


def _experts(xs, tables, w1, w3, w2):
    n_rows, dw = xs.shape
    d = dw - LANES
    nb = n_rows // MOE_BLOCK
    blk = lambda i, be, nbr: jnp.minimum(i, nbr[0] - 1)
    grid_spec = pltpu.PrefetchScalarGridSpec(
        num_scalar_prefetch=2,
        grid=(nb,),
        in_specs=[pl.BlockSpec((MOE_BLOCK, dw), lambda i, be, nbr: (blk(i, be, nbr), 0)),
                  pl.BlockSpec((1, d, D_EXPERT), lambda i, be, nbr: (be[blk(i, be, nbr)], 0, 0)),
                  pl.BlockSpec((1, d, D_EXPERT), lambda i, be, nbr: (be[blk(i, be, nbr)], 0, 0)),
                  pl.BlockSpec((1, D_EXPERT, d), lambda i, be, nbr: (be[blk(i, be, nbr)], 0, 0))],
        out_specs=pl.BlockSpec((MOE_BLOCK, d), lambda i, be, nbr: (i, 0)))
    return pl.pallas_call(
        _expert_kernel,
        grid_spec=grid_spec,
        out_shape=jax.ShapeDtypeStruct((n_rows, d), BF16),
        compiler_params=_params("arbitrary"),
        name="moe_experts",
    )(tables["blk_expert"], tables["n_blocks"], xs, w1.astype(BF16), w3.astype(BF16), w2.astype(BF16))


def _moe_rows(n_tiles):
    worst = n_tiles * (TOP_K * MOE_TILE + N_EXPERTS * (MOE_CHUNK - 1)) + N_EXPERTS * (MOE_BLOCK - MOE_CHUNK)
    return -(-worst // MOE_BLOCK) * MOE_BLOCK


def _moe_tables(cnt, n_rows):
    n_tiles = cnt.shape[0]
    per_blk = MOE_BLOCK // MOE_CHUNK
    nch = (cnt + MOE_CHUNK - 1) // MOE_CHUNK
    loff = jnp.cumsum(nch, axis=1) - nch
    seg = jnp.sum(nch, axis=0)
    blocks = (seg + per_blk - 1) // per_blk
    bend = jnp.cumsum(blocks)
    estart = (bend - blocks) * per_blk
    gbase = estart[None, :] + jnp.cumsum(nch, axis=0) - nch
    c = jnp.arange(MOE_TILE_CHUNKS, dtype=jnp.int32)
    owner = jnp.sum((c[None, :, None] >= (loff + nch)[:, None, :]).astype(jnp.int32), axis=-1)
    owner = jnp.minimum(owner, N_EXPERTS - 1)
    dmap = jnp.take_along_axis(gbase, owner, axis=1) + c[None, :] - jnp.take_along_axis(loff, owner, axis=1)
    dmap = jnp.clip(dmap, 0, n_rows // MOE_CHUNK - 1)
    nb = n_rows // MOE_BLOCK
    blk_expert = jnp.minimum(jnp.searchsorted(bend, jnp.arange(nb, dtype=jnp.int32), side="right"), N_EXPERTS - 1)
    return dict(dmap=dmap.reshape(-1).astype(jnp.int32), tile_chunks=jnp.sum(nch, axis=1).astype(jnp.int32),
                tail_start=(estart + seg).astype(jnp.int32), tail_n=(blocks * per_blk - seg).astype(jnp.int32),
                blk_expert=blk_expert.astype(jnp.int32), n_blocks=bend[-1:].astype(jnp.int32))


def _final_kernel(dmap_ref, nchk_ref, x1_ref, ri_ref, mod_ref, g_ref, y_hbm, o_ref, yloc, sem, *, nt):
    i = pl.program_id(0)
    slot = lax.rem(i, 2)

    def chunk_copy(t, c, sl):
        return pltpu.make_async_copy(y_hbm.at[_chunk_rows(dmap_ref[t * MOE_TILE_CHUNKS + c])],
                                     yloc.at[sl].at[_chunk_rows(c)], sem.at[sl])

    def gather(t, sl):
        lax.fori_loop(0, nchk_ref[t], lambda c, z: (chunk_copy(t, c, sl).start(), z)[1], 0)

    @pl.when(i == 0)
    def _():
        yloc[...] = jnp.zeros_like(yloc)
        gather(0, 0)

    @pl.when(i + 1 < nt)
    def _():
        gather(i + 1, 1 - slot)

    lax.fori_loop(0, nchk_ref[i], lambda c, z: (chunk_copy(i, c, slot).wait(), z)[1], 0)

    l_io = lax.broadcasted_iota(jnp.int32, (MOE_TILE, MOE_SLAB), 1).astype(F32)
    pick = ((l_io == ri_ref[:, 4:5]) | (l_io == ri_ref[:, 5:6])).astype(BF16)
    moe = _mm(pick, yloc[slot])
    x2 = x1_ref[...] + mod_ref[0, 5:6, :] * moe
    o_ref[...] = _rms(x2, g_ref[...])


def _final(x1, ri, y, mod3, final_g, tables, s):
    t, d = x1.shape
    nt = t // MOE_TILE
    per_b = s // MOE_TILE
    grid_spec = pltpu.PrefetchScalarGridSpec(
        num_scalar_prefetch=2,
        grid=(nt,),
        in_specs=[pl.BlockSpec((MOE_TILE, d), lambda i, *_: (i, 0)),
                  pl.BlockSpec((MOE_TILE, LANES), lambda i, *_: (i, 0)),
                  pl.BlockSpec((1, 6, d), lambda i, *_: (i // per_b, 0, 0)),
                  pl.BlockSpec((1, d), lambda i, *_: (0, 0)),
                  pl.BlockSpec(memory_space=pl.ANY)],
        out_specs=pl.BlockSpec((MOE_TILE, d), lambda i, *_: (i, 0)),
        scratch_shapes=[pltpu.VMEM((2, MOE_SLAB, d), BF16), pltpu.SemaphoreType.DMA((2,))])
    return pl.pallas_call(
        functools.partial(_final_kernel, nt=nt),
        grid_spec=grid_spec,
        out_shape=jax.ShapeDtypeStruct((t, d), F32),
        compiler_params=_params("arbitrary"),
        name="moe_combine_final",
    )(tables["dmap"], tables["tile_chunks"], x1, ri, mod3, final_g.reshape(1, d), y)


def kernel(x, c, positions, w_ada, b_ada, norm1_g, w_in, q_norm_g, w_uq, kv_norm_g, w_ukv, w_o, norm2_g,
           w_gr, b_gr, w_er, b_er, w1, w3, w2, final_g):
    bsz, s, d = x.shape
    assert w_ada.shape[0] == 1, "one layer"
    tm = min(512, s)
    tq = min(256, s)
    ts = min(1024, s)
    mod3 = _adaln(c, w_ada[0], b_ada[0]).reshape(bsz, 6, d)
    pos3 = positions.astype(F32).reshape(bsz, s, 1)
    q, k, vt, rq, rk, rv, rg = _pre(x, mod3, pos3, norm1_g[0], w_in[0], q_norm_g[0], w_uq[0], kv_norm_g[0], w_ukv[0],
                                    tm, tq)
    o_mla = _attention(q, k, vt, tq, ATTN_HEADS_PER_STEP)
    o_ret = _retention(rq, rk, rv, rg, ts)
    x1, h2, ri, rit, cnt = _post(x, o_mla, o_ret, mod3, w_o[0], norm2_g[0], w_gr[0], b_gr[0], w_er[0], b_er[0], tm)
    t = bsz * s
    n_rows = _moe_rows(t // MOE_TILE)
    counts = cnt[:, 0, N_GROUPS:N_GROUPS + N_EXPERTS].astype(jnp.int32)
    tables = _moe_tables(counts, n_rows)
    ri2 = ri.reshape(t, LANES)
    xs = _dispatch(h2.reshape(t, d), ri2, rit, tables, n_rows)
    y = _experts(xs, tables, w1[0], w3[0], w2[0])
    out = _final(x1.reshape(t, d), ri2, y, mod3, final_g, tables, s)
    return out.reshape(bsz, s, d)
```

```python
import functools

import jax
import jax.numpy as jnp
from jax import lax
from jax.experimental import pallas as pl
from jax.experimental.pallas import tpu as pltpu

MLA_HEADS = 8
MLA_NOPE = 64
MLA_ROPE = 32
MLA_V = 64
Q_LORA = 256
KV_LORA = 128
RET_HEADS = 4
RET_DK = 64
RET_DV = 128
RET_CHUNK = 128
ROPE_BASE = 10000.0
NORM_EPS = 1e-6
N_GROUPS = 4
EXPERTS_PER_GROUP = 8
N_EXPERTS = N_GROUPS * EXPERTS_PER_GROUP
TOP_K = 2
D_EXPERT = 256
MOE_TILE = 256
MOE_CHUNK = 16
MOE_SLAB = 1024
MOE_TILE_CHUNKS = MOE_SLAB // MOE_CHUNK
MOE_BLOCK = 512

LANES = 128
VMEM_LIMIT = 56 * 1024 * 1024

F32 = jnp.float32
BF16 = jnp.bfloat16
NEG = float(jnp.finfo(jnp.float32).min)
LOG2_E = 1.4426950408889634
ATTN_HEADS_PER_STEP = 8

_C_Q = 0
_C_KV = _C_Q + Q_LORA
_C_KPE = _C_KV + KV_LORA
_C_RQ = _C_KPE + LANES
_C_RK = _C_RQ + RET_HEADS * RET_DK
_C_RV = _C_RK + RET_HEADS * RET_DK
_C_RG = _C_RV + RET_HEADS * RET_DV
_IN_PERM = _C_RG + RET_HEADS * RET_DV


def _silu(v):
    return v / (1.0 + jnp.exp(-v))


def _mm(a, b):
    return jnp.dot(a, b, preferred_element_type=F32)


def _mm_nt(a, b):
    return lax.dot_general(a, b, (((1,), (1,)), ((), ())), preferred_element_type=F32)


def _params(*sem):
    return pltpu.CompilerParams(dimension_semantics=sem, vmem_limit_bytes=VMEM_LIMIT)


def _adaln_kernel(c_ref, w_ref, b_ref, o_ref):
    a = _silu(c_ref[...]).astype(BF16)
    o_ref[...] = _mm(a, w_ref[...].astype(BF16)) + b_ref[...]


def _adaln(c, w_ada, b_ada):
    bsz, d = c.shape
    n = w_ada.shape[1]
    tn = d
    return pl.pallas_call(
        _adaln_kernel,
        grid=(n // tn,),
        in_specs=[pl.BlockSpec((bsz, d), lambda j: (0, 0)),
                  pl.BlockSpec((d, tn), lambda j: (0, j)),
                  pl.BlockSpec((1, tn), lambda j: (0, j))],
        out_specs=pl.BlockSpec((bsz, tn), lambda j: (0, j)),
        out_shape=jax.ShapeDtypeStruct((bsz, n), F32),
        compiler_params=_params("arbitrary"),
        name="adaln",
    )(c, w_ada, b_ada.reshape(1, n))


def _rms(v, g):
    return v * lax.rsqrt(jnp.mean(v * v, axis=-1, keepdims=True) + NORM_EPS) * g


def _pre_kernel(x_ref, mod_ref, pos_ref, g1_ref, win_ref, qg_ref, wuq_ref, kvg_ref, wuk_ref,
                wuv_ref, invf_ref, q_ref, k_ref, v_ref, rq_ref, rk_ref, rv_ref, rg_ref):
    tq = v_ref.shape[-1]
    subs = [slice(i * tq, (i + 1) * tq) for i in range(v_ref.shape[1])]
    sh1 = mod_ref[0, 0:1, :]
    sc1 = mod_ref[0, 1:2, :]
    projs = [_mm((_rms(x_ref[0, rs, :], g1_ref[...]) * (1.0 + sc1) + sh1).astype(BF16), win_ref[...]) for rs in subs]

    cqs = [_rms(p[:, _C_Q:_C_Q + Q_LORA], qg_ref[...]).astype(BF16) for p in projs]
    ckvs = [_rms(p[:, _C_KV:_C_KV + KV_LORA], kvg_ref[...]).astype(BF16) for p in projs]
    qas = [_mm(cq, wuq_ref[...]) for cq in cqs]
    kns = [_mm(ckv, wuk_ref[...]) for ckv in ckvs]
    vvs = [_mm(ckv, wuv_ref[...]) for ckv in ckvs]

    lane = lax.broadcasted_iota(jnp.int32, (tq, LANES), 1)
    hi = lane >= RET_DK
    half_m, half_r = MLA_ROPE // 2, RET_DK // 2
    first_m = hi & (lane < RET_DK + half_m)
    first_r = (lane & half_r) == 0
    scale = (MLA_NOPE + MLA_ROPE) ** -0.5 * LOG2_E

    def rope(v, cos, sin, first, half):
        partner = jnp.where(first, pltpu.roll(v, LANES - half, 1), pltpu.roll(v, half, 1))
        return v * cos + partner * sin

    for i, rs in enumerate(subs):
        proj = projs[i]
        ang = pos_ref[0, rs, :] * invf_ref[...]
        cs = jnp.cos(ang)
        sn = jnp.sin(ang)
        c_mla = jnp.where(hi, cs, 1.0)
        s_mla = jnp.where(hi, jnp.where(first_m, -sn, sn), 0.0)
        c_ret = jnp.where(hi, pltpu.roll(cs, RET_DK, 1), cs)
        s_ret = jnp.where(hi, pltpu.roll(sn, RET_DK, 1), sn)
        s_ret = jnp.where(first_r, -s_ret, s_ret)

        kpe = rope(proj[:, _C_KPE:_C_KPE + LANES], c_mla, s_mla, first_m, half_m)
        for hd in range(MLA_HEADS):
            sl = slice(hd * LANES, (hd + 1) * LANES)
            q_ref[0, rs, sl] = (rope(qas[i][:, sl], c_mla, s_mla, first_m, half_m) * scale).astype(BF16)
            k_ref[0, rs, sl] = (kns[i][:, sl] + kpe).astype(BF16)
        v_ref[0, i] = vvs[i].T.astype(BF16)

        for j in range(RET_HEADS * RET_DK // LANES):
            o = j * LANES
            rq = rope(proj[:, _C_RQ + o:_C_RQ + o + LANES], c_ret, s_ret, first_r, half_r)
            rk = rope(proj[:, _C_RK + o:_C_RK + o + LANES], c_ret, s_ret, first_r, half_r)
            rq_ref[0, rs, o:o + LANES] = rq.astype(BF16)
            rk_ref[0, rs, o:o + LANES] = (rk * (RET_DK ** -0.5)).astype(BF16)
        rv_ref[0, rs, :] = proj[:, _C_RV:_C_RV + RET_HEADS * RET_DV].astype(BF16)
        rg_ref[0, rs, :] = proj[:, _C_RG:_C_RG + RET_HEADS * RET_DV]


def _pad_heads(w, width, left):
    k = w.shape[0]
    w3 = w.reshape(k, -1, width)
    w3 = jnp.pad(w3, ((0, 0), (0, 0), (left, LANES - left - width)))
    return w3.reshape(k, -1)


def _pre(x, mod3, pos3, norm1_g, w_in, q_norm_g, w_uq, kv_norm_g, w_ukv, tm, tq):
    bsz, s, d = x.shape
    o = 0
    parts = {}
    for name, width in (("cq", Q_LORA), ("ckv", KV_LORA), ("kr", MLA_ROPE), ("rq", RET_HEADS * RET_DK),
                        ("rk", RET_HEADS * RET_DK), ("rv", RET_HEADS * RET_DV), ("rg", RET_HEADS * RET_DV)):
        parts[name] = w_in[:, o:o + width]
        o += width
    w_in_p = jnp.concatenate([
        parts["cq"], parts["ckv"], _pad_heads(parts["kr"], MLA_ROPE, MLA_NOPE),
        parts["rq"], parts["rk"], parts["rv"], parts["rg"]], axis=1).astype(BF16)
    assert w_in_p.shape[1] == _IN_PERM
    w_uq_p = _pad_heads(w_uq, MLA_NOPE + MLA_ROPE, 0).astype(BF16)
    wkv3 = w_ukv.reshape(KV_LORA, MLA_HEADS, MLA_NOPE + MLA_V)
    w_uk_p = _pad_heads(wkv3[:, :, :MLA_NOPE].reshape(KV_LORA, -1), MLA_NOPE, 0).astype(BF16)
    w_uv = wkv3[:, :, MLA_NOPE:].reshape(KV_LORA, -1).astype(BF16)
    half_r, half_m = RET_DK // 2, MLA_ROPE // 2
    f_r = ROPE_BASE ** (-(jnp.arange(half_r, dtype=F32) / half_r))
    f_m = ROPE_BASE ** (-(jnp.arange(half_m, dtype=F32) / half_m))
    invf = jnp.concatenate([f_r, f_r, f_m, f_m, jnp.zeros((LANES - 2 * half_r - 2 * half_m,), F32)]).reshape(1, LANES)

    hq = MLA_HEADS * LANES
    const = lambda shape: pl.BlockSpec(shape, lambda b, i: (0,) * len(shape))
    tile = lambda w: pl.BlockSpec((1, tm, w), lambda b, i: (b, i, 0))
    return pl.pallas_call(
        _pre_kernel,
        grid=(bsz, s // tm),
        in_specs=[tile(d), pl.BlockSpec((1, 6, d), lambda b, i: (b, 0, 0)), tile(1), const((1, d)),
                  const((d, _IN_PERM)), const((1, Q_LORA)), const((Q_LORA, hq)),
                  const((1, KV_LORA)), const((KV_LORA, hq)), const((KV_LORA, MLA_HEADS * MLA_V)), const((1, LANES))],
        out_specs=[tile(hq), tile(hq),
                   pl.BlockSpec((1, tm // tq, MLA_HEADS * MLA_V, tq), lambda b, i: (b, i, 0, 0)),
                   tile(RET_HEADS * RET_DK), tile(RET_HEADS * RET_DK),
                   tile(RET_HEADS * RET_DV), tile(RET_HEADS * RET_DV)],
        out_shape=[jax.ShapeDtypeStruct((bsz, s, hq), BF16), jax.ShapeDtypeStruct((bsz, s, hq), BF16),
                   jax.ShapeDtypeStruct((bsz, s // tq, MLA_HEADS * MLA_V, tq), BF16),
                   jax.ShapeDtypeStruct((bsz, s, RET_HEADS * RET_DK), BF16),
                   jax.ShapeDtypeStruct((bsz, s, RET_HEADS * RET_DK), BF16),
                   jax.ShapeDtypeStruct((bsz, s, RET_HEADS * RET_DV), BF16),
                   jax.ShapeDtypeStruct((bsz, s, RET_HEADS * RET_DV), F32)],
        compiler_params=_params("parallel", "arbitrary"),
        name="pre_mixer",
    )(x, mod3, pos3, norm1_g.reshape(1, d), w_in_p, q_norm_g.reshape(1, -1), w_uq_p,
      kv_norm_g.reshape(1, -1), w_uk_p, w_uv, invf)


def _attn_kernel(q_ref, k_ref, vt_ref, o_ref, *, tq, hps):
    qi = pl.program_id(1)
    key = lax.broadcasted_iota(jnp.int32, (tq, tq), 0)
    qry = lax.broadcasted_iota(jnp.int32, (tq, tq), 1)
    hsl = [slice(hh * LANES, (hh + 1) * LANES) for hh in range(hps)]
    vsl = [slice(hh * MLA_V, (hh + 1) * MLA_V) for hh in range(hps)]
    qs = [q_ref[0, :, hs] for hs in hsl]

    def heads(carry, blk, nblk, masked=False):
        start = pl.multiple_of(blk * tq, tq)
        sts = [_mm_nt(k_ref[0, pl.ds(start, nblk * tq), hsl[hh]], qs[hh]) for hh in range(hps)]
        ps, stats = [], []
        for hh in range(hps):
            m, l, acc = carry[hh]
            st = jnp.where(key <= qry, sts[hh], NEG) if masked else sts[hh]
            m_new = jnp.maximum(m, jnp.max(st, axis=0, keepdims=True))
            p = jnp.exp2(st - m_new)
            alpha = jnp.exp2(m - m_new)
            stats.append((m_new, alpha * l + jnp.sum(p, axis=0, keepdims=True), alpha * acc))
            ps.append(p.astype(BF16))
        out = []
        for hh in range(hps):
            m_new, l, acc = stats[hh]
            for c in range(nblk):
                acc = acc + _mm(vt_ref[0, blk + c, vsl[hh], :], ps[hh][c * tq:(c + 1) * tq, :])
            out.append((m_new, l, acc))
        return tuple(out)

    init = (jnp.full((1, tq), NEG, F32), jnp.zeros((1, tq), F32), jnp.zeros((MLA_V, tq), F32))
    carry = (init,) * hps
    carry = lax.fori_loop(0, qi // 2, lambda j, c: heads(c, 2 * j, 2), carry)
    carry = lax.fori_loop(0, qi % 2, lambda j, c: heads(c, qi - 1, 1), carry)
    carry = heads(carry, qi, 1, masked=True)
    out_t = jnp.concatenate([acc / l for _, l, acc in carry], axis=0)
    o_ref[0] = out_t.T.astype(BF16)


def _attention(q, k, vt, tq, hps):
    bsz, s, _ = q.shape
    groups = MLA_HEADS // hps
    assert vt.shape == (bsz, s // tq, MLA_HEADS * MLA_V, tq)
    return pl.pallas_call(
        functools.partial(_attn_kernel, tq=tq, hps=hps),
        grid=(bsz * groups, s // tq),
        in_specs=[pl.BlockSpec((1, tq, hps * LANES), lambda g, i: (g // groups, i, g % groups)),
                  pl.BlockSpec((1, s, hps * LANES), lambda g, i: (g // groups, 0, g % groups)),
                  pl.BlockSpec((1, s // tq, hps * MLA_V, tq), lambda g, i: (g // groups, 0, g % groups, 0))],
        out_specs=pl.BlockSpec((1, tq, hps * MLA_V), lambda g, i: (g // groups, i, g % groups)),
        out_shape=jax.ShapeDtypeStruct((bsz, s, MLA_HEADS * MLA_V), BF16),
        compiler_params=_params("parallel", "arbitrary"),
        name="mla_attention",
    )(q, k, vt)


def _ret_kernel(rq_ref, rk_ref, rv_ref, rg_ref, dm_ref, xi_ref, zt_ref, dc_ref, o_ref, st_ref, *, ts):
    @pl.when(pl.program_id(1) == 0)
    def _():
        st_ref[...] = jnp.zeros_like(st_ref)

    lane = lax.broadcasted_iota(jnp.int32, (RET_CHUNK, LANES), 1)
    units = [(c, hd) for c in range(ts // RET_CHUNK) for hd in range(RET_HEADS)]
    rows = lambda c: slice(c * RET_CHUNK, (c + 1) * RET_CHUNK)
    vsl = lambda hd: slice(hd * RET_DV, (hd + 1) * RET_DV)

    def head_lanes(ref, c, hd):
        pair, sub = divmod(hd, LANES // RET_DK)
        mine = (lane >= sub * RET_DK) & (lane < (sub + 1) * RET_DK)
        return jnp.where(mine, ref[0, rows(c), pair * LANES:(pair + 1) * LANES], 0.0).astype(BF16)

    qh = {u: head_lanes(rq_ref, *u) for u in units}
    kh = {u: head_lanes(rk_ref, *u) for u in units}
    vh = {(c, hd): rv_ref[0, rows(c), vsl(hd)] for c, hd in units}
    sc = {u: _mm_nt(qh[u], kh[u]) for u in units}
    un = {u: _mm((kh[u].astype(F32) * zt_ref[u[1]]).astype(BF16).T, vh[u]) for u in units}
    prev = {}
    for hd in range(RET_HEADS):
        st = st_ref[hd]
        for c in range(ts // RET_CHUNK):
            prev[(c, hd)] = st.astype(BF16)
            st = st * dc_ref[hd] + un[(c, hd)]
        st_ref[hd] = st
    for c, hd in units:
        u = (c, hd)
        o = _mm((sc[u] * dm_ref[hd]).astype(BF16), vh[u])
        o = o + _mm((qh[u].astype(F32) * xi_ref[hd]).astype(BF16), prev[u])
        mu = jnp.mean(o, axis=-1, keepdims=True)
        oc = o - mu
        on = oc * lax.rsqrt(jnp.mean(oc * oc, axis=-1, keepdims=True) + NORM_EPS)
        o_ref[0, rows(c), vsl(hd)] = (_silu(rg_ref[0, rows(c), vsl(hd)]) * on).astype(BF16)


def _retention(rq, rk, rv, rg, ts):
    bsz, s, _ = rq.shape
    c = RET_CHUNK
    gamma = 1.0 - jnp.power(2.0, -5.0 - jnp.arange(RET_HEADS, dtype=F32))
    log_g = jnp.log(gamma)
    idx = jnp.arange(c, dtype=F32)
    diff = idx[:, None] - idx[None, :]
    dmask = jnp.where(diff[None] >= 0, jnp.exp(jnp.maximum(diff, 0.0)[None] * log_g[:, None, None]), 0.0)
    zeta = jnp.exp((c - 1.0 - idx)[None, :] * log_g[:, None])
    xi = jnp.exp((idx + 1.0)[None, :] * log_g[:, None])
    decay = jnp.exp(c * log_g)
    xi_b = jnp.broadcast_to(xi[:, :, None], (RET_HEADS, c, LANES))
    zt_b = jnp.broadcast_to(zeta[:, :, None], (RET_HEADS, c, LANES))
    dc_b = jnp.broadcast_to(decay[:, None, None], (RET_HEADS, LANES, RET_DV))

    tile = lambda w: pl.BlockSpec((1, ts, w), lambda b, i: (b, i, 0))
    const = lambda shape: pl.BlockSpec(shape, lambda b, i: (0,) * len(shape))
    return pl.pallas_call(
        functools.partial(_ret_kernel, ts=ts),
        grid=(bsz, s // ts),
        in_specs=[tile(RET_HEADS * RET_DK), tile(RET_HEADS * RET_DK), tile(RET_HEADS * RET_DV), tile(RET_HEADS * RET_DV),
                  const((RET_HEADS, c, c)), const((RET_HEADS, c, LANES)), const((RET_HEADS, c, LANES)),
                  const((RET_HEADS, LANES, RET_DV))],
        out_specs=tile(RET_HEADS * RET_DV),
        out_shape=jax.ShapeDtypeStruct((bsz, s, RET_HEADS * RET_DV), BF16),
        scratch_shapes=[pltpu.VMEM((RET_HEADS, LANES, RET_DV), F32)],
        compiler_params=_params("parallel", "arbitrary"),
        name="retention",
    )(rq, rk, rv, rg, dmask, xi_b, zt_b, dc_b)


def _post_kernel(x_ref, om_ref, or_ref, mod_ref, wo_ref, g2_ref, wrh_ref, wrl_ref, br_ref,
                 x1_ref, h2_ref, ri_ref, rit_ref, cnt_ref):
    half = om_ref.shape[-1]
    mix = _mm(om_ref[0], wo_ref[0:half, :]) + _mm(or_ref[0], wo_ref[half:, :])
    x1 = x_ref[0] + mod_ref[0, 2:3, :] * mix
    x1_ref[0] = x1
    h2 = _rms(x1, g2_ref[...]) * (1.0 + mod_ref[0, 4:5, :]) + mod_ref[0, 3:4, :]

    hi = h2.astype(BF16)
    h2_ref[0] = hi
    lo = (h2 - hi.astype(F32)).astype(BF16)
    lg = _mm(hi, wrh_ref[...]) + _mm(lo, wrh_ref[...]) + _mm(hi, wrl_ref[...]) + br_ref[...]
    lane = lax.broadcasted_iota(jnp.int32, lg.shape, 1)
    big = jnp.int32(1 << 20)

    gmask = lane < N_GROUPS
    gmax = jnp.max(jnp.where(gmask, lg, NEG), axis=-1, keepdims=True)
    ge = jnp.where(gmask, jnp.exp(lg - gmax), 0.0)
    pg = ge / jnp.sum(ge, axis=-1, keepdims=True)
    p_top = jnp.max(pg, axis=-1, keepdims=True)
    g_top = jnp.min(jnp.where(gmask & (pg == p_top), lane, big), axis=-1, keepdims=True)

    el = lane - N_GROUPS
    assert EXPERTS_PER_GROUP == 8
    emask = (el >= 0) & (el < N_EXPERTS) & (lax.shift_right_arithmetic(el, 3) == g_top)
    ev = jnp.where(emask, lg, NEG)
    v1 = jnp.max(ev, axis=-1, keepdims=True)
    i1 = jnp.min(jnp.where(emask & (ev == v1), lane, big), axis=-1, keepdims=True)
    emask2 = emask & (lane != i1)
    ev2 = jnp.where(emask2, lg, NEG)
    v2 = jnp.max(ev2, axis=-1, keepdims=True)
    i2 = jnp.min(jnp.where(emask2 & (ev2 == v2), lane, big), axis=-1, keepdims=True)
    e = jnp.exp(v2 - v1)
    den = 1.0 + e
    w1 = (1.0 / den) * p_top
    w2 = (e / den) * p_top

    tm = lg.shape[0]
    r_io = lax.broadcasted_iota(jnp.int32, (MOE_TILE, MOE_TILE), 0)
    c_io = lax.broadcasted_iota(jnp.int32, (MOE_TILE, MOE_TILE), 1)
    earlier_tok = (c_io < r_io).astype(BF16)
    lr_io = lax.broadcasted_iota(jnp.int32, (LANES, LANES), 0)
    lc_io = lax.broadcasted_iota(jnp.int32, (LANES, LANES), 1)
    earlier_lane = (lr_io < lc_io).astype(BF16)
    oh = [lane == i1, lane == i2]
    cnt_all = (oh[0] | oh[1]).astype(BF16)
    pos = []
    for hf in range(tm // MOE_TILE):
        cnt = cnt_all[hf * MOE_TILE:(hf + 1) * MOE_TILE, :]
        excl = _mm(earlier_tok, cnt)
        n = jnp.sum(cnt.astype(F32), axis=0, keepdims=True)
        npad = jnp.floor((n + (MOE_CHUNK - 1)) * (1.0 / MOE_CHUNK)) * MOE_CHUNK
        loff = _mm(jnp.broadcast_to(npad, (8, LANES)).astype(BF16), earlier_lane)
        pos.append(excl + loff[0:1, :])
        cnt_ref[hf] = jnp.broadcast_to(n, (8, LANES))
    pos = jnp.concatenate(pos, axis=0)
    lp0, lp1 = [jnp.sum(jnp.where(o, pos, 0.0), axis=-1, keepdims=True) for o in oh]
    cols = [(i1 - N_GROUPS).astype(F32), (i2 - N_GROUPS).astype(F32), w1, w2, lp0, lp1]
    ri = jnp.zeros(lg.shape, F32)
    for j, col in enumerate(cols):
        ri = jnp.where(lane == j, col, ri)
    ri_ref[0] = ri
    rit_ref[...] = ri.T


def _post(x, o_mla, o_ret, mod3, w_o, norm2_g, w_gr, b_gr, w_er, b_er, tm):
    bsz, s, d = x.shape
    w_r = jnp.concatenate([w_gr, w_er.reshape(d, N_EXPERTS), jnp.zeros((d, LANES - N_GROUPS - N_EXPERTS), F32)], axis=1)
    w_rh = w_r.astype(BF16)
    w_rl = (w_r - w_rh.astype(F32)).astype(BF16)
    b_r = jnp.concatenate([b_gr, b_er.reshape(-1), jnp.zeros((LANES - N_GROUPS - N_EXPERTS,), F32)]).reshape(1, LANES)
    tile = lambda w: pl.BlockSpec((1, tm, w), lambda b, i: (b, i, 0))
    const = lambda shape: pl.BlockSpec(shape, lambda b, i: (0,) * len(shape))
    per_b = s // tm
    sub = tm // MOE_TILE
    return pl.pallas_call(
        _post_kernel,
        grid=(bsz, per_b),
        in_specs=[tile(d), tile(o_mla.shape[-1]), tile(o_ret.shape[-1]), pl.BlockSpec((1, 6, d), lambda b, i: (b, 0, 0)),
                  const((d, d)), const((1, d)), const((d, LANES)), const((d, LANES)), const((1, LANES))],
        out_specs=[tile(d), tile(d), tile(LANES),
                   pl.BlockSpec((LANES, tm), lambda b, i: (0, b * per_b + i)),
                   pl.BlockSpec((sub, 8, LANES), lambda b, i: (b * per_b + i, 0, 0))],
        out_shape=[jax.ShapeDtypeStruct((bsz, s, d), F32), jax.ShapeDtypeStruct((bsz, s, d), BF16),
                   jax.ShapeDtypeStruct((bsz, s, LANES), F32),
                   jax.ShapeDtypeStruct((LANES, bsz * s), F32),
                   jax.ShapeDtypeStruct((bsz * s // MOE_TILE, 8, LANES), F32)],
        compiler_params=_params("parallel", "arbitrary"),
        name="post_mixer",
    )(x, o_mla, o_ret, mod3, w_o.astype(BF16), norm2_g.reshape(1, d), w_rh, w_rl, b_r)


def _chunk_rows(c):
    return pl.ds(pl.multiple_of(c * MOE_CHUNK, MOE_CHUNK), MOE_CHUNK)


def _dispatch_kernel(dmap_ref, nchk_ref, tstart_ref, tn_ref, nbr_ref, h2_ref, ri_ref, rit_ref, xs_hbm,
                     xloc, zblk, sem, zsem, bsem, *, nt, nb):
    i = pl.program_id(0)
    slot = lax.rem(i, 2)
    d = h2_ref.shape[-1]

    def zero_copy(e, c):
        return pltpu.make_async_copy(zblk.at[pl.ds(0, MOE_CHUNK)], xs_hbm.at[_chunk_rows(tstart_ref[e] + c)], zsem)

    def zero_block(j):
        rows = pl.ds(pl.multiple_of(j * MOE_BLOCK, MOE_BLOCK), MOE_BLOCK)
        return pltpu.make_async_copy(zblk, xs_hbm.at[rows], bsem)

    def chunk_copy(t, c, sl):
        return pltpu.make_async_copy(xloc.at[sl].at[_chunk_rows(c)],
                                     xs_hbm.at[_chunk_rows(dmap_ref[t * MOE_TILE_CHUNKS + c])], sem.at[sl])

    def wait_tile(t, sl):
        lax.fori_loop(0, nchk_ref[t], lambda c, z: (chunk_copy(t, c, sl).wait(), z)[1], 0)

    @pl.when(i == 0)
    def _():
        zblk[...] = jnp.zeros_like(zblk)
        lax.fori_loop(nbr_ref[0], nb, lambda j, z: (zero_block(j).start(), z)[1], 0)
        for e in range(N_EXPERTS):
            lax.fori_loop(0, tn_ref[e], lambda c, z, e=e: (zero_copy(e, c).start(), z)[1], 0)
        for e in range(N_EXPERTS):
            lax.fori_loop(0, tn_ref[e], lambda c, z, e=e: (zero_copy(e, c).wait(), z)[1], 0)

    @pl.when(i >= 2)
    def _():
        wait_tile(i - 2, slot)

    s_io = lax.broadcasted_iota(jnp.int32, (MOE_SLAB, MOE_TILE), 0).astype(F32)
    pm = [(s_io == rit_ref[4 + k:5 + k, :]).astype(BF16) for k in range(TOP_K)]
    xloc[slot, :, 0:d] = _mm(pm[0] + pm[1], h2_ref[...]).astype(BF16)
    lane = lax.broadcasted_iota(jnp.int32, (MOE_TILE, LANES), 1)
    wx = jnp.zeros((MOE_SLAB, LANES), F32)
    for k in range(TOP_K):
        w = ri_ref[:, TOP_K + k:TOP_K + k + 1]
        hi = w.astype(BF16).astype(F32)
        wx = wx + _mm(pm[k], jnp.where(lane == 0, hi, jnp.where(lane == 1, w - hi, 0.0)).astype(BF16))
    xloc[slot, :, d:d + LANES] = wx.astype(BF16)

    lax.fori_loop(0, nchk_ref[i], lambda c, z: (chunk_copy(i, c, slot).start(), z)[1], 0)

    @pl.when(i == nt - 1)
    def _():
        wait_tile(i, slot)
        if nt >= 2:
            wait_tile(i - 1, 1 - slot)
        lax.fori_loop(nbr_ref[0], nb, lambda j, z: (zero_block(j).wait(), z)[1], 0)


def _dispatch(h2, ri, rit, tables, n_rows):
    t, d = h2.shape
    nt = t // MOE_TILE
    assert MOE_SLAB >= TOP_K * MOE_TILE + N_EXPERTS * (MOE_CHUNK - 1)
    grid_spec = pltpu.PrefetchScalarGridSpec(
        num_scalar_prefetch=5,
        grid=(nt,),
        in_specs=[pl.BlockSpec((MOE_TILE, d), lambda i, *_: (i, 0)),
                  pl.BlockSpec((MOE_TILE, LANES), lambda i, *_: (i, 0)),
                  pl.BlockSpec((8, MOE_TILE), lambda i, *_: (0, i))],
        out_specs=pl.BlockSpec(memory_space=pl.ANY),
        scratch_shapes=[pltpu.VMEM((2, MOE_SLAB, d + LANES), BF16), pltpu.VMEM((MOE_BLOCK, d + LANES), BF16),
                        pltpu.SemaphoreType.DMA((2,)), pltpu.SemaphoreType.DMA(()), pltpu.SemaphoreType.DMA(())])
    return pl.pallas_call(
        functools.partial(_dispatch_kernel, nt=nt, nb=n_rows // MOE_BLOCK),
        grid_spec=grid_spec,
        out_shape=jax.ShapeDtypeStruct((n_rows, d + LANES), BF16),
        compiler_params=_params("arbitrary"),
        name="moe_dispatch",
    )(tables["dmap"], tables["tile_chunks"], tables["tail_start"], tables["tail_n"], tables["n_blocks"],
      h2, ri, rit)


def _expert_kernel(be_ref, nbr_ref, xs_ref, w1_ref, w3_ref, w2_ref, y_ref):
    used = pl.program_id(0) < nbr_ref[0]

    @pl.when(jnp.logical_not(used))
    def _():
        y_ref[...] = jnp.zeros_like(y_ref)

    @pl.when(used)
    def _():
        d = y_ref.shape[-1]
        xb = xs_ref[:, 0:d]
        gw = xs_ref[:, d:d + LANES].astype(F32)
        gate = gw[:, 0:1] + gw[:, 1:2]
        a = _mm(xb, w1_ref[0])
        b = _mm(xb, w3_ref[0])
        hm = (_silu(a) * b).astype(BF16)
        y_ref[...] = (_mm(hm, w2_ref[0]) * gate).astype(BF16)


def _experts(xs, tables, w1, w3, w2):
    n_rows, dw = xs.shape
    d = dw - LANES
    nb = n_rows // MOE_BLOCK
    blk = lambda i, be, nbr: jnp.minimum(i, nbr[0] - 1)
    grid_spec = pltpu.PrefetchScalarGridSpec(
        num_scalar_prefetch=2,
        grid=(nb,),
        in_specs=[pl.BlockSpec((MOE_BLOCK, dw), lambda i, be, nbr: (blk(i, be, nbr), 0)),
                  pl.BlockSpec((1, d, D_EXPERT), lambda i, be, nbr: (be[blk(i, be, nbr)], 0, 0)),
                  pl.BlockSpec((1, d, D_EXPERT), lambda i, be, nbr: (be[blk(i, be, nbr)], 0, 0)),
                  pl.BlockSpec((1, D_EXPERT, d), lambda i, be, nbr: (be[blk(i, be, nbr)], 0, 0))],
        out_specs=pl.BlockSpec((MOE_BLOCK, d), lambda i, be, nbr: (i, 0)))
    return pl.pallas_call(
        _expert_kernel,
        grid_spec=grid_spec,
        out_shape=jax.ShapeDtypeStruct((n_rows, d), BF16),
        compiler_params=_params("arbitrary"),
        name="moe_experts",
    )(tables["blk_expert"], tables["n_blocks"], xs, w1.astype(BF16), w3.astype(BF16), w2.astype(BF16))


def _moe_rows(n_tiles):
    worst = n_tiles * (TOP_K * MOE_TILE + N_EXPERTS * (MOE_CHUNK - 1)) + N_EXPERTS * (MOE_BLOCK - MOE_CHUNK)
    return -(-worst // MOE_BLOCK) * MOE_BLOCK


def _moe_tables(cnt, n_rows):
    n_tiles = cnt.shape[0]
    per_blk = MOE_BLOCK // MOE_CHUNK

    def excl_cumsum(a, axis):
        n = a.shape[axis]
        lower = jnp.arange(n)[:, None] > jnp.arange(n)[None, :]
        if axis == 0:
            return jnp.sum(jnp.where(lower[:, :, None], a[None, :, :], 0), axis=1)
        return jnp.sum(jnp.where(lower[None, :, :], a[:, None, :], 0), axis=2)

    nch = (cnt + MOE_CHUNK - 1) // MOE_CHUNK
    loff = excl_cumsum(nch, 1)
    seg = jnp.sum(nch, axis=0)
    blocks = (seg + per_blk - 1) // per_blk
    bstart = excl_cumsum(blocks[None, :], 1)[0]
    bend = bstart + blocks
    estart = bstart * per_blk
    gbase = estart[None, :] + excl_cumsum(nch, 0)
    c = jnp.arange(MOE_TILE_CHUNKS, dtype=jnp.int32)
    owner = jnp.sum((c[None, :, None] >= (loff + nch)[:, None, :]).astype(jnp.int32), axis=-1)
    owner = jnp.minimum(owner, N_EXPERTS - 1)
    is_owner = owner[:, :, None] == jnp.arange(N_EXPERTS, dtype=jnp.int32)[None, None, :]
    dmap = c[None, :] + jnp.sum(jnp.where(is_owner, (gbase - loff)[:, None, :], 0), axis=-1)
    dmap = jnp.clip(dmap, 0, n_rows // MOE_CHUNK - 1)
    nb = n_rows // MOE_BLOCK
    blk_expert = jnp.sum((jnp.arange(nb, dtype=jnp.int32)[:, None] >= bend[None, :]).astype(jnp.int32), axis=1)
    blk_expert = jnp.minimum(blk_expert, N_EXPERTS - 1)
    return dict(dmap=dmap.reshape(-1).astype(jnp.int32), tile_chunks=jnp.sum(nch, axis=1).astype(jnp.int32),
                tail_start=(estart + seg).astype(jnp.int32), tail_n=(blocks * per_blk - seg).astype(jnp.int32),
                blk_expert=blk_expert.astype(jnp.int32), n_blocks=bend[-1:].astype(jnp.int32))


def _final_kernel(dmap_ref, nchk_ref, x1_ref, ri_ref, mod_ref, g_ref, y_hbm, o_ref, yloc, sem, *, nt):
    i = pl.program_id(0)
    slot = lax.rem(i, 2)

    def chunk_copy(t, c, sl):
        return pltpu.make_async_copy(y_hbm.at[_chunk_rows(dmap_ref[t * MOE_TILE_CHUNKS + c])],
                                     yloc.at[sl].at[_chunk_rows(c)], sem.at[sl])

    def gather(t, sl):
        lax.fori_loop(0, nchk_ref[t], lambda c, z: (chunk_copy(t, c, sl).start(), z)[1], 0)

    @pl.when(i == 0)
    def _():
        yloc[...] = jnp.zeros_like(yloc)
        gather(0, 0)

    @pl.when(i + 1 < nt)
    def _():
        gather(i + 1, 1 - slot)

    lax.fori_loop(0, nchk_ref[i], lambda c, z: (chunk_copy(i, c, slot).wait(), z)[1], 0)

    l_io = lax.broadcasted_iota(jnp.int32, (MOE_TILE, MOE_SLAB), 1).astype(F32)
    pick = ((l_io == ri_ref[:, 4:5]) | (l_io == ri_ref[:, 5:6])).astype(BF16)
    moe = _mm(pick, yloc[slot])
    x2 = x1_ref[...] + mod_ref[0, 5:6, :] * moe
    o_ref[...] = _rms(x2, g_ref[...])


def _final(x1, ri, y, mod3, final_g, tables, s):
    t, d = x1.shape
    nt = t // MOE_TILE
    per_b = s // MOE_TILE
    grid_spec = pltpu.PrefetchScalarGridSpec(
        num_scalar_prefetch=2,
        grid=(nt,),
        in_specs=[pl.BlockSpec((MOE_TILE, d), lambda i, *_: (i, 0)),
                  pl.BlockSpec((MOE_TILE, LANES), lambda i, *_: (i, 0)),
                  pl.BlockSpec((1, 6, d), lambda i, *_: (i // per_b, 0, 0)),
                  pl.BlockSpec((1, d), lambda i, *_: (0, 0)),
                  pl.BlockSpec(memory_space=pl.ANY)],
        out_specs=pl.BlockSpec((MOE_TILE, d), lambda i, *_: (i, 0)),
        scratch_shapes=[pltpu.VMEM((2, MOE_SLAB, d), BF16), pltpu.SemaphoreType.DMA((2,))])
    return pl.pallas_call(
        functools.partial(_final_kernel, nt=nt),
        grid_spec=grid_spec,
        out_shape=jax.ShapeDtypeStruct((t, d), F32),
        compiler_params=_params("arbitrary"),
        name="moe_combine_final",
    )(tables["dmap"], tables["tile_chunks"], x1, ri, mod3, final_g.reshape(1, d), y)


def kernel(x, c, positions, w_ada, b_ada, norm1_g, w_in, q_norm_g, w_uq, kv_norm_g, w_ukv, w_o, norm2_g,
           w_gr, b_gr, w_er, b_er, w1, w3, w2, final_g):
    bsz, s, d = x.shape
    assert w_ada.shape[0] == 1, "one layer"
    tm = min(512, s)
    tq = min(256, s)
    ts = min(1024, s)
    mod3 = _adaln(c, w_ada[0], b_ada[0]).reshape(bsz, 6, d)
    pos3 = positions.astype(F32).reshape(bsz, s, 1)
    q, k, vt, rq, rk, rv, rg = _pre(x, mod3, pos3, norm1_g[0], w_in[0], q_norm_g[0], w_uq[0], kv_norm_g[0], w_ukv[0],
                                    min(1024, s), tq)
    o_mla = _attention(q, k, vt, tq, ATTN_HEADS_PER_STEP)
    o_ret = _retention(rq, rk, rv, rg, ts)
    x1, h2, ri, rit, cnt = _post(x, o_mla, o_ret, mod3, w_o[0], norm2_g[0], w_gr[0], b_gr[0], w_er[0], b_er[0], tm)
    t = bsz * s
    n_rows = _moe_rows(t // MOE_TILE)
    counts = cnt[:, 0, N_GROUPS:N_GROUPS + N_EXPERTS].astype(jnp.int32)
    tables = _moe_tables(counts, n_rows)
    ri2 = ri.reshape(t, LANES)
    xs = _dispatch(h2.reshape(t, d), ri2, rit, tables, n_rows)
    y = _experts(xs, tables, w1[0], w3[0], w2[0])
    out = _final(x1.reshape(t, d), ri2, y, mod3, final_g, tables, s)
    return out.reshape(bsz, s, d)
```

```python
import functools

import jax
import jax.numpy as jnp
from jax import lax
from jax.experimental import pallas as pl
from jax.experimental.pallas import tpu as pltpu

MLA_HEADS = 8
MLA_NOPE = 64
MLA_ROPE = 32
MLA_V = 64
Q_LORA = 256
KV_LORA = 128
RET_HEADS = 4
RET_DK = 64
RET_DV = 128
RET_CHUNK = 128
ROPE_BASE = 10000.0
NORM_EPS = 1e-6
N_GROUPS = 4
EXPERTS_PER_GROUP = 8
N_EXPERTS = N_GROUPS * EXPERTS_PER_GROUP
TOP_K = 2
D_EXPERT = 256
MOE_TILE = 256
MOE_CHUNK = 16
MOE_SLAB = 1024
MOE_TILE_CHUNKS = MOE_SLAB // MOE_CHUNK
MOE_BLOCK = 512

LANES = 128
VMEM_LIMIT = 56 * 1024 * 1024

F32 = jnp.float32
BF16 = jnp.bfloat16
NEG = float(jnp.finfo(jnp.float32).min)
LOG2_E = 1.4426950408889634
ATTN_HEADS_PER_STEP = 8
ATTN_LOOKAHEAD = 8
MLA_VROWS = MLA_V + 16

_C_Q = 0
_C_KV = _C_Q + Q_LORA
_C_KPE = _C_KV + KV_LORA
_C_RQ = _C_KPE + LANES
_C_RK = _C_RQ + RET_HEADS * RET_DK
_C_RV = _C_RK + RET_HEADS * RET_DK
_C_RG = _C_RV + RET_HEADS * RET_DV
_IN_PERM = _C_RG + RET_HEADS * RET_DV


def _silu(v):
    return v / (1.0 + jnp.exp(-v))


def _mm(a, b):
    return jnp.dot(a, b, preferred_element_type=F32)


def _mm_nt(a, b):
    return lax.dot_general(a, b, (((1,), (1,)), ((), ())), preferred_element_type=F32)


def _params(*sem):
    return pltpu.CompilerParams(dimension_semantics=sem, vmem_limit_bytes=VMEM_LIMIT)


def _adaln_kernel(c_ref, w_ref, b_ref, o_ref):
    a = _silu(c_ref[...]).astype(BF16)
    o_ref[...] = _mm(a, w_ref[...].astype(BF16)) + b_ref[...]


def _adaln(c, w_ada, b_ada):
    bsz, d = c.shape
    n = w_ada.shape[1]
    tn = d
    return pl.pallas_call(
        _adaln_kernel,
        grid=(n // tn,),
        in_specs=[pl.BlockSpec((bsz, d), lambda j: (0, 0)),
                  pl.BlockSpec((d, tn), lambda j: (0, j)),
                  pl.BlockSpec((1, tn), lambda j: (0, j))],
        out_specs=pl.BlockSpec((bsz, tn), lambda j: (0, j)),
        out_shape=jax.ShapeDtypeStruct((bsz, n), F32),
        compiler_params=_params("arbitrary"),
        name="adaln",
    )(c, w_ada, b_ada.reshape(1, n))


def _rms(v, g):
    return v * lax.rsqrt(jnp.mean(v * v, axis=-1, keepdims=True) + NORM_EPS) * g


def _pre_kernel(x_ref, mod_ref, pos_ref, g1_ref, win_ref, qg_ref, wuq_ref, kvg_ref, wuk_ref,
                wuv_ref, invf_ref, q_ref, k_ref, v_ref, rq_ref, rk_ref, rv_ref, rg_ref):
    tq = v_ref.shape[-1]
    subs = [slice(i * tq, (i + 1) * tq) for i in range(v_ref.shape[1])]
    sh1 = mod_ref[0, 0:1, :]
    sc1 = mod_ref[0, 1:2, :]
    projs = [_mm((_rms(x_ref[0, rs, :], g1_ref[...]) * (1.0 + sc1) + sh1).astype(BF16), win_ref[...]) for rs in subs]

    cqs = [_rms(p[:, _C_Q:_C_Q + Q_LORA], qg_ref[...]).astype(BF16) for p in projs]
    ckvs = [_rms(p[:, _C_KV:_C_KV + KV_LORA], kvg_ref[...]).astype(BF16) for p in projs]
    qas = [_mm(cq, wuq_ref[...]) for cq in cqs]
    kns = [_mm(ckv, wuk_ref[...]) for ckv in ckvs]
    vvs = [_mm(ckv, wuv_ref[...]) for ckv in ckvs]

    lane = lax.broadcasted_iota(jnp.int32, (tq, LANES), 1)
    hi = lane >= RET_DK
    half_m, half_r = MLA_ROPE // 2, RET_DK // 2
    first_m = hi & (lane < RET_DK + half_m)
    first_r = (lane & half_r) == 0
    scale = (MLA_NOPE + MLA_ROPE) ** -0.5 * LOG2_E

    def rope(v, cos, sin, first, half):
        partner = jnp.where(first, pltpu.roll(v, LANES - half, 1), pltpu.roll(v, half, 1))
        return v * cos + partner * sin

    for i, rs in enumerate(subs):
        proj = projs[i]
        ang = pos_ref[0, rs, :] * invf_ref[...]
        cs = jnp.cos(ang)
        sn = jnp.sin(ang)
        c_mla = jnp.where(hi, cs, 1.0)
        s_mla = jnp.where(hi, jnp.where(first_m, -sn, sn), 0.0)
        c_ret = jnp.where(hi, pltpu.roll(cs, RET_DK, 1), cs)
        s_ret = jnp.where(hi, pltpu.roll(sn, RET_DK, 1), sn)
        s_ret = jnp.where(first_r, -s_ret, s_ret)

        kpe = rope(proj[:, _C_KPE:_C_KPE + LANES], c_mla, s_mla, first_m, half_m)
        for hd in range(MLA_HEADS):
            sl = slice(hd * LANES, (hd + 1) * LANES)
            q_ref[0, rs, sl] = (rope(qas[i][:, sl], c_mla, s_mla, first_m, half_m) * scale).astype(BF16)
            k_ref[0, rs, sl] = (kns[i][:, sl] + kpe).astype(BF16)
        vt = vvs[i].T
        tail = jnp.where(lax.broadcasted_iota(jnp.int32, (MLA_VROWS - MLA_V, tq), 0) == 0, 1.0, 0.0)
        slab = [piece for hd in range(MLA_HEADS) for piece in (vt[hd * MLA_V:(hd + 1) * MLA_V, :], tail)]
        v_ref[0, i] = jnp.concatenate(slab, axis=0).astype(BF16)

        for j in range(RET_HEADS * RET_DK // LANES):
            o = j * LANES
            rq = rope(proj[:, _C_RQ + o:_C_RQ + o + LANES], c_ret, s_ret, first_r, half_r)
            rk = rope(proj[:, _C_RK + o:_C_RK + o + LANES], c_ret, s_ret, first_r, half_r)
            rq_ref[0, rs, o:o + LANES] = rq.astype(BF16)
            rk_ref[0, rs, o:o + LANES] = (rk * (RET_DK ** -0.5)).astype(BF16)
        rv_ref[0, rs, :] = proj[:, _C_RV:_C_RV + RET_HEADS * RET_DV].astype(BF16)
        rg_ref[0, rs, :] = proj[:, _C_RG:_C_RG + RET_HEADS * RET_DV]


def _pad_heads(w, width, left):
    k = w.shape[0]
    w3 = w.reshape(k, -1, width)
    w3 = jnp.pad(w3, ((0, 0), (0, 0), (left, LANES - left - width)))
    return w3.reshape(k, -1)


def _pre(x, mod3, pos3, norm1_g, w_in, q_norm_g, w_uq, kv_norm_g, w_ukv, tm, tq):
    bsz, s, d = x.shape
    o = 0
    parts = {}
    for name, width in (("cq", Q_LORA), ("ckv", KV_LORA), ("kr", MLA_ROPE), ("rq", RET_HEADS * RET_DK),
                        ("rk", RET_HEADS * RET_DK), ("rv", RET_HEADS * RET_DV), ("rg", RET_HEADS * RET_DV)):
        parts[name] = w_in[:, o:o + width]
        o += width
    w_in_p = jnp.concatenate([
        parts["cq"], parts["ckv"], _pad_heads(parts["kr"], MLA_ROPE, MLA_NOPE),
        parts["rq"], parts["rk"], parts["rv"], parts["rg"]], axis=1).astype(BF16)
    assert w_in_p.shape[1] == _IN_PERM
    w_uq_p = _pad_heads(w_uq, MLA_NOPE + MLA_ROPE, 0).astype(BF16)
    wkv3 = w_ukv.reshape(KV_LORA, MLA_HEADS, MLA_NOPE + MLA_V)
    w_uk_p = _pad_heads(wkv3[:, :, :MLA_NOPE].reshape(KV_LORA, -1), MLA_NOPE, 0).astype(BF16)
    w_uv = wkv3[:, :, MLA_NOPE:].reshape(KV_LORA, -1).astype(BF16)
    half_r, half_m = RET_DK // 2, MLA_ROPE // 2
    f_r = ROPE_BASE ** (-(jnp.arange(half_r, dtype=F32) / half_r))
    f_m = ROPE_BASE ** (-(jnp.arange(half_m, dtype=F32) / half_m))
    invf = jnp.concatenate([f_r, f_r, f_m, f_m, jnp.zeros((LANES - 2 * half_r - 2 * half_m,), F32)]).reshape(1, LANES)

    hq = MLA_HEADS * LANES
    const = lambda shape: pl.BlockSpec(shape, lambda b, i: (0,) * len(shape))
    tile = lambda w: pl.BlockSpec((1, tm, w), lambda b, i: (b, i, 0))
    return pl.pallas_call(
        _pre_kernel,
        grid=(bsz, s // tm),
        in_specs=[tile(d), pl.BlockSpec((1, 6, d), lambda b, i: (b, 0, 0)), tile(1), const((1, d)),
                  const((d, _IN_PERM)), const((1, Q_LORA)), const((Q_LORA, hq)),
                  const((1, KV_LORA)), const((KV_LORA, hq)), const((KV_LORA, MLA_HEADS * MLA_V)), const((1, LANES))],
        out_specs=[tile(hq), tile(hq),
                   pl.BlockSpec((1, tm // tq, MLA_HEADS * MLA_VROWS, tq), lambda b, i: (b, i, 0, 0)),
                   tile(RET_HEADS * RET_DK), tile(RET_HEADS * RET_DK),
                   tile(RET_HEADS * RET_DV), tile(RET_HEADS * RET_DV)],
        out_shape=[jax.ShapeDtypeStruct((bsz, s, hq), BF16), jax.ShapeDtypeStruct((bsz, s, hq), BF16),
                   jax.ShapeDtypeStruct((bsz, s // tq, MLA_HEADS * MLA_VROWS, tq), BF16),
                   jax.ShapeDtypeStruct((bsz, s, RET_HEADS * RET_DK), BF16),
                   jax.ShapeDtypeStruct((bsz, s, RET_HEADS * RET_DK), BF16),
                   jax.ShapeDtypeStruct((bsz, s, RET_HEADS * RET_DV), BF16),
                   jax.ShapeDtypeStruct((bsz, s, RET_HEADS * RET_DV), F32)],
        compiler_params=_params("parallel", "arbitrary"),
        name="pre_mixer",
    )(x, mod3, pos3, norm1_g.reshape(1, d), w_in_p, q_norm_g.reshape(1, -1), w_uq_p,
      kv_norm_g.reshape(1, -1), w_uk_p, w_uv, invf)


def _attn_kernel(q_ref, k_ref, vt_ref, o_ref, *, tq, hps):
    qi = pl.program_id(1)
    key = lax.broadcasted_iota(jnp.int32, (tq, tq), 0)
    qry = lax.broadcasted_iota(jnp.int32, (tq, tq), 1)
    hsl = [slice(hh * LANES, (hh + 1) * LANES) for hh in range(hps)]
    vsl = [slice(hh * MLA_VROWS, (hh + 1) * MLA_VROWS) for hh in range(hps)]
    qs = [q_ref[0, :, hs] for hs in hsl]

    def heads(carry, blk, nblk, masked=False):
        start = pl.multiple_of(blk * tq, tq)
        sts, ps, stats, out = {}, {}, {}, {}
        for step in range(hps + ATTN_LOOKAHEAD + 1):
            if step < hps:
                sts[step] = _mm_nt(k_ref[0, pl.ds(start, nblk * tq), hsl[step]], qs[step])
            hh = step - ATTN_LOOKAHEAD
            if 0 <= hh < hps:
                m, acc = carry[hh]
                st = sts.pop(hh)
                if masked:
                    st = jnp.where(key <= qry, st, NEG)
                m_new = jnp.maximum(m, jnp.max(st, axis=0, keepdims=True))
                ps[hh] = jnp.exp2(st - m_new).astype(BF16)
                stats[hh] = (m_new, jnp.exp2(m - m_new) * acc)
            hh = step - ATTN_LOOKAHEAD - 1
            if 0 <= hh < hps:
                m_new, acc = stats.pop(hh)
                p = ps.pop(hh)
                for c in range(nblk):
                    acc = acc + _mm(vt_ref[0, blk + c, vsl[hh], :], p[c * tq:(c + 1) * tq, :])
                out[hh] = (m_new, acc)
        return tuple(out[hh] for hh in range(hps))

    init = (jnp.full((1, tq), NEG, F32), jnp.zeros((MLA_VROWS, tq), F32))
    carry = (init,) * hps
    carry = lax.fori_loop(0, qi // 2, lambda j, c: heads(c, 2 * j, 2), carry)
    carry = lax.fori_loop(0, qi % 2, lambda j, c: heads(c, qi - 1, 1), carry)
    carry = heads(carry, qi, 1, masked=True)
    out_t = jnp.concatenate([acc[0:MLA_V] / acc[MLA_V:MLA_V + 1] for _, acc in carry], axis=0)
    o_ref[0] = out_t.T.astype(BF16)


def _attention(q, k, vt, tq, hps):
    bsz, s, _ = q.shape
    groups = MLA_HEADS // hps
    assert vt.shape == (bsz, s // tq, MLA_HEADS * MLA_VROWS, tq)
    return pl.pallas_call(
        functools.partial(_attn_kernel, tq=tq, hps=hps),
        grid=(bsz * groups, s // tq),
        in_specs=[pl.BlockSpec((1, tq, hps * LANES), lambda g, i: (g // groups, i, g % groups)),
                  pl.BlockSpec((1, s, hps * LANES), lambda g, i: (g // groups, 0, g % groups)),
                  pl.BlockSpec((1, s // tq, hps * MLA_VROWS, tq), lambda g, i: (g // groups, 0, g % groups, 0))],
        out_specs=pl.BlockSpec((1, tq, hps * MLA_V), lambda g, i: (g // groups, i, g % groups)),
        out_shape=jax.ShapeDtypeStruct((bsz, s, MLA_HEADS * MLA_V), BF16),
        compiler_params=_params("parallel", "arbitrary"),
        name="mla_attention",
    )(q, k, vt)


def _ret_kernel(rq_ref, rk_ref, rv_ref, rg_ref, dm_ref, xi_ref, zt_ref, dc_ref, o_ref, st_ref, *, ts):
    @pl.when(pl.program_id(1) == 0)
    def _():
        st_ref[...] = jnp.zeros_like(st_ref)

    lane = lax.broadcasted_iota(jnp.int32, (RET_CHUNK, LANES), 1)
    units = [(c, hd) for c in range(ts // RET_CHUNK) for hd in range(RET_HEADS)]
    rows = lambda c: slice(c * RET_CHUNK, (c + 1) * RET_CHUNK)
    vsl = lambda hd: slice(hd * RET_DV, (hd + 1) * RET_DV)

    def head_lanes(ref, c, hd):
        pair, sub = divmod(hd, LANES // RET_DK)
        mine = (lane >= sub * RET_DK) & (lane < (sub + 1) * RET_DK)
        return jnp.where(mine, ref[0, rows(c), pair * LANES:(pair + 1) * LANES], 0.0).astype(BF16)

    qh = {u: head_lanes(rq_ref, *u) for u in units}
    kh = {u: head_lanes(rk_ref, *u) for u in units}
    vh = {(c, hd): rv_ref[0, rows(c), vsl(hd)] for c, hd in units}
    sc = {u: _mm_nt(qh[u], kh[u]) for u in units}
    un = {u: _mm((kh[u].astype(F32) * zt_ref[u[1]]).astype(BF16).T, vh[u]) for u in units}
    prev = {}
    for hd in range(RET_HEADS):
        st = st_ref[hd]
        for c in range(ts // RET_CHUNK):
            prev[(c, hd)] = st.astype(BF16)
            st = st * dc_ref[hd] + un[(c, hd)]
        st_ref[hd] = st
    for c, hd in units:
        u = (c, hd)
        o = _mm((sc[u] * dm_ref[hd]).astype(BF16), vh[u])
        o = o + _mm((qh[u].astype(F32) * xi_ref[hd]).astype(BF16), prev[u])
        mu = jnp.mean(o, axis=-1, keepdims=True)
        oc = o - mu
        on = oc * lax.rsqrt(jnp.mean(oc * oc, axis=-1, keepdims=True) + NORM_EPS)
        o_ref[0, rows(c), vsl(hd)] = (_silu(rg_ref[0, rows(c), vsl(hd)]) * on).astype(BF16)


def _retention(rq, rk, rv, rg, ts):
    bsz, s, _ = rq.shape
    c = RET_CHUNK
    gamma = 1.0 - jnp.power(2.0, -5.0 - jnp.arange(RET_HEADS, dtype=F32))
    log_g = jnp.log(gamma)
    idx = jnp.arange(c, dtype=F32)
    diff = idx[:, None] - idx[None, :]
    dmask = jnp.where(diff[None] >= 0, jnp.exp(jnp.maximum(diff, 0.0)[None] * log_g[:, None, None]), 0.0)
    zeta = jnp.exp((c - 1.0 - idx)[None, :] * log_g[:, None])
    xi = jnp.exp((idx + 1.0)[None, :] * log_g[:, None])
    decay = jnp.exp(c * log_g)
    xi_b = jnp.broadcast_to(xi[:, :, None], (RET_HEADS, c, LANES))
    zt_b = jnp.broadcast_to(zeta[:, :, None], (RET_HEADS, c, LANES))
    dc_b = jnp.broadcast_to(decay[:, None, None], (RET_HEADS, LANES, RET_DV))

    tile = lambda w: pl.BlockSpec((1, ts, w), lambda b, i: (b, i, 0))
    const = lambda shape: pl.BlockSpec(shape, lambda b, i: (0,) * len(shape))
    return pl.pallas_call(
        functools.partial(_ret_kernel, ts=ts),
        grid=(bsz, s // ts),
        in_specs=[tile(RET_HEADS * RET_DK), tile(RET_HEADS * RET_DK), tile(RET_HEADS * RET_DV), tile(RET_HEADS * RET_DV),
                  const((RET_HEADS, c, c)), const((RET_HEADS, c, LANES)), const((RET_HEADS, c, LANES)),
                  const((RET_HEADS, LANES, RET_DV))],
        out_specs=tile(RET_HEADS * RET_DV),
        out_shape=jax.ShapeDtypeStruct((bsz, s, RET_HEADS * RET_DV), BF16),
        scratch_shapes=[pltpu.VMEM((RET_HEADS, LANES, RET_DV), F32)],
        compiler_params=_params("parallel", "arbitrary"),
        name="retention",
    )(rq, rk, rv, rg, dmask, xi_b, zt_b, dc_b)


def _post_kernel(x_ref, om_ref, or_ref, mod_ref, wo_ref, g2_ref, wrh_ref, wrl_ref, br_ref,
                 x1_ref, h2_ref, ri_ref, rit_ref, cnt_ref):
    half = om_ref.shape[-1]
    mix = _mm(om_ref[0], wo_ref[0:half, :]) + _mm(or_ref[0], wo_ref[half:, :])
    x1 = x_ref[0] + mod_ref[0, 2:3, :] * mix
    x1_ref[0] = x1
    h2 = _rms(x1, g2_ref[...]) * (1.0 + mod_ref[0, 4:5, :]) + mod_ref[0, 3:4, :]

    hi = h2.astype(BF16)
    h2_ref[0] = hi
    lo = (h2 - hi.astype(F32)).astype(BF16)
    lg = _mm(hi, wrh_ref[...]) + _mm(lo, wrh_ref[...]) + _mm(hi, wrl_ref[...]) + br_ref[...]
    lane = lax.broadcasted_iota(jnp.int32, lg.shape, 1)
    big = jnp.int32(1 << 20)

    gmask = lane < N_GROUPS
    gmax = jnp.max(jnp.where(gmask, lg, NEG), axis=-1, keepdims=True)
    ge = jnp.where(gmask, jnp.exp(lg - gmax), 0.0)
    pg = ge / jnp.sum(ge, axis=-1, keepdims=True)
    p_top = jnp.max(pg, axis=-1, keepdims=True)
    g_top = jnp.min(jnp.where(gmask & (pg == p_top), lane, big), axis=-1, keepdims=True)

    el = lane - N_GROUPS
    assert EXPERTS_PER_GROUP == 8
    emask = (el >= 0) & (el < N_EXPERTS) & (lax.shift_right_arithmetic(el, 3) == g_top)
    ev = jnp.where(emask, lg, NEG)
    v1 = jnp.max(ev, axis=-1, keepdims=True)
    i1 = jnp.min(jnp.where(emask & (ev == v1), lane, big), axis=-1, keepdims=True)
    emask2 = emask & (lane != i1)
    ev2 = jnp.where(emask2, lg, NEG)
    v2 = jnp.max(ev2, axis=-1, keepdims=True)
    i2 = jnp.min(jnp.where(emask2 & (ev2 == v2), lane, big), axis=-1, keepdims=True)
    e = jnp.exp(v2 - v1)
    den = 1.0 + e
    w1 = (1.0 / den) * p_top
    w2 = (e / den) * p_top

    tm = lg.shape[0]
    r_io = lax.broadcasted_iota(jnp.int32, (MOE_TILE, MOE_TILE), 0)
    c_io = lax.broadcasted_iota(jnp.int32, (MOE_TILE, MOE_TILE), 1)
    earlier_tok = (c_io < r_io).astype(BF16)
    lr_io = lax.broadcasted_iota(jnp.int32, (LANES, LANES), 0)
    lc_io = lax.broadcasted_iota(jnp.int32, (LANES, LANES), 1)
    earlier_lane = (lr_io < lc_io).astype(BF16)
    oh = [lane == i1, lane == i2]
    cnt_all = (oh[0] | oh[1]).astype(BF16)
    pos = []
    for hf in range(tm // MOE_TILE):
        cnt = cnt_all[hf * MOE_TILE:(hf + 1) * MOE_TILE, :]
        excl = _mm(earlier_tok, cnt)
        n = jnp.sum(cnt.astype(F32), axis=0, keepdims=True)
        npad = jnp.floor((n + (MOE_CHUNK - 1)) * (1.0 / MOE_CHUNK)) * MOE_CHUNK
        loff = _mm(jnp.broadcast_to(npad, (8, LANES)).astype(BF16), earlier_lane)
        pos.append(excl + loff[0:1, :])
        cnt_ref[hf] = jnp.broadcast_to(n, (8, LANES))
    pos = jnp.concatenate(pos, axis=0)
    lp0, lp1 = [jnp.sum(jnp.where(o, pos, 0.0), axis=-1, keepdims=True) for o in oh]
    cols = [(i1 - N_GROUPS).astype(F32), (i2 - N_GROUPS).astype(F32), w1, w2, lp0, lp1]
    ri = jnp.zeros(lg.shape, F32)
    for j, col in enumerate(cols):
        ri = jnp.where(lane == j, col, ri)
    ri_ref[0] = ri
    rit_ref[...] = ri.T


def _post(x, o_mla, o_ret, mod3, w_o, norm2_g, w_gr, b_gr, w_er, b_er, tm):
    bsz, s, d = x.shape
    w_r = jnp.concatenate([w_gr, w_er.reshape(d, N_EXPERTS), jnp.zeros((d, LANES - N_GROUPS - N_EXPERTS), F32)], axis=1)
    w_rh = w_r.astype(BF16)
    w_rl = (w_r - w_rh.astype(F32)).astype(BF16)
    b_r = jnp.concatenate([b_gr, b_er.reshape(-1), jnp.zeros((LANES - N_GROUPS - N_EXPERTS,), F32)]).reshape(1, LANES)
    tile = lambda w: pl.BlockSpec((1, tm, w), lambda b, i: (b, i, 0))
    const = lambda shape: pl.BlockSpec(shape, lambda b, i: (0,) * len(shape))
    per_b = s // tm
    sub = tm // MOE_TILE
    return pl.pallas_call(
        _post_kernel,
        grid=(bsz, per_b),
        in_specs=[tile(d), tile(o_mla.shape[-1]), tile(o_ret.shape[-1]), pl.BlockSpec((1, 6, d), lambda b, i: (b, 0, 0)),
                  const((d, d)), const((1, d)), const((d, LANES)), const((d, LANES)), const((1, LANES))],
        out_specs=[tile(d), tile(d), tile(LANES),
                   pl.BlockSpec((LANES, tm), lambda b, i: (0, b * per_b + i)),
                   pl.BlockSpec((sub, 8, LANES), lambda b, i: (b * per_b + i, 0, 0))],
        out_shape=[jax.ShapeDtypeStruct((bsz, s, d), F32), jax.ShapeDtypeStruct((bsz, s, d), BF16),
                   jax.ShapeDtypeStruct((bsz, s, LANES), F32),
                   jax.ShapeDtypeStruct((LANES, bsz * s), F32),
                   jax.ShapeDtypeStruct((bsz * s // MOE_TILE, 8, LANES), F32)],
        compiler_params=_params("parallel", "arbitrary"),
        name="post_mixer",
    )(x, o_mla, o_ret, mod3, w_o.astype(BF16), norm2_g.reshape(1, d), w_rh, w_rl, b_r)


def _chunk_rows(c):
    return pl.ds(pl.multiple_of(c * MOE_CHUNK, MOE_CHUNK), MOE_CHUNK)


def _dispatch_kernel(dmap_ref, nchk_ref, tstart_ref, tn_ref, nbr_ref, h2_ref, ri_ref, rit_ref, xs_hbm,
                     xloc, zblk, sem, zsem, bsem, *, nt, nb):
    i = pl.program_id(0)
    slot = lax.rem(i, 2)
    d = h2_ref.shape[-1]

    def zero_copy(e, c):
        return pltpu.make_async_copy(zblk.at[pl.ds(0, MOE_CHUNK)], xs_hbm.at[_chunk_rows(tstart_ref[e] + c)], zsem)

    def zero_block(j):
        rows = pl.ds(pl.multiple_of(j * MOE_BLOCK, MOE_BLOCK), MOE_BLOCK)
        return pltpu.make_async_copy(zblk, xs_hbm.at[rows], bsem)

    def chunk_copy(t, c, sl):
        return pltpu.make_async_copy(xloc.at[sl].at[_chunk_rows(c)],
                                     xs_hbm.at[_chunk_rows(dmap_ref[t * MOE_TILE_CHUNKS + c])], sem.at[sl])

    def wait_tile(t, sl):
        lax.fori_loop(0, nchk_ref[t], lambda c, z: (chunk_copy(t, c, sl).wait(), z)[1], 0)

    @pl.when(i == 0)
    def _():
        zblk[...] = jnp.zeros_like(zblk)
        lax.fori_loop(nbr_ref[0], nb, lambda j, z: (zero_block(j).start(), z)[1], 0)
        for e in range(N_EXPERTS):
            lax.fori_loop(0, tn_ref[e], lambda c, z, e=e: (zero_copy(e, c).start(), z)[1], 0)
        for e in range(N_EXPERTS):
            lax.fori_loop(0, tn_ref[e], lambda c, z, e=e: (zero_copy(e, c).wait(), z)[1], 0)

    @pl.when(i >= 2)
    def _():
        wait_tile(i - 2, slot)

    s_io = lax.broadcasted_iota(jnp.int32, (MOE_SLAB, MOE_TILE), 0).astype(F32)
    pm = [(s_io == rit_ref[4 + k:5 + k, :]).astype(BF16) for k in range(TOP_K)]
    xloc[slot, :, 0:d] = _mm(pm[0] + pm[1], h2_ref[...]).astype(BF16)
    lane = lax.broadcasted_iota(jnp.int32, (MOE_TILE, LANES), 1)
    wx = jnp.zeros((MOE_SLAB, LANES), F32)
    for k in range(TOP_K):
        w = ri_ref[:, TOP_K + k:TOP_K + k + 1]
        hi = w.astype(BF16).astype(F32)
        wx = wx + _mm(pm[k], jnp.where(lane == 0, hi, jnp.where(lane == 1, w - hi, 0.0)).astype(BF16))
    xloc[slot, :, d:d + LANES] = wx.astype(BF16)

    lax.fori_loop(0, nchk_ref[i], lambda c, z: (chunk_copy(i, c, slot).start(), z)[1], 0)

    @pl.when(i == nt - 1)
    def _():
        wait_tile(i, slot)
        if nt >= 2:
            wait_tile(i - 1, 1 - slot)
        lax.fori_loop(nbr_ref[0], nb, lambda j, z: (zero_block(j).wait(), z)[1], 0)


def _dispatch(h2, ri, rit, tables, n_rows):
    t, d = h2.shape
    nt = t // MOE_TILE
    assert MOE_SLAB >= TOP_K * MOE_TILE + N_EXPERTS * (MOE_CHUNK - 1)
    grid_spec = pltpu.PrefetchScalarGridSpec(
        num_scalar_prefetch=5,
        grid=(nt,),
        in_specs=[pl.BlockSpec((MOE_TILE, d), lambda i, *_: (i, 0)),
                  pl.BlockSpec((MOE_TILE, LANES), lambda i, *_: (i, 0)),
                  pl.BlockSpec((8, MOE_TILE), lambda i, *_: (0, i))],
        out_specs=pl.BlockSpec(memory_space=pl.ANY),
        scratch_shapes=[pltpu.VMEM((2, MOE_SLAB, d + LANES), BF16), pltpu.VMEM((MOE_BLOCK, d + LANES), BF16),
                        pltpu.SemaphoreType.DMA((2,)), pltpu.SemaphoreType.DMA(()), pltpu.SemaphoreType.DMA(())])
    return pl.pallas_call(
        functools.partial(_dispatch_kernel, nt=nt, nb=n_rows // MOE_BLOCK),
        grid_spec=grid_spec,
        out_shape=jax.ShapeDtypeStruct((n_rows, d + LANES), BF16),
        compiler_params=_params("arbitrary"),
        name="moe_dispatch",
    )(tables["dmap"], tables["tile_chunks"], tables["tail_start"], tables["tail_n"], tables["n_blocks"],
      h2, ri, rit)


def _expert_kernel(be_ref, nbr_ref, xs_ref, w1_ref, w3_ref, w2_ref, y_ref):
    used = pl.program_id(0) < nbr_ref[0]

    @pl.when(jnp.logical_not(used))
    def _():
        y_ref[...] = jnp.zeros_like(y_ref)

    @pl.when(used)
    def _():
        d = y_ref.shape[-1]
        xb = xs_ref[:, 0:d]
        gw = xs_ref[:, d:d + LANES].astype(F32)
        gate = gw[:, 0:1] + gw[:, 1:2]
        a = _mm(xb, w1_ref[0])
        b = _mm(xb, w3_ref[0])
        hm = (_silu(a) * b).astype(BF16)
        y_ref[...] = (_mm(hm, w2_ref[0]) * gate).astype(BF16)


def _experts(xs, tables, w1, w3, w2):
    n_rows, dw = xs.shape
    d = dw - LANES
    nb = n_rows // MOE_BLOCK
    blk = lambda i, be, nbr: jnp.minimum(i, nbr[0] - 1)
    grid_spec = pltpu.PrefetchScalarGridSpec(
        num_scalar_prefetch=2,
        grid=(nb,),
        in_specs=[pl.BlockSpec((MOE_BLOCK, dw), lambda i, be, nbr: (blk(i, be, nbr), 0)),
                  pl.BlockSpec((1, d, D_EXPERT), lambda i, be, nbr: (be[blk(i, be, nbr)], 0, 0)),
                  pl.BlockSpec((1, d, D_EXPERT), lambda i, be, nbr: (be[blk(i, be, nbr)], 0, 0)),
                  pl.BlockSpec((1, D_EXPERT, d), lambda i, be, nbr: (be[blk(i, be, nbr)], 0, 0))],
        out_specs=pl.BlockSpec((MOE_BLOCK, d), lambda i, be, nbr: (i, 0)))
    return pl.pallas_call(
        _expert_kernel,
        grid_spec=grid_spec,
        out_shape=jax.ShapeDtypeStruct((n_rows, d), BF16),
        compiler_params=_params("arbitrary"),
        name="moe_experts",
    )(tables["blk_expert"], tables["n_blocks"], xs, w1.astype(BF16), w3.astype(BF16), w2.astype(BF16))


def _moe_rows(n_tiles):
    worst = n_tiles * (TOP_K * MOE_TILE + N_EXPERTS * (MOE_CHUNK - 1)) + N_EXPERTS * (MOE_BLOCK - MOE_CHUNK)
    return -(-worst // MOE_BLOCK) * MOE_BLOCK


def _moe_tables(cnt, n_rows):
    n_tiles = cnt.shape[0]
    per_blk = MOE_BLOCK // MOE_CHUNK

    def excl_cumsum(a, axis):
        n = a.shape[axis]
        lower = jnp.arange(n)[:, None] > jnp.arange(n)[None, :]
        if axis == 0:
            return jnp.sum(jnp.where(lower[:, :, None], a[None, :, :], 0), axis=1)
        return jnp.sum(jnp.where(lower[None, :, :], a[:, None, :], 0), axis=2)

    nch = (cnt + MOE_CHUNK - 1) // MOE_CHUNK
    loff = excl_cumsum(nch, 1)
    seg = jnp.sum(nch, axis=0)
    blocks = (seg + per_blk - 1) // per_blk
    bstart = excl_cumsum(blocks[None, :], 1)[0]
    bend = bstart + blocks
    estart = bstart * per_blk
    gbase = estart[None, :] + excl_cumsum(nch, 0)
    c = jnp.arange(MOE_TILE_CHUNKS, dtype=jnp.int32)
    owner = jnp.sum((c[None, :, None] >= (loff + nch)[:, None, :]).astype(jnp.int32), axis=-1)
    owner = jnp.minimum(owner, N_EXPERTS - 1)
    is_owner = owner[:, :, None] == jnp.arange(N_EXPERTS, dtype=jnp.int32)[None, None, :]
    dmap = c[None, :] + jnp.sum(jnp.where(is_owner, (gbase - loff)[:, None, :], 0), axis=-1)
    dmap = jnp.clip(dmap, 0, n_rows // MOE_CHUNK - 1)
    nb = n_rows // MOE_BLOCK
    blk_expert = jnp.sum((jnp.arange(nb, dtype=jnp.int32)[:, None] >= bend[None, :]).astype(jnp.int32), axis=1)
    blk_expert = jnp.minimum(blk_expert, N_EXPERTS - 1)
    return dict(dmap=dmap.reshape(-1).astype(jnp.int32), tile_chunks=jnp.sum(nch, axis=1).astype(jnp.int32),
                tail_start=(estart + seg).astype(jnp.int32), tail_n=(blocks * per_blk - seg).astype(jnp.int32),
                blk_expert=blk_expert.astype(jnp.int32), n_blocks=bend[-1:].astype(jnp.int32))


def _final_kernel(dmap_ref, nchk_ref, x1_ref, ri_ref, mod_ref, g_ref, y_hbm, o_ref, yloc, sem, *, nt):
    i = pl.program_id(0)
    slot = lax.rem(i, 2)

    def chunk_copy(t, c, sl):
        return pltpu.make_async_copy(y_hbm.at[_chunk_rows(dmap_ref[t * MOE_TILE_CHUNKS + c])],
                                     yloc.at[sl].at[_chunk_rows(c)], sem.at[sl])

    def gather(t, sl):
        lax.fori_loop(0, nchk_ref[t], lambda c, z: (chunk_copy(t, c, sl).start(), z)[1], 0)

    @pl.when(i == 0)
    def _():
        yloc[...] = jnp.zeros_like(yloc)
        gather(0, 0)

    @pl.when(i + 1 < nt)
    def _():
        gather(i + 1, 1 - slot)

    lax.fori_loop(0, nchk_ref[i], lambda c, z: (chunk_copy(i, c, slot).wait(), z)[1], 0)

    l_io = lax.broadcasted_iota(jnp.int32, (MOE_TILE, MOE_SLAB), 1).astype(F32)
    pick = ((l_io == ri_ref[:, 4:5]) | (l_io == ri_ref[:, 5:6])).astype(BF16)
    moe = _mm(pick, yloc[slot])
    x2 = x1_ref[...] + mod_ref[0, 5:6, :] * moe
    o_ref[...] = _rms(x2, g_ref[...])


def _final(x1, ri, y, mod3, final_g, tables, s):
    t, d = x1.shape
    nt = t // MOE_TILE
    per_b = s // MOE_TILE
    grid_spec = pltpu.PrefetchScalarGridSpec(
        num_scalar_prefetch=2,
        grid=(nt,),
        in_specs=[pl.BlockSpec((MOE_TILE, d), lambda i, *_: (i, 0)),
                  pl.BlockSpec((MOE_TILE, LANES), lambda i, *_: (i, 0)),
                  pl.BlockSpec((1, 6, d), lambda i, *_: (i // per_b, 0, 0)),
                  pl.BlockSpec((1, d), lambda i, *_: (0, 0)),
                  pl.BlockSpec(memory_space=pl.ANY)],
        out_specs=pl.BlockSpec((MOE_TILE, d), lambda i, *_: (i, 0)),
        scratch_shapes=[pltpu.VMEM((2, MOE_SLAB, d), BF16), pltpu.SemaphoreType.DMA((2,))])
    return pl.pallas_call(
        functools.partial(_final_kernel, nt=nt),
        grid_spec=grid_spec,
        out_shape=jax.ShapeDtypeStruct((t, d), F32),
        compiler_params=_params("arbitrary"),
        name="moe_combine_final",
    )(tables["dmap"], tables["tile_chunks"], x1, ri, mod3, final_g.reshape(1, d), y)


def kernel(x, c, positions, w_ada, b_ada, norm1_g, w_in, q_norm_g, w_uq, kv_norm_g, w_ukv, w_o, norm2_g,
           w_gr, b_gr, w_er, b_er, w1, w3, w2, final_g):
    bsz, s, d = x.shape
    assert w_ada.shape[0] == 1, "one layer"
    tm = min(512, s)
    tq = min(256, s)
    ts = min(1024, s)
    mod3 = _adaln(c, w_ada[0], b_ada[0]).reshape(bsz, 6, d)
    pos3 = positions.astype(F32).reshape(bsz, s, 1)
    q, k, vt, rq, rk, rv, rg = _pre(x, mod3, pos3, norm1_g[0], w_in[0], q_norm_g[0], w_uq[0], kv_norm_g[0], w_ukv[0],
                                    min(1024, s), tq)
    o_mla = _attention(q, k, vt, tq, ATTN_HEADS_PER_STEP)
    o_ret = _retention(rq, rk, rv, rg, ts)
    x1, h2, ri, rit, cnt = _post(x, o_mla, o_ret, mod3, w_o[0], norm2_g[0], w_gr[0], b_gr[0], w_er[0], b_er[0], tm)
    t = bsz * s
    n_rows = _moe_rows(t // MOE_TILE)
    counts = cnt[:, 0, N_GROUPS:N_GROUPS + N_EXPERTS].astype(jnp.int32)
    tables = _moe_tables(counts, n_rows)
    ri2 = ri.reshape(t, LANES)
    xs = _dispatch(h2.reshape(t, d), ri2, rit, tables, n_rows)
    y = _experts(xs, tables, w1[0], w3[0], w2[0])
    out = _final(x1.reshape(t, d), ri2, y, mod3, final_g, tables, s)
    return out.reshape(bsz, s, d)
```

```python
import functools

import jax
import jax.numpy as jnp
from jax import lax
from jax.experimental import pallas as pl
from jax.experimental.pallas import tpu as pltpu

MLA_HEADS = 8
MLA_NOPE = 64
MLA_ROPE = 32
MLA_V = 64
Q_LORA = 256
KV_LORA = 128
RET_HEADS = 4
RET_DK = 64
RET_DV = 128
RET_CHUNK = 128
ROPE_BASE = 10000.0
NORM_EPS = 1e-6
N_GROUPS = 4
EXPERTS_PER_GROUP = 8
N_EXPERTS = N_GROUPS * EXPERTS_PER_GROUP
TOP_K = 2
D_EXPERT = 256
MOE_TILE = 256
MOE_CHUNK = 16
MOE_SLAB = 1024
MOE_TILE_CHUNKS = MOE_SLAB // MOE_CHUNK
MOE_BLOCK = 512

LANES = 128
VMEM_LIMIT = 56 * 1024 * 1024

F32 = jnp.float32
BF16 = jnp.bfloat16
NEG = float(jnp.finfo(jnp.float32).min)
LOG2_E = 1.4426950408889634
ATTN_HEADS_PER_STEP = 8
ATTN_LOOKAHEAD = 8
ATTN_BLOCK_GROUPS = (2, 1)
MLA_VROWS = MLA_V + 16

_C_Q = 0
_C_KV = _C_Q + Q_LORA
_C_KPE = _C_KV + KV_LORA
_C_RQ = _C_KPE + LANES
_C_RK = _C_RQ + RET_HEADS * RET_DK
_C_RV = _C_RK + RET_HEADS * RET_DK
_C_RG = _C_RV + RET_HEADS * RET_DV
_IN_PERM = _C_RG + RET_HEADS * RET_DV


def _silu(v):
    return v / (1.0 + jnp.exp(-v))


def _mm(a, b):
    return jnp.dot(a, b, preferred_element_type=F32)


def _mm_nt(a, b):
    return lax.dot_general(a, b, (((1,), (1,)), ((), ())), preferred_element_type=F32)


def _params(*sem):
    return pltpu.CompilerParams(dimension_semantics=sem, vmem_limit_bytes=VMEM_LIMIT)


def _adaln_kernel(c_ref, w_ref, b_ref, o_ref):
    a = _silu(c_ref[...]).astype(BF16)
    o_ref[...] = _mm(a, w_ref[...].astype(BF16)) + b_ref[...]


def _adaln(c, w_ada, b_ada):
    bsz, d = c.shape
    n = w_ada.shape[1]
    tn = d
    return pl.pallas_call(
        _adaln_kernel,
        grid=(n // tn,),
        in_specs=[pl.BlockSpec((bsz, d), lambda j: (0, 0)),
                  pl.BlockSpec((d, tn), lambda j: (0, j)),
                  pl.BlockSpec((1, tn), lambda j: (0, j))],
        out_specs=pl.BlockSpec((bsz, tn), lambda j: (0, j)),
        out_shape=jax.ShapeDtypeStruct((bsz, n), F32),
        compiler_params=_params("arbitrary"),
        name="adaln",
    )(c, w_ada, b_ada.reshape(1, n))


def _rms(v, g):
    return v * lax.rsqrt(jnp.mean(v * v, axis=-1, keepdims=True) + NORM_EPS) * g


def _pre_kernel(x_ref, mod_ref, pos_ref, g1_ref, win_ref, qg_ref, wuq_ref, kvg_ref, wuk_ref,
                wuv_ref, invf_ref, q_ref, k_ref, v_ref, rq_ref, rk_ref, rv_ref, rg_ref):
    tq = v_ref.shape[-1]
    subs = [slice(i * tq, (i + 1) * tq) for i in range(v_ref.shape[1])]
    sh1 = mod_ref[0, 0:1, :]
    sc1 = mod_ref[0, 1:2, :]
    projs = [_mm((_rms(x_ref[0, rs, :], g1_ref[...]) * (1.0 + sc1) + sh1).astype(BF16), win_ref[...]) for rs in subs]

    cqs = [_rms(p[:, _C_Q:_C_Q + Q_LORA], qg_ref[...]).astype(BF16) for p in projs]
    ckvs = [_rms(p[:, _C_KV:_C_KV + KV_LORA], kvg_ref[...]).astype(BF16) for p in projs]
    qas = [_mm(cq, wuq_ref[...]) for cq in cqs]
    kns = [_mm(ckv, wuk_ref[...]) for ckv in ckvs]
    vvs = [_mm(ckv, wuv_ref[...]) for ckv in ckvs]

    lane = lax.broadcasted_iota(jnp.int32, (tq, LANES), 1)
    hi = lane >= RET_DK
    half_m, half_r = MLA_ROPE // 2, RET_DK // 2
    first_m = hi & (lane < RET_DK + half_m)
    first_r = (lane & half_r) == 0
    scale = (MLA_NOPE + MLA_ROPE) ** -0.5 * LOG2_E

    def rope(v, cos, sin, first, half):
        partner = jnp.where(first, pltpu.roll(v, LANES - half, 1), pltpu.roll(v, half, 1))
        return v * cos + partner * sin

    for i, rs in enumerate(subs):
        proj = projs[i]
        ang = pos_ref[0, rs, :] * invf_ref[...]
        cs = jnp.cos(ang)
        sn = jnp.sin(ang)
        c_mla = jnp.where(hi, cs, 1.0)
        s_mla = jnp.where(hi, jnp.where(first_m, -sn, sn), 0.0)
        c_ret = jnp.where(hi, pltpu.roll(cs, RET_DK, 1), cs)
        s_ret = jnp.where(hi, pltpu.roll(sn, RET_DK, 1), sn)
        s_ret = jnp.where(first_r, -s_ret, s_ret)

        kpe = rope(proj[:, _C_KPE:_C_KPE + LANES], c_mla, s_mla, first_m, half_m)
        for hd in range(MLA_HEADS):
            sl = slice(hd * LANES, (hd + 1) * LANES)
            q_ref[0, rs, sl] = (rope(qas[i][:, sl], c_mla, s_mla, first_m, half_m) * scale).astype(BF16)
            k_ref[0, rs, sl] = (kns[i][:, sl] + kpe).astype(BF16)
        vt = vvs[i].T
        tail = jnp.where(lax.broadcasted_iota(jnp.int32, (MLA_VROWS - MLA_V, tq), 0) == 0, 1.0, 0.0)
        slab = [piece for hd in range(MLA_HEADS) for piece in (vt[hd * MLA_V:(hd + 1) * MLA_V, :], tail)]
        v_ref[0, i] = jnp.concatenate(slab, axis=0).astype(BF16)

        for j in range(RET_HEADS * RET_DK // LANES):
            o = j * LANES
            rq = rope(proj[:, _C_RQ + o:_C_RQ + o + LANES], c_ret, s_ret, first_r, half_r)
            rk = rope(proj[:, _C_RK + o:_C_RK + o + LANES], c_ret, s_ret, first_r, half_r)
            rq_ref[0, rs, o:o + LANES] = rq.astype(BF16)
            rk_ref[0, rs, o:o + LANES] = (rk * (RET_DK ** -0.5)).astype(BF16)
        rv_ref[0, rs, :] = proj[:, _C_RV:_C_RV + RET_HEADS * RET_DV].astype(BF16)
        rg_ref[0, rs, :] = proj[:, _C_RG:_C_RG + RET_HEADS * RET_DV]


def _pad_heads(w, width, left):
    k = w.shape[0]
    w3 = w.reshape(k, -1, width)
    w3 = jnp.pad(w3, ((0, 0), (0, 0), (left, LANES - left - width)))
    return w3.reshape(k, -1)


def _pre(x, mod3, pos3, norm1_g, w_in, q_norm_g, w_uq, kv_norm_g, w_ukv, tm, tq):
    bsz, s, d = x.shape
    o = 0
    parts = {}
    for name, width in (("cq", Q_LORA), ("ckv", KV_LORA), ("kr", MLA_ROPE), ("rq", RET_HEADS * RET_DK),
                        ("rk", RET_HEADS * RET_DK), ("rv", RET_HEADS * RET_DV), ("rg", RET_HEADS * RET_DV)):
        parts[name] = w_in[:, o:o + width]
        o += width
    w_in_p = jnp.concatenate([
        parts["cq"], parts["ckv"], _pad_heads(parts["kr"], MLA_ROPE, MLA_NOPE),
        parts["rq"], parts["rk"], parts["rv"], parts["rg"]], axis=1).astype(BF16)
    assert w_in_p.shape[1] == _IN_PERM
    w_uq_p = _pad_heads(w_uq, MLA_NOPE + MLA_ROPE, 0).astype(BF16)
    wkv3 = w_ukv.reshape(KV_LORA, MLA_HEADS, MLA_NOPE + MLA_V)
    w_uk_p = _pad_heads(wkv3[:, :, :MLA_NOPE].reshape(KV_LORA, -1), MLA_NOPE, 0).astype(BF16)
    w_uv = wkv3[:, :, MLA_NOPE:].reshape(KV_LORA, -1).astype(BF16)
    half_r, half_m = RET_DK // 2, MLA_ROPE // 2
    f_r = ROPE_BASE ** (-(jnp.arange(half_r, dtype=F32) / half_r))
    f_m = ROPE_BASE ** (-(jnp.arange(half_m, dtype=F32) / half_m))
    invf = jnp.concatenate([f_r, f_r, f_m, f_m, jnp.zeros((LANES - 2 * half_r - 2 * half_m,), F32)]).reshape(1, LANES)

    hq = MLA_HEADS * LANES
    const = lambda shape: pl.BlockSpec(shape, lambda b, i: (0,) * len(shape))
    tile = lambda w: pl.BlockSpec((1, tm, w), lambda b, i: (b, i, 0))
    return pl.pallas_call(
        _pre_kernel,
        grid=(bsz, s // tm),
        in_specs=[tile(d), pl.BlockSpec((1, 6, d), lambda b, i: (b, 0, 0)), tile(1), const((1, d)),
                  const((d, _IN_PERM)), const((1, Q_LORA)), const((Q_LORA, hq)),
                  const((1, KV_LORA)), const((KV_LORA, hq)), const((KV_LORA, MLA_HEADS * MLA_V)), const((1, LANES))],
        out_specs=[tile(hq), tile(hq),
                   pl.BlockSpec((1, tm // tq, MLA_HEADS * MLA_VROWS, tq), lambda b, i: (b, i, 0, 0)),
                   tile(RET_HEADS * RET_DK), tile(RET_HEADS * RET_DK),
                   tile(RET_HEADS * RET_DV), tile(RET_HEADS * RET_DV)],
        out_shape=[jax.ShapeDtypeStruct((bsz, s, hq), BF16), jax.ShapeDtypeStruct((bsz, s, hq), BF16),
                   jax.ShapeDtypeStruct((bsz, s // tq, MLA_HEADS * MLA_VROWS, tq), BF16),
                   jax.ShapeDtypeStruct((bsz, s, RET_HEADS * RET_DK), BF16),
                   jax.ShapeDtypeStruct((bsz, s, RET_HEADS * RET_DK), BF16),
                   jax.ShapeDtypeStruct((bsz, s, RET_HEADS * RET_DV), BF16),
                   jax.ShapeDtypeStruct((bsz, s, RET_HEADS * RET_DV), F32)],
        compiler_params=_params("parallel", "arbitrary"),
        name="pre_mixer",
    )(x, mod3, pos3, norm1_g.reshape(1, d), w_in_p, q_norm_g.reshape(1, -1), w_uq_p,
      kv_norm_g.reshape(1, -1), w_uk_p, w_uv, invf)


def _attn_kernel(q_ref, k_ref, vt_ref, o_ref, *, tq, hps):
    qi = pl.program_id(1)
    key = lax.broadcasted_iota(jnp.int32, (tq, tq), 0)
    qry = lax.broadcasted_iota(jnp.int32, (tq, tq), 1)
    hsl = [slice(hh * LANES, (hh + 1) * LANES) for hh in range(hps)]
    vsl = [slice(hh * MLA_VROWS, (hh + 1) * MLA_VROWS) for hh in range(hps)]
    qs = [q_ref[0, :, hs] for hs in hsl]

    def heads(carry, blk, nblk, masked=False):
        start = pl.multiple_of(blk * tq, tq)
        sts, bms, ps, stats, out = {}, {}, {}, {}, {}
        for step in range(hps + ATTN_LOOKAHEAD + 1):
            if step < hps:
                st = _mm_nt(k_ref[0, pl.ds(start, nblk * tq), hsl[step]], qs[step])
                if masked:
                    st = jnp.where(key <= qry, st, NEG)
                sts[step] = st
                bms[step] = jnp.max(st, axis=0, keepdims=True)
            hh = step - ATTN_LOOKAHEAD
            if 0 <= hh < hps:
                m, acc = carry[hh]
                st = sts.pop(hh)
                m_new = jnp.maximum(m, bms.pop(hh))
                ps[hh] = jnp.exp2(st - m_new).astype(BF16)
                stats[hh] = (m_new, jnp.exp2(m - m_new) * acc)
            hh = step - ATTN_LOOKAHEAD - 1
            if 0 <= hh < hps:
                m_new, acc = stats.pop(hh)
                p = ps.pop(hh)
                for c in range(nblk):
                    acc = acc + _mm(vt_ref[0, blk + c, vsl[hh], :], p[c * tq:(c + 1) * tq, :])
                out[hh] = (m_new, acc)
        return tuple(out[hh] for hh in range(hps))

    init = (jnp.full((1, tq), NEG, F32), jnp.zeros((MLA_VROWS, tq), F32))
    carry = (init,) * hps
    done = 0
    for n in ATTN_BLOCK_GROUPS:
        cnt = (qi - done) // n
        carry = lax.fori_loop(0, cnt, lambda j, c, n=n, done=done: heads(c, done + j * n, n), carry)
        done = done + cnt * n
    carry = heads(carry, qi, 1, masked=True)
    out_t = jnp.concatenate([acc[0:MLA_V] / acc[MLA_V:MLA_V + 1] for _, acc in carry], axis=0)
    o_ref[0] = out_t.T.astype(BF16)


def _attention(q, k, vt, tq, hps):
    bsz, s, _ = q.shape
    groups = MLA_HEADS // hps
    assert vt.shape == (bsz, s // tq, MLA_HEADS * MLA_VROWS, tq)
    return pl.pallas_call(
        functools.partial(_attn_kernel, tq=tq, hps=hps),
        grid=(bsz * groups, s // tq),
        in_specs=[pl.BlockSpec((1, tq, hps * LANES), lambda g, i: (g // groups, i, g % groups)),
                  pl.BlockSpec((1, s, hps * LANES), lambda g, i: (g // groups, 0, g % groups)),
                  pl.BlockSpec((1, s // tq, hps * MLA_VROWS, tq), lambda g, i: (g // groups, 0, g % groups, 0))],
        out_specs=pl.BlockSpec((1, tq, hps * MLA_V), lambda g, i: (g // groups, i, g % groups)),
        out_shape=jax.ShapeDtypeStruct((bsz, s, MLA_HEADS * MLA_V), BF16),
        compiler_params=_params("parallel", "arbitrary"),
        name="mla_attention",
    )(q, k, vt)


def _ret_kernel(rq_ref, rk_ref, rv_ref, rg_ref, dm_ref, xi_ref, zt_ref, dc_ref, o_ref, st_ref, *, ts):
    @pl.when(pl.program_id(1) == 0)
    def _():
        st_ref[...] = jnp.zeros_like(st_ref)

    lane = lax.broadcasted_iota(jnp.int32, (RET_CHUNK, LANES), 1)
    units = [(c, hd) for c in range(ts // RET_CHUNK) for hd in range(RET_HEADS)]
    rows = lambda c: slice(c * RET_CHUNK, (c + 1) * RET_CHUNK)
    vsl = lambda hd: slice(hd * RET_DV, (hd + 1) * RET_DV)

    def head_lanes(ref, c, hd):
        pair, sub = divmod(hd, LANES // RET_DK)
        mine = (lane >= sub * RET_DK) & (lane < (sub + 1) * RET_DK)
        return jnp.where(mine, ref[0, rows(c), pair * LANES:(pair + 1) * LANES], 0.0).astype(BF16)

    qh = {u: head_lanes(rq_ref, *u) for u in units}
    kh = {u: head_lanes(rk_ref, *u) for u in units}
    vh = {(c, hd): rv_ref[0, rows(c), vsl(hd)] for c, hd in units}
    sc = {u: _mm_nt(qh[u], kh[u]) for u in units}
    un = {u: _mm((kh[u].astype(F32) * zt_ref[u[1]]).astype(BF16).T, vh[u]) for u in units}
    prev = {}
    for hd in range(RET_HEADS):
        st = st_ref[hd]
        for c in range(ts // RET_CHUNK):
            prev[(c, hd)] = st.astype(BF16)
            st = st * dc_ref[hd] + un[(c, hd)]
        st_ref[hd] = st
    for c, hd in units:
        u = (c, hd)
        o = _mm((sc[u] * dm_ref[hd]).astype(BF16), vh[u])
        o = o + _mm((qh[u].astype(F32) * xi_ref[hd]).astype(BF16), prev[u])
        mu = jnp.mean(o, axis=-1, keepdims=True)
        oc = o - mu
        on = oc * lax.rsqrt(jnp.mean(oc * oc, axis=-1, keepdims=True) + NORM_EPS)
        o_ref[0, rows(c), vsl(hd)] = (_silu(rg_ref[0, rows(c), vsl(hd)]) * on).astype(BF16)


def _retention(rq, rk, rv, rg, ts):
    bsz, s, _ = rq.shape
    c = RET_CHUNK
    gamma = 1.0 - jnp.power(2.0, -5.0 - jnp.arange(RET_HEADS, dtype=F32))
    log_g = jnp.log(gamma)
    idx = jnp.arange(c, dtype=F32)
    diff = idx[:, None] - idx[None, :]
    dmask = jnp.where(diff[None] >= 0, jnp.exp(jnp.maximum(diff, 0.0)[None] * log_g[:, None, None]), 0.0)
    zeta = jnp.exp((c - 1.0 - idx)[None, :] * log_g[:, None])
    xi = jnp.exp((idx + 1.0)[None, :] * log_g[:, None])
    decay = jnp.exp(c * log_g)
    xi_b = jnp.broadcast_to(xi[:, :, None], (RET_HEADS, c, LANES))
    zt_b = jnp.broadcast_to(zeta[:, :, None], (RET_HEADS, c, LANES))
    dc_b = jnp.broadcast_to(decay[:, None, None], (RET_HEADS, LANES, RET_DV))

    tile = lambda w: pl.BlockSpec((1, ts, w), lambda b, i: (b, i, 0))
    const = lambda shape: pl.BlockSpec(shape, lambda b, i: (0,) * len(shape))
    return pl.pallas_call(
        functools.partial(_ret_kernel, ts=ts),
        grid=(bsz, s // ts),
        in_specs=[tile(RET_HEADS * RET_DK), tile(RET_HEADS * RET_DK), tile(RET_HEADS * RET_DV), tile(RET_HEADS * RET_DV),
                  const((RET_HEADS, c, c)), const((RET_HEADS, c, LANES)), const((RET_HEADS, c, LANES)),
                  const((RET_HEADS, LANES, RET_DV))],
        out_specs=tile(RET_HEADS * RET_DV),
        out_shape=jax.ShapeDtypeStruct((bsz, s, RET_HEADS * RET_DV), BF16),
        scratch_shapes=[pltpu.VMEM((RET_HEADS, LANES, RET_DV), F32)],
        compiler_params=_params("parallel", "arbitrary"),
        name="retention",
    )(rq, rk, rv, rg, dmask, xi_b, zt_b, dc_b)


def _post_kernel(x_ref, om_ref, or_ref, mod_ref, wo_ref, g2_ref, wr_ref, br_ref,
                 x1_ref, h2_ref, ri_ref, rit_ref, cnt_ref):
    half = om_ref.shape[-1]
    subs = [slice(i * MOE_TILE, (i + 1) * MOE_TILE) for i in range(x_ref.shape[1] // MOE_TILE)]
    mixes = [_mm(om_ref[0, rs, :], wo_ref[0:half, :]) + _mm(or_ref[0, rs, :], wo_ref[half:, :]) for rs in subs]

    lgs = []
    for rs, mix in zip(subs, mixes):
        x1 = x_ref[0, rs, :] + mod_ref[0, 2:3, :] * mix
        x1_ref[0, rs, :] = x1
        h2 = _rms(x1, g2_ref[...]) * (1.0 + mod_ref[0, 4:5, :]) + mod_ref[0, 3:4, :]
        hi = h2.astype(BF16)
        h2_ref[0, rs, :] = hi
        lo = (h2 - hi.astype(F32)).astype(BF16)
        both = _mm(hi, wr_ref[...])
        lgs.append(both[:, 0:LANES] + both[:, LANES:2 * LANES] + _mm(lo, wr_ref[:, 0:LANES]) + br_ref[...])

    lane = lax.broadcasted_iota(jnp.int32, (MOE_TILE, LANES), 1)
    big = jnp.int32(1 << 20)
    gmask = lane < N_GROUPS
    el = lane - N_GROUPS
    assert EXPERTS_PER_GROUP == 8
    routed = []
    for lg in lgs:
        gmax = jnp.max(jnp.where(gmask, lg, NEG), axis=-1, keepdims=True)
        ge = jnp.where(gmask, jnp.exp(lg - gmax), 0.0)
        pg = ge / jnp.sum(ge, axis=-1, keepdims=True)
        p_top = jnp.max(pg, axis=-1, keepdims=True)
        g_top = jnp.min(jnp.where(gmask & (pg == p_top), lane, big), axis=-1, keepdims=True)

        emask = (el >= 0) & (el < N_EXPERTS) & (lax.shift_right_arithmetic(el, 3) == g_top)
        ev = jnp.where(emask, lg, NEG)
        v1 = jnp.max(ev, axis=-1, keepdims=True)
        i1 = jnp.min(jnp.where(emask & (ev == v1), lane, big), axis=-1, keepdims=True)
        emask2 = emask & (lane != i1)
        ev2 = jnp.where(emask2, lg, NEG)
        v2 = jnp.max(ev2, axis=-1, keepdims=True)
        i2 = jnp.min(jnp.where(emask2 & (ev2 == v2), lane, big), axis=-1, keepdims=True)
        e = jnp.exp(v2 - v1)
        den = 1.0 + e
        routed.append((i1, i2, (1.0 / den) * p_top, (e / den) * p_top))

    r_io = lax.broadcasted_iota(jnp.int32, (MOE_TILE, MOE_TILE), 0)
    c_io = lax.broadcasted_iota(jnp.int32, (MOE_TILE, MOE_TILE), 1)
    earlier_tok = (c_io < r_io).astype(BF16)
    lr_io = lax.broadcasted_iota(jnp.int32, (LANES, LANES), 0)
    lc_io = lax.broadcasted_iota(jnp.int32, (LANES, LANES), 1)
    earlier_lane = (lr_io < lc_io).astype(BF16)
    for hf, (rs, (i1, i2, w1, w2)) in enumerate(zip(subs, routed)):
        oh = [lane == i1, lane == i2]
        cnt = (oh[0] | oh[1]).astype(BF16)
        excl = _mm(earlier_tok, cnt)
        n = jnp.sum(cnt.astype(F32), axis=0, keepdims=True)
        npad = jnp.floor((n + (MOE_CHUNK - 1)) * (1.0 / MOE_CHUNK)) * MOE_CHUNK
        loff = _mm(jnp.broadcast_to(npad, (8, LANES)).astype(BF16), earlier_lane)
        pos = excl + loff[0:1, :]
        cnt_ref[hf] = jnp.broadcast_to(n, (8, LANES))
        lp0, lp1 = [jnp.sum(jnp.where(o, pos, 0.0), axis=-1, keepdims=True) for o in oh]
        cols = [(i1 - N_GROUPS).astype(F32), (i2 - N_GROUPS).astype(F32), w1, w2, lp0, lp1]
        ri = jnp.zeros((MOE_TILE, LANES), F32)
        for j, col in enumerate(cols):
            ri = jnp.where(lane == j, col, ri)
        ri_ref[0, rs, :] = ri
        rit_ref[:, rs] = ri.T


def _post(x, o_mla, o_ret, mod3, w_o, norm2_g, w_gr, b_gr, w_er, b_er, tm):
    bsz, s, d = x.shape
    w_r = jnp.concatenate([w_gr, w_er.reshape(d, N_EXPERTS), jnp.zeros((d, LANES - N_GROUPS - N_EXPERTS), F32)], axis=1)
    w_rh = w_r.astype(BF16)
    w_rl = (w_r - w_rh.astype(F32)).astype(BF16)
    w_r2 = jnp.concatenate([w_rh, w_rl], axis=1)
    b_r =jnp.concatenate([b_gr, b_er.reshape(-1), jnp.zeros((LANES - N_GROUPS - N_EXPERTS,), F32)]).reshape(1, LANES)
    tile = lambda w: pl.BlockSpec((1, tm, w), lambda b, i: (b, i, 0))
    const = lambda shape: pl.BlockSpec(shape, lambda b, i: (0,) * len(shape))
    per_b = s // tm
    sub = tm // MOE_TILE
    return pl.pallas_call(
        _post_kernel,
        grid=(bsz, per_b),
        in_specs=[tile(d), tile(o_mla.shape[-1]), tile(o_ret.shape[-1]), pl.BlockSpec((1, 6, d), lambda b, i: (b, 0, 0)),
                  const((d, d)), const((1, d)), const((d, 2 * LANES)), const((1, LANES))],
        out_specs=[tile(d), tile(d), tile(LANES),
                   pl.BlockSpec((LANES, tm), lambda b, i: (0, b * per_b + i)),
                   pl.BlockSpec((sub, 8, LANES), lambda b, i: (b * per_b + i, 0, 0))],
        out_shape=[jax.ShapeDtypeStruct((bsz, s, d), F32), jax.ShapeDtypeStruct((bsz, s, d), BF16),
                   jax.ShapeDtypeStruct((bsz, s, LANES), F32),
                   jax.ShapeDtypeStruct((LANES, bsz * s), F32),
                   jax.ShapeDtypeStruct((bsz * s // MOE_TILE, 8, LANES), F32)],
        compiler_params=_params("parallel", "arbitrary"),
        name="post_mixer",
    )(x, o_mla, o_ret, mod3, w_o.astype(BF16), norm2_g.reshape(1, d), w_r2, b_r)


def _chunk_rows(c):
    return pl.ds(pl.multiple_of(c * MOE_CHUNK, MOE_CHUNK), MOE_CHUNK)


def _dispatch_kernel(dmap_ref, nchk_ref, tstart_ref, tn_ref, nbr_ref, h2_ref, ri_ref, rit_ref, xs_hbm,
                     xloc, zblk, sem, zsem, bsem, *, nt, nb):
    i = pl.program_id(0)
    slot = lax.rem(i, 2)
    d = h2_ref.shape[-1]

    def zero_copy(e, c):
        return pltpu.make_async_copy(zblk.at[pl.ds(0, MOE_CHUNK)], xs_hbm.at[_chunk_rows(tstart_ref[e] + c)], zsem)

    def zero_block(j):
        rows = pl.ds(pl.multiple_of(j * MOE_BLOCK, MOE_BLOCK), MOE_BLOCK)
        return pltpu.make_async_copy(zblk, xs_hbm.at[rows], bsem)

    def chunk_copy(t, c, sl):
        return pltpu.make_async_copy(xloc.at[sl].at[_chunk_rows(c)],
                                     xs_hbm.at[_chunk_rows(dmap_ref[t * MOE_TILE_CHUNKS + c])], sem.at[sl])

    def wait_tile(t, sl):
        lax.fori_loop(0, nchk_ref[t], lambda c, z: (chunk_copy(t, c, sl).wait(), z)[1], 0)

    @pl.when(i == 0)
    def _():
        zblk[...] = jnp.zeros_like(zblk)
        lax.fori_loop(nbr_ref[0], nb, lambda j, z: (zero_block(j).start(), z)[1], 0)
        for e in range(N_EXPERTS):
            lax.fori_loop(0, tn_ref[e], lambda c, z, e=e: (zero_copy(e, c).start(), z)[1], 0)
        for e in range(N_EXPERTS):
            lax.fori_loop(0, tn_ref[e], lambda c, z, e=e: (zero_copy(e, c).wait(), z)[1], 0)

    @pl.when(i >= 2)
    def _():
        wait_tile(i - 2, slot)

    s_io = lax.broadcasted_iota(jnp.int32, (MOE_SLAB, MOE_TILE), 0).astype(F32)
    pm = [(s_io == rit_ref[4 + k:5 + k, :]).astype(BF16) for k in range(TOP_K)]
    xloc[slot, :, 0:d] = _mm(pm[0] + pm[1], h2_ref[...]).astype(BF16)
    lane = lax.broadcasted_iota(jnp.int32, (MOE_TILE, LANES), 1)
    wx = jnp.zeros((MOE_SLAB, LANES), F32)
    for k in range(TOP_K):
        w = ri_ref[:, TOP_K + k:TOP_K + k + 1]
        hi = w.astype(BF16).astype(F32)
        wx = wx + _mm(pm[k], jnp.where(lane == 0, hi, jnp.where(lane == 1, w - hi, 0.0)).astype(BF16))
    xloc[slot, :, d:d + LANES] = wx.astype(BF16)

    lax.fori_loop(0, nchk_ref[i], lambda c, z: (chunk_copy(i, c, slot).start(), z)[1], 0)

    @pl.when(i == nt - 1)
    def _():
        wait_tile(i, slot)
        if nt >= 2:
            wait_tile(i - 1, 1 - slot)
        lax.fori_loop(nbr_ref[0], nb, lambda j, z: (zero_block(j).wait(), z)[1], 0)


def _dispatch(h2, ri, rit, tables, n_rows):
    t, d = h2.shape
    nt = t // MOE_TILE
    assert MOE_SLAB >= TOP_K * MOE_TILE + N_EXPERTS * (MOE_CHUNK - 1)
    grid_spec = pltpu.PrefetchScalarGridSpec(
        num_scalar_prefetch=5,
        grid=(nt,),
        in_specs=[pl.BlockSpec((MOE_TILE, d), lambda i, *_: (i, 0)),
                  pl.BlockSpec((MOE_TILE, LANES), lambda i, *_: (i, 0)),
                  pl.BlockSpec((8, MOE_TILE), lambda i, *_: (0, i))],
        out_specs=pl.BlockSpec(memory_space=pl.ANY),
        scratch_shapes=[pltpu.VMEM((2, MOE_SLAB, d + LANES), BF16), pltpu.VMEM((MOE_BLOCK, d + LANES), BF16),
                        pltpu.SemaphoreType.DMA((2,)), pltpu.SemaphoreType.DMA(()), pltpu.SemaphoreType.DMA(())])
    return pl.pallas_call(
        functools.partial(_dispatch_kernel, nt=nt, nb=n_rows // MOE_BLOCK),
        grid_spec=grid_spec,
        out_shape=jax.ShapeDtypeStruct((n_rows, d + LANES), BF16),
        compiler_params=_params("arbitrary"),
        name="moe_dispatch",
    )(tables["dmap"], tables["tile_chunks"], tables["tail_start"], tables["tail_n"], tables["n_blocks"],
      h2, ri, rit)


def _expert_kernel(be_ref, nbr_ref, xs_ref, w1_ref, w3_ref, w2_ref, y_ref):
    used = pl.program_id(0) < nbr_ref[0]

    @pl.when(jnp.logical_not(used))
    def _():
        y_ref[...] = jnp.zeros_like(y_ref)

    @pl.when(used)
    def _():
        d = y_ref.shape[-1]
        xb = xs_ref[:, 0:d]
        gw = xs_ref[:, d:d + LANES].astype(F32)
        gate = gw[:, 0:1] + gw[:, 1:2]
        a = _mm(xb, w1_ref[0])
        b = _mm(xb, w3_ref[0])
        hm = (_silu(a) * b).astype(BF16)
        y_ref[...] = (_mm(hm, w2_ref[0]) * gate).astype(BF16)


def _experts(xs, tables, w1, w3, w2):
    n_rows, dw = xs.shape
    d = dw - LANES
    nb = n_rows // MOE_BLOCK
    blk = lambda i, be, nbr: jnp.minimum(i, nbr[0] - 1)
    grid_spec = pltpu.PrefetchScalarGridSpec(
        num_scalar_prefetch=2,
        grid=(nb,),
        in_specs=[pl.BlockSpec((MOE_BLOCK, dw), lambda i, be, nbr: (blk(i, be, nbr), 0)),
                  pl.BlockSpec((1, d, D_EXPERT), lambda i, be, nbr: (be[blk(i, be, nbr)], 0, 0)),
                  pl.BlockSpec((1, d, D_EXPERT), lambda i, be, nbr: (be[blk(i, be, nbr)], 0, 0)),
                  pl.BlockSpec((1, D_EXPERT, d), lambda i, be, nbr: (be[blk(i, be, nbr)], 0, 0))],
        out_specs=pl.BlockSpec((MOE_BLOCK, d), lambda i, be, nbr: (i, 0)))
    return pl.pallas_call(
        _expert_kernel,
        grid_spec=grid_spec,
        out_shape=jax.ShapeDtypeStruct((n_rows, d), BF16),
        compiler_params=_params("arbitrary"),
        name="moe_experts",
    )(tables["blk_expert"], tables["n_blocks"], xs, w1.astype(BF16), w3.astype(BF16), w2.astype(BF16))


def _moe_rows(n_tiles):
    worst = n_tiles * (TOP_K * MOE_TILE + N_EXPERTS * (MOE_CHUNK - 1)) + N_EXPERTS * (MOE_BLOCK - MOE_CHUNK)
    return -(-worst // MOE_BLOCK) * MOE_BLOCK


def _moe_tables(cnt, n_rows):
    n_tiles = cnt.shape[0]
    per_blk = MOE_BLOCK // MOE_CHUNK

    def excl_cumsum(a, axis):
        n = a.shape[axis]
        lower = jnp.arange(n)[:, None] > jnp.arange(n)[None, :]
        if axis == 0:
            return jnp.sum(jnp.where(lower[:, :, None], a[None, :, :], 0), axis=1)
        return jnp.sum(jnp.where(lower[None, :, :], a[:, None, :], 0), axis=2)

    nch = (cnt + MOE_CHUNK - 1) // MOE_CHUNK
    loff = excl_cumsum(nch, 1)
    seg = jnp.sum(nch, axis=0)
    blocks = (seg + per_blk - 1) // per_blk
    bstart = excl_cumsum(blocks[None, :], 1)[0]
    bend = bstart + blocks
    estart = bstart * per_blk
    gbase = estart[None, :] + excl_cumsum(nch, 0)
    c = jnp.arange(MOE_TILE_CHUNKS, dtype=jnp.int32)
    owner = jnp.sum((c[None, :, None] >= (loff + nch)[:, None, :]).astype(jnp.int32), axis=-1)
    owner = jnp.minimum(owner, N_EXPERTS - 1)
    is_owner = owner[:, :, None] == jnp.arange(N_EXPERTS, dtype=jnp.int32)[None, None, :]
    dmap = c[None, :] + jnp.sum(jnp.where(is_owner, (gbase - loff)[:, None, :], 0), axis=-1)
    dmap = jnp.clip(dmap, 0, n_rows // MOE_CHUNK - 1)
    nb = n_rows // MOE_BLOCK
    blk_expert = jnp.sum((jnp.arange(nb, dtype=jnp.int32)[:, None] >= bend[None, :]).astype(jnp.int32), axis=1)
    blk_expert = jnp.minimum(blk_expert, N_EXPERTS - 1)
    return dict(dmap=dmap.reshape(-1).astype(jnp.int32), tile_chunks=jnp.sum(nch, axis=1).astype(jnp.int32),
                tail_start=(estart + seg).astype(jnp.int32), tail_n=(blocks * per_blk - seg).astype(jnp.int32),
                blk_expert=blk_expert.astype(jnp.int32), n_blocks=bend[-1:].astype(jnp.int32))


def _final_kernel(dmap_ref, nchk_ref, x1_ref, ri_ref, mod_ref, g_ref, y_hbm, o_ref, yloc, sem, *, nt):
    i = pl.program_id(0)
    slot = lax.rem(i, 2)

    def chunk_copy(t, c, sl):
        return pltpu.make_async_copy(y_hbm.at[_chunk_rows(dmap_ref[t * MOE_TILE_CHUNKS + c])],
                                     yloc.at[sl].at[_chunk_rows(c)], sem.at[sl])

    def gather(t, sl):
        lax.fori_loop(0, nchk_ref[t], lambda c, z: (chunk_copy(t, c, sl).start(), z)[1], 0)

    @pl.when(i == 0)
    def _():
        yloc[...] = jnp.zeros_like(yloc)
        gather(0, 0)

    @pl.when(i + 1 < nt)
    def _():
        gather(i + 1, 1 - slot)

    lax.fori_loop(0, nchk_ref[i], lambda c, z: (chunk_copy(i, c, slot).wait(), z)[1], 0)

    l_io = lax.broadcasted_iota(jnp.int32, (MOE_TILE, MOE_SLAB), 1).astype(F32)
    pick = ((l_io == ri_ref[:, 4:5]) | (l_io == ri_ref[:, 5:6])).astype(BF16)
    moe = _mm(pick, yloc[slot])
    x2 = x1_ref[...] + mod_ref[0, 5:6, :] * moe
    o_ref[...] = _rms(x2, g_ref[...])


def _final(x1, ri, y, mod3, final_g, tables, s):
    t, d = x1.shape
    nt = t // MOE_TILE
    per_b = s // MOE_TILE
    grid_spec = pltpu.PrefetchScalarGridSpec(
        num_scalar_prefetch=2,
        grid=(nt,),
        in_specs=[pl.BlockSpec((MOE_TILE, d), lambda i, *_: (i, 0)),
                  pl.BlockSpec((MOE_TILE, LANES), lambda i, *_: (i, 0)),
                  pl.BlockSpec((1, 6, d), lambda i, *_: (i // per_b, 0, 0)),
                  pl.BlockSpec((1, d), lambda i, *_: (0, 0)),
                  pl.BlockSpec(memory_space=pl.ANY)],
        out_specs=pl.BlockSpec((MOE_TILE, d), lambda i, *_: (i, 0)),
        scratch_shapes=[pltpu.VMEM((2, MOE_SLAB, d), BF16), pltpu.SemaphoreType.DMA((2,))])
    return pl.pallas_call(
        functools.partial(_final_kernel, nt=nt),
        grid_spec=grid_spec,
        out_shape=jax.ShapeDtypeStruct((t, d), F32),
        compiler_params=_params("arbitrary"),
        name="moe_combine_final",
    )(tables["dmap"], tables["tile_chunks"], x1, ri, mod3, final_g.reshape(1, d), y)


def kernel(x, c, positions, w_ada, b_ada, norm1_g, w_in, q_norm_g, w_uq, kv_norm_g, w_ukv, w_o, norm2_g,
           w_gr, b_gr, w_er, b_er, w1, w3, w2, final_g):
    bsz, s, d = x.shape
    assert w_ada.shape[0] == 1, "one layer"
    tm = min(1024, s)
    tq = min(256, s)
    ts = min(1024, s)
    mod3 = _adaln(c, w_ada[0], b_ada[0]).reshape(bsz, 6, d)
    pos3 = positions.astype(F32).reshape(bsz, s, 1)
    q, k, vt, rq, rk, rv, rg = _pre(x, mod3, pos3, norm1_g[0], w_in[0], q_norm_g[0], w_uq[0], kv_norm_g[0], w_ukv[0],
                                    tm, tq)
    o_mla = _attention(q, k, vt, tq, ATTN_HEADS_PER_STEP)
    o_ret = _retention(rq, rk, rv, rg, ts)
    x1, h2, ri, rit, cnt = _post(x, o_mla, o_ret, mod3, w_o[0], norm2_g[0], w_gr[0], b_gr[0], w_er[0], b_er[0], tm)
    t = bsz * s
    n_rows = _moe_rows(t // MOE_TILE)
    counts = cnt[:, 0, N_GROUPS:N_GROUPS + N_EXPERTS].astype(jnp.int32)
    tables = _moe_tables(counts, n_rows)
    ri2 = ri.reshape(t, LANES)
    xs = _dispatch(h2.reshape(t, d), ri2, rit, tables, n_rows)
    y = _experts(xs, tables, w1[0], w3[0], w2[0])
    out = _final(x1.reshape(t, d), ri2, y, mod3, final_g, tables, s)
    return out.reshape(bsz, s, d)
```

```python
import functools

import jax
import jax.numpy as jnp
from jax import lax
from jax.experimental import pallas as pl
from jax.experimental.pallas import tpu as pltpu

MLA_HEADS = 8
MLA_NOPE = 64
MLA_ROPE = 32
MLA_V = 64
Q_LORA = 256
KV_LORA = 128
RET_HEADS = 4
RET_DK = 64
RET_DV = 128
RET_CHUNK = 128
ROPE_BASE = 10000.0
NORM_EPS = 1e-6
N_GROUPS = 4
EXPERTS_PER_GROUP = 8
N_EXPERTS = N_GROUPS * EXPERTS_PER_GROUP
TOP_K = 2
D_EXPERT = 256
MOE_TILE = 256
MOE_CHUNK = 16
MOE_SLAB = 1024
MOE_TILE_CHUNKS = MOE_SLAB // MOE_CHUNK
MOE_BLOCK = 512

LANES = 128
VMEM_LIMIT = 56 * 1024 * 1024

F32 = jnp.float32
BF16 = jnp.bfloat16
NEG = float(jnp.finfo(jnp.float32).min)
LOG2_E = 1.4426950408889634
ATTN_HEADS_PER_STEP = 8
ATTN_PAIRS_PER_ITER = 2
MLA_VROWS = MLA_V + 16

_C_Q = 0
_C_KV = _C_Q + Q_LORA
_C_KPE = _C_KV + KV_LORA
_C_RQ = _C_KPE + LANES
_C_RK = _C_RQ + RET_HEADS * RET_DK
_C_RV = _C_RK + RET_HEADS * RET_DK
_C_RG = _C_RV + RET_HEADS * RET_DV
_IN_PERM = _C_RG + RET_HEADS * RET_DV


def _silu(v):
    return v / (1.0 + jnp.exp(-v))


def _mm(a, b):
    return jnp.dot(a, b, preferred_element_type=F32)


def _mm_nt(a, b):
    return lax.dot_general(a, b, (((1,), (1,)), ((), ())), preferred_element_type=F32)


def _params(*sem):
    return pltpu.CompilerParams(dimension_semantics=sem, vmem_limit_bytes=VMEM_LIMIT)


def _adaln_kernel(c_ref, w_ref, b_ref, o_ref):
    a = _silu(c_ref[...]).astype(BF16)
    o_ref[...] = _mm(a, w_ref[...].astype(BF16)) + b_ref[...]


def _adaln(c, w_ada, b_ada):
    bsz, d = c.shape
    n = w_ada.shape[1]
    tn = d
    return pl.pallas_call(
        _adaln_kernel,
        grid=(n // tn,),
        in_specs=[pl.BlockSpec((bsz, d), lambda j: (0, 0)),
                  pl.BlockSpec((d, tn), lambda j: (0, j)),
                  pl.BlockSpec((1, tn), lambda j: (0, j))],
        out_specs=pl.BlockSpec((bsz, tn), lambda j: (0, j)),
        out_shape=jax.ShapeDtypeStruct((bsz, n), F32),
        compiler_params=_params("arbitrary"),
        name="adaln",
    )(c, w_ada, b_ada.reshape(1, n))


def _rms(v, g):
    return v * lax.rsqrt(jnp.mean(v * v, axis=-1, keepdims=True) + NORM_EPS) * g


def _pre_kernel(x_ref, mod_ref, pos_ref, g1_ref, win_ref, qg_ref, wuq_ref, kvg_ref, wuk_ref,
                wuv_ref, invf_ref, q_ref, k_ref, v_ref, rq_ref, rk_ref, rv_ref, rg_ref):
    tq = v_ref.shape[-1]
    subs = [slice(i * tq, (i + 1) * tq) for i in range(v_ref.shape[1])]
    sh1 = mod_ref[0, 0:1, :]
    sc1 = mod_ref[0, 1:2, :]
    projs = [_mm((_rms(x_ref[0, rs, :], g1_ref[...]) * (1.0 + sc1) + sh1).astype(BF16), win_ref[...]) for rs in subs]

    cqs = [_rms(p[:, _C_Q:_C_Q + Q_LORA], qg_ref[...]).astype(BF16) for p in projs]
    ckvs = [_rms(p[:, _C_KV:_C_KV + KV_LORA], kvg_ref[...]).astype(BF16) for p in projs]
    qas = [_mm(cq, wuq_ref[...]) for cq in cqs]
    kns = [_mm(ckv, wuk_ref[...]) for ckv in ckvs]
    vvs = [_mm(ckv, wuv_ref[...]) for ckv in ckvs]

    lane = lax.broadcasted_iota(jnp.int32, (tq, LANES), 1)
    hi = lane >= RET_DK
    half_m, half_r = MLA_ROPE // 2, RET_DK // 2
    first_m = hi & (lane < RET_DK + half_m)
    first_r = (lane & half_r) == 0
    scale = (MLA_NOPE + MLA_ROPE) ** -0.5 * LOG2_E

    def rope(v, cos, sin, first, half):
        partner = jnp.where(first, pltpu.roll(v, LANES - half, 1), pltpu.roll(v, half, 1))
        return v * cos + partner * sin

    for i, rs in enumerate(subs):
        proj = projs[i]
        ang = pos_ref[0, rs, :] * invf_ref[...]
        cs = jnp.cos(ang)
        sn = jnp.sin(ang)
        c_mla = jnp.where(hi, cs, 1.0)
        s_mla = jnp.where(hi, jnp.where(first_m, -sn, sn), 0.0)
        c_ret = jnp.where(hi, pltpu.roll(cs, RET_DK, 1), cs)
        s_ret = jnp.where(hi, pltpu.roll(sn, RET_DK, 1), sn)
        s_ret = jnp.where(first_r, -s_ret, s_ret)

        kpe = rope(proj[:, _C_KPE:_C_KPE + LANES], c_mla, s_mla, first_m, half_m)
        for hd in range(MLA_HEADS):
            sl = slice(hd * LANES, (hd + 1) * LANES)
            q_ref[0, rs, sl] = (rope(qas[i][:, sl], c_mla, s_mla, first_m, half_m) * scale).astype(BF16)
            k_ref[0, rs, sl] = (kns[i][:, sl] + kpe).astype(BF16)
        vt = vvs[i].T
        tail = jnp.where(lax.broadcasted_iota(jnp.int32, (MLA_VROWS - MLA_V, tq), 0) == 0, 1.0, 0.0)
        slab = [piece for hd in range(MLA_HEADS) for piece in (vt[hd * MLA_V:(hd + 1) * MLA_V, :], tail)]
        v_ref[0, i] = jnp.concatenate(slab, axis=0).astype(BF16)

        for j in range(RET_HEADS * RET_DK // LANES):
            o = j * LANES
            rq = rope(proj[:, _C_RQ + o:_C_RQ + o + LANES], c_ret, s_ret, first_r, half_r)
            rk = rope(proj[:, _C_RK + o:_C_RK + o + LANES], c_ret, s_ret, first_r, half_r)
            rq_ref[0, rs, o:o + LANES] = rq.astype(BF16)
            rk_ref[0, rs, o:o + LANES] = (rk * (RET_DK ** -0.5)).astype(BF16)
        rv_ref[0, rs, :] = proj[:, _C_RV:_C_RV + RET_HEADS * RET_DV].astype(BF16)
        rg_ref[0, rs, :] = proj[:, _C_RG:_C_RG + RET_HEADS * RET_DV]


def _pad_heads(w, width, left):
    k = w.shape[0]
    w3 = w.reshape(k, -1, width)
    w3 = jnp.pad(w3, ((0, 0), (0, 0), (left, LANES - left - width)))
    return w3.reshape(k, -1)


def _pre(x, mod3, pos3, norm1_g, w_in, q_norm_g, w_uq, kv_norm_g, w_ukv, tm, tq):
    bsz, s, d = x.shape
    o = 0
    parts = {}
    for name, width in (("cq", Q_LORA), ("ckv", KV_LORA), ("kr", MLA_ROPE), ("rq", RET_HEADS * RET_DK),
                        ("rk", RET_HEADS * RET_DK), ("rv", RET_HEADS * RET_DV), ("rg", RET_HEADS * RET_DV)):
        parts[name] = w_in[:, o:o + width]
        o += width
    w_in_p = jnp.concatenate([
        parts["cq"], parts["ckv"], _pad_heads(parts["kr"], MLA_ROPE, MLA_NOPE),
        parts["rq"], parts["rk"], parts["rv"], parts["rg"]], axis=1).astype(BF16)
    assert w_in_p.shape[1] == _IN_PERM
    w_uq_p = _pad_heads(w_uq, MLA_NOPE + MLA_ROPE, 0).astype(BF16)
    wkv3 = w_ukv.reshape(KV_LORA, MLA_HEADS, MLA_NOPE + MLA_V)
    w_uk_p = _pad_heads(wkv3[:, :, :MLA_NOPE].reshape(KV_LORA, -1), MLA_NOPE, 0).astype(BF16)
    w_uv = wkv3[:, :, MLA_NOPE:].reshape(KV_LORA, -1).astype(BF16)
    half_r, half_m = RET_DK // 2, MLA_ROPE // 2
    f_r = ROPE_BASE ** (-(jnp.arange(half_r, dtype=F32) / half_r))
    f_m = ROPE_BASE ** (-(jnp.arange(half_m, dtype=F32) / half_m))
    invf = jnp.concatenate([f_r, f_r, f_m, f_m, jnp.zeros((LANES - 2 * half_r - 2 * half_m,), F32)]).reshape(1, LANES)

    hq = MLA_HEADS * LANES
    const = lambda shape: pl.BlockSpec(shape, lambda b, i: (0,) * len(shape))
    tile = lambda w: pl.BlockSpec((1, tm, w), lambda b, i: (b, i, 0))
    return pl.pallas_call(
        _pre_kernel,
        grid=(bsz, s // tm),
        in_specs=[tile(d), pl.BlockSpec((1, 6, d), lambda b, i: (b, 0, 0)), tile(1), const((1, d)),
                  const((d, _IN_PERM)), const((1, Q_LORA)), const((Q_LORA, hq)),
                  const((1, KV_LORA)), const((KV_LORA, hq)), const((KV_LORA, MLA_HEADS * MLA_V)), const((1, LANES))],
        out_specs=[tile(hq), tile(hq),
                   pl.BlockSpec((1, tm // tq, MLA_HEADS * MLA_VROWS, tq), lambda b, i: (b, i, 0, 0)),
                   tile(RET_HEADS * RET_DK), tile(RET_HEADS * RET_DK),
                   tile(RET_HEADS * RET_DV), tile(RET_HEADS * RET_DV)],
        out_shape=[jax.ShapeDtypeStruct((bsz, s, hq), BF16), jax.ShapeDtypeStruct((bsz, s, hq), BF16),
                   jax.ShapeDtypeStruct((bsz, s // tq, MLA_HEADS * MLA_VROWS, tq), BF16),
                   jax.ShapeDtypeStruct((bsz, s, RET_HEADS * RET_DK), BF16),
                   jax.ShapeDtypeStruct((bsz, s, RET_HEADS * RET_DK), BF16),
                   jax.ShapeDtypeStruct((bsz, s, RET_HEADS * RET_DV), BF16),
                   jax.ShapeDtypeStruct((bsz, s, RET_HEADS * RET_DV), F32)],
        compiler_params=_params("parallel", "arbitrary"),
        name="pre_mixer",
    )(x, mod3, pos3, norm1_g.reshape(1, d), w_in_p, q_norm_g.reshape(1, -1), w_uq_p,
      kv_norm_g.reshape(1, -1), w_uk_p, w_uv, invf)


def _attn_kernel(q_ref, k_ref, vt_ref, o_ref, st_x, st_y, bm_x, bm_y, *, tq, kb, hps):
    qi = pl.program_id(1)
    assert tq == 2 * kb, "a query tile spans two key blocks: the last two blocks of a tile are masked"
    key = lax.broadcasted_iota(jnp.int32, (kb, tq), 0)
    qry = lax.broadcasted_iota(jnp.int32, (kb, tq), 1)
    hsl = [slice(hh * LANES, (hh + 1) * LANES) for hh in range(hps)]
    vsl = [slice(hh * MLA_VROWS, (hh + 1) * MLA_VROWS) for hh in range(hps)]
    qs = [q_ref[0, :, hs] for hs in hsl]
    bufs = {"x": (st_x, bm_x), "y": (st_y, bm_y)}
    n_blk = 2 * qi + 2

    def scores(blk, buf, diag=None):
        st_ref, bm_ref = bufs[buf]
        start = pl.multiple_of(blk * kb, kb)
        for hh in range(hps):
            st = _mm_nt(k_ref[0, pl.ds(start, kb), hsl[hh]], qs[hh])
            if diag is not None:
                st = jnp.where(key + diag * kb <= qry, st, NEG)
            st_ref[hh] = st
            bm_ref[hh, 0:1, :] = jnp.max(st, axis=0, keepdims=True)

    def update(blk, buf, carry):
        st_ref, bm_ref = bufs[buf]
        out = []
        for hh in range(hps):
            m, acc = carry[hh]
            m_new = jnp.maximum(m, bm_ref[hh, 0:1, :])
            p = jnp.exp2(st_ref[hh] - m_new).astype(BF16)
            out.append((m_new, jnp.exp2(m - m_new) * acc + _mm(vt_ref[0, blk, vsl[hh], :], p)))
        return tuple(out)

    def when_loop(pred, body, carry):
        return lax.fori_loop(0, pred.astype(jnp.int32), lambda _, c: body(c), carry)

    init = (jnp.full((1, tq), NEG, F32), jnp.zeros((MLA_VROWS, tq), F32))
    carry = (init,) * hps

    @pl.when(qi >= 1)
    def _():
        scores(0, "x")

    def steady(base, pairs, c):
        for r in range(pairs):
            scores(base + 2 * r + 1, "y")
            c = update(base + 2 * r, "x", c)
            scores(base + 2 * r + 2, "x")
            c = update(base + 2 * r + 1, "y", c)
        return c

    n_pairs = jnp.maximum(qi - 1, 0)
    n_long = n_pairs // ATTN_PAIRS_PER_ITER
    carry = lax.fori_loop(0, n_long, lambda i, c: steady(2 * ATTN_PAIRS_PER_ITER * i, ATTN_PAIRS_PER_ITER, c), carry)
    done = n_long * ATTN_PAIRS_PER_ITER
    carry = lax.fori_loop(done, n_pairs, lambda i, c: steady(2 * i, 1, c), carry)

    def tail(c):
        scores(n_blk - 3, "y")
        c = update(n_blk - 4, "x", c)
        scores(n_blk - 2, "x", diag=0)
        c = update(n_blk - 3, "y", c)
        scores(n_blk - 1, "y", diag=1)
        c = update(n_blk - 2, "x", c)
        return update(n_blk - 1, "y", c)

    def tail_first(c):
        scores(0, "x", diag=0)
        scores(1, "y", diag=1)
        c = update(0, "x", c)
        return update(1, "y", c)

    carry = when_loop(qi >= 1, tail, carry)
    carry = when_loop(qi == 0, tail_first, carry)
    out_t = jnp.concatenate([acc[0:MLA_V] / acc[MLA_V:MLA_V + 1] for _, acc in carry], axis=0)
    o_ref[0] = out_t.T.astype(BF16)


def _attention(q, k, vt, kb, hps):
    bsz, s, _ = q.shape
    groups = MLA_HEADS // hps
    tq = 2 * kb
    assert vt.shape == (bsz, s // kb, MLA_HEADS * MLA_VROWS, kb)
    return pl.pallas_call(
        functools.partial(_attn_kernel, tq=tq, kb=kb, hps=hps),
        grid=(bsz * groups, s // tq),
        in_specs=[pl.BlockSpec((1, tq, hps * LANES), lambda g, i: (g // groups, i, g % groups)),
                  pl.BlockSpec((1, s, hps * LANES), lambda g, i: (g // groups, 0, g % groups)),
                  pl.BlockSpec((1, s // kb, hps * MLA_VROWS, kb), lambda g, i: (g // groups, 0, g % groups, 0))],
        out_specs=pl.BlockSpec((1, tq, hps * MLA_V), lambda g, i: (g // groups, i, g % groups)),
        out_shape=jax.ShapeDtypeStruct((bsz, s, MLA_HEADS * MLA_V), BF16),
        scratch_shapes=[pltpu.VMEM((hps, kb, tq), F32), pltpu.VMEM((hps, kb, tq), F32),
                        pltpu.VMEM((hps, 8, tq), F32), pltpu.VMEM((hps, 8, tq), F32)],
        compiler_params=_params("parallel", "arbitrary"),
        name="mla_attention",
    )(q, k, vt)


def _ret_kernel(rq_ref, rk_ref, rv_ref, rg_ref, dm_ref, xi_ref, zt_ref, dc_ref, o_ref, st_ref, *, ts):
    @pl.when(pl.program_id(1) == 0)
    def _():
        st_ref[...] = jnp.zeros_like(st_ref)

    lane = lax.broadcasted_iota(jnp.int32, (RET_CHUNK, LANES), 1)
    units = [(c, hd) for c in range(ts // RET_CHUNK) for hd in range(RET_HEADS)]
    rows = lambda c: slice(c * RET_CHUNK, (c + 1) * RET_CHUNK)
    vsl = lambda hd: slice(hd * RET_DV, (hd + 1) * RET_DV)

    def head_lanes(ref, c, hd):
        pair, sub = divmod(hd, LANES // RET_DK)
        mine = (lane >= sub * RET_DK) & (lane < (sub + 1) * RET_DK)
        return jnp.where(mine, ref[0, rows(c), pair * LANES:(pair + 1) * LANES], 0.0).astype(BF16)

    qh = {u: head_lanes(rq_ref, *u) for u in units}
    kh = {u: head_lanes(rk_ref, *u) for u in units}
    vh = {(c, hd): rv_ref[0, rows(c), vsl(hd)] for c, hd in units}
    sc = {u: _mm_nt(qh[u], kh[u]) for u in units}
    un = {u: _mm((kh[u].astype(F32) * zt_ref[u[1]]).astype(BF16).T, vh[u]) for u in units}
    prev = {}
    for hd in range(RET_HEADS):
        st = st_ref[hd]
        for c in range(ts // RET_CHUNK):
            prev[(c, hd)] = st.astype(BF16)
            st = st * dc_ref[hd] + un[(c, hd)]
        st_ref[hd] = st
    for c, hd in units:
        u = (c, hd)
        o = _mm((sc[u] * dm_ref[hd]).astype(BF16), vh[u])
        o = o + _mm((qh[u].astype(F32) * xi_ref[hd]).astype(BF16), prev[u])
        mu = jnp.mean(o, axis=-1, keepdims=True)
        oc = o - mu
        on = oc * lax.rsqrt(jnp.mean(oc * oc, axis=-1, keepdims=True) + NORM_EPS)
        o_ref[0, rows(c), vsl(hd)] = (_silu(rg_ref[0, rows(c), vsl(hd)]) * on).astype(BF16)


def _retention(rq, rk, rv, rg, ts):
    bsz, s, _ = rq.shape
    c = RET_CHUNK
    gamma = 1.0 - jnp.power(2.0, -5.0 - jnp.arange(RET_HEADS, dtype=F32))
    log_g = jnp.log(gamma)
    idx = jnp.arange(c, dtype=F32)
    diff = idx[:, None] - idx[None, :]
    dmask = jnp.where(diff[None] >= 0, jnp.exp(jnp.maximum(diff, 0.0)[None] * log_g[:, None, None]), 0.0)
    zeta = jnp.exp((c - 1.0 - idx)[None, :] * log_g[:, None])
    xi = jnp.exp((idx + 1.0)[None, :] * log_g[:, None])
    decay = jnp.exp(c * log_g)
    xi_b = jnp.broadcast_to(xi[:, :, None], (RET_HEADS, c, LANES))
    zt_b = jnp.broadcast_to(zeta[:, :, None], (RET_HEADS, c, LANES))
    dc_b = jnp.broadcast_to(decay[:, None, None], (RET_HEADS, LANES, RET_DV))

    tile = lambda w: pl.BlockSpec((1, ts, w), lambda b, i: (b, i, 0))
    const = lambda shape: pl.BlockSpec(shape, lambda b, i: (0,) * len(shape))
    return pl.pallas_call(
        functools.partial(_ret_kernel, ts=ts),
        grid=(bsz, s // ts),
        in_specs=[tile(RET_HEADS * RET_DK), tile(RET_HEADS * RET_DK), tile(RET_HEADS * RET_DV), tile(RET_HEADS * RET_DV),
                  const((RET_HEADS, c, c)), const((RET_HEADS, c, LANES)), const((RET_HEADS, c, LANES)),
                  const((RET_HEADS, LANES, RET_DV))],
        out_specs=tile(RET_HEADS * RET_DV),
        out_shape=jax.ShapeDtypeStruct((bsz, s, RET_HEADS * RET_DV), BF16),
        scratch_shapes=[pltpu.VMEM((RET_HEADS, LANES, RET_DV), F32)],
        compiler_params=_params("parallel", "arbitrary"),
        name="retention",
    )(rq, rk, rv, rg, dmask, xi_b, zt_b, dc_b)


def _post_kernel(x_ref, om_ref, or_ref, mod_ref, wo_ref, g2_ref, wr_ref, br_ref,
                 x1_ref, h2_ref, ri_ref, rit_ref, cnt_ref):
    half = om_ref.shape[-1]
    subs = [slice(i * MOE_TILE, (i + 1) * MOE_TILE) for i in range(x_ref.shape[1] // MOE_TILE)]
    mixes = [_mm(om_ref[0, rs, :], wo_ref[0:half, :]) + _mm(or_ref[0, rs, :], wo_ref[half:, :]) for rs in subs]

    lgs = []
    for rs, mix in zip(subs, mixes):
        x1 = x_ref[0, rs, :] + mod_ref[0, 2:3, :] * mix
        x1_ref[0, rs, :] = x1
        h2 = _rms(x1, g2_ref[...]) * (1.0 + mod_ref[0, 4:5, :]) + mod_ref[0, 3:4, :]
        hi = h2.astype(BF16)
        h2_ref[0, rs, :] = hi
        lo = (h2 - hi.astype(F32)).astype(BF16)
        both = _mm(hi, wr_ref[...])
        lgs.append(both[:, 0:LANES] + both[:, LANES:2 * LANES] + _mm(lo, wr_ref[:, 0:LANES]) + br_ref[...])

    lane = lax.broadcasted_iota(jnp.int32, (MOE_TILE, LANES), 1)
    big = jnp.int32(1 << 20)
    gmask = lane < N_GROUPS
    el = lane - N_GROUPS
    assert EXPERTS_PER_GROUP == 8
    routed = []
    for lg in lgs:
        gmax = jnp.max(jnp.where(gmask, lg, NEG), axis=-1, keepdims=True)
        ge = jnp.where(gmask, jnp.exp(lg - gmax), 0.0)
        pg = ge / jnp.sum(ge, axis=-1, keepdims=True)
        p_top = jnp.max(pg, axis=-1, keepdims=True)
        g_top = jnp.min(jnp.where(gmask & (pg == p_top), lane, big), axis=-1, keepdims=True)

        emask = (el >= 0) & (el < N_EXPERTS) & (lax.shift_right_arithmetic(el, 3) == g_top)
        ev = jnp.where(emask, lg, NEG)
        v1 = jnp.max(ev, axis=-1, keepdims=True)
        i1 = jnp.min(jnp.where(emask & (ev == v1), lane, big), axis=-1, keepdims=True)
        emask2 = emask & (lane != i1)
        ev2 = jnp.where(emask2, lg, NEG)
        v2 = jnp.max(ev2, axis=-1, keepdims=True)
        i2 = jnp.min(jnp.where(emask2 & (ev2 == v2), lane, big), axis=-1, keepdims=True)
        e = jnp.exp(v2 - v1)
        den = 1.0 + e
        routed.append((i1, i2, (1.0 / den) * p_top, (e / den) * p_top))

    r_io = lax.broadcasted_iota(jnp.int32, (MOE_TILE, MOE_TILE), 0)
    c_io = lax.broadcasted_iota(jnp.int32, (MOE_TILE, MOE_TILE), 1)
    earlier_tok = (c_io < r_io).astype(BF16)
    lr_io = lax.broadcasted_iota(jnp.int32, (LANES, LANES), 0)
    lc_io = lax.broadcasted_iota(jnp.int32, (LANES, LANES), 1)
    earlier_lane = (lr_io < lc_io).astype(BF16)
    for hf, (rs, (i1, i2, w1, w2)) in enumerate(zip(subs, routed)):
        oh = [lane == i1, lane == i2]
        cnt = (oh[0] | oh[1]).astype(BF16)
        excl = _mm(earlier_tok, cnt)
        n = jnp.sum(cnt.astype(F32), axis=0, keepdims=True)
        npad = jnp.floor((n + (MOE_CHUNK - 1)) * (1.0 / MOE_CHUNK)) * MOE_CHUNK
        loff = _mm(jnp.broadcast_to(npad, (8, LANES)).astype(BF16), earlier_lane)
        pos = excl + loff[0:1, :]
        cnt_ref[hf] = jnp.broadcast_to(n, (8, LANES))
        lp0, lp1 = [jnp.sum(jnp.where(o, pos, 0.0), axis=-1, keepdims=True) for o in oh]
        cols = [(i1 - N_GROUPS).astype(F32), (i2 - N_GROUPS).astype(F32), w1, w2, lp0, lp1]
        ri = jnp.zeros((MOE_TILE, LANES), F32)
        for j, col in enumerate(cols):
            ri = jnp.where(lane == j, col, ri)
        ri_ref[0, rs, :] = ri
        rit_ref[:, rs] = ri.T


def _post(x, o_mla, o_ret, mod3, w_o, norm2_g, w_gr, b_gr, w_er, b_er, tm):
    bsz, s, d = x.shape
    w_r = jnp.concatenate([w_gr, w_er.reshape(d, N_EXPERTS), jnp.zeros((d, LANES - N_GROUPS - N_EXPERTS), F32)], axis=1)
    w_rh = w_r.astype(BF16)
    w_rl = (w_r - w_rh.astype(F32)).astype(BF16)
    w_r2 = jnp.concatenate([w_rh, w_rl], axis=1)
    b_r =jnp.concatenate([b_gr, b_er.reshape(-1), jnp.zeros((LANES - N_GROUPS - N_EXPERTS,), F32)]).reshape(1, LANES)
    tile = lambda w: pl.BlockSpec((1, tm, w), lambda b, i: (b, i, 0))
    const = lambda shape: pl.BlockSpec(shape, lambda b, i: (0,) * len(shape))
    per_b = s // tm
    sub = tm // MOE_TILE
    return pl.pallas_call(
        _post_kernel,
        grid=(bsz, per_b),
        in_specs=[tile(d), tile(o_mla.shape[-1]), tile(o_ret.shape[-1]), pl.BlockSpec((1, 6, d), lambda b, i: (b, 0, 0)),
                  const((d, d)), const((1, d)), const((d, 2 * LANES)), const((1, LANES))],
        out_specs=[tile(d), tile(d), tile(LANES),
                   pl.BlockSpec((LANES, tm), lambda b, i: (0, b * per_b + i)),
                   pl.BlockSpec((sub, 8, LANES), lambda b, i: (b * per_b + i, 0, 0))],
        out_shape=[jax.ShapeDtypeStruct((bsz, s, d), F32), jax.ShapeDtypeStruct((bsz, s, d), BF16),
                   jax.ShapeDtypeStruct((bsz, s, LANES), F32),
                   jax.ShapeDtypeStruct((LANES, bsz * s), F32),
                   jax.ShapeDtypeStruct((bsz * s // MOE_TILE, 8, LANES), F32)],
        compiler_params=_params("parallel", "arbitrary"),
        name="post_mixer",
    )(x, o_mla, o_ret, mod3, w_o.astype(BF16), norm2_g.reshape(1, d), w_r2, b_r)


def _chunk_rows(c):
    return pl.ds(pl.multiple_of(c * MOE_CHUNK, MOE_CHUNK), MOE_CHUNK)


def _dispatch_kernel(dmap_ref, nchk_ref, tstart_ref, tn_ref, nbr_ref, h2_ref, ri_ref, rit_ref, xs_hbm,
                     xloc, zblk, sem, zsem, bsem, *, nt, nb):
    i = pl.program_id(0)
    slot = lax.rem(i, 2)
    d = h2_ref.shape[-1]

    def zero_copy(e, c):
        return pltpu.make_async_copy(zblk.at[pl.ds(0, MOE_CHUNK)], xs_hbm.at[_chunk_rows(tstart_ref[e] + c)], zsem)

    def zero_block(j):
        rows = pl.ds(pl.multiple_of(j * MOE_BLOCK, MOE_BLOCK), MOE_BLOCK)
        return pltpu.make_async_copy(zblk, xs_hbm.at[rows], bsem)

    def chunk_copy(t, c, sl):
        return pltpu.make_async_copy(xloc.at[sl].at[_chunk_rows(c)],
                                     xs_hbm.at[_chunk_rows(dmap_ref[t * MOE_TILE_CHUNKS + c])], sem.at[sl])

    def wait_tile(t, sl):
        lax.fori_loop(0, nchk_ref[t], lambda c, z: (chunk_copy(t, c, sl).wait(), z)[1], 0)

    @pl.when(i == 0)
    def _():
        zblk[...] = jnp.zeros_like(zblk)
        lax.fori_loop(nbr_ref[0], nb, lambda j, z: (zero_block(j).start(), z)[1], 0)
        for e in range(N_EXPERTS):
            lax.fori_loop(0, tn_ref[e], lambda c, z, e=e: (zero_copy(e, c).start(), z)[1], 0)
        for e in range(N_EXPERTS):
            lax.fori_loop(0, tn_ref[e], lambda c, z, e=e: (zero_copy(e, c).wait(), z)[1], 0)

    @pl.when(i >= 2)
    def _():
        wait_tile(i - 2, slot)

    s_io = lax.broadcasted_iota(jnp.int32, (MOE_SLAB, MOE_TILE), 0).astype(F32)
    pm = [(s_io == rit_ref[4 + k:5 + k, :]).astype(BF16) for k in range(TOP_K)]
    xloc[slot, :, 0:d] = _mm(pm[0] + pm[1], h2_ref[...]).astype(BF16)
    lane = lax.broadcasted_iota(jnp.int32, (MOE_TILE, LANES), 1)
    wx = jnp.zeros((MOE_SLAB, LANES), F32)
    for k in range(TOP_K):
        w = ri_ref[:, TOP_K + k:TOP_K + k + 1]
        hi = w.astype(BF16).astype(F32)
        wx = wx + _mm(pm[k], jnp.where(lane == 0, hi, jnp.where(lane == 1, w - hi, 0.0)).astype(BF16))
    xloc[slot, :, d:d + LANES] = wx.astype(BF16)

    lax.fori_loop(0, nchk_ref[i], lambda c, z: (chunk_copy(i, c, slot).start(), z)[1], 0)

    @pl.when(i == nt - 1)
    def _():
        wait_tile(i, slot)
        if nt >= 2:
            wait_tile(i - 1, 1 - slot)
        lax.fori_loop(nbr_ref[0], nb, lambda j, z: (zero_block(j).wait(), z)[1], 0)


def _dispatch(h2, ri, rit, tables, n_rows):
    t, d = h2.shape
    nt = t // MOE_TILE
    assert MOE_SLAB >= TOP_K * MOE_TILE + N_EXPERTS * (MOE_CHUNK - 1)
    grid_spec = pltpu.PrefetchScalarGridSpec(
        num_scalar_prefetch=5,
        grid=(nt,),
        in_specs=[pl.BlockSpec((MOE_TILE, d), lambda i, *_: (i, 0)),
                  pl.BlockSpec((MOE_TILE, LANES), lambda i, *_: (i, 0)),
                  pl.BlockSpec((8, MOE_TILE), lambda i, *_: (0, i))],
        out_specs=pl.BlockSpec(memory_space=pl.ANY),
        scratch_shapes=[pltpu.VMEM((2, MOE_SLAB, d + LANES), BF16), pltpu.VMEM((MOE_BLOCK, d + LANES), BF16),
                        pltpu.SemaphoreType.DMA((2,)), pltpu.SemaphoreType.DMA(()), pltpu.SemaphoreType.DMA(())])
    return pl.pallas_call(
        functools.partial(_dispatch_kernel, nt=nt, nb=n_rows // MOE_BLOCK),
        grid_spec=grid_spec,
        out_shape=jax.ShapeDtypeStruct((n_rows, d + LANES), BF16),
        compiler_params=_params("arbitrary"),
        name="moe_dispatch",
    )(tables["dmap"], tables["tile_chunks"], tables["tail_start"], tables["tail_n"], tables["n_blocks"],
      h2, ri, rit)


def _expert_kernel(be_ref, nbr_ref, xs_ref, w1_ref, w3_ref, w2_ref, y_ref):
    used = pl.program_id(0) < nbr_ref[0]

    @pl.when(jnp.logical_not(used))
    def _():
        y_ref[...] = jnp.zeros_like(y_ref)

    @pl.when(used)
    def _():
        d = y_ref.shape[-1]
        xb = xs_ref[:, 0:d]
        gw = xs_ref[:, d:d + LANES].astype(F32)
        gate = gw[:, 0:1] + gw[:, 1:2]
        a = _mm(xb, w1_ref[0])
        b = _mm(xb, w3_ref[0])
        hm = (_silu(a) * b).astype(BF16)
        y_ref[...] = (_mm(hm, w2_ref[0]) * gate).astype(BF16)


def _experts(xs, tables, w1, w3, w2):
    n_rows, dw = xs.shape
    d = dw - LANES
    nb = n_rows // MOE_BLOCK
    blk = lambda i, be, nbr: jnp.minimum(i, nbr[0] - 1)
    grid_spec = pltpu.PrefetchScalarGridSpec(
        num_scalar_prefetch=2,
        grid=(nb,),
        in_specs=[pl.BlockSpec((MOE_BLOCK, dw), lambda i, be, nbr: (blk(i, be, nbr), 0)),
                  pl.BlockSpec((1, d, D_EXPERT), lambda i, be, nbr: (be[blk(i, be, nbr)], 0, 0)),
                  pl.BlockSpec((1, d, D_EXPERT), lambda i, be, nbr: (be[blk(i, be, nbr)], 0, 0)),
                  pl.BlockSpec((1, D_EXPERT, d), lambda i, be, nbr: (be[blk(i, be, nbr)], 0, 0))],
        out_specs=pl.BlockSpec((MOE_BLOCK, d), lambda i, be, nbr: (i, 0)))
    return pl.pallas_call(
        _expert_kernel,
        grid_spec=grid_spec,
        out_shape=jax.ShapeDtypeStruct((n_rows, d), BF16),
        compiler_params=_params("arbitrary"),
        name="moe_experts",
    )(tables["blk_expert"], tables["n_blocks"], xs, w1.astype(BF16), w3.astype(BF16), w2.astype(BF16))


def _moe_rows(n_tiles):
    worst = n_tiles * (TOP_K * MOE_TILE + N_EXPERTS * (MOE_CHUNK - 1)) + N_EXPERTS * (MOE_BLOCK - MOE_CHUNK)
    return -(-worst // MOE_BLOCK) * MOE_BLOCK


def _moe_tables(cnt, n_rows):
    n_tiles = cnt.shape[0]
    per_blk = MOE_BLOCK // MOE_CHUNK

    def excl_cumsum(a, axis):
        n = a.shape[axis]
        lower = jnp.arange(n)[:, None] > jnp.arange(n)[None, :]
        if axis == 0:
            return jnp.sum(jnp.where(lower[:, :, None], a[None, :, :], 0), axis=1)
        return jnp.sum(jnp.where(lower[None, :, :], a[:, None, :], 0), axis=2)

    nch = (cnt + MOE_CHUNK - 1) // MOE_CHUNK
    loff = excl_cumsum(nch, 1)
    seg = jnp.sum(nch, axis=0)
    blocks = (seg + per_blk - 1) // per_blk
    bstart = excl_cumsum(blocks[None, :], 1)[0]
    bend = bstart + blocks
    estart = bstart * per_blk
    gbase = estart[None, :] + excl_cumsum(nch, 0)
    c = jnp.arange(MOE_TILE_CHUNKS, dtype=jnp.int32)
    owner = jnp.sum((c[None, :, None] >= (loff + nch)[:, None, :]).astype(jnp.int32), axis=-1)
    owner = jnp.minimum(owner, N_EXPERTS - 1)
    is_owner = owner[:, :, None] == jnp.arange(N_EXPERTS, dtype=jnp.int32)[None, None, :]
    dmap = c[None, :] + jnp.sum(jnp.where(is_owner, (gbase - loff)[:, None, :], 0), axis=-1)
    dmap = jnp.clip(dmap, 0, n_rows // MOE_CHUNK - 1)
    nb = n_rows // MOE_BLOCK
    blk_expert = jnp.sum((jnp.arange(nb, dtype=jnp.int32)[:, None] >= bend[None, :]).astype(jnp.int32), axis=1)
    blk_expert = jnp.minimum(blk_expert, N_EXPERTS - 1)
    return dict(dmap=dmap.reshape(-1).astype(jnp.int32), tile_chunks=jnp.sum(nch, axis=1).astype(jnp.int32),
                tail_start=(estart + seg).astype(jnp.int32), tail_n=(blocks * per_blk - seg).astype(jnp.int32),
                blk_expert=blk_expert.astype(jnp.int32), n_blocks=bend[-1:].astype(jnp.int32))


def _final_kernel(dmap_ref, nchk_ref, x1_ref, ri_ref, mod_ref, g_ref, y_hbm, o_ref, yloc, sem, *, nt):
    i = pl.program_id(0)
    slot = lax.rem(i, 2)

    def chunk_copy(t, c, sl):
        return pltpu.make_async_copy(y_hbm.at[_chunk_rows(dmap_ref[t * MOE_TILE_CHUNKS + c])],
                                     yloc.at[sl].at[_chunk_rows(c)], sem.at[sl])

    def gather(t, sl):
        lax.fori_loop(0, nchk_ref[t], lambda c, z: (chunk_copy(t, c, sl).start(), z)[1], 0)

    @pl.when(i == 0)
    def _():
        yloc[...] = jnp.zeros_like(yloc)
        gather(0, 0)

    @pl.when(i + 1 < nt)
    def _():
        gather(i + 1, 1 - slot)

    lax.fori_loop(0, nchk_ref[i], lambda c, z: (chunk_copy(i, c, slot).wait(), z)[1], 0)

    l_io = lax.broadcasted_iota(jnp.int32, (MOE_TILE, MOE_SLAB), 1).astype(F32)
    pick = ((l_io == ri_ref[:, 4:5]) | (l_io == ri_ref[:, 5:6])).astype(BF16)
    moe = _mm(pick, yloc[slot])
    x2 = x1_ref[...] + mod_ref[0, 5:6, :] * moe
    o_ref[...] = _rms(x2, g_ref[...])


def _final(x1, ri, y, mod3, final_g, tables, s):
    t, d = x1.shape
    nt = t // MOE_TILE
    per_b = s // MOE_TILE
    grid_spec = pltpu.PrefetchScalarGridSpec(
        num_scalar_prefetch=2,
        grid=(nt,),
        in_specs=[pl.BlockSpec((MOE_TILE, d), lambda i, *_: (i, 0)),
                  pl.BlockSpec((MOE_TILE, LANES), lambda i, *_: (i, 0)),
                  pl.BlockSpec((1, 6, d), lambda i, *_: (i // per_b, 0, 0)),
                  pl.BlockSpec((1, d), lambda i, *_: (0, 0)),
                  pl.BlockSpec(memory_space=pl.ANY)],
        out_specs=pl.BlockSpec((MOE_TILE, d), lambda i, *_: (i, 0)),
        scratch_shapes=[pltpu.VMEM((2, MOE_SLAB, d), BF16), pltpu.SemaphoreType.DMA((2,))])
    return pl.pallas_call(
        functools.partial(_final_kernel, nt=nt),
        grid_spec=grid_spec,
        out_shape=jax.ShapeDtypeStruct((t, d), F32),
        compiler_params=_params("arbitrary"),
        name="moe_combine_final",
    )(tables["dmap"], tables["tile_chunks"], x1, ri, mod3, final_g.reshape(1, d), y)


def kernel(x, c, positions, w_ada, b_ada, norm1_g, w_in, q_norm_g, w_uq, kv_norm_g, w_ukv, w_o, norm2_g,
           w_gr, b_gr, w_er, b_er, w1, w3, w2, final_g):
    bsz, s, d = x.shape
    assert w_ada.shape[0] == 1, "one layer"
    tm = min(1024, s)
    tq = min(256, s)
    ts = min(1024, s)
    mod3 = _adaln(c, w_ada[0], b_ada[0]).reshape(bsz, 6, d)
    pos3 = positions.astype(F32).reshape(bsz, s, 1)
    q, k, vt, rq, rk, rv, rg = _pre(x, mod3, pos3, norm1_g[0], w_in[0], q_norm_g[0], w_uq[0], kv_norm_g[0], w_ukv[0],
                                    tm, tq)
    o_mla = _attention(q, k, vt, tq, ATTN_HEADS_PER_STEP)
    o_ret = _retention(rq, rk, rv, rg, ts)
    x1, h2, ri, rit, cnt = _post(x, o_mla, o_ret, mod3, w_o[0], norm2_g[0], w_gr[0], b_gr[0], w_er[0], b_er[0], tm)
    t = bsz * s
    n_rows = _moe_rows(t // MOE_TILE)
    counts = cnt[:, 0, N_GROUPS:N_GROUPS + N_EXPERTS].astype(jnp.int32)
    tables = _moe_tables(counts, n_rows)
    ri2 = ri.reshape(t, LANES)
    xs = _dispatch(h2.reshape(t, d), ri2, rit, tables, n_rows)
    y = _experts(xs, tables, w1[0], w3[0], w2[0])
    out = _final(x1.reshape(t, d), ri2, y, mod3, final_g, tables, s)
    return out.reshape(bsz, s, d)
```

```python
import functools

import jax
import jax.numpy as jnp
from jax import lax
from jax.experimental import pallas as pl
from jax.experimental.pallas import tpu as pltpu

MLA_HEADS = 8
MLA_NOPE = 64
MLA_ROPE = 32
MLA_V = 64
Q_LORA = 256
KV_LORA = 128
RET_HEADS = 4
RET_DK = 64
RET_DV = 128
RET_CHUNK = 128
ROPE_BASE = 10000.0
NORM_EPS = 1e-6
N_GROUPS = 4
EXPERTS_PER_GROUP = 8
N_EXPERTS = N_GROUPS * EXPERTS_PER_GROUP
TOP_K = 2
D_EXPERT = 256
MOE_TILE = 256
MOE_CHUNK = 16
MOE_SLAB = 1024
MOE_TILE_CHUNKS = MOE_SLAB // MOE_CHUNK
MOE_TPS = 2
MOE_BLOCK = 1024

LANES = 128
VMEM_LIMIT = 56 * 1024 * 1024

F32 = jnp.float32
BF16 = jnp.bfloat16
NEG = float(jnp.finfo(jnp.float32).min)
LOG2_E = 1.4426950408889634
ATTN_HEADS_PER_STEP = 8
ATTN_PAIRS_PER_ITER = 2
MLA_VROWS = MLA_V + 16

_C_Q = 0
_C_KV = _C_Q + Q_LORA
_C_KPE = _C_KV + KV_LORA
_C_RQ = _C_KPE + LANES
_C_RK = _C_RQ + RET_HEADS * RET_DK
_C_RV = _C_RK + RET_HEADS * RET_DK
_C_RG = _C_RV + RET_HEADS * RET_DV
_IN_PERM = _C_RG + RET_HEADS * RET_DV


def _silu(v):
    return v / (1.0 + jnp.exp(-v))


def _mm(a, b):
    return jnp.dot(a, b, preferred_element_type=F32)


def _mm_nt(a, b):
    return lax.dot_general(a, b, (((1,), (1,)), ((), ())), preferred_element_type=F32)


def _params(*sem):
    return pltpu.CompilerParams(dimension_semantics=sem, vmem_limit_bytes=VMEM_LIMIT)


def _adaln_kernel(c_ref, w_ref, b_ref, o_ref):
    a = _silu(c_ref[...]).astype(BF16)
    o_ref[...] = _mm(a, w_ref[...].astype(BF16)) + b_ref[...]


def _adaln(c, w_ada, b_ada):
    bsz, d = c.shape
    n = w_ada.shape[1]
    tn = d
    return pl.pallas_call(
        _adaln_kernel,
        grid=(n // tn,),
        in_specs=[pl.BlockSpec((bsz, d), lambda j: (0, 0)),
                  pl.BlockSpec((d, tn), lambda j: (0, j)),
                  pl.BlockSpec((1, tn), lambda j: (0, j))],
        out_specs=pl.BlockSpec((bsz, tn), lambda j: (0, j)),
        out_shape=jax.ShapeDtypeStruct((bsz, n), F32),
        compiler_params=_params("arbitrary"),
        name="adaln",
    )(c, w_ada, b_ada.reshape(1, n))


def _rms(v, g):
    return v * lax.rsqrt(jnp.mean(v * v, axis=-1, keepdims=True) + NORM_EPS) * g


def _pre_kernel(x_ref, mod_ref, pos_ref, g1_ref, win_ref, qg_ref, wuq_ref, kvg_ref, wuk_ref,
                wuv_ref, invf_ref, q_ref, k_ref, v_ref, rq_ref, rk_ref, rv_ref, rg_ref):
    tq = v_ref.shape[-1]
    subs = [slice(i * tq, (i + 1) * tq) for i in range(v_ref.shape[1])]
    sh1 = mod_ref[0, 0:1, :]
    sc1 = mod_ref[0, 1:2, :]
    projs = [_mm((_rms(x_ref[0, rs, :], g1_ref[...]) * (1.0 + sc1) + sh1).astype(BF16), win_ref[...]) for rs in subs]

    cqs = [_rms(p[:, _C_Q:_C_Q + Q_LORA], qg_ref[...]).astype(BF16) for p in projs]
    ckvs = [_rms(p[:, _C_KV:_C_KV + KV_LORA], kvg_ref[...]).astype(BF16) for p in projs]
    qas = [_mm(cq, wuq_ref[...]) for cq in cqs]
    kns = [_mm(ckv, wuk_ref[...]) for ckv in ckvs]
    vvs = [_mm(ckv, wuv_ref[...]) for ckv in ckvs]

    lane = lax.broadcasted_iota(jnp.int32, (tq, LANES), 1)
    hi = lane >= RET_DK
    half_m, half_r = MLA_ROPE // 2, RET_DK // 2
    first_m = hi & (lane < RET_DK + half_m)
    first_r = (lane & half_r) == 0
    scale = (MLA_NOPE + MLA_ROPE) ** -0.5 * LOG2_E

    def rope(v, cos, sin, first, half):
        partner = jnp.where(first, pltpu.roll(v, LANES - half, 1), pltpu.roll(v, half, 1))
        return v * cos + partner * sin

    for i, rs in enumerate(subs):
        proj = projs[i]
        ang = pos_ref[0, rs, :] * invf_ref[...]
        cs = jnp.cos(ang)
        sn = jnp.sin(ang)
        c_mla = jnp.where(hi, cs, 1.0)
        s_mla = jnp.where(hi, jnp.where(first_m, -sn, sn), 0.0)
        c_ret = jnp.where(hi, pltpu.roll(cs, RET_DK, 1), cs)
        s_ret = jnp.where(hi, pltpu.roll(sn, RET_DK, 1), sn)
        s_ret = jnp.where(first_r, -s_ret, s_ret)

        kpe = rope(proj[:, _C_KPE:_C_KPE + LANES], c_mla, s_mla, first_m, half_m)
        for hd in range(MLA_HEADS):
            sl = slice(hd * LANES, (hd + 1) * LANES)
            q_ref[0, rs, sl] = (rope(qas[i][:, sl], c_mla, s_mla, first_m, half_m) * scale).astype(BF16)
            k_ref[0, rs, sl] = (kns[i][:, sl] + kpe).astype(BF16)
        vt = vvs[i].T
        tail = jnp.where(lax.broadcasted_iota(jnp.int32, (MLA_VROWS - MLA_V, tq), 0) == 0, 1.0, 0.0)
        slab = [piece for hd in range(MLA_HEADS) for piece in (vt[hd * MLA_V:(hd + 1) * MLA_V, :], tail)]
        v_ref[0, i] = jnp.concatenate(slab, axis=0).astype(BF16)

        for j in range(RET_HEADS * RET_DK // LANES):
            o = j * LANES
            rq = rope(proj[:, _C_RQ + o:_C_RQ + o + LANES], c_ret, s_ret, first_r, half_r)
            rk = rope(proj[:, _C_RK + o:_C_RK + o + LANES], c_ret, s_ret, first_r, half_r)
            rq_ref[0, rs, o:o + LANES] = rq.astype(BF16)
            rk_ref[0, rs, o:o + LANES] = (rk * (RET_DK ** -0.5)).astype(BF16)
        rv_ref[0, rs, :] = proj[:, _C_RV:_C_RV + RET_HEADS * RET_DV].astype(BF16)
        rg_ref[0, rs, :] = proj[:, _C_RG:_C_RG + RET_HEADS * RET_DV]


def _pad_heads(w, width, left):
    k = w.shape[0]
    w3 = w.reshape(k, -1, width)
    w3 = jnp.pad(w3, ((0, 0), (0, 0), (left, LANES - left - width)))
    return w3.reshape(k, -1)


def _pre(x, mod3, pos3, norm1_g, w_in, q_norm_g, w_uq, kv_norm_g, w_ukv, tm, tq):
    bsz, s, d = x.shape
    o = 0
    parts = {}
    for name, width in (("cq", Q_LORA), ("ckv", KV_LORA), ("kr", MLA_ROPE), ("rq", RET_HEADS * RET_DK),
                        ("rk", RET_HEADS * RET_DK), ("rv", RET_HEADS * RET_DV), ("rg", RET_HEADS * RET_DV)):
        parts[name] = w_in[:, o:o + width]
        o += width
    w_in_p = jnp.concatenate([
        parts["cq"], parts["ckv"], _pad_heads(parts["kr"], MLA_ROPE, MLA_NOPE),
        parts["rq"], parts["rk"], parts["rv"], parts["rg"]], axis=1).astype(BF16)
    assert w_in_p.shape[1] == _IN_PERM
    w_uq_p = _pad_heads(w_uq, MLA_NOPE + MLA_ROPE, 0).astype(BF16)
    wkv3 = w_ukv.reshape(KV_LORA, MLA_HEADS, MLA_NOPE + MLA_V)
    w_uk_p = _pad_heads(wkv3[:, :, :MLA_NOPE].reshape(KV_LORA, -1), MLA_NOPE, 0).astype(BF16)
    w_uv = wkv3[:, :, MLA_NOPE:].reshape(KV_LORA, -1).astype(BF16)
    half_r, half_m = RET_DK // 2, MLA_ROPE // 2
    f_r = ROPE_BASE ** (-(jnp.arange(half_r, dtype=F32) / half_r))
    f_m = ROPE_BASE ** (-(jnp.arange(half_m, dtype=F32) / half_m))
    invf = jnp.concatenate([f_r, f_r, f_m, f_m, jnp.zeros((LANES - 2 * half_r - 2 * half_m,), F32)]).reshape(1, LANES)

    hq = MLA_HEADS * LANES
    const = lambda shape: pl.BlockSpec(shape, lambda b, i: (0,) * len(shape))
    tile = lambda w: pl.BlockSpec((1, tm, w), lambda b, i: (b, i, 0))
    return pl.pallas_call(
        _pre_kernel,
        grid=(bsz, s // tm),
        in_specs=[tile(d), pl.BlockSpec((1, 6, d), lambda b, i: (b, 0, 0)), tile(1), const((1, d)),
                  const((d, _IN_PERM)), const((1, Q_LORA)), const((Q_LORA, hq)),
                  const((1, KV_LORA)), const((KV_LORA, hq)), const((KV_LORA, MLA_HEADS * MLA_V)), const((1, LANES))],
        out_specs=[tile(hq), tile(hq),
                   pl.BlockSpec((1, tm // tq, MLA_HEADS * MLA_VROWS, tq), lambda b, i: (b, i, 0, 0)),
                   tile(RET_HEADS * RET_DK), tile(RET_HEADS * RET_DK),
                   tile(RET_HEADS * RET_DV), tile(RET_HEADS * RET_DV)],
        out_shape=[jax.ShapeDtypeStruct((bsz, s, hq), BF16), jax.ShapeDtypeStruct((bsz, s, hq), BF16),
                   jax.ShapeDtypeStruct((bsz, s // tq, MLA_HEADS * MLA_VROWS, tq), BF16),
                   jax.ShapeDtypeStruct((bsz, s, RET_HEADS * RET_DK), BF16),
                   jax.ShapeDtypeStruct((bsz, s, RET_HEADS * RET_DK), BF16),
                   jax.ShapeDtypeStruct((bsz, s, RET_HEADS * RET_DV), BF16),
                   jax.ShapeDtypeStruct((bsz, s, RET_HEADS * RET_DV), F32)],
        compiler_params=_params("parallel", "arbitrary"),
        name="pre_mixer",
    )(x, mod3, pos3, norm1_g.reshape(1, d), w_in_p, q_norm_g.reshape(1, -1), w_uq_p,
      kv_norm_g.reshape(1, -1), w_uk_p, w_uv, invf)


def _attn_kernel(q_ref, k_ref, vt_ref, o_ref, st_x, st_y, bm_x, bm_y, *, tq, kb, hps):
    qi = pl.program_id(1)
    assert tq == 2 * kb, "a query tile spans two key blocks: the last two blocks of a tile are masked"
    key = lax.broadcasted_iota(jnp.int32, (kb, tq), 0)
    qry = lax.broadcasted_iota(jnp.int32, (kb, tq), 1)
    hsl = [slice(hh * LANES, (hh + 1) * LANES) for hh in range(hps)]
    vsl = [slice(hh * MLA_VROWS, (hh + 1) * MLA_VROWS) for hh in range(hps)]
    qs = [q_ref[0, :, hs] for hs in hsl]
    bufs = {"x": (st_x, bm_x), "y": (st_y, bm_y)}
    n_blk = 2 * qi + 2

    def scores(blk, buf, diag=None):
        st_ref, bm_ref = bufs[buf]
        start = pl.multiple_of(blk * kb, kb)
        for hh in range(hps):
            st = _mm_nt(k_ref[0, pl.ds(start, kb), hsl[hh]], qs[hh])
            if diag is not None:
                st = jnp.where(key + diag * kb <= qry, st, NEG)
            st_ref[hh] = st
            bm_ref[hh, 0:1, :] = jnp.max(st, axis=0, keepdims=True)

    def update(blk, buf, carry):
        st_ref, bm_ref = bufs[buf]
        out = []
        for hh in range(hps):
            m, acc = carry[hh]
            m_new = jnp.maximum(m, bm_ref[hh, 0:1, :])
            p = jnp.exp2(st_ref[hh] - m_new).astype(BF16)
            out.append((m_new, jnp.exp2(m - m_new) * acc + _mm(vt_ref[0, blk, vsl[hh], :], p)))
        return tuple(out)

    def when_loop(pred, body, carry):
        return lax.fori_loop(0, pred.astype(jnp.int32), lambda _, c: body(c), carry)

    init = (jnp.full((1, tq), NEG, F32), jnp.zeros((MLA_VROWS, tq), F32))
    carry = (init,) * hps

    @pl.when(qi >= 1)
    def _():
        scores(0, "x")

    def steady(base, pairs, c):
        for r in range(pairs):
            scores(base + 2 * r + 1, "y")
            c = update(base + 2 * r, "x", c)
            scores(base + 2 * r + 2, "x")
            c = update(base + 2 * r + 1, "y", c)
        return c

    n_pairs = jnp.maximum(qi - 1, 0)
    n_long = n_pairs // ATTN_PAIRS_PER_ITER
    carry = lax.fori_loop(0, n_long, lambda i, c: steady(2 * ATTN_PAIRS_PER_ITER * i, ATTN_PAIRS_PER_ITER, c), carry)
    done = n_long * ATTN_PAIRS_PER_ITER
    carry = lax.fori_loop(done, n_pairs, lambda i, c: steady(2 * i, 1, c), carry)

    def tail(c):
        scores(n_blk - 3, "y")
        c = update(n_blk - 4, "x", c)
        scores(n_blk - 2, "x", diag=0)
        c = update(n_blk - 3, "y", c)
        scores(n_blk - 1, "y", diag=1)
        c = update(n_blk - 2, "x", c)
        return update(n_blk - 1, "y", c)

    def tail_first(c):
        scores(0, "x", diag=0)
        scores(1, "y", diag=1)
        c = update(0, "x", c)
        return update(1, "y", c)

    carry = when_loop(qi >= 1, tail, carry)
    carry = when_loop(qi == 0, tail_first, carry)
    out_t = jnp.concatenate([acc[0:MLA_V] / acc[MLA_V:MLA_V + 1] for _, acc in carry], axis=0)
    o_ref[0] = out_t.T.astype(BF16)


def _attention(q, k, vt, kb, hps):
    bsz, s, _ = q.shape
    groups = MLA_HEADS // hps
    tq = 2 * kb
    assert vt.shape == (bsz, s // kb, MLA_HEADS * MLA_VROWS, kb)
    return pl.pallas_call(
        functools.partial(_attn_kernel, tq=tq, kb=kb, hps=hps),
        grid=(bsz * groups, s // tq),
        in_specs=[pl.BlockSpec((1, tq, hps * LANES), lambda g, i: (g // groups, i, g % groups)),
                  pl.BlockSpec((1, s, hps * LANES), lambda g, i: (g // groups, 0, g % groups)),
                  pl.BlockSpec((1, s // kb, hps * MLA_VROWS, kb), lambda g, i: (g // groups, 0, g % groups, 0))],
        out_specs=pl.BlockSpec((1, tq, hps * MLA_V), lambda g, i: (g // groups, i, g % groups)),
        out_shape=jax.ShapeDtypeStruct((bsz, s, MLA_HEADS * MLA_V), BF16),
        scratch_shapes=[pltpu.VMEM((hps, kb, tq), F32), pltpu.VMEM((hps, kb, tq), F32),
                        pltpu.VMEM((hps, 8, tq), F32), pltpu.VMEM((hps, 8, tq), F32)],
        compiler_params=_params("parallel", "arbitrary"),
        name="mla_attention",
    )(q, k, vt)


def _ret_kernel(rq_ref, rk_ref, rv_ref, rg_ref, dm_ref, xi_ref, zt_ref, dc_ref, o_ref, st_ref, *, ts):
    @pl.when(pl.program_id(1) == 0)
    def _():
        st_ref[...] = jnp.zeros_like(st_ref)

    lane = lax.broadcasted_iota(jnp.int32, (RET_CHUNK, LANES), 1)
    units = [(c, hd) for c in range(ts // RET_CHUNK) for hd in range(RET_HEADS)]
    rows = lambda c: slice(c * RET_CHUNK, (c + 1) * RET_CHUNK)
    vsl = lambda hd: slice(hd * RET_DV, (hd + 1) * RET_DV)

    def head_lanes(ref, c, hd):
        pair, sub = divmod(hd, LANES // RET_DK)
        mine = (lane >= sub * RET_DK) & (lane < (sub + 1) * RET_DK)
        return jnp.where(mine, ref[0, rows(c), pair * LANES:(pair + 1) * LANES], 0.0).astype(BF16)

    qh = {u: head_lanes(rq_ref, *u) for u in units}
    kh = {u: head_lanes(rk_ref, *u) for u in units}
    vh = {(c, hd): rv_ref[0, rows(c), vsl(hd)] for c, hd in units}
    sc = {u: _mm_nt(qh[u], kh[u]) for u in units}
    un = {u: _mm((kh[u].astype(F32) * zt_ref[u[1]]).astype(BF16).T, vh[u]) for u in units}
    prev = {}
    for hd in range(RET_HEADS):
        st = st_ref[hd]
        for c in range(ts // RET_CHUNK):
            prev[(c, hd)] = st.astype(BF16)
            st = st * dc_ref[hd] + un[(c, hd)]
        st_ref[hd] = st
    for c, hd in units:
        u = (c, hd)
        o = _mm((sc[u] * dm_ref[hd]).astype(BF16), vh[u])
        o = o + _mm((qh[u].astype(F32) * xi_ref[hd]).astype(BF16), prev[u])
        mu = jnp.mean(o, axis=-1, keepdims=True)
        oc = o - mu
        on = oc * lax.rsqrt(jnp.mean(oc * oc, axis=-1, keepdims=True) + NORM_EPS)
        o_ref[0, rows(c), vsl(hd)] = (_silu(rg_ref[0, rows(c), vsl(hd)]) * on).astype(BF16)


def _retention(rq, rk, rv, rg, ts):
    bsz, s, _ = rq.shape
    c = RET_CHUNK
    gamma = 1.0 - jnp.power(2.0, -5.0 - jnp.arange(RET_HEADS, dtype=F32))
    log_g = jnp.log(gamma)
    idx = jnp.arange(c, dtype=F32)
    diff = idx[:, None] - idx[None, :]
    dmask = jnp.where(diff[None] >= 0, jnp.exp(jnp.maximum(diff, 0.0)[None] * log_g[:, None, None]), 0.0)
    zeta = jnp.exp((c - 1.0 - idx)[None, :] * log_g[:, None])
    xi = jnp.exp((idx + 1.0)[None, :] * log_g[:, None])
    decay = jnp.exp(c * log_g)
    xi_b = jnp.broadcast_to(xi[:, :, None], (RET_HEADS, c, LANES))
    zt_b = jnp.broadcast_to(zeta[:, :, None], (RET_HEADS, c, LANES))
    dc_b = jnp.broadcast_to(decay[:, None, None], (RET_HEADS, LANES, RET_DV))

    tile = lambda w: pl.BlockSpec((1, ts, w), lambda b, i: (b, i, 0))
    const = lambda shape: pl.BlockSpec(shape, lambda b, i: (0,) * len(shape))
    return pl.pallas_call(
        functools.partial(_ret_kernel, ts=ts),
        grid=(bsz, s // ts),
        in_specs=[tile(RET_HEADS * RET_DK), tile(RET_HEADS * RET_DK), tile(RET_HEADS * RET_DV), tile(RET_HEADS * RET_DV),
                  const((RET_HEADS, c, c)), const((RET_HEADS, c, LANES)), const((RET_HEADS, c, LANES)),
                  const((RET_HEADS, LANES, RET_DV))],
        out_specs=tile(RET_HEADS * RET_DV),
        out_shape=jax.ShapeDtypeStruct((bsz, s, RET_HEADS * RET_DV), BF16),
        scratch_shapes=[pltpu.VMEM((RET_HEADS, LANES, RET_DV), F32)],
        compiler_params=_params("parallel", "arbitrary"),
        name="retention",
    )(rq, rk, rv, rg, dmask, xi_b, zt_b, dc_b)


def _post_kernel(x_ref, om_ref, or_ref, mod_ref, wo_ref, g2_ref, wr_ref, br_ref,
                 x1_ref, h2_ref, ri_ref, rit_ref, cnt_ref):
    half = om_ref.shape[-1]
    subs = [slice(i * MOE_TILE, (i + 1) * MOE_TILE) for i in range(x_ref.shape[1] // MOE_TILE)]
    mixes = [_mm(om_ref[0, rs, :], wo_ref[0:half, :]) + _mm(or_ref[0, rs, :], wo_ref[half:, :]) for rs in subs]

    lgs = []
    for rs, mix in zip(subs, mixes):
        x1 = x_ref[0, rs, :] + mod_ref[0, 2:3, :] * mix
        x1_ref[0, rs, :] = x1
        h2 = _rms(x1, g2_ref[...]) * (1.0 + mod_ref[0, 4:5, :]) + mod_ref[0, 3:4, :]
        hi = h2.astype(BF16)
        h2_ref[0, rs, :] = hi
        lo = (h2 - hi.astype(F32)).astype(BF16)
        both = _mm(hi, wr_ref[...])
        lgs.append(both[:, 0:LANES] + both[:, LANES:2 * LANES] + _mm(lo, wr_ref[:, 0:LANES]) + br_ref[...])

    lane = lax.broadcasted_iota(jnp.int32, (MOE_TILE, LANES), 1)
    big = jnp.int32(1 << 20)
    gmask = lane < N_GROUPS
    el = lane - N_GROUPS
    assert EXPERTS_PER_GROUP == 8
    routed = []
    for lg in lgs:
        gmax = jnp.max(jnp.where(gmask, lg, NEG), axis=-1, keepdims=True)
        ge = jnp.where(gmask, jnp.exp(lg - gmax), 0.0)
        pg = ge / jnp.sum(ge, axis=-1, keepdims=True)
        p_top = jnp.max(pg, axis=-1, keepdims=True)
        g_top = jnp.min(jnp.where(gmask & (pg == p_top), lane, big), axis=-1, keepdims=True)

        emask = (el >= 0) & (el < N_EXPERTS) & (lax.shift_right_arithmetic(el, 3) == g_top)
        ev = jnp.where(emask, lg, NEG)
        v1 = jnp.max(ev, axis=-1, keepdims=True)
        i1 = jnp.min(jnp.where(emask & (ev == v1), lane, big), axis=-1, keepdims=True)
        emask2 = emask & (lane != i1)
        ev2 = jnp.where(emask2, lg, NEG)
        v2 = jnp.max(ev2, axis=-1, keepdims=True)
        i2 = jnp.min(jnp.where(emask2 & (ev2 == v2), lane, big), axis=-1, keepdims=True)
        e = jnp.exp(v2 - v1)
        den = 1.0 + e
        routed.append((i1, i2, (1.0 / den) * p_top, (e / den) * p_top))

    r_io = lax.broadcasted_iota(jnp.int32, (MOE_TILE, MOE_TILE), 0)
    c_io = lax.broadcasted_iota(jnp.int32, (MOE_TILE, MOE_TILE), 1)
    earlier_tok = (c_io < r_io).astype(BF16)
    lr_io = lax.broadcasted_iota(jnp.int32, (LANES, LANES), 0)
    lc_io = lax.broadcasted_iota(jnp.int32, (LANES, LANES), 1)
    earlier_lane = (lr_io < lc_io).astype(BF16)
    for hf, (rs, (i1, i2, w1, w2)) in enumerate(zip(subs, routed)):
        oh = [lane == i1, lane == i2]
        cnt = (oh[0] | oh[1]).astype(BF16)
        excl = _mm(earlier_tok, cnt)
        n = jnp.sum(cnt.astype(F32), axis=0, keepdims=True)
        npad = jnp.floor((n + (MOE_CHUNK - 1)) * (1.0 / MOE_CHUNK)) * MOE_CHUNK
        loff = _mm(jnp.broadcast_to(npad, (8, LANES)).astype(BF16), earlier_lane)
        pos = excl + loff[0:1, :]
        cnt_ref[hf] = jnp.broadcast_to(n, (8, LANES))
        lp0, lp1 = [jnp.sum(jnp.where(o, pos, 0.0), axis=-1, keepdims=True) for o in oh]
        cols = [(i1 - N_GROUPS).astype(F32), (i2 - N_GROUPS).astype(F32), w1, w2, lp0, lp1]
        ri = jnp.zeros((MOE_TILE, LANES), F32)
        for j, col in enumerate(cols):
            ri = jnp.where(lane == j, col, ri)
        ri_ref[0, rs, :] = ri
        rit_ref[:, rs] = ri.T


def _post(x, o_mla, o_ret, mod3, w_o, norm2_g, w_gr, b_gr, w_er, b_er, tm):
    bsz, s, d = x.shape
    w_r = jnp.concatenate([w_gr, w_er.reshape(d, N_EXPERTS), jnp.zeros((d, LANES - N_GROUPS - N_EXPERTS), F32)], axis=1)
    w_rh = w_r.astype(BF16)
    w_rl = (w_r - w_rh.astype(F32)).astype(BF16)
    w_r2 = jnp.concatenate([w_rh, w_rl], axis=1)
    b_r =jnp.concatenate([b_gr, b_er.reshape(-1), jnp.zeros((LANES - N_GROUPS - N_EXPERTS,), F32)]).reshape(1, LANES)
    tile = lambda w: pl.BlockSpec((1, tm, w), lambda b, i: (b, i, 0))
    const = lambda shape: pl.BlockSpec(shape, lambda b, i: (0,) * len(shape))
    per_b = s // tm
    sub = tm // MOE_TILE
    return pl.pallas_call(
        _post_kernel,
        grid=(bsz, per_b),
        in_specs=[tile(d), tile(o_mla.shape[-1]), tile(o_ret.shape[-1]), pl.BlockSpec((1, 6, d), lambda b, i: (b, 0, 0)),
                  const((d, d)), const((1, d)), const((d, 2 * LANES)), const((1, LANES))],
        out_specs=[tile(d), tile(d), tile(LANES),
                   pl.BlockSpec((LANES, tm), lambda b, i: (0, b * per_b + i)),
                   pl.BlockSpec((sub, 8, LANES), lambda b, i: (b * per_b + i, 0, 0))],
        out_shape=[jax.ShapeDtypeStruct((bsz, s, d), F32), jax.ShapeDtypeStruct((bsz, s, d), BF16),
                   jax.ShapeDtypeStruct((bsz, s, LANES), F32),
                   jax.ShapeDtypeStruct((LANES, bsz * s), F32),
                   jax.ShapeDtypeStruct((bsz * s // MOE_TILE, 8, LANES), F32)],
        compiler_params=_params("parallel", "arbitrary"),
        name="post_mixer",
    )(x, o_mla, o_ret, mod3, w_o.astype(BF16), norm2_g.reshape(1, d), w_r2, b_r)


def _chunk_rows(c):
    return pl.ds(pl.multiple_of(c * MOE_CHUNK, MOE_CHUNK), MOE_CHUNK)


def _dispatch_kernel(dmap_ref, nchk_ref, tstart_ref, tn_ref, nbr_ref, h2_ref, ri_ref, rit_ref, xs_hbm,
                     xloc, zblk, sem, zsem, bsem, *, nsteps, nb):
    i = pl.program_id(0)
    slot = lax.rem(i, 2)
    d = h2_ref.shape[-1]

    def zero_copy(e, c):
        return pltpu.make_async_copy(zblk.at[pl.ds(0, MOE_CHUNK)], xs_hbm.at[_chunk_rows(tstart_ref[e] + c)], zsem)

    def zero_block(j):
        rows = pl.ds(pl.multiple_of(j * MOE_BLOCK, MOE_BLOCK), MOE_BLOCK)
        return pltpu.make_async_copy(zblk, xs_hbm.at[rows], bsem)

    def chunk_copy(t, u, c, sl):
        return pltpu.make_async_copy(xloc.at[sl].at[u].at[_chunk_rows(c)],
                                     xs_hbm.at[_chunk_rows(dmap_ref[t * MOE_TILE_CHUNKS + c])], sem.at[sl])

    def wait_step(step, sl):
        for u in range(MOE_TPS):
            _wait_chunks(nchk_ref[step * MOE_TPS + u],
                         lambda rows: pltpu.make_async_copy(xloc.at[sl].at[0].at[rows], xs_hbm.at[rows], sem.at[sl]))

    @pl.when(i == 0)
    def _():
        zblk[...] = jnp.zeros_like(zblk)
        lax.fori_loop(nbr_ref[0], nb, lambda j, z: (zero_block(j).start(), z)[1], 0)
        for e in range(N_EXPERTS):
            lax.fori_loop(0, tn_ref[e], lambda c, z, e=e: (zero_copy(e, c).start(), z)[1], 0)
        for e in range(N_EXPERTS):
            lax.fori_loop(0, tn_ref[e], lambda c, z, e=e: (zero_copy(e, c).wait(), z)[1], 0)

    @pl.when(i >= 2)
    def _():
        wait_step(i - 2, slot)

    s_io = lax.broadcasted_iota(jnp.int32, (MOE_SLAB, MOE_TILE), 0).astype(F32)
    lane = lax.broadcasted_iota(jnp.int32, (MOE_TILE, LANES), 1)
    tiles = [slice(u * MOE_TILE, (u + 1) * MOE_TILE) for u in range(MOE_TPS)]
    pms = [[(s_io == rit_ref[4 + k:5 + k, ts]).astype(BF16) for k in range(TOP_K)] for ts in tiles]
    mains = [_mm(pm[0] + pm[1], h2_ref[ts, :]) for pm, ts in zip(pms, tiles)]
    for u, (pm, ts) in enumerate(zip(pms, tiles)):
        xloc[slot, u, :, 0:d] = mains[u].astype(BF16)
        wx = jnp.zeros((MOE_SLAB, LANES), F32)
        for k in range(TOP_K):
            w = ri_ref[ts, TOP_K + k:TOP_K + k + 1]
            hi = w.astype(BF16).astype(F32)
            wx = wx + _mm(pm[k], jnp.where(lane == 0, hi, jnp.where(lane == 1, w - hi, 0.0)).astype(BF16))
        xloc[slot, u, :, d:d + LANES] = wx.astype(BF16)

    for u in range(MOE_TPS):
        t = i * MOE_TPS + u
        lax.fori_loop(0, nchk_ref[t], lambda c, z, t=t, u=u: (chunk_copy(t, u, c, slot).start(), z)[1], 0)

    @pl.when(i == nsteps - 1)
    def _():
        wait_step(i, slot)
        if nsteps >= 2:
            wait_step(i - 1, 1 - slot)
        lax.fori_loop(nbr_ref[0], nb, lambda j, z: (zero_block(j).wait(), z)[1], 0)


def _wait_chunks(n, copy_of_rows):
    bit = MOE_TILE_CHUNKS
    while bit >= 1:
        @pl.when((n & bit) != 0)
        def _(bit=bit):
            copy_of_rows(pl.ds(0, bit * MOE_CHUNK)).wait()
        bit //= 2


def _dispatch(h2, ri, rit, tables, n_rows):
    t, d = h2.shape
    nt = t // MOE_TILE
    assert nt % MOE_TPS == 0 and MOE_SLAB >= TOP_K * MOE_TILE + N_EXPERTS * (MOE_CHUNK - 1)
    rows = MOE_TPS * MOE_TILE
    grid_spec = pltpu.PrefetchScalarGridSpec(
        num_scalar_prefetch=5,
        grid=(nt // MOE_TPS,),
        in_specs=[pl.BlockSpec((rows, d), lambda i, *_: (i, 0)),
                  pl.BlockSpec((rows, LANES), lambda i, *_: (i, 0)),
                  pl.BlockSpec((8, rows), lambda i, *_: (0, i))],
        out_specs=pl.BlockSpec(memory_space=pl.ANY),
        scratch_shapes=[pltpu.VMEM((2, MOE_TPS, MOE_SLAB, d + LANES), BF16), pltpu.VMEM((MOE_BLOCK, d + LANES), BF16),
                        pltpu.SemaphoreType.DMA((2,)), pltpu.SemaphoreType.DMA(()), pltpu.SemaphoreType.DMA(())])
    return pl.pallas_call(
        functools.partial(_dispatch_kernel, nsteps=nt // MOE_TPS, nb=n_rows // MOE_BLOCK),
        grid_spec=grid_spec,
        out_shape=jax.ShapeDtypeStruct((n_rows, d + LANES), BF16),
        compiler_params=_params("arbitrary"),
        name="moe_dispatch",
    )(tables["dmap"], tables["tile_chunks"], tables["tail_start"], tables["tail_n"], tables["n_blocks"],
      h2, ri, rit)


def _expert_kernel(be_ref, nbr_ref, xs_ref, w1_ref, w3_ref, w2_ref, y_ref):
    used = pl.program_id(0) < nbr_ref[0]

    @pl.when(jnp.logical_not(used))
    def _():
        y_ref[...] = jnp.zeros_like(y_ref)

    @pl.when(used)
    def _():
        d = y_ref.shape[-1]
        subs = [slice(i * MOE_TILE, (i + 1) * MOE_TILE) for i in range(y_ref.shape[0] // MOE_TILE)]
        ab = [(_mm(xs_ref[rs, 0:d], w1_ref[0]), _mm(xs_ref[rs, 0:d], w3_ref[0])) for rs in subs]
        hms = [(_silu(a) * b).astype(BF16) for a, b in ab]
        for rs, hm in zip(subs, hms):
            gw = xs_ref[rs, d:d + LANES].astype(F32)
            y_ref[rs, :] = (_mm(hm, w2_ref[0]) * (gw[:, 0:1] + gw[:, 1:2])).astype(BF16)


def _experts(xs, tables, w1, w3, w2):
    n_rows, dw = xs.shape
    d = dw - LANES
    nb = n_rows // MOE_BLOCK
    blk = lambda i, be, nbr: jnp.minimum(i, nbr[0] - 1)
    grid_spec = pltpu.PrefetchScalarGridSpec(
        num_scalar_prefetch=2,
        grid=(nb,),
        in_specs=[pl.BlockSpec((MOE_BLOCK, dw), lambda i, be, nbr: (blk(i, be, nbr), 0)),
                  pl.BlockSpec((1, d, D_EXPERT), lambda i, be, nbr: (be[blk(i, be, nbr)], 0, 0)),
                  pl.BlockSpec((1, d, D_EXPERT), lambda i, be, nbr: (be[blk(i, be, nbr)], 0, 0)),
                  pl.BlockSpec((1, D_EXPERT, d), lambda i, be, nbr: (be[blk(i, be, nbr)], 0, 0))],
        out_specs=pl.BlockSpec((MOE_BLOCK, d), lambda i, be, nbr: (i, 0)))
    return pl.pallas_call(
        _expert_kernel,
        grid_spec=grid_spec,
        out_shape=jax.ShapeDtypeStruct((n_rows, d), BF16),
        compiler_params=_params("arbitrary"),
        name="moe_experts",
    )(tables["blk_expert"], tables["n_blocks"], xs, w1.astype(BF16), w3.astype(BF16), w2.astype(BF16))


def _moe_rows(n_tiles):
    worst = n_tiles * (TOP_K * MOE_TILE + N_EXPERTS * (MOE_CHUNK - 1)) + N_EXPERTS * (MOE_BLOCK - MOE_CHUNK)
    return -(-worst // MOE_BLOCK) * MOE_BLOCK


def _moe_tables(cnt, n_rows):
    n_tiles = cnt.shape[0]
    per_blk = MOE_BLOCK // MOE_CHUNK

    def excl_cumsum(a, axis):
        n = a.shape[axis]
        lower = jnp.arange(n)[:, None] > jnp.arange(n)[None, :]
        if axis == 0:
            return jnp.sum(jnp.where(lower[:, :, None], a[None, :, :], 0), axis=1)
        return jnp.sum(jnp.where(lower[None, :, :], a[:, None, :], 0), axis=2)

    nch = (cnt + MOE_CHUNK - 1) // MOE_CHUNK
    loff = excl_cumsum(nch, 1)
    seg = jnp.sum(nch, axis=0)
    blocks = (seg + per_blk - 1) // per_blk
    bstart = excl_cumsum(blocks[None, :], 1)[0]
    bend = bstart + blocks
    estart = bstart * per_blk
    gbase = estart[None, :] + excl_cumsum(nch, 0)
    c = jnp.arange(MOE_TILE_CHUNKS, dtype=jnp.int32)
    owner = jnp.sum((c[None, :, None] >= (loff + nch)[:, None, :]).astype(jnp.int32), axis=-1)
    owner = jnp.minimum(owner, N_EXPERTS - 1)
    is_owner = owner[:, :, None] == jnp.arange(N_EXPERTS, dtype=jnp.int32)[None, None, :]
    dmap = c[None, :] + jnp.sum(jnp.where(is_owner, (gbase - loff)[:, None, :], 0), axis=-1)
    dmap = jnp.clip(dmap, 0, n_rows // MOE_CHUNK - 1)
    nb = n_rows // MOE_BLOCK
    blk_expert = jnp.sum((jnp.arange(nb, dtype=jnp.int32)[:, None] >= bend[None, :]).astype(jnp.int32), axis=1)
    blk_expert = jnp.minimum(blk_expert, N_EXPERTS - 1)
    return dict(dmap=dmap.reshape(-1).astype(jnp.int32), tile_chunks=jnp.sum(nch, axis=1).astype(jnp.int32),
                tail_start=(estart + seg).astype(jnp.int32), tail_n=(blocks * per_blk - seg).astype(jnp.int32),
                blk_expert=blk_expert.astype(jnp.int32), n_blocks=bend[-1:].astype(jnp.int32))


def _final_kernel(dmap_ref, nchk_ref, x1_ref, ri_ref, mod_ref, g_ref, y_hbm, o_ref, yloc, sem, *, nsteps):
    i = pl.program_id(0)
    slot = lax.rem(i, 2)

    def chunk_copy(t, u, c, sl):
        return pltpu.make_async_copy(y_hbm.at[_chunk_rows(dmap_ref[t * MOE_TILE_CHUNKS + c])],
                                     yloc.at[sl].at[u].at[_chunk_rows(c)], sem.at[sl])

    def gather(step, sl):
        for u in range(MOE_TPS):
            t = step * MOE_TPS + u
            lax.fori_loop(0, nchk_ref[t], lambda c, z, t=t, u=u: (chunk_copy(t, u, c, sl).start(), z)[1], 0)

    @pl.when(i == 0)
    def _():
        yloc[...] = jnp.zeros_like(yloc)
        gather(0, 0)

    @pl.when(i + 1 < nsteps)
    def _():
        gather(i + 1, 1 - slot)

    for u in range(MOE_TPS):
        _wait_chunks(nchk_ref[i * MOE_TPS + u],
                     lambda rows: pltpu.make_async_copy(y_hbm.at[rows], yloc.at[slot].at[0].at[rows], sem.at[slot]))

    l_io = lax.broadcasted_iota(jnp.int32, (MOE_TILE, MOE_SLAB), 1).astype(F32)
    tiles = [slice(u * MOE_TILE, (u + 1) * MOE_TILE) for u in range(MOE_TPS)]
    picks = [((l_io == ri_ref[ts, 4:5]) | (l_io == ri_ref[ts, 5:6])).astype(BF16) for ts in tiles]
    moes = [_mm(pick, yloc[slot, u]) for u, pick in enumerate(picks)]
    for ts, moe in zip(tiles, moes):
        x2 = x1_ref[ts, :] + mod_ref[0, 5:6, :] * moe
        o_ref[ts, :] = _rms(x2, g_ref[...])


def _final(x1, ri, y, mod3, final_g, tables, s):
    t, d = x1.shape
    rows = MOE_TPS * MOE_TILE
    nsteps = t // rows
    per_b = s // rows
    grid_spec = pltpu.PrefetchScalarGridSpec(
        num_scalar_prefetch=2,
        grid=(nsteps,),
        in_specs=[pl.BlockSpec((rows, d), lambda i, *_: (i, 0)),
                  pl.BlockSpec((rows, LANES), lambda i, *_: (i, 0)),
                  pl.BlockSpec((1, 6, d), lambda i, *_: (i // per_b, 0, 0)),
                  pl.BlockSpec((1, d), lambda i, *_: (0, 0)),
                  pl.BlockSpec(memory_space=pl.ANY)],
        out_specs=pl.BlockSpec((rows, d), lambda i, *_: (i, 0)),
        scratch_shapes=[pltpu.VMEM((2, MOE_TPS, MOE_SLAB, d), BF16), pltpu.SemaphoreType.DMA((2,))])
    return pl.pallas_call(
        functools.partial(_final_kernel, nsteps=nsteps),
        grid_spec=grid_spec,
        out_shape=jax.ShapeDtypeStruct((t, d), F32),
        compiler_params=_params("arbitrary"),
        name="moe_combine_final",
    )(tables["dmap"], tables["tile_chunks"], x1, ri, mod3, final_g.reshape(1, d), y)


def kernel(x, c, positions, w_ada, b_ada, norm1_g, w_in, q_norm_g, w_uq, kv_norm_g, w_ukv, w_o, norm2_g,
           w_gr, b_gr, w_er, b_er, w1, w3, w2, final_g):
    bsz, s, d = x.shape
    assert w_ada.shape[0] == 1, "one layer"
    tm = min(1024, s)
    tq = min(256, s)
    ts = min(1024, s)
    mod3 = _adaln(c, w_ada[0], b_ada[0]).reshape(bsz, 6, d)
    pos3 = positions.astype(F32).reshape(bsz, s, 1)
    q, k, vt, rq, rk, rv, rg = _pre(x, mod3, pos3, norm1_g[0], w_in[0], q_norm_g[0], w_uq[0], kv_norm_g[0], w_ukv[0],
                                    tm, tq)
    o_mla = _attention(q, k, vt, tq, ATTN_HEADS_PER_STEP)
    o_ret = _retention(rq, rk, rv, rg, ts)
    x1, h2, ri, rit, cnt = _post(x, o_mla, o_ret, mod3, w_o[0], norm2_g[0], w_gr[0], b_gr[0], w_er[0], b_er[0], tm)
    t = bsz * s
    n_rows = _moe_rows(t // MOE_TILE)
    counts = cnt[:, 0, N_GROUPS:N_GROUPS + N_EXPERTS].astype(jnp.int32)
    tables = _moe_tables(counts, n_rows)
    ri2 = ri.reshape(t, LANES)
    xs = _dispatch(h2.reshape(t, d), ri2, rit, tables, n_rows)
    y = _experts(xs, tables, w1[0], w3[0], w2[0])
    out = _final(x1.reshape(t, d), ri2, y, mod3, final_g, tables, s)
    return out.reshape(bsz, s, d)
```

```python
import functools

import jax
import jax.numpy as jnp
from jax import lax
from jax.experimental import pallas as pl
from jax.experimental.pallas import tpu as pltpu

MLA_HEADS = 8
MLA_NOPE = 64
MLA_ROPE = 32
MLA_V = 64
Q_LORA = 256
KV_LORA = 128
RET_HEADS = 4
RET_DK = 64
RET_DV = 128
RET_CHUNK = 128
ROPE_BASE = 10000.0
NORM_EPS = 1e-6
N_GROUPS = 4
EXPERTS_PER_GROUP = 8
N_EXPERTS = N_GROUPS * EXPERTS_PER_GROUP
TOP_K = 2
D_EXPERT = 256
MOE_TILE = 256
MOE_CHUNK = 16
MOE_SLAB = 1024
MOE_TILE_CHUNKS = MOE_SLAB // MOE_CHUNK
MOE_TPS = 2
MOE_BLOCK = 1024

LANES = 128
VMEM_LIMIT = 56 * 1024 * 1024

F32 = jnp.float32
BF16 = jnp.bfloat16
NEG = float(jnp.finfo(jnp.float32).min)
LOG2_E = 1.4426950408889634
ATTN_HEADS_PER_STEP = 8
ATTN_PAIRS_PER_ITER = 2
MLA_VROWS = MLA_V + 16

_C_Q = 0
_C_KV = _C_Q + Q_LORA
_C_KPE = _C_KV + KV_LORA
_C_RQ = _C_KPE + LANES
_C_RK = _C_RQ + RET_HEADS * RET_DK
_C_RV = _C_RK + RET_HEADS * RET_DK
_C_RG = _C_RV + RET_HEADS * RET_DV
_IN_PERM = _C_RG + RET_HEADS * RET_DV


def _silu(v):
    return v / (1.0 + jnp.exp(-v))


def _mm(a, b):
    return jnp.dot(a, b, preferred_element_type=F32)


def _mm_nt(a, b):
    return lax.dot_general(a, b, (((1,), (1,)), ((), ())), preferred_element_type=F32)


def _params(*sem):
    return pltpu.CompilerParams(dimension_semantics=sem, vmem_limit_bytes=VMEM_LIMIT)


def _adaln_kernel(c_ref, w_ref, b_ref, o_ref):
    a = _silu(c_ref[...]).astype(BF16)
    o_ref[...] = _mm(a, w_ref[...].astype(BF16)) + b_ref[...]


def _adaln(c, w_ada, b_ada):
    bsz, d = c.shape
    n = w_ada.shape[1]
    tn = d
    return pl.pallas_call(
        _adaln_kernel,
        grid=(n // tn,),
        in_specs=[pl.BlockSpec((bsz, d), lambda j: (0, 0)),
                  pl.BlockSpec((d, tn), lambda j: (0, j)),
                  pl.BlockSpec((1, tn), lambda j: (0, j))],
        out_specs=pl.BlockSpec((bsz, tn), lambda j: (0, j)),
        out_shape=jax.ShapeDtypeStruct((bsz, n), F32),
        compiler_params=_params("arbitrary"),
        name="adaln",
    )(c, w_ada, b_ada.reshape(1, n))


def _rms(v, g):
    return v * lax.rsqrt(jnp.mean(v * v, axis=-1, keepdims=True) + NORM_EPS) * g


def _pre_kernel(x_ref, mod_ref, pos_ref, g1_ref, win_ref, qg_ref, wuq_ref, kvg_ref, wuk_ref,
                wuv_ref, invf_ref, q_ref, k_ref, v_ref, rq_ref, rk_ref, rv_ref, rg_ref):
    tq = v_ref.shape[-1]
    subs = [slice(i * tq, (i + 1) * tq) for i in range(v_ref.shape[1])]
    sh1 = mod_ref[0, 0:1, :]
    sc1 = mod_ref[0, 1:2, :]
    projs = [_mm((_rms(x_ref[0, rs, :], g1_ref[...]) * (1.0 + sc1) + sh1).astype(BF16), win_ref[...]) for rs in subs]

    cqs = [_rms(p[:, _C_Q:_C_Q + Q_LORA], qg_ref[...]).astype(BF16) for p in projs]
    ckvs = [_rms(p[:, _C_KV:_C_KV + KV_LORA], kvg_ref[...]).astype(BF16) for p in projs]
    qas = [_mm(cq, wuq_ref[...]) for cq in cqs]
    kns = [_mm(ckv, wuk_ref[...]) for ckv in ckvs]
    vvs = [_mm(ckv, wuv_ref[...]) for ckv in ckvs]

    lane = lax.broadcasted_iota(jnp.int32, (tq, LANES), 1)
    hi = lane >= RET_DK
    half_m, half_r = MLA_ROPE // 2, RET_DK // 2
    first_m = hi & (lane < RET_DK + half_m)
    first_r = (lane & half_r) == 0
    scale = (MLA_NOPE + MLA_ROPE) ** -0.5 * LOG2_E

    def rope(v, cos, sin, first, half):
        partner = jnp.where(first, pltpu.roll(v, LANES - half, 1), pltpu.roll(v, half, 1))
        return v * cos + partner * sin

    for i, rs in enumerate(subs):
        proj = projs[i]
        ang = pos_ref[0, rs, :] * invf_ref[...]
        cs = jnp.cos(ang)
        sn = jnp.sin(ang)
        c_mla = jnp.where(hi, cs, 1.0)
        s_mla = jnp.where(hi, jnp.where(first_m, -sn, sn), 0.0)
        c_ret = jnp.where(hi, pltpu.roll(cs, RET_DK, 1), cs)
        s_ret = jnp.where(hi, pltpu.roll(sn, RET_DK, 1), sn)
        s_ret = jnp.where(first_r, -s_ret, s_ret)

        kpe = rope(proj[:, _C_KPE:_C_KPE + LANES], c_mla, s_mla, first_m, half_m)
        cq_s = c_mla * scale
        sq_s = jnp.where(hi, sn, 0.0) * scale
        for hd in range(MLA_HEADS):
            sl = slice(hd * LANES, (hd + 1) * LANES)
            sr = slice((MLA_HEADS + hd) * LANES, (MLA_HEADS + hd + 1) * LANES)
            q_ref[0, rs, sl] = (qas[i][:, sl] * cq_s + qas[i][:, sr] * sq_s).astype(BF16)
            k_ref[0, rs, sl] = (kns[i][:, sl] + kpe).astype(BF16)
        vt = vvs[i].T
        tail = jnp.where(lax.broadcasted_iota(jnp.int32, (MLA_VROWS - MLA_V, tq), 0) == 0, 1.0, 0.0)
        slab = [piece for hd in range(MLA_HEADS) for piece in (vt[hd * MLA_V:(hd + 1) * MLA_V, :], tail)]
        v_ref[0, i] = jnp.concatenate(slab, axis=0).astype(BF16)

        for j in range(RET_HEADS * RET_DK // LANES):
            o = j * LANES
            rq = rope(proj[:, _C_RQ + o:_C_RQ + o + LANES], c_ret, s_ret, first_r, half_r)
            rk = rope(proj[:, _C_RK + o:_C_RK + o + LANES], c_ret, s_ret, first_r, half_r)
            rq_ref[0, rs, o:o + LANES] = rq.astype(BF16)
            rk_ref[0, rs, o:o + LANES] = (rk * (RET_DK ** -0.5)).astype(BF16)
        rv_ref[0, rs, :] = proj[:, _C_RV:_C_RV + RET_HEADS * RET_DV].astype(BF16)
        rg_ref[0, rs, :] = proj[:, _C_RG:_C_RG + RET_HEADS * RET_DV]


def _pad_heads(w, width, left):
    k = w.shape[0]
    w3 = w.reshape(k, -1, width)
    w3 = jnp.pad(w3, ((0, 0), (0, 0), (left, LANES - left - width)))
    return w3.reshape(k, -1)


def _pre(x, mod3, pos3, norm1_g, w_in, q_norm_g, w_uq, kv_norm_g, w_ukv, tm, tq):
    bsz, s, d = x.shape
    o = 0
    parts = {}
    for name, width in (("cq", Q_LORA), ("ckv", KV_LORA), ("kr", MLA_ROPE), ("rq", RET_HEADS * RET_DK),
                        ("rk", RET_HEADS * RET_DK), ("rv", RET_HEADS * RET_DV), ("rg", RET_HEADS * RET_DV)):
        parts[name] = w_in[:, o:o + width]
        o += width
    w_in_p = jnp.concatenate([
        parts["cq"], parts["ckv"], _pad_heads(parts["kr"], MLA_ROPE, MLA_NOPE),
        parts["rq"], parts["rk"], parts["rv"], parts["rg"]], axis=1).astype(BF16)
    assert w_in_p.shape[1] == _IN_PERM
    wq_rope = w_uq.reshape(Q_LORA, MLA_HEADS, MLA_NOPE + MLA_ROPE)[:, :, MLA_NOPE:].reshape(Q_LORA, MLA_HEADS, 2, -1)
    wq_rot = jnp.stack([-wq_rope[:, :, 1], wq_rope[:, :, 0]], axis=2).reshape(Q_LORA, -1)
    w_uq_p = jnp.concatenate([_pad_heads(w_uq, MLA_NOPE + MLA_ROPE, 0),
                              _pad_heads(wq_rot, MLA_ROPE, MLA_NOPE)], axis=1).astype(BF16)
    wkv3 = w_ukv.reshape(KV_LORA, MLA_HEADS, MLA_NOPE + MLA_V)
    w_uk_p = _pad_heads(wkv3[:, :, :MLA_NOPE].reshape(KV_LORA, -1), MLA_NOPE, 0).astype(BF16)
    w_uv = wkv3[:, :, MLA_NOPE:].reshape(KV_LORA, -1).astype(BF16)
    half_r, half_m = RET_DK // 2, MLA_ROPE // 2
    f_r = ROPE_BASE ** (-(jnp.arange(half_r, dtype=F32) / half_r))
    f_m = ROPE_BASE ** (-(jnp.arange(half_m, dtype=F32) / half_m))
    invf = jnp.concatenate([f_r, f_r, f_m, f_m, jnp.zeros((LANES - 2 * half_r - 2 * half_m,), F32)]).reshape(1, LANES)

    hq = MLA_HEADS * LANES
    const = lambda shape: pl.BlockSpec(shape, lambda b, i: (0,) * len(shape))
    tile = lambda w: pl.BlockSpec((1, tm, w), lambda b, i: (b, i, 0))
    return pl.pallas_call(
        _pre_kernel,
        grid=(bsz, s // tm),
        in_specs=[tile(d), pl.BlockSpec((1, 6, d), lambda b, i: (b, 0, 0)), tile(1), const((1, d)),
                  const((d, _IN_PERM)), const((1, Q_LORA)), const((Q_LORA, 2 * hq)),
                  const((1, KV_LORA)), const((KV_LORA, hq)), const((KV_LORA, MLA_HEADS * MLA_V)), const((1, LANES))],
        out_specs=[tile(hq), tile(hq),
                   pl.BlockSpec((1, tm // tq, MLA_HEADS * MLA_VROWS, tq), lambda b, i: (b, i, 0, 0)),
                   tile(RET_HEADS * RET_DK), tile(RET_HEADS * RET_DK),
                   tile(RET_HEADS * RET_DV), tile(RET_HEADS * RET_DV)],
        out_shape=[jax.ShapeDtypeStruct((bsz, s, hq), BF16), jax.ShapeDtypeStruct((bsz, s, hq), BF16),
                   jax.ShapeDtypeStruct((bsz, s // tq, MLA_HEADS * MLA_VROWS, tq), BF16),
                   jax.ShapeDtypeStruct((bsz, s, RET_HEADS * RET_DK), BF16),
                   jax.ShapeDtypeStruct((bsz, s, RET_HEADS * RET_DK), BF16),
                   jax.ShapeDtypeStruct((bsz, s, RET_HEADS * RET_DV), BF16),
                   jax.ShapeDtypeStruct((bsz, s, RET_HEADS * RET_DV), F32)],
        compiler_params=_params("parallel", "arbitrary"),
        name="pre_mixer",
    )(x, mod3, pos3, norm1_g.reshape(1, d), w_in_p, q_norm_g.reshape(1, -1), w_uq_p,
      kv_norm_g.reshape(1, -1), w_uk_p, w_uv, invf)


def _attn_kernel(q_ref, k_ref, vt_ref, o_ref, st_x, st_y, bm_x, bm_y, *, tq, kb, hps):
    qi = pl.program_id(1)
    assert tq == 2 * kb, "a query tile spans two key blocks: the last two blocks of a tile are masked"
    key = lax.broadcasted_iota(jnp.int32, (kb, tq), 0)
    qry = lax.broadcasted_iota(jnp.int32, (kb, tq), 1)
    hsl = [slice(hh * LANES, (hh + 1) * LANES) for hh in range(hps)]
    vsl = [slice(hh * MLA_VROWS, (hh + 1) * MLA_VROWS) for hh in range(hps)]
    qs = [q_ref[0, :, hs] for hs in hsl]
    bufs = {"x": (st_x, bm_x), "y": (st_y, bm_y)}
    n_blk = 2 * qi + 2

    def scores(blk, buf, diag=None):
        st_ref, bm_ref = bufs[buf]
        start = pl.multiple_of(blk * kb, kb)
        for hh in range(hps):
            st = _mm_nt(k_ref[0, pl.ds(start, kb), hsl[hh]], qs[hh])
            if diag is not None:
                st = jnp.where(key + diag * kb <= qry, st, NEG)
            st_ref[hh] = st
            bm_ref[hh, 0:1, :] = jnp.max(st, axis=0, keepdims=True)

    def update(blk, buf, carry):
        st_ref, bm_ref = bufs[buf]
        out = []
        for hh in range(hps):
            m, acc = carry[hh]
            m_new = jnp.maximum(m, bm_ref[hh, 0:1, :])
            p = jnp.exp2(st_ref[hh] - m_new).astype(BF16)
            out.append((m_new, jnp.exp2(m - m_new) * acc + _mm(vt_ref[0, blk, vsl[hh], :], p)))
        return tuple(out)

    def when_loop(pred, body, carry):
        return lax.fori_loop(0, pred.astype(jnp.int32), lambda _, c: body(c), carry)

    init = (jnp.full((1, tq), NEG, F32), jnp.zeros((MLA_VROWS, tq), F32))
    carry = (init,) * hps

    @pl.when(qi >= 1)
    def _():
        scores(0, "x")

    def steady(base, pairs, c):
        for r in range(pairs):
            scores(base + 2 * r + 1, "y")
            c = update(base + 2 * r, "x", c)
            scores(base + 2 * r + 2, "x")
            c = update(base + 2 * r + 1, "y", c)
        return c

    n_pairs = jnp.maximum(qi - 1, 0)
    n_long = n_pairs // ATTN_PAIRS_PER_ITER
    carry = lax.fori_loop(0, n_long, lambda i, c: steady(2 * ATTN_PAIRS_PER_ITER * i, ATTN_PAIRS_PER_ITER, c), carry)
    done = n_long * ATTN_PAIRS_PER_ITER
    carry = lax.fori_loop(done, n_pairs, lambda i, c: steady(2 * i, 1, c), carry)

    def tail(c):
        scores(n_blk - 3, "y")
        c = update(n_blk - 4, "x", c)
        scores(n_blk - 2, "x", diag=0)
        c = update(n_blk - 3, "y", c)
        scores(n_blk - 1, "y", diag=1)
        c = update(n_blk - 2, "x", c)
        return update(n_blk - 1, "y", c)

    def tail_first(c):
        scores(0, "x", diag=0)
        scores(1, "y", diag=1)
        c = update(0, "x", c)
        return update(1, "y", c)

    carry = when_loop(qi >= 1, tail, carry)
    carry = when_loop(qi == 0, tail_first, carry)
    out_t = jnp.concatenate([acc[0:MLA_V] * (1.0 / acc[MLA_V:MLA_V + 1]) for _, acc in carry], axis=0)
    o_ref[0] = out_t.T.astype(BF16)


def _attention(q, k, vt, kb, hps):
    bsz, s, _ = q.shape
    groups = MLA_HEADS // hps
    tq = 2 * kb
    assert vt.shape == (bsz, s // kb, MLA_HEADS * MLA_VROWS, kb)
    return pl.pallas_call(
        functools.partial(_attn_kernel, tq=tq, kb=kb, hps=hps),
        grid=(bsz * groups, s // tq),
        in_specs=[pl.BlockSpec((1, tq, hps * LANES), lambda g, i: (g // groups, i, g % groups)),
                  pl.BlockSpec((1, s, hps * LANES), lambda g, i: (g // groups, 0, g % groups)),
                  pl.BlockSpec((1, s // kb, hps * MLA_VROWS, kb), lambda g, i: (g // groups, 0, g % groups, 0))],
        out_specs=pl.BlockSpec((1, tq, hps * MLA_V), lambda g, i: (g // groups, i, g % groups)),
        out_shape=jax.ShapeDtypeStruct((bsz, s, MLA_HEADS * MLA_V), BF16),
        scratch_shapes=[pltpu.VMEM((hps, kb, tq), F32), pltpu.VMEM((hps, kb, tq), F32),
                        pltpu.VMEM((hps, 8, tq), F32), pltpu.VMEM((hps, 8, tq), F32)],
        compiler_params=_params("parallel", "arbitrary"),
        name="mla_attention",
    )(q, k, vt)


def _ret_kernel(rq_ref, rk_ref, rv_ref, rg_ref, dm_ref, xi_ref, zt_ref, dc_ref, o_ref, st_ref, *, ts):
    @pl.when(pl.program_id(1) == 0)
    def _():
        st_ref[...] = jnp.zeros_like(st_ref)

    lane = lax.broadcasted_iota(jnp.int32, (RET_CHUNK, LANES), 1)
    units = [(c, hd) for c in range(ts // RET_CHUNK) for hd in range(RET_HEADS)]
    rows = lambda c: slice(c * RET_CHUNK, (c + 1) * RET_CHUNK)
    vsl = lambda hd: slice(hd * RET_DV, (hd + 1) * RET_DV)

    def head_lanes(ref, c, hd):
        pair, sub = divmod(hd, LANES // RET_DK)
        mine = (lane >= sub * RET_DK) & (lane < (sub + 1) * RET_DK)
        return jnp.where(mine, ref[0, rows(c), pair * LANES:(pair + 1) * LANES], 0.0).astype(BF16)

    qh = {u: head_lanes(rq_ref, *u) for u in units}
    kh = {u: head_lanes(rk_ref, *u) for u in units}
    vh = {(c, hd): rv_ref[0, rows(c), vsl(hd)] for c, hd in units}
    sc = {u: _mm_nt(qh[u], kh[u]) for u in units}
    un = {u: _mm((kh[u].astype(F32) * zt_ref[u[1]]).astype(BF16).T, vh[u]) for u in units}
    prev = {}
    for hd in range(RET_HEADS):
        st = st_ref[hd]
        for c in range(ts // RET_CHUNK):
            prev[(c, hd)] = st.astype(BF16)
            st = st * dc_ref[hd] + un[(c, hd)]
        st_ref[hd] = st
    for c, hd in units:
        u = (c, hd)
        o = _mm((sc[u] * dm_ref[hd]).astype(BF16), vh[u])
        o = o + _mm((qh[u].astype(F32) * xi_ref[hd]).astype(BF16), prev[u])
        mu = jnp.mean(o, axis=-1, keepdims=True)
        oc = o - mu
        on = oc * lax.rsqrt(jnp.mean(oc * oc, axis=-1, keepdims=True) + NORM_EPS)
        o_ref[0, rows(c), vsl(hd)] = (_silu(rg_ref[0, rows(c), vsl(hd)]) * on).astype(BF16)


def _retention(rq, rk, rv, rg, ts):
    bsz, s, _ = rq.shape
    c = RET_CHUNK
    gamma = 1.0 - jnp.power(2.0, -5.0 - jnp.arange(RET_HEADS, dtype=F32))
    log_g = jnp.log(gamma)
    idx = jnp.arange(c, dtype=F32)
    diff = idx[:, None] - idx[None, :]
    dmask = jnp.where(diff[None] >= 0, jnp.exp(jnp.maximum(diff, 0.0)[None] * log_g[:, None, None]), 0.0)
    zeta = jnp.exp((c - 1.0 - idx)[None, :] * log_g[:, None])
    xi = jnp.exp((idx + 1.0)[None, :] * log_g[:, None])
    decay = jnp.exp(c * log_g)
    xi_b = jnp.broadcast_to(xi[:, :, None], (RET_HEADS, c, LANES))
    zt_b = jnp.broadcast_to(zeta[:, :, None], (RET_HEADS, c, LANES))
    dc_b = jnp.broadcast_to(decay[:, None, None], (RET_HEADS, LANES, RET_DV))

    tile = lambda w: pl.BlockSpec((1, ts, w), lambda b, i: (b, i, 0))
    const = lambda shape: pl.BlockSpec(shape, lambda b, i: (0,) * len(shape))
    return pl.pallas_call(
        functools.partial(_ret_kernel, ts=ts),
        grid=(bsz, s // ts),
        in_specs=[tile(RET_HEADS * RET_DK), tile(RET_HEADS * RET_DK), tile(RET_HEADS * RET_DV), tile(RET_HEADS * RET_DV),
                  const((RET_HEADS, c, c)), const((RET_HEADS, c, LANES)), const((RET_HEADS, c, LANES)),
                  const((RET_HEADS, LANES, RET_DV))],
        out_specs=tile(RET_HEADS * RET_DV),
        out_shape=jax.ShapeDtypeStruct((bsz, s, RET_HEADS * RET_DV), BF16),
        scratch_shapes=[pltpu.VMEM((RET_HEADS, LANES, RET_DV), F32)],
        compiler_params=_params("parallel", "arbitrary"),
        name="retention",
    )(rq, rk, rv, rg, dmask, xi_b, zt_b, dc_b)


def _post_kernel(x_ref, om_ref, or_ref, mod_ref, wo_ref, g2_ref, wr_ref, br_ref,
                 x1_ref, h2_ref, ri_ref, rit_ref, cnt_ref):
    half = om_ref.shape[-1]
    subs = [slice(i * MOE_TILE, (i + 1) * MOE_TILE) for i in range(x_ref.shape[1] // MOE_TILE)]
    mixes = [_mm(om_ref[0, rs, :], wo_ref[0:half, :]) + _mm(or_ref[0, rs, :], wo_ref[half:, :]) for rs in subs]

    lgs = []
    for rs, mix in zip(subs, mixes):
        x1 = x_ref[0, rs, :] + mod_ref[0, 2:3, :] * mix
        x1_ref[0, rs, :] = x1
        h2 = _rms(x1, g2_ref[...]) * (1.0 + mod_ref[0, 4:5, :]) + mod_ref[0, 3:4, :]
        hi = h2.astype(BF16)
        h2_ref[0, rs, :] = hi
        lo = (h2 - hi.astype(F32)).astype(BF16)
        both = _mm(hi, wr_ref[...])
        lgs.append(both[:, 0:LANES] + both[:, LANES:2 * LANES] + _mm(lo, wr_ref[:, 0:LANES]) + br_ref[...])

    lane = lax.broadcasted_iota(jnp.int32, (MOE_TILE, LANES), 1)
    big = jnp.int32(1 << 20)
    gmask = lane < N_GROUPS
    el = lane - N_GROUPS
    assert EXPERTS_PER_GROUP == 8
    routed = []
    for lg in lgs:
        gmax = jnp.max(jnp.where(gmask, lg, NEG), axis=-1, keepdims=True)
        ge = jnp.where(gmask, jnp.exp(lg - gmax), 0.0)
        pg = ge / jnp.sum(ge, axis=-1, keepdims=True)
        p_top = jnp.max(pg, axis=-1, keepdims=True)
        g_top = jnp.min(jnp.where(gmask & (pg == p_top), lane, big), axis=-1, keepdims=True)

        emask = (el >= 0) & (el < N_EXPERTS) & (lax.shift_right_arithmetic(el, 3) == g_top)
        ev = jnp.where(emask, lg, NEG)
        v1 = jnp.max(ev, axis=-1, keepdims=True)
        i1 = jnp.min(jnp.where(emask & (ev == v1), lane, big), axis=-1, keepdims=True)
        emask2 = emask & (lane != i1)
        ev2 = jnp.where(emask2, lg, NEG)
        v2 = jnp.max(ev2, axis=-1, keepdims=True)
        i2 = jnp.min(jnp.where(emask2 & (ev2 == v2), lane, big), axis=-1, keepdims=True)
        e = jnp.exp(v2 - v1)
        den = 1.0 + e
        routed.append((i1, i2, (1.0 / den) * p_top, (e / den) * p_top))

    r_io = lax.broadcasted_iota(jnp.int32, (MOE_TILE, MOE_TILE), 0)
    c_io = lax.broadcasted_iota(jnp.int32, (MOE_TILE, MOE_TILE), 1)
    earlier_tok = (c_io < r_io).astype(BF16)
    lr_io = lax.broadcasted_iota(jnp.int32, (LANES, LANES), 0)
    lc_io = lax.broadcasted_iota(jnp.int32, (LANES, LANES), 1)
    earlier_lane = (lr_io < lc_io).astype(BF16)
    for hf, (rs, (i1, i2, w1, w2)) in enumerate(zip(subs, routed)):
        oh = [lane == i1, lane == i2]
        cnt = (oh[0] | oh[1]).astype(BF16)
        excl = _mm(earlier_tok, cnt)
        n = jnp.sum(cnt.astype(F32), axis=0, keepdims=True)
        npad = jnp.floor((n + (MOE_CHUNK - 1)) * (1.0 / MOE_CHUNK)) * MOE_CHUNK
        loff = _mm(jnp.broadcast_to(npad, (8, LANES)).astype(BF16), earlier_lane)
        pos = excl + loff[0:1, :]
        cnt_ref[hf] = jnp.broadcast_to(n, (8, LANES))
        lp0, lp1 = [jnp.sum(jnp.where(o, pos, 0.0), axis=-1, keepdims=True) for o in oh]
        cols = [(i1 - N_GROUPS).astype(F32), (i2 - N_GROUPS).astype(F32), w1, w2, lp0, lp1]
        ri = jnp.zeros((MOE_TILE, LANES), F32)
        for j, col in enumerate(cols):
            ri = jnp.where(lane == j, col, ri)
        ri_ref[0, rs, :] = ri
        rit_ref[:, rs] = ri.T


def _post(x, o_mla, o_ret, mod3, w_o, norm2_g, w_gr, b_gr, w_er, b_er, tm):
    bsz, s, d = x.shape
    w_r = jnp.concatenate([w_gr, w_er.reshape(d, N_EXPERTS), jnp.zeros((d, LANES - N_GROUPS - N_EXPERTS), F32)], axis=1)
    w_rh = w_r.astype(BF16)
    w_rl = (w_r - w_rh.astype(F32)).astype(BF16)
    w_r2 = jnp.concatenate([w_rh, w_rl], axis=1)
    b_r =jnp.concatenate([b_gr, b_er.reshape(-1), jnp.zeros((LANES - N_GROUPS - N_EXPERTS,), F32)]).reshape(1, LANES)
    tile = lambda w: pl.BlockSpec((1, tm, w), lambda b, i: (b, i, 0))
    const = lambda shape: pl.BlockSpec(shape, lambda b, i: (0,) * len(shape))
    per_b = s // tm
    sub = tm // MOE_TILE
    return pl.pallas_call(
        _post_kernel,
        grid=(bsz, per_b),
        in_specs=[tile(d), tile(o_mla.shape[-1]), tile(o_ret.shape[-1]), pl.BlockSpec((1, 6, d), lambda b, i: (b, 0, 0)),
                  const((d, d)), const((1, d)), const((d, 2 * LANES)), const((1, LANES))],
        out_specs=[tile(d), tile(d), tile(LANES),
                   pl.BlockSpec((LANES, tm), lambda b, i: (0, b * per_b + i)),
                   pl.BlockSpec((sub, 8, LANES), lambda b, i: (b * per_b + i, 0, 0))],
        out_shape=[jax.ShapeDtypeStruct((bsz, s, d), F32), jax.ShapeDtypeStruct((bsz, s, d), BF16),
                   jax.ShapeDtypeStruct((bsz, s, LANES), F32),
                   jax.ShapeDtypeStruct((LANES, bsz * s), F32),
                   jax.ShapeDtypeStruct((bsz * s // MOE_TILE, 8, LANES), F32)],
        compiler_params=_params("parallel", "arbitrary"),
        name="post_mixer",
    )(x, o_mla, o_ret, mod3, w_o.astype(BF16), norm2_g.reshape(1, d), w_r2, b_r)


def _chunk_rows(c):
    return pl.ds(pl.multiple_of(c * MOE_CHUNK, MOE_CHUNK), MOE_CHUNK)


def _dispatch_kernel(dmap_ref, nchk_ref, tstart_ref, tn_ref, nbr_ref, h2_ref, ri_ref, rit_ref, xs_hbm,
                     xloc, zblk, sem, zsem, bsem, *, nsteps, nb):
    i = pl.program_id(0)
    slot = lax.rem(i, 2)
    d = h2_ref.shape[-1]

    def zero_copy(e, c):
        return pltpu.make_async_copy(zblk.at[pl.ds(0, MOE_CHUNK)], xs_hbm.at[_chunk_rows(tstart_ref[e] + c)], zsem)

    def zero_block(j):
        rows = pl.ds(pl.multiple_of(j * MOE_BLOCK, MOE_BLOCK), MOE_BLOCK)
        return pltpu.make_async_copy(zblk, xs_hbm.at[rows], bsem)

    def chunk_copy(t, u, c, sl):
        return pltpu.make_async_copy(xloc.at[sl].at[u].at[_chunk_rows(c)],
                                     xs_hbm.at[_chunk_rows(dmap_ref[t * MOE_TILE_CHUNKS + c])], sem.at[sl])

    def wait_step(step, sl):
        for u in range(MOE_TPS):
            _wait_chunks(nchk_ref[step * MOE_TPS + u],
                         lambda rows: pltpu.make_async_copy(xloc.at[sl].at[0].at[rows], xs_hbm.at[rows], sem.at[sl]))

    @pl.when(i == 0)
    def _():
        zblk[...] = jnp.zeros_like(zblk)
        lax.fori_loop(nbr_ref[0], nb, lambda j, z: (zero_block(j).start(), z)[1], 0)
        for e in range(N_EXPERTS):
            lax.fori_loop(0, tn_ref[e], lambda c, z, e=e: (zero_copy(e, c).start(), z)[1], 0)
        for e in range(N_EXPERTS):
            lax.fori_loop(0, tn_ref[e], lambda c, z, e=e: (zero_copy(e, c).wait(), z)[1], 0)

    @pl.when(i >= 2)
    def _():
        wait_step(i - 2, slot)

    s_io = lax.broadcasted_iota(jnp.int32, (MOE_SLAB, MOE_TILE), 0).astype(F32)
    lane = lax.broadcasted_iota(jnp.int32, (MOE_TILE, LANES), 1)
    tiles = [slice(u * MOE_TILE, (u + 1) * MOE_TILE) for u in range(MOE_TPS)]
    pms = [[(s_io == rit_ref[4 + k:5 + k, ts]).astype(BF16) for k in range(TOP_K)] for ts in tiles]
    mains = [_mm(pm[0] + pm[1], h2_ref[ts, :]) for pm, ts in zip(pms, tiles)]
    for u, (pm, ts) in enumerate(zip(pms, tiles)):
        xloc[slot, u, :, 0:d] = mains[u].astype(BF16)
        wx = jnp.zeros((MOE_SLAB, LANES), F32)
        for k in range(TOP_K):
            w = ri_ref[ts, TOP_K + k:TOP_K + k + 1]
            hi = w.astype(BF16).astype(F32)
            wx = wx + _mm(pm[k], jnp.where(lane == 0, hi, jnp.where(lane == 1, w - hi, 0.0)).astype(BF16))
        xloc[slot, u, :, d:d + LANES] = wx.astype(BF16)

    for u in range(MOE_TPS):
        t = i * MOE_TPS + u
        lax.fori_loop(0, nchk_ref[t], lambda c, z, t=t, u=u: (chunk_copy(t, u, c, slot).start(), z)[1], 0)

    @pl.when(i == nsteps - 1)
    def _():
        wait_step(i, slot)
        if nsteps >= 2:
            wait_step(i - 1, 1 - slot)
        lax.fori_loop(nbr_ref[0], nb, lambda j, z: (zero_block(j).wait(), z)[1], 0)


def _wait_chunks(n, copy_of_rows):
    bit = MOE_TILE_CHUNKS
    while bit >= 1:
        @pl.when((n & bit) != 0)
        def _(bit=bit):
            copy_of_rows(pl.ds(0, bit * MOE_CHUNK)).wait()
        bit //= 2


def _dispatch(h2, ri, rit, tables, n_rows):
    t, d = h2.shape
    nt = t // MOE_TILE
    assert nt % MOE_TPS == 0 and MOE_SLAB >= TOP_K * MOE_TILE + N_EXPERTS * (MOE_CHUNK - 1)
    rows = MOE_TPS * MOE_TILE
    grid_spec = pltpu.PrefetchScalarGridSpec(
        num_scalar_prefetch=5,
        grid=(nt // MOE_TPS,),
        in_specs=[pl.BlockSpec((rows, d), lambda i, *_: (i, 0)),
                  pl.BlockSpec((rows, LANES), lambda i, *_: (i, 0)),
                  pl.BlockSpec((8, rows), lambda i, *_: (0, i))],
        out_specs=pl.BlockSpec(memory_space=pl.ANY),
        scratch_shapes=[pltpu.VMEM((2, MOE_TPS, MOE_SLAB, d + LANES), BF16), pltpu.VMEM((MOE_BLOCK, d + LANES), BF16),
                        pltpu.SemaphoreType.DMA((2,)), pltpu.SemaphoreType.DMA(()), pltpu.SemaphoreType.DMA(())])
    return pl.pallas_call(
        functools.partial(_dispatch_kernel, nsteps=nt // MOE_TPS, nb=n_rows // MOE_BLOCK),
        grid_spec=grid_spec,
        out_shape=jax.ShapeDtypeStruct((n_rows, d + LANES), BF16),
        compiler_params=_params("arbitrary"),
        name="moe_dispatch",
    )(tables["dmap"], tables["tile_chunks"], tables["tail_start"], tables["tail_n"], tables["n_blocks"],
      h2, ri, rit)


def _expert_kernel(be_ref, nbr_ref, xs_ref, w1_ref, w3_ref, w2_ref, y_ref):
    used = pl.program_id(0) < nbr_ref[0]

    @pl.when(jnp.logical_not(used))
    def _():
        y_ref[...] = jnp.zeros_like(y_ref)

    @pl.when(used)
    def _():
        d = y_ref.shape[-1]
        subs = [slice(i * MOE_TILE, (i + 1) * MOE_TILE) for i in range(y_ref.shape[0] // MOE_TILE)]
        ab = [(_mm(xs_ref[rs, 0:d], w1_ref[0]), _mm(xs_ref[rs, 0:d], w3_ref[0])) for rs in subs]
        hms = [(_silu(a) * b).astype(BF16) for a, b in ab]
        for rs, hm in zip(subs, hms):
            gw = xs_ref[rs, d:d + LANES].astype(F32)
            y_ref[rs, :] = (_mm(hm, w2_ref[0]) * (gw[:, 0:1] + gw[:, 1:2])).astype(BF16)


def _experts(xs, tables, w1, w3, w2):
    n_rows, dw = xs.shape
    d = dw - LANES
    nb = n_rows // MOE_BLOCK
    blk = lambda i, be, nbr: jnp.minimum(i, nbr[0] - 1)
    grid_spec = pltpu.PrefetchScalarGridSpec(
        num_scalar_prefetch=2,
        grid=(nb,),
        in_specs=[pl.BlockSpec((MOE_BLOCK, dw), lambda i, be, nbr: (blk(i, be, nbr), 0)),
                  pl.BlockSpec((1, d, D_EXPERT), lambda i, be, nbr: (be[blk(i, be, nbr)], 0, 0)),
                  pl.BlockSpec((1, d, D_EXPERT), lambda i, be, nbr: (be[blk(i, be, nbr)], 0, 0)),
                  pl.BlockSpec((1, D_EXPERT, d), lambda i, be, nbr: (be[blk(i, be, nbr)], 0, 0))],
        out_specs=pl.BlockSpec((MOE_BLOCK, d), lambda i, be, nbr: (i, 0)))
    return pl.pallas_call(
        _expert_kernel,
        grid_spec=grid_spec,
        out_shape=jax.ShapeDtypeStruct((n_rows, d), BF16),
        compiler_params=_params("arbitrary"),
        name="moe_experts",
    )(tables["blk_expert"], tables["n_blocks"], xs, w1.astype(BF16), w3.astype(BF16), w2.astype(BF16))


def _moe_rows(n_tiles):
    worst = n_tiles * (TOP_K * MOE_TILE + N_EXPERTS * (MOE_CHUNK - 1)) + N_EXPERTS * (MOE_BLOCK - MOE_CHUNK)
    return -(-worst // MOE_BLOCK) * MOE_BLOCK


def _moe_tables(cnt, n_rows):
    n_tiles = cnt.shape[0]
    per_blk = MOE_BLOCK // MOE_CHUNK

    def excl_cumsum(a, axis):
        n = a.shape[axis]
        lower = jnp.arange(n)[:, None] > jnp.arange(n)[None, :]
        if axis == 0:
            return jnp.sum(jnp.where(lower[:, :, None], a[None, :, :], 0), axis=1)
        return jnp.sum(jnp.where(lower[None, :, :], a[:, None, :], 0), axis=2)

    nch = (cnt + MOE_CHUNK - 1) // MOE_CHUNK
    loff = excl_cumsum(nch, 1)
    seg = jnp.sum(nch, axis=0)
    blocks = (seg + per_blk - 1) // per_blk
    bstart = excl_cumsum(blocks[None, :], 1)[0]
    bend = bstart + blocks
    estart = bstart * per_blk
    gbase = estart[None, :] + excl_cumsum(nch, 0)
    c = jnp.arange(MOE_TILE_CHUNKS, dtype=jnp.int32)
    owner = jnp.sum((c[None, :, None] >= (loff + nch)[:, None, :]).astype(jnp.int32), axis=-1)
    owner = jnp.minimum(owner, N_EXPERTS - 1)
    is_owner = owner[:, :, None] == jnp.arange(N_EXPERTS, dtype=jnp.int32)[None, None, :]
    dmap = c[None, :] + jnp.sum(jnp.where(is_owner, (gbase - loff)[:, None, :], 0), axis=-1)
    dmap = jnp.clip(dmap, 0, n_rows // MOE_CHUNK - 1)
    nb = n_rows // MOE_BLOCK
    blk_expert = jnp.sum((jnp.arange(nb, dtype=jnp.int32)[:, None] >= bend[None, :]).astype(jnp.int32), axis=1)
    blk_expert = jnp.minimum(blk_expert, N_EXPERTS - 1)
    return dict(dmap=dmap.reshape(-1).astype(jnp.int32), tile_chunks=jnp.sum(nch, axis=1).astype(jnp.int32),
                tail_start=(estart + seg).astype(jnp.int32), tail_n=(blocks * per_blk - seg).astype(jnp.int32),
                blk_expert=blk_expert.astype(jnp.int32), n_blocks=bend[-1:].astype(jnp.int32))


def _final_kernel(dmap_ref, nchk_ref, x1_ref, ri_ref, mod_ref, g_ref, y_hbm, o_ref, yloc, sem, *, nsteps):
    i = pl.program_id(0)
    slot = lax.rem(i, 2)

    def chunk_copy(t, u, c, sl):
        return pltpu.make_async_copy(y_hbm.at[_chunk_rows(dmap_ref[t * MOE_TILE_CHUNKS + c])],
                                     yloc.at[sl].at[u].at[_chunk_rows(c)], sem.at[sl])

    def gather(step, sl):
        for u in range(MOE_TPS):
            t = step * MOE_TPS + u
            lax.fori_loop(0, nchk_ref[t], lambda c, z, t=t, u=u: (chunk_copy(t, u, c, sl).start(), z)[1], 0)

    @pl.when(i == 0)
    def _():
        yloc[...] = jnp.zeros_like(yloc)
        gather(0, 0)

    @pl.when(i + 1 < nsteps)
    def _():
        gather(i + 1, 1 - slot)

    for u in range(MOE_TPS):
        _wait_chunks(nchk_ref[i * MOE_TPS + u],
                     lambda rows: pltpu.make_async_copy(y_hbm.at[rows], yloc.at[slot].at[0].at[rows], sem.at[slot]))

    l_io = lax.broadcasted_iota(jnp.int32, (MOE_TILE, MOE_SLAB), 1).astype(F32)
    tiles = [slice(u * MOE_TILE, (u + 1) * MOE_TILE) for u in range(MOE_TPS)]
    picks = [((l_io == ri_ref[ts, 4:5]) | (l_io == ri_ref[ts, 5:6])).astype(BF16) for ts in tiles]
    moes = [_mm(pick, yloc[slot, u]) for u, pick in enumerate(picks)]
    for ts, moe in zip(tiles, moes):
        x2 = x1_ref[ts, :] + mod_ref[0, 5:6, :] * moe
        o_ref[ts, :] = _rms(x2, g_ref[...])


def _final(x1, ri, y, mod3, final_g, tables, s):
    t, d = x1.shape
    rows = MOE_TPS * MOE_TILE
    nsteps = t // rows
    per_b = s // rows
    grid_spec = pltpu.PrefetchScalarGridSpec(
        num_scalar_prefetch=2,
        grid=(nsteps,),
        in_specs=[pl.BlockSpec((rows, d), lambda i, *_: (i, 0)),
                  pl.BlockSpec((rows, LANES), lambda i, *_: (i, 0)),
                  pl.BlockSpec((1, 6, d), lambda i, *_: (i // per_b, 0, 0)),
                  pl.BlockSpec((1, d), lambda i, *_: (0, 0)),
                  pl.BlockSpec(memory_space=pl.ANY)],
        out_specs=pl.BlockSpec((rows, d), lambda i, *_: (i, 0)),
        scratch_shapes=[pltpu.VMEM((2, MOE_TPS, MOE_SLAB, d), BF16), pltpu.SemaphoreType.DMA((2,))])
    return pl.pallas_call(
        functools.partial(_final_kernel, nsteps=nsteps),
        grid_spec=grid_spec,
        out_shape=jax.ShapeDtypeStruct((t, d), F32),
        compiler_params=_params("arbitrary"),
        name="moe_combine_final",
    )(tables["dmap"], tables["tile_chunks"], x1, ri, mod3, final_g.reshape(1, d), y)


def kernel(x, c, positions, w_ada, b_ada, norm1_g, w_in, q_norm_g, w_uq, kv_norm_g, w_ukv, w_o, norm2_g,
           w_gr, b_gr, w_er, b_er, w1, w3, w2, final_g):
    bsz, s, d = x.shape
    assert w_ada.shape[0] == 1, "one layer"
    tm = min(1024, s)
    tq = min(256, s)
    ts = min(1024, s)
    mod3 = _adaln(c, w_ada[0], b_ada[0]).reshape(bsz, 6, d)
    pos3 = positions.astype(F32).reshape(bsz, s, 1)
    q, k, vt, rq, rk, rv, rg = _pre(x, mod3, pos3, norm1_g[0], w_in[0], q_norm_g[0], w_uq[0], kv_norm_g[0], w_ukv[0],
                                    tm, tq)
    o_mla = _attention(q, k, vt, tq, ATTN_HEADS_PER_STEP)
    o_ret = _retention(rq, rk, rv, rg, ts)
    x1, h2, ri, rit, cnt = _post(x, o_mla, o_ret, mod3, w_o[0], norm2_g[0], w_gr[0], b_gr[0], w_er[0], b_er[0], tm)
    t = bsz * s
    n_rows = _moe_rows(t // MOE_TILE)
    counts = cnt[:, 0, N_GROUPS:N_GROUPS + N_EXPERTS].astype(jnp.int32)
    tables = _moe_tables(counts, n_rows)
    ri2 = ri.reshape(t, LANES)
    xs = _dispatch(h2.reshape(t, d), ri2, rit, tables, n_rows)
    y = _experts(xs, tables, w1[0], w3[0], w2[0])
    out = _final(x1.reshape(t, d), ri2, y, mod3, final_g, tables, s)
    return out.reshape(bsz, s, d)
```

```python
import functools

import jax
import jax.numpy as jnp
from jax import lax
from jax.experimental import pallas as pl
from jax.experimental.pallas import tpu as pltpu

MLA_HEADS = 8
MLA_NOPE = 64
MLA_ROPE = 32
MLA_V = 64
Q_LORA = 256
KV_LORA = 128
RET_HEADS = 4
RET_DK = 64
RET_DV = 128
RET_CHUNK = 128
ROPE_BASE = 10000.0
NORM_EPS = 1e-6
N_GROUPS = 4
EXPERTS_PER_GROUP = 8
N_EXPERTS = N_GROUPS * EXPERTS_PER_GROUP
TOP_K = 2
D_EXPERT = 256
MOE_TILE = 256
MOE_CHUNK = 16
MOE_SLAB = 1024
MOE_TILE_CHUNKS = MOE_SLAB // MOE_CHUNK
MOE_TPS = 2
MOE_BLOCK = 1024

LANES = 128
VMEM_LIMIT = 56 * 1024 * 1024

F32 = jnp.float32
BF16 = jnp.bfloat16
NEG = float(jnp.finfo(jnp.float32).min)
LOG2_E = 1.4426950408889634
ATTN_HEADS_PER_STEP = 8
ATTN_PAIRS_PER_ITER = 2
MLA_VROWS = MLA_V + 16

_C_Q = 0
_C_KV = _C_Q + Q_LORA
_C_KPE = _C_KV + KV_LORA
_C_RQ = _C_KPE + LANES
_C_RK = _C_RQ + RET_HEADS * RET_DK
_C_RV = _C_RK + RET_HEADS * RET_DK
_C_RG = _C_RV + RET_HEADS * RET_DV
_IN_PERM = _C_RG + RET_HEADS * RET_DV


def _silu(v):
    return v / (1.0 + jnp.exp(-v))


def _mm(a, b):
    return jnp.dot(a, b, preferred_element_type=F32)


def _mm_nt(a, b):
    return lax.dot_general(a, b, (((1,), (1,)), ((), ())), preferred_element_type=F32)


def _params(*sem):
    return pltpu.CompilerParams(dimension_semantics=sem, vmem_limit_bytes=VMEM_LIMIT)


def _adaln_kernel(c_ref, w_ref, b_ref, o_ref):
    a = _silu(c_ref[...]).astype(BF16)
    o_ref[...] = _mm(a, w_ref[...].astype(BF16)) + b_ref[...]


def _adaln(c, w_ada, b_ada):
    bsz, d = c.shape
    n = w_ada.shape[1]
    tn = d
    return pl.pallas_call(
        _adaln_kernel,
        grid=(n // tn,),
        in_specs=[pl.BlockSpec((bsz, d), lambda j: (0, 0)),
                  pl.BlockSpec((d, tn), lambda j: (0, j)),
                  pl.BlockSpec((1, tn), lambda j: (0, j))],
        out_specs=pl.BlockSpec((bsz, tn), lambda j: (0, j)),
        out_shape=jax.ShapeDtypeStruct((bsz, n), F32),
        compiler_params=_params("arbitrary"),
        name="adaln",
    )(c, w_ada, b_ada.reshape(1, n))


def _rms(v, g):
    return v * lax.rsqrt(jnp.mean(v * v, axis=-1, keepdims=True) + NORM_EPS) * g


def _pre_kernel(x_ref, mod_ref, pos_ref, g1_ref, win_ref, qg_ref, wuq_ref, kvg_ref, wuk_ref,
                wuv_ref, invf_ref, q_ref, k_ref, v_ref, rq_ref, rk_ref, rv_ref, rg_ref):
    tq = v_ref.shape[-1]
    subs = [slice(i * tq, (i + 1) * tq) for i in range(v_ref.shape[1])]
    sh1 = mod_ref[0, 0:1, :]
    sc1 = mod_ref[0, 1:2, :]
    projs = [_mm((_rms(x_ref[0, rs, :], g1_ref[...]) * (1.0 + sc1) + sh1).astype(BF16), win_ref[...]) for rs in subs]

    cqs = [_rms(p[:, _C_Q:_C_Q + Q_LORA], qg_ref[...]).astype(BF16) for p in projs]
    ckvs = [_rms(p[:, _C_KV:_C_KV + KV_LORA], kvg_ref[...]).astype(BF16) for p in projs]
    qas = [_mm(cq, wuq_ref[...]) for cq in cqs]
    kns = [_mm(ckv, wuk_ref[...]) for ckv in ckvs]
    vvs = [_mm(ckv, wuv_ref[...]) for ckv in ckvs]

    lane = lax.broadcasted_iota(jnp.int32, (tq, LANES), 1)
    hi = lane >= RET_DK
    half_m, half_r = MLA_ROPE // 2, RET_DK // 2
    first_m = hi & (lane < RET_DK + half_m)
    first_r = (lane & half_r) == 0
    scale = (MLA_NOPE + MLA_ROPE) ** -0.5 * LOG2_E

    def rope(v, cos, sin, first, half):
        partner = jnp.where(first, pltpu.roll(v, LANES - half, 1), pltpu.roll(v, half, 1))
        return v * cos + partner * sin

    for i, rs in enumerate(subs):
        proj = projs[i]
        ang = pos_ref[0, rs, :] * invf_ref[...]
        cs = jnp.cos(ang)
        sn = jnp.sin(ang)
        c_mla = jnp.where(hi, cs, 1.0)
        s_mla = jnp.where(hi, jnp.where(first_m, -sn, sn), 0.0)
        c_ret = jnp.where(hi, pltpu.roll(cs, RET_DK, 1), cs)
        s_ret = jnp.where(hi, pltpu.roll(sn, RET_DK, 1), sn)
        s_ret = jnp.where(first_r, -s_ret, s_ret)

        kpe = rope(proj[:, _C_KPE:_C_KPE + LANES], c_mla, s_mla, first_m, half_m)
        cq_s = c_mla * scale
        sq_s = jnp.where(hi, sn, 0.0) * scale
        for hd in range(MLA_HEADS):
            sl = slice(hd * LANES, (hd + 1) * LANES)
            sr = slice((MLA_HEADS + hd) * LANES, (MLA_HEADS + hd + 1) * LANES)
            q_ref[0, rs, sl] = (qas[i][:, sl] * cq_s + qas[i][:, sr] * sq_s).astype(BF16)
            k_ref[0, rs, sl] = (kns[i][:, sl] + kpe).astype(BF16)
        vt = vvs[i].T
        tail = jnp.where(lax.broadcasted_iota(jnp.int32, (MLA_VROWS - MLA_V, tq), 0) == 0, 1.0, 0.0)
        slab = [piece for hd in range(MLA_HEADS) for piece in (vt[hd * MLA_V:(hd + 1) * MLA_V, :], tail)]
        v_ref[0, i] = jnp.concatenate(slab, axis=0).astype(BF16)

        for j in range(RET_HEADS * RET_DK // LANES):
            o = j * LANES
            rq = rope(proj[:, _C_RQ + o:_C_RQ + o + LANES], c_ret, s_ret, first_r, half_r)
            rk = rope(proj[:, _C_RK + o:_C_RK + o + LANES], c_ret, s_ret, first_r, half_r)
            rq_ref[0, rs, o:o + LANES] = rq.astype(BF16)
            rk_ref[0, rs, o:o + LANES] = (rk * (RET_DK ** -0.5)).astype(BF16)
        rv_ref[0, rs, :] = proj[:, _C_RV:_C_RV + RET_HEADS * RET_DV].astype(BF16)
        rg_ref[0, rs, :] = proj[:, _C_RG:_C_RG + RET_HEADS * RET_DV]


def _pad_heads(w, width, left):
    k = w.shape[0]
    w3 = w.reshape(k, -1, width)
    w3 = jnp.pad(w3, ((0, 0), (0, 0), (left, LANES - left - width)))
    return w3.reshape(k, -1)


def _pre(x, mod3, pos3, norm1_g, w_in, q_norm_g, w_uq, kv_norm_g, w_ukv, tm, tq):
    bsz, s, d = x.shape
    o = 0
    parts = {}
    for name, width in (("cq", Q_LORA), ("ckv", KV_LORA), ("kr", MLA_ROPE), ("rq", RET_HEADS * RET_DK),
                        ("rk", RET_HEADS * RET_DK), ("rv", RET_HEADS * RET_DV), ("rg", RET_HEADS * RET_DV)):
        parts[name] = w_in[:, o:o + width]
        o += width
    w_in_p = jnp.concatenate([
        parts["cq"], parts["ckv"], _pad_heads(parts["kr"], MLA_ROPE, MLA_NOPE),
        parts["rq"], parts["rk"], parts["rv"], parts["rg"]], axis=1).astype(BF16)
    assert w_in_p.shape[1] == _IN_PERM
    wq_rope = w_uq.reshape(Q_LORA, MLA_HEADS, MLA_NOPE + MLA_ROPE)[:, :, MLA_NOPE:].reshape(Q_LORA, MLA_HEADS, 2, -1)
    wq_rot = jnp.stack([-wq_rope[:, :, 1], wq_rope[:, :, 0]], axis=2).reshape(Q_LORA, -1)
    w_uq_p = jnp.concatenate([_pad_heads(w_uq, MLA_NOPE + MLA_ROPE, 0),
                              _pad_heads(wq_rot, MLA_ROPE, MLA_NOPE)], axis=1).astype(BF16)
    wkv3 = w_ukv.reshape(KV_LORA, MLA_HEADS, MLA_NOPE + MLA_V)
    w_uk_p = _pad_heads(wkv3[:, :, :MLA_NOPE].reshape(KV_LORA, -1), MLA_NOPE, 0).astype(BF16)
    w_uv = wkv3[:, :, MLA_NOPE:].reshape(KV_LORA, -1).astype(BF16)
    half_r, half_m = RET_DK // 2, MLA_ROPE // 2
    f_r = ROPE_BASE ** (-(jnp.arange(half_r, dtype=F32) / half_r))
    f_m = ROPE_BASE ** (-(jnp.arange(half_m, dtype=F32) / half_m))
    invf = jnp.concatenate([f_r, f_r, f_m, f_m, jnp.zeros((LANES - 2 * half_r - 2 * half_m,), F32)]).reshape(1, LANES)

    hq = MLA_HEADS * LANES
    const = lambda shape: pl.BlockSpec(shape, lambda b, i: (0,) * len(shape))
    tile = lambda w: pl.BlockSpec((1, tm, w), lambda b, i: (b, i, 0))
    return pl.pallas_call(
        _pre_kernel,
        grid=(bsz, s // tm),
        in_specs=[tile(d), pl.BlockSpec((1, 6, d), lambda b, i: (b, 0, 0)), tile(1), const((1, d)),
                  const((d, _IN_PERM)), const((1, Q_LORA)), const((Q_LORA, 2 * hq)),
                  const((1, KV_LORA)), const((KV_LORA, hq)), const((KV_LORA, MLA_HEADS * MLA_V)), const((1, LANES))],
        out_specs=[tile(hq), tile(hq),
                   pl.BlockSpec((1, tm // tq, MLA_HEADS * MLA_VROWS, tq), lambda b, i: (b, i, 0, 0)),
                   tile(RET_HEADS * RET_DK), tile(RET_HEADS * RET_DK),
                   tile(RET_HEADS * RET_DV), tile(RET_HEADS * RET_DV)],
        out_shape=[jax.ShapeDtypeStruct((bsz, s, hq), BF16), jax.ShapeDtypeStruct((bsz, s, hq), BF16),
                   jax.ShapeDtypeStruct((bsz, s // tq, MLA_HEADS * MLA_VROWS, tq), BF16),
                   jax.ShapeDtypeStruct((bsz, s, RET_HEADS * RET_DK), BF16),
                   jax.ShapeDtypeStruct((bsz, s, RET_HEADS * RET_DK), BF16),
                   jax.ShapeDtypeStruct((bsz, s, RET_HEADS * RET_DV), BF16),
                   jax.ShapeDtypeStruct((bsz, s, RET_HEADS * RET_DV), F32)],
        compiler_params=_params("parallel", "arbitrary"),
        name="pre_mixer",
    )(x, mod3, pos3, norm1_g.reshape(1, d), w_in_p, q_norm_g.reshape(1, -1), w_uq_p,
      kv_norm_g.reshape(1, -1), w_uk_p, w_uv, invf)


def _attn_kernel(q_ref, k_ref, vt_ref, o_ref, st_x, st_y, bm_x, bm_y, *, tq, kb, hps):
    qi = pl.program_id(1)
    assert tq == 2 * kb, "a query tile spans two key blocks: the last two blocks of a tile are masked"
    key = lax.broadcasted_iota(jnp.int32, (kb, tq), 0)
    qry = lax.broadcasted_iota(jnp.int32, (kb, tq), 1)
    hsl = [slice(hh * LANES, (hh + 1) * LANES) for hh in range(hps)]
    vsl = [slice(hh * MLA_VROWS, (hh + 1) * MLA_VROWS) for hh in range(hps)]
    qs = [q_ref[0, :, hs] for hs in hsl]
    bufs = {"x": (st_x, bm_x), "y": (st_y, bm_y)}
    n_blk = 2 * qi + 2

    def scores(blk, buf, diag=None):
        st_ref, bm_ref = bufs[buf]
        start = pl.multiple_of(blk * kb, kb)
        for hh in range(hps):
            st = _mm_nt(k_ref[0, pl.ds(start, kb), hsl[hh]], qs[hh])
            if diag is not None:
                st = jnp.where(key + diag * kb <= qry, st, NEG)
            st_ref[hh] = st
            bm_ref[hh, 0:1, :] = jnp.max(st, axis=0, keepdims=True)

    def update(blk, buf, carry):
        st_ref, bm_ref = bufs[buf]
        out = []
        for hh in range(hps):
            m, acc = carry[hh]
            m_new = jnp.maximum(m, bm_ref[hh, 0:1, :])
            p = jnp.exp2(st_ref[hh] - m_new).astype(BF16)
            out.append((m_new, jnp.exp2(m - m_new) * acc + _mm(vt_ref[0, blk, vsl[hh], :], p)))
        return tuple(out)

    def when_loop(pred, body, carry):
        return lax.fori_loop(0, pred.astype(jnp.int32), lambda _, c: body(c), carry)

    init = (jnp.full((1, tq), NEG, F32), jnp.zeros((MLA_VROWS, tq), F32))
    carry = (init,) * hps

    @pl.when(qi >= 1)
    def _():
        scores(0, "x")

    def steady(base, pairs, c):
        for r in range(pairs):
            scores(base + 2 * r + 1, "y")
            c = update(base + 2 * r, "x", c)
            scores(base + 2 * r + 2, "x")
            c = update(base + 2 * r + 1, "y", c)
        return c

    n_pairs = jnp.maximum(qi - 1, 0)
    n_long = n_pairs // ATTN_PAIRS_PER_ITER
    carry = lax.fori_loop(0, n_long, lambda i, c: steady(2 * ATTN_PAIRS_PER_ITER * i, ATTN_PAIRS_PER_ITER, c), carry)
    done = n_long * ATTN_PAIRS_PER_ITER
    carry = lax.fori_loop(done, n_pairs, lambda i, c: steady(2 * i, 1, c), carry)

    def tail(c):
        scores(n_blk - 3, "y")
        c = update(n_blk - 4, "x", c)
        scores(n_blk - 2, "x", diag=0)
        c = update(n_blk - 3, "y", c)
        scores(n_blk - 1, "y", diag=1)
        c = update(n_blk - 2, "x", c)
        return update(n_blk - 1, "y", c)

    def tail_first(c):
        scores(0, "x", diag=0)
        scores(1, "y", diag=1)
        c = update(0, "x", c)
        return update(1, "y", c)

    carry = when_loop(qi >= 1, tail, carry)
    carry = when_loop(qi == 0, tail_first, carry)
    out_t = jnp.concatenate([acc[0:MLA_V] * (1.0 / acc[MLA_V:MLA_V + 1]) for _, acc in carry], axis=0)
    o_ref[0] = out_t.T.astype(BF16)


def _attention(q, k, vt, kb, hps):
    bsz, s, _ = q.shape
    groups = MLA_HEADS // hps
    tq = 2 * kb
    assert vt.shape == (bsz, s // kb, MLA_HEADS * MLA_VROWS, kb)
    return pl.pallas_call(
        functools.partial(_attn_kernel, tq=tq, kb=kb, hps=hps),
        grid=(bsz * groups, s // tq),
        in_specs=[pl.BlockSpec((1, tq, hps * LANES), lambda g, i: (g // groups, i, g % groups)),
                  pl.BlockSpec((1, s, hps * LANES), lambda g, i: (g // groups, 0, g % groups)),
                  pl.BlockSpec((1, s // kb, hps * MLA_VROWS, kb), lambda g, i: (g // groups, 0, g % groups, 0))],
        out_specs=pl.BlockSpec((1, tq, hps * MLA_V), lambda g, i: (g // groups, i, g % groups)),
        out_shape=jax.ShapeDtypeStruct((bsz, s, MLA_HEADS * MLA_V), BF16),
        scratch_shapes=[pltpu.VMEM((hps, kb, tq), F32), pltpu.VMEM((hps, kb, tq), F32),
                        pltpu.VMEM((hps, 8, tq), F32), pltpu.VMEM((hps, 8, tq), F32)],
        compiler_params=_params("parallel", "arbitrary"),
        name="mla_attention",
    )(q, k, vt)


def _ret_kernel(rq_ref, rk_ref, rv_ref, rg_ref, dm_ref, xi_ref, zt_ref, dc_ref, o_ref, st_ref, *, ts):
    @pl.when(pl.program_id(1) == 0)
    def _():
        st_ref[...] = jnp.zeros_like(st_ref)

    lane = lax.broadcasted_iota(jnp.int32, (RET_CHUNK, LANES), 1)
    row = lax.broadcasted_iota(jnp.int32, (LANES, RET_DV), 0)
    hpt = LANES // RET_DK
    n_chunks = ts // RET_CHUNK
    tiles = [(c, p) for c in range(n_chunks) for p in range(RET_HEADS // hpt)]
    units = [(c, hd) for c in range(n_chunks) for hd in range(RET_HEADS)]
    rows = lambda c: slice(c * RET_CHUNK, (c + 1) * RET_CHUNK)
    vsl = lambda hd: slice(hd * RET_DV, (hd + 1) * RET_DV)
    mine = [(lane >= sub * RET_DK) & (lane < (sub + 1) * RET_DK) for sub in range(hpt)]
    mine_row = [(row >= sub * RET_DK) & (row < (sub + 1) * RET_DK) for sub in range(hpt)]

    q2 = {(c, p): rq_ref[0, rows(c), p * LANES:(p + 1) * LANES] for c, p in tiles}
    k2 = {(c, p): rk_ref[0, rows(c), p * LANES:(p + 1) * LANES] for c, p in tiles}
    vh = {(c, hd): rv_ref[0, rows(c), vsl(hd)] for c, hd in units}
    sc, un = {}, {}
    for c, p in tiles:
        heads = [p * hpt + sub for sub in range(hpt)]
        kcat = jnp.concatenate([jnp.where(mine[sub], k2[(c, p)], 0.0).astype(BF16) for sub in range(hpt)], axis=0)
        s_all = _mm_nt(q2[(c, p)], kcat)
        kz = jnp.zeros((RET_CHUNK, LANES), F32)
        for sub, hd in enumerate(heads):
            sc[(c, hd)] = s_all[:, sub * RET_CHUNK:(sub + 1) * RET_CHUNK]
            kz = kz + jnp.where(mine[sub], k2[(c, p)].astype(F32) * zt_ref[hd], 0.0)
        u_all = _mm(kz.astype(BF16).T, jnp.concatenate([vh[(c, hd)] for hd in heads], axis=1))
        for sub, hd in enumerate(heads):
            un[(c, hd)] = jnp.where(mine_row[sub], u_all[:, sub * RET_DV:(sub + 1) * RET_DV], 0.0)
    prev = {}
    for hd in range(RET_HEADS):
        st = st_ref[hd]
        for c in range(n_chunks):
            prev[(c, hd)] = st.astype(BF16)
            st = st * dc_ref[hd] + un[(c, hd)]
        st_ref[hd] = st
    for c, hd in units:
        u = (c, hd)
        lhs = jnp.concatenate([(sc[u] * dm_ref[hd]).astype(BF16),
                               (q2[(c, hd // hpt)].astype(F32) * xi_ref[hd]).astype(BF16)], axis=1)
        o = _mm(lhs, jnp.concatenate([vh[u], prev[u]], axis=0))
        mu = jnp.mean(o, axis=-1, keepdims=True)
        oc = o - mu
        on = oc * lax.rsqrt(jnp.mean(oc * oc, axis=-1, keepdims=True) + NORM_EPS)
        o_ref[0, rows(c), vsl(hd)] = (_silu(rg_ref[0, rows(c), vsl(hd)]) * on).astype(BF16)


def _retention(rq, rk, rv, rg, ts):
    bsz, s, _ = rq.shape
    c = RET_CHUNK
    gamma = 1.0 - jnp.power(2.0, -5.0 - jnp.arange(RET_HEADS, dtype=F32))
    log_g = jnp.log(gamma)
    idx = jnp.arange(c, dtype=F32)
    diff = idx[:, None] - idx[None, :]
    dmask = jnp.where(diff[None] >= 0, jnp.exp(jnp.maximum(diff, 0.0)[None] * log_g[:, None, None]), 0.0)
    zeta = jnp.exp((c - 1.0 - idx)[None, :] * log_g[:, None])
    xi = jnp.exp((idx + 1.0)[None, :] * log_g[:, None])
    decay = jnp.exp(c * log_g)
    xi_b = jnp.broadcast_to(xi[:, :, None], (RET_HEADS, c, LANES))
    zt_b = jnp.broadcast_to(zeta[:, :, None], (RET_HEADS, c, LANES))
    dc_b = jnp.broadcast_to(decay[:, None, None], (RET_HEADS, LANES, RET_DV))

    tile = lambda w: pl.BlockSpec((1, ts, w), lambda b, i: (b, i, 0))
    const = lambda shape: pl.BlockSpec(shape, lambda b, i: (0,) * len(shape))
    return pl.pallas_call(
        functools.partial(_ret_kernel, ts=ts),
        grid=(bsz, s // ts),
        in_specs=[tile(RET_HEADS * RET_DK), tile(RET_HEADS * RET_DK), tile(RET_HEADS * RET_DV), tile(RET_HEADS * RET_DV),
                  const((RET_HEADS, c, c)), const((RET_HEADS, c, LANES)), const((RET_HEADS, c, LANES)),
                  const((RET_HEADS, LANES, RET_DV))],
        out_specs=tile(RET_HEADS * RET_DV),
        out_shape=jax.ShapeDtypeStruct((bsz, s, RET_HEADS * RET_DV), BF16),
        scratch_shapes=[pltpu.VMEM((RET_HEADS, LANES, RET_DV), F32)],
        compiler_params=_params("parallel", "arbitrary"),
        name="retention",
    )(rq, rk, rv, rg, dmask, xi_b, zt_b, dc_b)


def _post_kernel(x_ref, om_ref, or_ref, mod_ref, wo_ref, g2_ref, wr_ref, br_ref,
                 x1_ref, h2_ref, ri_ref, rit_ref, cnt_ref):
    half = om_ref.shape[-1]
    subs = [slice(i * MOE_TILE, (i + 1) * MOE_TILE) for i in range(x_ref.shape[1] // MOE_TILE)]
    mixes = [_mm(om_ref[0, rs, :], wo_ref[0:half, :]) + _mm(or_ref[0, rs, :], wo_ref[half:, :]) for rs in subs]

    lgs = []
    for rs, mix in zip(subs, mixes):
        x1 = x_ref[0, rs, :] + mod_ref[0, 2:3, :] * mix
        x1_ref[0, rs, :] = x1
        h2 = _rms(x1, g2_ref[...]) * (1.0 + mod_ref[0, 4:5, :]) + mod_ref[0, 3:4, :]
        hi = h2.astype(BF16)
        h2_ref[0, rs, :] = hi
        lo = (h2 - hi.astype(F32)).astype(BF16)
        both = _mm(hi, wr_ref[...])
        lgs.append(both[:, 0:LANES] + both[:, LANES:2 * LANES] + _mm(lo, wr_ref[:, 0:LANES]) + br_ref[...])

    lane = lax.broadcasted_iota(jnp.int32, (MOE_TILE, LANES), 1)
    big = jnp.int32(1 << 20)
    gmask = lane < N_GROUPS
    el = lane - N_GROUPS
    assert EXPERTS_PER_GROUP == 8
    routed = []
    for lg in lgs:
        gmax = jnp.max(jnp.where(gmask, lg, NEG), axis=-1, keepdims=True)
        ge = jnp.where(gmask, jnp.exp(lg - gmax), 0.0)
        pg = ge / jnp.sum(ge, axis=-1, keepdims=True)
        p_top = jnp.max(pg, axis=-1, keepdims=True)
        g_top = jnp.min(jnp.where(gmask & (pg == p_top), lane, big), axis=-1, keepdims=True)

        emask = (el >= 0) & (el < N_EXPERTS) & (lax.shift_right_arithmetic(el, 3) == g_top)
        ev = jnp.where(emask, lg, NEG)
        v1 = jnp.max(ev, axis=-1, keepdims=True)
        i1 = jnp.min(jnp.where(emask & (ev == v1), lane, big), axis=-1, keepdims=True)
        emask2 = emask & (lane != i1)
        ev2 = jnp.where(emask2, lg, NEG)
        v2 = jnp.max(ev2, axis=-1, keepdims=True)
        i2 = jnp.min(jnp.where(emask2 & (ev2 == v2), lane, big), axis=-1, keepdims=True)
        e = jnp.exp(v2 - v1)
        den = 1.0 + e
        routed.append((i1, i2, (1.0 / den) * p_top, (e / den) * p_top))

    r_io = lax.broadcasted_iota(jnp.int32, (MOE_TILE, MOE_TILE), 0)
    c_io = lax.broadcasted_iota(jnp.int32, (MOE_TILE, MOE_TILE), 1)
    earlier_tok = (c_io < r_io).astype(BF16)
    lr_io = lax.broadcasted_iota(jnp.int32, (LANES, LANES), 0)
    lc_io = lax.broadcasted_iota(jnp.int32, (LANES, LANES), 1)
    earlier_lane = (lr_io < lc_io).astype(BF16)
    for hf, (rs, (i1, i2, w1, w2)) in enumerate(zip(subs, routed)):
        oh = [lane == i1, lane == i2]
        cnt = (oh[0] | oh[1]).astype(BF16)
        excl = _mm(earlier_tok, cnt)
        n = jnp.sum(cnt.astype(F32), axis=0, keepdims=True)
        npad = jnp.floor((n + (MOE_CHUNK - 1)) * (1.0 / MOE_CHUNK)) * MOE_CHUNK
        loff = _mm(jnp.broadcast_to(npad, (8, LANES)).astype(BF16), earlier_lane)
        pos = excl + loff[0:1, :]
        cnt_ref[hf] = jnp.broadcast_to(n, (8, LANES))
        lp0, lp1 = [jnp.sum(jnp.where(o, pos, 0.0), axis=-1, keepdims=True) for o in oh]
        cols = [(i1 - N_GROUPS).astype(F32), (i2 - N_GROUPS).astype(F32), w1, w2, lp0, lp1]
        ri = jnp.zeros((MOE_TILE, LANES), F32)
        for j, col in enumerate(cols):
            ri = jnp.where(lane == j, col, ri)
        ri_ref[0, rs, :] = ri
        rit_ref[:, rs] = ri.T


def _post(x, o_mla, o_ret, mod3, w_o, norm2_g, w_gr, b_gr, w_er, b_er, tm):
    bsz, s, d = x.shape
    w_r = jnp.concatenate([w_gr, w_er.reshape(d, N_EXPERTS), jnp.zeros((d, LANES - N_GROUPS - N_EXPERTS), F32)], axis=1)
    w_rh = w_r.astype(BF16)
    w_rl = (w_r - w_rh.astype(F32)).astype(BF16)
    w_r2 = jnp.concatenate([w_rh, w_rl], axis=1)
    b_r =jnp.concatenate([b_gr, b_er.reshape(-1), jnp.zeros((LANES - N_GROUPS - N_EXPERTS,), F32)]).reshape(1, LANES)
    tile = lambda w: pl.BlockSpec((1, tm, w), lambda b, i: (b, i, 0))
    const = lambda shape: pl.BlockSpec(shape, lambda b, i: (0,) * len(shape))
    per_b = s // tm
    sub = tm // MOE_TILE
    return pl.pallas_call(
        _post_kernel,
        grid=(bsz, per_b),
        in_specs=[tile(d), tile(o_mla.shape[-1]), tile(o_ret.shape[-1]), pl.BlockSpec((1, 6, d), lambda b, i: (b, 0, 0)),
                  const((d, d)), const((1, d)), const((d, 2 * LANES)), const((1, LANES))],
        out_specs=[tile(d), tile(d), tile(LANES),
                   pl.BlockSpec((LANES, tm), lambda b, i: (0, b * per_b + i)),
                   pl.BlockSpec((sub, 8, LANES), lambda b, i: (b * per_b + i, 0, 0))],
        out_shape=[jax.ShapeDtypeStruct((bsz, s, d), F32), jax.ShapeDtypeStruct((bsz, s, d), BF16),
                   jax.ShapeDtypeStruct((bsz, s, LANES), F32),
                   jax.ShapeDtypeStruct((LANES, bsz * s), F32),
                   jax.ShapeDtypeStruct((bsz * s // MOE_TILE, 8, LANES), F32)],
        compiler_params=_params("parallel", "arbitrary"),
        name="post_mixer",
    )(x, o_mla, o_ret, mod3, w_o.astype(BF16), norm2_g.reshape(1, d), w_r2, b_r)


def _chunk_rows(c):
    return pl.ds(pl.multiple_of(c * MOE_CHUNK, MOE_CHUNK), MOE_CHUNK)


def _dispatch_kernel(dmap_ref, nchk_ref, tstart_ref, tn_ref, nbr_ref, h2_ref, ri_ref, rit_ref, xs_hbm,
                     xloc, zblk, sem, zsem, bsem, *, nsteps, nb):
    i = pl.program_id(0)
    slot = lax.rem(i, 2)
    d = h2_ref.shape[-1]

    def zero_copy(e, c):
        return pltpu.make_async_copy(zblk.at[pl.ds(0, MOE_CHUNK)], xs_hbm.at[_chunk_rows(tstart_ref[e] + c)], zsem)

    def zero_block(j):
        rows = pl.ds(pl.multiple_of(j * MOE_BLOCK, MOE_BLOCK), MOE_BLOCK)
        return pltpu.make_async_copy(zblk, xs_hbm.at[rows], bsem)

    def chunk_copy(t, u, c, sl):
        return pltpu.make_async_copy(xloc.at[sl].at[u].at[_chunk_rows(c)],
                                     xs_hbm.at[_chunk_rows(dmap_ref[t * MOE_TILE_CHUNKS + c])], sem.at[sl])

    def wait_step(step, sl):
        for u in range(MOE_TPS):
            _wait_chunks(nchk_ref[step * MOE_TPS + u],
                         lambda rows: pltpu.make_async_copy(xloc.at[sl].at[0].at[rows], xs_hbm.at[rows], sem.at[sl]))

    @pl.when(i == 0)
    def _():
        zblk[...] = jnp.zeros_like(zblk)
        lax.fori_loop(nbr_ref[0], nb, lambda j, z: (zero_block(j).start(), z)[1], 0)
        for e in range(N_EXPERTS):
            lax.fori_loop(0, tn_ref[e], lambda c, z, e=e: (zero_copy(e, c).start(), z)[1], 0)
        for e in range(N_EXPERTS):
            lax.fori_loop(0, tn_ref[e], lambda c, z, e=e: (zero_copy(e, c).wait(), z)[1], 0)

    @pl.when(i >= 2)
    def _():
        wait_step(i - 2, slot)

    s_io = lax.broadcasted_iota(jnp.int32, (MOE_SLAB, MOE_TILE), 0).astype(F32)
    lane = lax.broadcasted_iota(jnp.int32, (MOE_TILE, LANES), 1)
    tiles = [slice(u * MOE_TILE, (u + 1) * MOE_TILE) for u in range(MOE_TPS)]
    pms = [[(s_io == rit_ref[4 + k:5 + k, ts]).astype(BF16) for k in range(TOP_K)] for ts in tiles]
    mains = [_mm(pm[0] + pm[1], h2_ref[ts, :]) for pm, ts in zip(pms, tiles)]
    for u, (pm, ts) in enumerate(zip(pms, tiles)):
        xloc[slot, u, :, 0:d] = mains[u].astype(BF16)
        wx = jnp.zeros((MOE_SLAB, LANES), F32)
        for k in range(TOP_K):
            w = ri_ref[ts, TOP_K + k:TOP_K + k + 1]
            hi = w.astype(BF16).astype(F32)
            wx = wx + _mm(pm[k], jnp.where(lane == 0, hi, jnp.where(lane == 1, w - hi, 0.0)).astype(BF16))
        xloc[slot, u, :, d:d + LANES] = wx.astype(BF16)

    for u in range(MOE_TPS):
        t = i * MOE_TPS + u
        lax.fori_loop(0, nchk_ref[t], lambda c, z, t=t, u=u: (chunk_copy(t, u, c, slot).start(), z)[1], 0)

    @pl.when(i == nsteps - 1)
    def _():
        wait_step(i, slot)
        if nsteps >= 2:
            wait_step(i - 1, 1 - slot)
        lax.fori_loop(nbr_ref[0], nb, lambda j, z: (zero_block(j).wait(), z)[1], 0)


def _wait_chunks(n, copy_of_rows):
    bit = MOE_TILE_CHUNKS
    while bit >= 1:
        @pl.when((n & bit) != 0)
        def _(bit=bit):
            copy_of_rows(pl.ds(0, bit * MOE_CHUNK)).wait()
        bit //= 2


def _dispatch(h2, ri, rit, tables, n_rows):
    t, d = h2.shape
    nt = t // MOE_TILE
    assert nt % MOE_TPS == 0 and MOE_SLAB >= TOP_K * MOE_TILE + N_EXPERTS * (MOE_CHUNK - 1)
    rows = MOE_TPS * MOE_TILE
    grid_spec = pltpu.PrefetchScalarGridSpec(
        num_scalar_prefetch=5,
        grid=(nt // MOE_TPS,),
        in_specs=[pl.BlockSpec((rows, d), lambda i, *_: (i, 0)),
                  pl.BlockSpec((rows, LANES), lambda i, *_: (i, 0)),
                  pl.BlockSpec((8, rows), lambda i, *_: (0, i))],
        out_specs=pl.BlockSpec(memory_space=pl.ANY),
        scratch_shapes=[pltpu.VMEM((2, MOE_TPS, MOE_SLAB, d + LANES), BF16), pltpu.VMEM((MOE_BLOCK, d + LANES), BF16),
                        pltpu.SemaphoreType.DMA((2,)), pltpu.SemaphoreType.DMA(()), pltpu.SemaphoreType.DMA(())])
    return pl.pallas_call(
        functools.partial(_dispatch_kernel, nsteps=nt // MOE_TPS, nb=n_rows // MOE_BLOCK),
        grid_spec=grid_spec,
        out_shape=jax.ShapeDtypeStruct((n_rows, d + LANES), BF16),
        compiler_params=_params("arbitrary"),
        name="moe_dispatch",
    )(tables["dmap"], tables["tile_chunks"], tables["tail_start"], tables["tail_n"], tables["n_blocks"],
      h2, ri, rit)


def _expert_kernel(be_ref, nbr_ref, xs_ref, w1_ref, w3_ref, w2_ref, y_ref, w1b, w3b, w2b):
    i = pl.program_id(0)
    used = i < nbr_ref[0]

    @pl.when(jnp.logical_not(used))
    def _():
        y_ref[...] = jnp.zeros_like(y_ref)

    @pl.when(used & ((i == 0) | (be_ref[i] != be_ref[jnp.maximum(i - 1, 0)])))
    def _():
        w1b[...] = w1_ref[0].astype(BF16)
        w3b[...] = w3_ref[0].astype(BF16)
        w2b[...] = w2_ref[0].astype(BF16)

    @pl.when(used)
    def _():
        d = y_ref.shape[-1]
        subs = [slice(j * MOE_TILE, (j + 1) * MOE_TILE) for j in range(y_ref.shape[0] // MOE_TILE)]
        ab = [(_mm(xs_ref[rs, 0:d], w1b[...]), _mm(xs_ref[rs, 0:d], w3b[...])) for rs in subs]
        hms = [(_silu(a) * b).astype(BF16) for a, b in ab]
        for rs, hm in zip(subs, hms):
            gw = xs_ref[rs, d:d + LANES].astype(F32)
            y_ref[rs, :] = (_mm(hm, w2b[...]) * (gw[:, 0:1] + gw[:, 1:2])).astype(BF16)


def _experts(xs, tables, w1, w3, w2):
    n_rows, dw = xs.shape
    d = dw - LANES
    nb = n_rows // MOE_BLOCK
    blk = lambda i, be, nbr: jnp.minimum(i, nbr[0] - 1)
    grid_spec = pltpu.PrefetchScalarGridSpec(
        num_scalar_prefetch=2,
        grid=(nb,),
        in_specs=[pl.BlockSpec((MOE_BLOCK, dw), lambda i, be, nbr: (blk(i, be, nbr), 0)),
                  pl.BlockSpec((1, d, D_EXPERT), lambda i, be, nbr: (be[blk(i, be, nbr)], 0, 0)),
                  pl.BlockSpec((1, d, D_EXPERT), lambda i, be, nbr: (be[blk(i, be, nbr)], 0, 0)),
                  pl.BlockSpec((1, D_EXPERT, d), lambda i, be, nbr: (be[blk(i, be, nbr)], 0, 0))],
        out_specs=pl.BlockSpec((MOE_BLOCK, d), lambda i, be, nbr: (i, 0)),
        scratch_shapes=[pltpu.VMEM((d, D_EXPERT), BF16), pltpu.VMEM((d, D_EXPERT), BF16),
                        pltpu.VMEM((D_EXPERT, d), BF16)])
    return pl.pallas_call(
        _expert_kernel,
        grid_spec=grid_spec,
        out_shape=jax.ShapeDtypeStruct((n_rows, d), BF16),
        compiler_params=_params("arbitrary"),
        name="moe_experts",
    )(tables["blk_expert"], tables["n_blocks"], xs, w1, w3, w2)


def _moe_rows(n_tiles):
    worst = n_tiles * (TOP_K * MOE_TILE + N_EXPERTS * (MOE_CHUNK - 1)) + N_EXPERTS * (MOE_BLOCK - MOE_CHUNK)
    return -(-worst // MOE_BLOCK) * MOE_BLOCK


def _moe_tables(cnt, n_rows):
    n_tiles = cnt.shape[0]
    per_blk = MOE_BLOCK // MOE_CHUNK

    def excl_cumsum(a, axis):
        n = a.shape[axis]
        lower = jnp.arange(n)[:, None] > jnp.arange(n)[None, :]
        if axis == 0:
            return jnp.sum(jnp.where(lower[:, :, None], a[None, :, :], 0), axis=1)
        return jnp.sum(jnp.where(lower[None, :, :], a[:, None, :], 0), axis=2)

    nch = (cnt + MOE_CHUNK - 1) // MOE_CHUNK
    loff = excl_cumsum(nch, 1)
    seg = jnp.sum(nch, axis=0)
    blocks = (seg + per_blk - 1) // per_blk
    bstart = excl_cumsum(blocks[None, :], 1)[0]
    bend = bstart + blocks
    estart = bstart * per_blk
    gbase = estart[None, :] + excl_cumsum(nch, 0)
    c = jnp.arange(MOE_TILE_CHUNKS, dtype=jnp.int32)
    owner = jnp.sum((c[None, :, None] >= (loff + nch)[:, None, :]).astype(jnp.int32), axis=-1)
    owner = jnp.minimum(owner, N_EXPERTS - 1)
    is_owner = owner[:, :, None] == jnp.arange(N_EXPERTS, dtype=jnp.int32)[None, None, :]
    dmap = c[None, :] + jnp.sum(jnp.where(is_owner, (gbase - loff)[:, None, :], 0), axis=-1)
    dmap = jnp.clip(dmap, 0, n_rows // MOE_CHUNK - 1)
    nb = n_rows // MOE_BLOCK
    blk_expert = jnp.sum((jnp.arange(nb, dtype=jnp.int32)[:, None] >= bend[None, :]).astype(jnp.int32), axis=1)
    blk_expert = jnp.minimum(blk_expert, N_EXPERTS - 1)
    return dict(dmap=dmap.reshape(-1).astype(jnp.int32), tile_chunks=jnp.sum(nch, axis=1).astype(jnp.int32),
                tail_start=(estart + seg).astype(jnp.int32), tail_n=(blocks * per_blk - seg).astype(jnp.int32),
                blk_expert=blk_expert.astype(jnp.int32), n_blocks=bend[-1:].astype(jnp.int32))


def _final_kernel(dmap_ref, nchk_ref, x1_ref, ri_ref, mod_ref, g_ref, y_hbm, o_ref, yloc, sem, *, nsteps):
    i = pl.program_id(0)
    slot = lax.rem(i, 2)

    def chunk_copy(t, u, c, sl):
        return pltpu.make_async_copy(y_hbm.at[_chunk_rows(dmap_ref[t * MOE_TILE_CHUNKS + c])],
                                     yloc.at[sl].at[u].at[_chunk_rows(c)], sem.at[sl])

    def gather(step, sl):
        for u in range(MOE_TPS):
            t = step * MOE_TPS + u
            lax.fori_loop(0, nchk_ref[t], lambda c, z, t=t, u=u: (chunk_copy(t, u, c, sl).start(), z)[1], 0)

    @pl.when(i == 0)
    def _():
        yloc[...] = jnp.zeros_like(yloc)
        gather(0, 0)

    @pl.when(i + 1 < nsteps)
    def _():
        gather(i + 1, 1 - slot)

    for u in range(MOE_TPS):
        _wait_chunks(nchk_ref[i * MOE_TPS + u],
                     lambda rows: pltpu.make_async_copy(y_hbm.at[rows], yloc.at[slot].at[0].at[rows], sem.at[slot]))

    l_io = lax.broadcasted_iota(jnp.int32, (MOE_TILE, MOE_SLAB), 1).astype(F32)
    tiles = [slice(u * MOE_TILE, (u + 1) * MOE_TILE) for u in range(MOE_TPS)]
    picks = [((l_io == ri_ref[ts, 4:5]) | (l_io == ri_ref[ts, 5:6])).astype(BF16) for ts in tiles]
    moes = [_mm(pick, yloc[slot, u]) for u, pick in enumerate(picks)]
    for ts, moe in zip(tiles, moes):
        x2 = x1_ref[ts, :] + mod_ref[0, 5:6, :] * moe
        o_ref[ts, :] = _rms(x2, g_ref[...])


def _final(x1, ri, y, mod3, final_g, tables, s):
    t, d = x1.shape
    rows = MOE_TPS * MOE_TILE
    nsteps = t // rows
    per_b = s // rows
    grid_spec = pltpu.PrefetchScalarGridSpec(
        num_scalar_prefetch=2,
        grid=(nsteps,),
        in_specs=[pl.BlockSpec((rows, d), lambda i, *_: (i, 0)),
                  pl.BlockSpec((rows, LANES), lambda i, *_: (i, 0)),
                  pl.BlockSpec((1, 6, d), lambda i, *_: (i // per_b, 0, 0)),
                  pl.BlockSpec((1, d), lambda i, *_: (0, 0)),
                  pl.BlockSpec(memory_space=pl.ANY)],
        out_specs=pl.BlockSpec((rows, d), lambda i, *_: (i, 0)),
        scratch_shapes=[pltpu.VMEM((2, MOE_TPS, MOE_SLAB, d), BF16), pltpu.SemaphoreType.DMA((2,))])
    return pl.pallas_call(
        functools.partial(_final_kernel, nsteps=nsteps),
        grid_spec=grid_spec,
        out_shape=jax.ShapeDtypeStruct((t, d), F32),
        compiler_params=_params("arbitrary"),
        name="moe_combine_final",
    )(tables["dmap"], tables["tile_chunks"], x1, ri, mod3, final_g.reshape(1, d), y)


def kernel(x, c, positions, w_ada, b_ada, norm1_g, w_in, q_norm_g, w_uq, kv_norm_g, w_ukv, w_o, norm2_g,
           w_gr, b_gr, w_er, b_er, w1, w3, w2, final_g):
    bsz, s, d = x.shape
    assert w_ada.shape[0] == 1, "one layer"
    tm = min(1024, s)
    tq = min(256, s)
    ts = min(1024, s)
    mod3 = _adaln(c, w_ada[0], b_ada[0]).reshape(bsz, 6, d)
    pos3 = positions.astype(F32).reshape(bsz, s, 1)
    q, k, vt, rq, rk, rv, rg = _pre(x, mod3, pos3, norm1_g[0], w_in[0], q_norm_g[0], w_uq[0], kv_norm_g[0], w_ukv[0],
                                    tm, tq)
    o_mla = _attention(q, k, vt, tq, ATTN_HEADS_PER_STEP)
    o_ret = _retention(rq, rk, rv, rg, ts)
    x1, h2, ri, rit, cnt = _post(x, o_mla, o_ret, mod3, w_o[0], norm2_g[0], w_gr[0], b_gr[0], w_er[0], b_er[0], tm)
    t = bsz * s
    n_rows = _moe_rows(t // MOE_TILE)
    counts = cnt[:, 0, N_GROUPS:N_GROUPS + N_EXPERTS].astype(jnp.int32)
    tables = _moe_tables(counts, n_rows)
    ri2 = ri.reshape(t, LANES)
    xs = _dispatch(h2.reshape(t, d), ri2, rit, tables, n_rows)
    y = _experts(xs, tables, w1[0], w3[0], w2[0])
    out = _final(x1.reshape(t, d), ri2, y, mod3, final_g, tables, s)
    return out.reshape(bsz, s, d)
```

```python
import functools

import jax
import jax.numpy as jnp
from jax import lax
from jax.experimental import pallas as pl
from jax.experimental.pallas import tpu as pltpu

MLA_HEADS = 8
MLA_NOPE = 64
MLA_ROPE = 32
MLA_V = 64
Q_LORA = 256
KV_LORA = 128
RET_HEADS = 4
RET_DK = 64
RET_DV = 128
RET_CHUNK = 128
ROPE_BASE = 10000.0
NORM_EPS = 1e-6
N_GROUPS = 4
EXPERTS_PER_GROUP = 8
N_EXPERTS = N_GROUPS * EXPERTS_PER_GROUP
TOP_K = 2
D_EXPERT = 256
MOE_TILE = 256
MOE_CHUNK = 16
MOE_SLAB = 1024
MOE_TILE_CHUNKS = MOE_SLAB // MOE_CHUNK
MOE_TPS = 2
MOE_BLOCK = 1024

LANES = 128
VMEM_LIMIT = 56 * 1024 * 1024

F32 = jnp.float32
BF16 = jnp.bfloat16
NEG = float(jnp.finfo(jnp.float32).min)
LOG2_E = 1.4426950408889634
ATTN_HEADS_PER_STEP = 8
ATTN_PAIRS_PER_ITER = 2
MLA_VROWS = MLA_V + 16

_C_Q = 0
_C_KV = _C_Q + Q_LORA
_C_KPE = _C_KV + KV_LORA
_C_RQ = _C_KPE + LANES
_C_RK = _C_RQ + RET_HEADS * RET_DK
_C_RV = _C_RK + RET_HEADS * RET_DK
_C_RG = _C_RV + RET_HEADS * RET_DV
_IN_PERM = _C_RG + RET_HEADS * RET_DV


def _silu(v):
    return v / (1.0 + jnp.exp(-v))


def _mm(a, b):
    return jnp.dot(a, b, preferred_element_type=F32)


def _mm_nt(a, b):
    return lax.dot_general(a, b, (((1,), (1,)), ((), ())), preferred_element_type=F32)


def _params(*sem):
    return pltpu.CompilerParams(dimension_semantics=sem, vmem_limit_bytes=VMEM_LIMIT)


def _adaln_kernel(c_ref, w_ref, b_ref, o_ref):
    a = _silu(c_ref[...]).astype(BF16)
    o_ref[...] = _mm(a, w_ref[...].astype(BF16)) + b_ref[...]


def _adaln(c, w_ada, b_ada):
    bsz, d = c.shape
    n = w_ada.shape[1]
    tn = d
    return pl.pallas_call(
        _adaln_kernel,
        grid=(n // tn,),
        in_specs=[pl.BlockSpec((bsz, d), lambda j: (0, 0)),
                  pl.BlockSpec((d, tn), lambda j: (0, j)),
                  pl.BlockSpec((1, tn), lambda j: (0, j))],
        out_specs=pl.BlockSpec((bsz, tn), lambda j: (0, j)),
        out_shape=jax.ShapeDtypeStruct((bsz, n), F32),
        compiler_params=_params("arbitrary"),
        name="adaln",
    )(c, w_ada, b_ada.reshape(1, n))


def _rms(v, g):
    return v * lax.rsqrt(jnp.mean(v * v, axis=-1, keepdims=True) + NORM_EPS) * g


def _pre_kernel(x_ref, mod_ref, pos_ref, g1_ref, win_ref, qg_ref, wuq_ref, kvg_ref, wuk_ref,
                wuv_ref, invf_ref, q_ref, k_ref, v_ref, rq_ref, rk_ref, rv_ref, rg_ref):
    tq = v_ref.shape[-1]
    subs = [slice(i * tq, (i + 1) * tq) for i in range(v_ref.shape[1])]
    sh1 = mod_ref[0, 0:1, :]
    sc1 = mod_ref[0, 1:2, :]
    projs = [_mm((_rms(x_ref[0, rs, :], g1_ref[...]) * (1.0 + sc1) + sh1).astype(BF16), win_ref[...]) for rs in subs]

    cqs = [_rms(p[:, _C_Q:_C_Q + Q_LORA], qg_ref[...]).astype(BF16) for p in projs]
    ckvs = [_rms(p[:, _C_KV:_C_KV + KV_LORA], kvg_ref[...]).astype(BF16) for p in projs]
    qas = [_mm(cq, wuq_ref[...]) for cq in cqs]
    kns = [_mm(ckv, wuk_ref[...]) for ckv in ckvs]
    vvs = [_mm(ckv, wuv_ref[...]) for ckv in ckvs]

    lane = lax.broadcasted_iota(jnp.int32, (tq, LANES), 1)
    hi = lane >= RET_DK
    half_m, half_r = MLA_ROPE // 2, RET_DK // 2
    first_m = hi & (lane < RET_DK + half_m)
    first_r = (lane & half_r) == 0
    scale = (MLA_NOPE + MLA_ROPE) ** -0.5 * LOG2_E

    def rope(v, cos, sin, first, half):
        partner = jnp.where(first, pltpu.roll(v, LANES - half, 1), pltpu.roll(v, half, 1))
        return v * cos + partner * sin

    for i, rs in enumerate(subs):
        proj = projs[i]
        ang = pos_ref[0, rs, :] * invf_ref[...]
        cs = jnp.cos(ang)
        sn = jnp.sin(ang)
        c_mla = jnp.where(hi, cs, 1.0)
        s_mla = jnp.where(hi, jnp.where(first_m, -sn, sn), 0.0)
        c_ret = jnp.where(hi, pltpu.roll(cs, RET_DK, 1), cs)
        s_ret = jnp.where(hi, pltpu.roll(sn, RET_DK, 1), sn)
        s_ret = jnp.where(first_r, -s_ret, s_ret)

        kpe = rope(proj[:, _C_KPE:_C_KPE + LANES], c_mla, s_mla, first_m, half_m)
        cq_s = c_mla * scale
        sq_s = jnp.where(hi, sn, 0.0) * scale
        for hd in range(MLA_HEADS):
            sl = slice(hd * LANES, (hd + 1) * LANES)
            sr = slice((MLA_HEADS + hd) * LANES, (MLA_HEADS + hd + 1) * LANES)
            q_ref[0, rs, sl] = (qas[i][:, sl] * cq_s + qas[i][:, sr] * sq_s).astype(BF16)
            k_ref[0, rs, sl] = (kns[i][:, sl] + kpe).astype(BF16)
        vt = vvs[i].T
        tail = jnp.where(lax.broadcasted_iota(jnp.int32, (MLA_VROWS - MLA_V, tq), 0) == 0, 1.0, 0.0)
        slab = [piece for hd in range(MLA_HEADS) for piece in (vt[hd * MLA_V:(hd + 1) * MLA_V, :], tail)]
        v_ref[0, i] = jnp.concatenate(slab, axis=0).astype(BF16)

        for j in range(RET_HEADS * RET_DK // LANES):
            o = j * LANES
            rq = rope(proj[:, _C_RQ + o:_C_RQ + o + LANES], c_ret, s_ret, first_r, half_r)
            rk = rope(proj[:, _C_RK + o:_C_RK + o + LANES], c_ret, s_ret, first_r, half_r)
            rq_ref[0, rs, o:o + LANES] = rq.astype(BF16)
            rk_ref[0, rs, o:o + LANES] = (rk * (RET_DK ** -0.5)).astype(BF16)
        rv_ref[0, rs, :] = proj[:, _C_RV:_C_RV + RET_HEADS * RET_DV].astype(BF16)
        rg_ref[0, rs, :] = proj[:, _C_RG:_C_RG + RET_HEADS * RET_DV]


def _pad_heads(w, width, left):
    k = w.shape[0]
    w3 = w.reshape(k, -1, width)
    w3 = jnp.pad(w3, ((0, 0), (0, 0), (left, LANES - left - width)))
    return w3.reshape(k, -1)


def _pre(x, mod3, pos3, norm1_g, w_in, q_norm_g, w_uq, kv_norm_g, w_ukv, tm, tq):
    bsz, s, d = x.shape
    o = 0
    parts = {}
    for name, width in (("cq", Q_LORA), ("ckv", KV_LORA), ("kr", MLA_ROPE), ("rq", RET_HEADS * RET_DK),
                        ("rk", RET_HEADS * RET_DK), ("rv", RET_HEADS * RET_DV), ("rg", RET_HEADS * RET_DV)):
        parts[name] = w_in[:, o:o + width]
        o += width
    w_in_p = jnp.concatenate([
        parts["cq"], parts["ckv"], _pad_heads(parts["kr"], MLA_ROPE, MLA_NOPE),
        parts["rq"], parts["rk"], parts["rv"], parts["rg"]], axis=1).astype(BF16)
    assert w_in_p.shape[1] == _IN_PERM
    wq_rope = w_uq.reshape(Q_LORA, MLA_HEADS, MLA_NOPE + MLA_ROPE)[:, :, MLA_NOPE:].reshape(Q_LORA, MLA_HEADS, 2, -1)
    wq_rot = jnp.stack([-wq_rope[:, :, 1], wq_rope[:, :, 0]], axis=2).reshape(Q_LORA, -1)
    w_uq_p = jnp.concatenate([_pad_heads(w_uq, MLA_NOPE + MLA_ROPE, 0),
                              _pad_heads(wq_rot, MLA_ROPE, MLA_NOPE)], axis=1).astype(BF16)
    wkv3 = w_ukv.reshape(KV_LORA, MLA_HEADS, MLA_NOPE + MLA_V)
    w_uk_p = _pad_heads(wkv3[:, :, :MLA_NOPE].reshape(KV_LORA, -1), MLA_NOPE, 0).astype(BF16)
    w_uv = wkv3[:, :, MLA_NOPE:].reshape(KV_LORA, -1).astype(BF16)
    half_r, half_m = RET_DK // 2, MLA_ROPE // 2
    f_r = ROPE_BASE ** (-(jnp.arange(half_r, dtype=F32) / half_r))
    f_m = ROPE_BASE ** (-(jnp.arange(half_m, dtype=F32) / half_m))
    invf = jnp.concatenate([f_r, f_r, f_m, f_m, jnp.zeros((LANES - 2 * half_r - 2 * half_m,), F32)]).reshape(1, LANES)

    hq = MLA_HEADS * LANES
    const = lambda shape: pl.BlockSpec(shape, lambda b, i: (0,) * len(shape))
    tile = lambda w: pl.BlockSpec((1, tm, w), lambda b, i: (b, i, 0))
    return pl.pallas_call(
        _pre_kernel,
        grid=(bsz, s // tm),
        in_specs=[tile(d), pl.BlockSpec((1, 6, d), lambda b, i: (b, 0, 0)), tile(1), const((1, d)),
                  const((d, _IN_PERM)), const((1, Q_LORA)), const((Q_LORA, 2 * hq)),
                  const((1, KV_LORA)), const((KV_LORA, hq)), const((KV_LORA, MLA_HEADS * MLA_V)), const((1, LANES))],
        out_specs=[tile(hq), tile(hq),
                   pl.BlockSpec((1, tm // tq, MLA_HEADS * MLA_VROWS, tq), lambda b, i: (b, i, 0, 0)),
                   tile(RET_HEADS * RET_DK), tile(RET_HEADS * RET_DK),
                   tile(RET_HEADS * RET_DV), tile(RET_HEADS * RET_DV)],
        out_shape=[jax.ShapeDtypeStruct((bsz, s, hq), BF16), jax.ShapeDtypeStruct((bsz, s, hq), BF16),
                   jax.ShapeDtypeStruct((bsz, s // tq, MLA_HEADS * MLA_VROWS, tq), BF16),
                   jax.ShapeDtypeStruct((bsz, s, RET_HEADS * RET_DK), BF16),
                   jax.ShapeDtypeStruct((bsz, s, RET_HEADS * RET_DK), BF16),
                   jax.ShapeDtypeStruct((bsz, s, RET_HEADS * RET_DV), BF16),
                   jax.ShapeDtypeStruct((bsz, s, RET_HEADS * RET_DV), F32)],
        compiler_params=_params("parallel", "arbitrary"),
        name="pre_mixer",
    )(x, mod3, pos3, norm1_g.reshape(1, d), w_in_p, q_norm_g.reshape(1, -1), w_uq_p,
      kv_norm_g.reshape(1, -1), w_uk_p, w_uv, invf)


def _attn_kernel(q_ref, k_ref, vt_ref, o_ref, st_x, st_y, bm_x, bm_y, *, tq, kb, hps):
    qi = pl.program_id(1)
    assert tq == 2 * kb, "a query tile spans two key blocks: the last two blocks of a tile are masked"
    key = lax.broadcasted_iota(jnp.int32, (kb, tq), 0)
    qry = lax.broadcasted_iota(jnp.int32, (kb, tq), 1)
    hsl = [slice(hh * LANES, (hh + 1) * LANES) for hh in range(hps)]
    vsl = [slice(hh * MLA_VROWS, (hh + 1) * MLA_VROWS) for hh in range(hps)]
    qs = [q_ref[0, :, hs] for hs in hsl]
    bufs = {"x": (st_x, bm_x), "y": (st_y, bm_y)}
    n_blk = 2 * qi + 2

    def scores(blk, buf, diag=None):
        st_ref, bm_ref = bufs[buf]
        start = pl.multiple_of(blk * kb, kb)
        for hh in range(hps):
            st = _mm_nt(k_ref[0, pl.ds(start, kb), hsl[hh]], qs[hh])
            if diag is not None:
                st = jnp.where(key + diag * kb <= qry, st, NEG)
            st_ref[hh] = st
            bm_ref[hh, 0:1, :] = jnp.max(st, axis=0, keepdims=True)

    def update(blk, buf, carry):
        st_ref, bm_ref = bufs[buf]
        out = []
        for hh in range(hps):
            m, acc = carry[hh]
            m_new = jnp.maximum(m, bm_ref[hh, 0:1, :])
            p = jnp.exp2(st_ref[hh] - m_new).astype(BF16)
            out.append((m_new, jnp.exp2(m - m_new) * acc + _mm(vt_ref[0, blk, vsl[hh], :], p)))
        return tuple(out)

    def when_loop(pred, body, carry):
        return lax.fori_loop(0, pred.astype(jnp.int32), lambda _, c: body(c), carry)

    init = (jnp.full((1, tq), NEG, F32), jnp.zeros((MLA_VROWS, tq), F32))
    carry = (init,) * hps

    @pl.when(qi >= 1)
    def _():
        scores(0, "x")

    def steady(base, pairs, c):
        for r in range(pairs):
            scores(base + 2 * r + 1, "y")
            c = update(base + 2 * r, "x", c)
            scores(base + 2 * r + 2, "x")
            c = update(base + 2 * r + 1, "y", c)
        return c

    n_pairs = jnp.maximum(qi - 1, 0)
    n_long = n_pairs // ATTN_PAIRS_PER_ITER
    carry = lax.fori_loop(0, n_long, lambda i, c: steady(2 * ATTN_PAIRS_PER_ITER * i, ATTN_PAIRS_PER_ITER, c), carry)
    done = n_long * ATTN_PAIRS_PER_ITER
    carry = lax.fori_loop(done, n_pairs, lambda i, c: steady(2 * i, 1, c), carry)

    def tail(c):
        scores(n_blk - 3, "y")
        c = update(n_blk - 4, "x", c)
        scores(n_blk - 2, "x", diag=0)
        c = update(n_blk - 3, "y", c)
        scores(n_blk - 1, "y", diag=1)
        c = update(n_blk - 2, "x", c)
        return update(n_blk - 1, "y", c)

    def tail_first(c):
        scores(0, "x", diag=0)
        scores(1, "y", diag=1)
        c = update(0, "x", c)
        return update(1, "y", c)

    carry = when_loop(qi >= 1, tail, carry)
    carry = when_loop(qi == 0, tail_first, carry)
    out_t = jnp.concatenate([acc[0:MLA_V] * (1.0 / acc[MLA_V:MLA_V + 1]) for _, acc in carry], axis=0)
    o_ref[0] = out_t.T.astype(BF16)


def _attention(q, k, vt, kb, hps):
    bsz, s, _ = q.shape
    groups = MLA_HEADS // hps
    tq = 2 * kb
    assert vt.shape == (bsz, s // kb, MLA_HEADS * MLA_VROWS, kb)
    return pl.pallas_call(
        functools.partial(_attn_kernel, tq=tq, kb=kb, hps=hps),
        grid=(bsz * groups, s // tq),
        in_specs=[pl.BlockSpec((1, tq, hps * LANES), lambda g, i: (g // groups, i, g % groups)),
                  pl.BlockSpec((1, s, hps * LANES), lambda g, i: (g // groups, 0, g % groups)),
                  pl.BlockSpec((1, s // kb, hps * MLA_VROWS, kb), lambda g, i: (g // groups, 0, g % groups, 0))],
        out_specs=pl.BlockSpec((1, tq, hps * MLA_V), lambda g, i: (g // groups, i, g % groups)),
        out_shape=jax.ShapeDtypeStruct((bsz, s, MLA_HEADS * MLA_V), BF16),
        scratch_shapes=[pltpu.VMEM((hps, kb, tq), F32), pltpu.VMEM((hps, kb, tq), F32),
                        pltpu.VMEM((hps, 8, tq), F32), pltpu.VMEM((hps, 8, tq), F32)],
        compiler_params=_params("parallel", "arbitrary"),
        name="mla_attention",
    )(q, k, vt)


def _retention_tile(rq_ref, rk_ref, rv_ref, rg_ref, dm_ref, xi_ref, zt_ref, dc_ref, o_ref, st_ref, *, ts):
    @pl.when(pl.program_id(1) == 0)
    def _():
        st_ref[...] = jnp.zeros_like(st_ref)

    lane = lax.broadcasted_iota(jnp.int32, (RET_CHUNK, LANES), 1)
    row = lax.broadcasted_iota(jnp.int32, (LANES, RET_DV), 0)
    hpt = LANES // RET_DK
    n_chunks = ts // RET_CHUNK
    tiles = [(c, p) for c in range(n_chunks) for p in range(RET_HEADS // hpt)]
    units = [(c, hd) for c in range(n_chunks) for hd in range(RET_HEADS)]
    rows = lambda c: slice(c * RET_CHUNK, (c + 1) * RET_CHUNK)
    vsl = lambda hd: slice(hd * RET_DV, (hd + 1) * RET_DV)
    mine = [(lane >= sub * RET_DK) & (lane < (sub + 1) * RET_DK) for sub in range(hpt)]
    mine_row = [(row >= sub * RET_DK) & (row < (sub + 1) * RET_DK) for sub in range(hpt)]

    q2 = {(c, p): rq_ref[0, rows(c), p * LANES:(p + 1) * LANES] for c, p in tiles}
    k2 = {(c, p): rk_ref[0, rows(c), p * LANES:(p + 1) * LANES] for c, p in tiles}
    vh = {(c, hd): rv_ref[0, rows(c), vsl(hd)] for c, hd in units}
    sc, un = {}, {}
    for c, p in tiles:
        heads = [p * hpt + sub for sub in range(hpt)]
        kcat = jnp.concatenate([jnp.where(mine[sub], k2[(c, p)], 0.0).astype(BF16) for sub in range(hpt)], axis=0)
        s_all = _mm_nt(q2[(c, p)], kcat)
        kz = jnp.zeros((RET_CHUNK, LANES), F32)
        for sub, hd in enumerate(heads):
            sc[(c, hd)] = s_all[:, sub * RET_CHUNK:(sub + 1) * RET_CHUNK]
            kz = kz + jnp.where(mine[sub], k2[(c, p)].astype(F32) * zt_ref[hd], 0.0)
        u_all = _mm(kz.astype(BF16).T, jnp.concatenate([vh[(c, hd)] for hd in heads], axis=1))
        for sub, hd in enumerate(heads):
            un[(c, hd)] = jnp.where(mine_row[sub], u_all[:, sub * RET_DV:(sub + 1) * RET_DV], 0.0)
    prev = {}
    for hd in range(RET_HEADS):
        st = st_ref[hd]
        for c in range(n_chunks):
            prev[(c, hd)] = st.astype(BF16)
            st = st * dc_ref[hd] + un[(c, hd)]
        st_ref[hd] = st
    for c, hd in units:
        u = (c, hd)
        lhs = jnp.concatenate([(sc[u] * dm_ref[hd]).astype(BF16),
                               (q2[(c, hd // hpt)].astype(F32) * xi_ref[hd]).astype(BF16)], axis=1)
        o = _mm(lhs, jnp.concatenate([vh[u], prev[u]], axis=0))
        mu = jnp.mean(o, axis=-1, keepdims=True)
        oc = o - mu
        on = oc * lax.rsqrt(jnp.mean(oc * oc, axis=-1, keepdims=True) + NORM_EPS)
        o_ref[rows(c), vsl(hd)] = (_silu(rg_ref[0, rows(c), vsl(hd)]) * on).astype(BF16)


def _retention_tables():
    c = RET_CHUNK
    gamma = 1.0 - jnp.power(2.0, -5.0 - jnp.arange(RET_HEADS, dtype=F32))
    log_g = jnp.log(gamma)
    idx = jnp.arange(c, dtype=F32)
    diff = idx[:, None] - idx[None, :]
    dmask = jnp.where(diff[None] >= 0, jnp.exp(jnp.maximum(diff, 0.0)[None] * log_g[:, None, None]), 0.0)
    zeta = jnp.exp((c - 1.0 - idx)[None, :] * log_g[:, None])
    xi = jnp.exp((idx + 1.0)[None, :] * log_g[:, None])
    decay = jnp.exp(c * log_g)
    xi_b = jnp.broadcast_to(xi[:, :, None], (RET_HEADS, c, LANES))
    zt_b = jnp.broadcast_to(zeta[:, :, None], (RET_HEADS, c, LANES))
    dc_b = jnp.broadcast_to(decay[:, None, None], (RET_HEADS, LANES, RET_DV))
    return dmask, xi_b, zt_b, dc_b


def _post_kernel(x_ref, om_ref, rq_ref, rk_ref, rv_ref, rg_ref, dm_ref, xi_ref, zt_ref, dc_ref,
                 mod_ref, wo_ref, g2_ref, wr_ref, br_ref,
                 x1_ref, h2_ref, ri_ref, rit_ref, cnt_ref, or_scr, st_ref):
    _retention_tile(rq_ref, rk_ref, rv_ref, rg_ref, dm_ref, xi_ref, zt_ref, dc_ref, or_scr, st_ref,
                    ts=x_ref.shape[1])
    half = om_ref.shape[-1]
    subs = [slice(i * MOE_TILE, (i + 1) * MOE_TILE) for i in range(x_ref.shape[1] // MOE_TILE)]
    mixes = [_mm(om_ref[0, rs, :], wo_ref[0:half, :]) + _mm(or_scr[rs, :], wo_ref[half:, :]) for rs in subs]

    lgs = []
    for rs, mix in zip(subs, mixes):
        x1 = x_ref[0, rs, :] + mod_ref[0, 2:3, :] * mix
        x1_ref[0, rs, :] = x1
        h2 = _rms(x1, g2_ref[...]) * (1.0 + mod_ref[0, 4:5, :]) + mod_ref[0, 3:4, :]
        hi = h2.astype(BF16)
        h2_ref[0, rs, :] = hi
        lo = (h2 - hi.astype(F32)).astype(BF16)
        both = _mm(hi, wr_ref[...])
        lgs.append(both[:, 0:LANES] + both[:, LANES:2 * LANES] + _mm(lo, wr_ref[:, 0:LANES]) + br_ref[...])

    lane = lax.broadcasted_iota(jnp.int32, (MOE_TILE, LANES), 1)
    big = jnp.int32(1 << 20)
    gmask = lane < N_GROUPS
    el = lane - N_GROUPS
    assert EXPERTS_PER_GROUP == 8
    routed = []
    for lg in lgs:
        gmax = jnp.max(jnp.where(gmask, lg, NEG), axis=-1, keepdims=True)
        ge = jnp.where(gmask, jnp.exp(lg - gmax), 0.0)
        pg = ge / jnp.sum(ge, axis=-1, keepdims=True)
        p_top = jnp.max(pg, axis=-1, keepdims=True)
        g_top = jnp.min(jnp.where(gmask & (pg == p_top), lane, big), axis=-1, keepdims=True)

        emask = (el >= 0) & (el < N_EXPERTS) & (lax.shift_right_arithmetic(el, 3) == g_top)
        ev = jnp.where(emask, lg, NEG)
        v1 = jnp.max(ev, axis=-1, keepdims=True)
        i1 = jnp.min(jnp.where(emask & (ev == v1), lane, big), axis=-1, keepdims=True)
        emask2 = emask & (lane != i1)
        ev2 = jnp.where(emask2, lg, NEG)
        v2 = jnp.max(ev2, axis=-1, keepdims=True)
        i2 = jnp.min(jnp.where(emask2 & (ev2 == v2), lane, big), axis=-1, keepdims=True)
        e = jnp.exp(v2 - v1)
        den = 1.0 + e
        routed.append((i1, i2, (1.0 / den) * p_top, (e / den) * p_top))

    r_io = lax.broadcasted_iota(jnp.int32, (MOE_TILE, MOE_TILE), 0)
    c_io = lax.broadcasted_iota(jnp.int32, (MOE_TILE, MOE_TILE), 1)
    earlier_tok = (c_io < r_io).astype(BF16)
    lr_io = lax.broadcasted_iota(jnp.int32, (LANES, LANES), 0)
    lc_io = lax.broadcasted_iota(jnp.int32, (LANES, LANES), 1)
    earlier_lane = (lr_io < lc_io).astype(BF16)
    for hf, (rs, (i1, i2, w1, w2)) in enumerate(zip(subs, routed)):
        oh = [lane == i1, lane == i2]
        cnt = (oh[0] | oh[1]).astype(BF16)
        excl = _mm(earlier_tok, cnt)
        n = jnp.sum(cnt.astype(F32), axis=0, keepdims=True)
        npad = jnp.floor((n + (MOE_CHUNK - 1)) * (1.0 / MOE_CHUNK)) * MOE_CHUNK
        loff = _mm(jnp.broadcast_to(npad, (8, LANES)).astype(BF16), earlier_lane)
        pos = excl + loff[0:1, :]
        cnt_ref[hf] = jnp.broadcast_to(n, (8, LANES))
        lp0, lp1 = [jnp.sum(jnp.where(o, pos, 0.0), axis=-1, keepdims=True) for o in oh]
        cols = [(i1 - N_GROUPS).astype(F32), (i2 - N_GROUPS).astype(F32), w1, w2, lp0, lp1]
        ri = jnp.zeros((MOE_TILE, LANES), F32)
        for j, col in enumerate(cols):
            ri = jnp.where(lane == j, col, ri)
        ri_ref[0, rs, :] = ri
        rit_ref[:, rs] = ri.T


def _post(x, o_mla, rq, rk, rv, rg, mod3, w_o, norm2_g, w_gr, b_gr, w_er, b_er, tm):
    bsz, s, d = x.shape
    ret_tables = _retention_tables()
    c = RET_CHUNK
    w_r = jnp.concatenate([w_gr, w_er.reshape(d, N_EXPERTS), jnp.zeros((d, LANES - N_GROUPS - N_EXPERTS), F32)], axis=1)
    w_rh = w_r.astype(BF16)
    w_rl = (w_r - w_rh.astype(F32)).astype(BF16)
    w_r2 = jnp.concatenate([w_rh, w_rl], axis=1)
    b_r =jnp.concatenate([b_gr, b_er.reshape(-1), jnp.zeros((LANES - N_GROUPS - N_EXPERTS,), F32)]).reshape(1, LANES)
    tile = lambda w: pl.BlockSpec((1, tm, w), lambda b, i: (b, i, 0))
    const = lambda shape: pl.BlockSpec(shape, lambda b, i: (0,) * len(shape))
    per_b = s // tm
    sub = tm // MOE_TILE
    return pl.pallas_call(
        _post_kernel,
        grid=(bsz, per_b),
        in_specs=[tile(d), tile(o_mla.shape[-1]),
                  tile(RET_HEADS * RET_DK), tile(RET_HEADS * RET_DK), tile(RET_HEADS * RET_DV), tile(RET_HEADS * RET_DV),
                  const((RET_HEADS, c, c)), const((RET_HEADS, c, LANES)), const((RET_HEADS, c, LANES)),
                  const((RET_HEADS, LANES, RET_DV)),
                  pl.BlockSpec((1, 6, d), lambda b, i: (b, 0, 0)),
                  const((d, d)), const((1, d)), const((d, 2 * LANES)), const((1, LANES))],
        out_specs=[tile(d), tile(d), tile(LANES),
                   pl.BlockSpec((LANES, tm), lambda b, i: (0, b * per_b + i)),
                   pl.BlockSpec((sub, 8, LANES), lambda b, i: (b * per_b + i, 0, 0))],
        out_shape=[jax.ShapeDtypeStruct((bsz, s, d), F32), jax.ShapeDtypeStruct((bsz, s, d), BF16),
                   jax.ShapeDtypeStruct((bsz, s, LANES), F32),
                   jax.ShapeDtypeStruct((LANES, bsz * s), F32),
                   jax.ShapeDtypeStruct((bsz * s // MOE_TILE, 8, LANES), F32)],
        scratch_shapes=[pltpu.VMEM((tm, RET_HEADS * RET_DV), BF16), pltpu.VMEM((RET_HEADS, LANES, RET_DV), F32)],
        compiler_params=_params("parallel", "arbitrary"),
        name="post_mixer",
    )(x, o_mla, rq, rk, rv, rg, *ret_tables, mod3, w_o.astype(BF16), norm2_g.reshape(1, d), w_r2, b_r)


def _chunk_rows(c):
    return pl.ds(pl.multiple_of(c * MOE_CHUNK, MOE_CHUNK), MOE_CHUNK)


def _dispatch_kernel(dmap_ref, nchk_ref, tstart_ref, tn_ref, nbr_ref, h2_ref, ri_ref, rit_ref, xs_hbm,
                     xloc, zblk, sem, zsem, bsem, *, nsteps, nb):
    i = pl.program_id(0)
    slot = lax.rem(i, 2)
    d = h2_ref.shape[-1]

    def zero_copy(e, c):
        return pltpu.make_async_copy(zblk.at[pl.ds(0, MOE_CHUNK)], xs_hbm.at[_chunk_rows(tstart_ref[e] + c)], zsem)

    def zero_block(j):
        rows = pl.ds(pl.multiple_of(j * MOE_BLOCK, MOE_BLOCK), MOE_BLOCK)
        return pltpu.make_async_copy(zblk, xs_hbm.at[rows], bsem)

    def chunk_copy(t, u, c, sl):
        return pltpu.make_async_copy(xloc.at[sl].at[u].at[_chunk_rows(c)],
                                     xs_hbm.at[_chunk_rows(dmap_ref[t * MOE_TILE_CHUNKS + c])], sem.at[sl])

    def wait_step(step, sl):
        for u in range(MOE_TPS):
            _wait_chunks(nchk_ref[step * MOE_TPS + u],
                         lambda rows: pltpu.make_async_copy(xloc.at[sl].at[0].at[rows], xs_hbm.at[rows], sem.at[sl]))

    @pl.when(i == 0)
    def _():
        zblk[...] = jnp.zeros_like(zblk)
        lax.fori_loop(nbr_ref[0], nb, lambda j, z: (zero_block(j).start(), z)[1], 0)
        for e in range(N_EXPERTS):
            lax.fori_loop(0, tn_ref[e], lambda c, z, e=e: (zero_copy(e, c).start(), z)[1], 0)
        for e in range(N_EXPERTS):
            lax.fori_loop(0, tn_ref[e], lambda c, z, e=e: (zero_copy(e, c).wait(), z)[1], 0)

    @pl.when(i >= 2)
    def _():
        wait_step(i - 2, slot)

    s_io = lax.broadcasted_iota(jnp.int32, (MOE_SLAB, MOE_TILE), 0).astype(F32)
    lane = lax.broadcasted_iota(jnp.int32, (MOE_TILE, LANES), 1)
    tiles = [slice(u * MOE_TILE, (u + 1) * MOE_TILE) for u in range(MOE_TPS)]
    pms = [[(s_io == rit_ref[4 + k:5 + k, ts]).astype(BF16) for k in range(TOP_K)] for ts in tiles]
    mains = [_mm(pm[0] + pm[1], h2_ref[ts, :]) for pm, ts in zip(pms, tiles)]
    for u, (pm, ts) in enumerate(zip(pms, tiles)):
        xloc[slot, u, :, 0:d] = mains[u].astype(BF16)
        wx = jnp.zeros((MOE_SLAB, LANES), F32)
        for k in range(TOP_K):
            w = ri_ref[ts, TOP_K + k:TOP_K + k + 1]
            hi = w.astype(BF16).astype(F32)
            wx = wx + _mm(pm[k], jnp.where(lane == 0, hi, jnp.where(lane == 1, w - hi, 0.0)).astype(BF16))
        xloc[slot, u, :, d:d + LANES] = wx.astype(BF16)

    for u in range(MOE_TPS):
        t = i * MOE_TPS + u
        lax.fori_loop(0, nchk_ref[t], lambda c, z, t=t, u=u: (chunk_copy(t, u, c, slot).start(), z)[1], 0)

    @pl.when(i == nsteps - 1)
    def _():
        wait_step(i, slot)
        if nsteps >= 2:
            wait_step(i - 1, 1 - slot)
        lax.fori_loop(nbr_ref[0], nb, lambda j, z: (zero_block(j).wait(), z)[1], 0)


def _wait_chunks(n, copy_of_rows):
    bit = MOE_TILE_CHUNKS
    while bit >= 1:
        @pl.when((n & bit) != 0)
        def _(bit=bit):
            copy_of_rows(pl.ds(0, bit * MOE_CHUNK)).wait()
        bit //= 2


def _dispatch(h2, ri, rit, tables, n_rows):
    t, d = h2.shape
    nt = t // MOE_TILE
    assert nt % MOE_TPS == 0 and MOE_SLAB >= TOP_K * MOE_TILE + N_EXPERTS * (MOE_CHUNK - 1)
    rows = MOE_TPS * MOE_TILE
    grid_spec = pltpu.PrefetchScalarGridSpec(
        num_scalar_prefetch=5,
        grid=(nt // MOE_TPS,),
        in_specs=[pl.BlockSpec((rows, d), lambda i, *_: (i, 0)),
                  pl.BlockSpec((rows, LANES), lambda i, *_: (i, 0)),
                  pl.BlockSpec((8, rows), lambda i, *_: (0, i))],
        out_specs=pl.BlockSpec(memory_space=pl.ANY),
        scratch_shapes=[pltpu.VMEM((2, MOE_TPS, MOE_SLAB, d + LANES), BF16), pltpu.VMEM((MOE_BLOCK, d + LANES), BF16),
                        pltpu.SemaphoreType.DMA((2,)), pltpu.SemaphoreType.DMA(()), pltpu.SemaphoreType.DMA(())])
    return pl.pallas_call(
        functools.partial(_dispatch_kernel, nsteps=nt // MOE_TPS, nb=n_rows // MOE_BLOCK),
        grid_spec=grid_spec,
        out_shape=jax.ShapeDtypeStruct((n_rows, d + LANES), BF16),
        compiler_params=_params("arbitrary"),
        name="moe_dispatch",
    )(tables["dmap"], tables["tile_chunks"], tables["tail_start"], tables["tail_n"], tables["n_blocks"],
      h2, ri, rit)


def _expert_kernel(be_ref, nbr_ref, xs_ref, w1_ref, w3_ref, w2_ref, y_ref, w1b, w3b, w2b):
    i = pl.program_id(0)
    used = i < nbr_ref[0]

    @pl.when(jnp.logical_not(used))
    def _():
        y_ref[...] = jnp.zeros_like(y_ref)

    @pl.when(used & ((i == 0) | (be_ref[i] != be_ref[jnp.maximum(i - 1, 0)])))
    def _():
        w1b[...] = w1_ref[0].astype(BF16)
        w3b[...] = w3_ref[0].astype(BF16)
        w2b[...] = w2_ref[0].astype(BF16)

    @pl.when(used)
    def _():
        d = y_ref.shape[-1]
        subs = [slice(j * MOE_TILE, (j + 1) * MOE_TILE) for j in range(y_ref.shape[0] // MOE_TILE)]
        ab = [(_mm(xs_ref[rs, 0:d], w1b[...]), _mm(xs_ref[rs, 0:d], w3b[...])) for rs in subs]
        hms = [(_silu(a) * b).astype(BF16) for a, b in ab]
        for rs, hm in zip(subs, hms):
            gw = xs_ref[rs, d:d + LANES].astype(F32)
            y_ref[rs, :] = (_mm(hm, w2b[...]) * (gw[:, 0:1] + gw[:, 1:2])).astype(BF16)


def _experts(xs, tables, w1, w3, w2):
    n_rows, dw = xs.shape
    d = dw - LANES
    nb = n_rows // MOE_BLOCK
    blk = lambda i, be, nbr: jnp.minimum(i, nbr[0] - 1)
    grid_spec = pltpu.PrefetchScalarGridSpec(
        num_scalar_prefetch=2,
        grid=(nb,),
        in_specs=[pl.BlockSpec((MOE_BLOCK, dw), lambda i, be, nbr: (blk(i, be, nbr), 0)),
                  pl.BlockSpec((1, d, D_EXPERT), lambda i, be, nbr: (be[blk(i, be, nbr)], 0, 0)),
                  pl.BlockSpec((1, d, D_EXPERT), lambda i, be, nbr: (be[blk(i, be, nbr)], 0, 0)),
                  pl.BlockSpec((1, D_EXPERT, d), lambda i, be, nbr: (be[blk(i, be, nbr)], 0, 0))],
        out_specs=pl.BlockSpec((MOE_BLOCK, d), lambda i, be, nbr: (i, 0)),
        scratch_shapes=[pltpu.VMEM((d, D_EXPERT), BF16), pltpu.VMEM((d, D_EXPERT), BF16),
                        pltpu.VMEM((D_EXPERT, d), BF16)])
    return pl.pallas_call(
        _expert_kernel,
        grid_spec=grid_spec,
        out_shape=jax.ShapeDtypeStruct((n_rows, d), BF16),
        compiler_params=_params("arbitrary"),
        name="moe_experts",
    )(tables["blk_expert"], tables["n_blocks"], xs, w1, w3, w2)


def _moe_rows(n_tiles):
    worst = n_tiles * (TOP_K * MOE_TILE + N_EXPERTS * (MOE_CHUNK - 1)) + N_EXPERTS * (MOE_BLOCK - MOE_CHUNK)
    return -(-worst // MOE_BLOCK) * MOE_BLOCK


def _moe_tables(cnt, n_rows):
    n_tiles = cnt.shape[0]
    per_blk = MOE_BLOCK // MOE_CHUNK

    def excl_cumsum(a, axis):
        n = a.shape[axis]
        lower = jnp.arange(n)[:, None] > jnp.arange(n)[None, :]
        if axis == 0:
            return jnp.sum(jnp.where(lower[:, :, None], a[None, :, :], 0), axis=1)
        return jnp.sum(jnp.where(lower[None, :, :], a[:, None, :], 0), axis=2)

    nch = (cnt + MOE_CHUNK - 1) // MOE_CHUNK
    loff = excl_cumsum(nch, 1)
    seg = jnp.sum(nch, axis=0)
    blocks = (seg + per_blk - 1) // per_blk
    bstart = excl_cumsum(blocks[None, :], 1)[0]
    bend = bstart + blocks
    estart = bstart * per_blk
    gbase = estart[None, :] + excl_cumsum(nch, 0)
    c = jnp.arange(MOE_TILE_CHUNKS, dtype=jnp.int32)
    owner = jnp.sum((c[None, :, None] >= (loff + nch)[:, None, :]).astype(jnp.int32), axis=-1)
    owner = jnp.minimum(owner, N_EXPERTS - 1)
    is_owner = owner[:, :, None] == jnp.arange(N_EXPERTS, dtype=jnp.int32)[None, None, :]
    dmap = c[None, :] + jnp.sum(jnp.where(is_owner, (gbase - loff)[:, None, :], 0), axis=-1)
    dmap = jnp.clip(dmap, 0, n_rows // MOE_CHUNK - 1)
    nb = n_rows // MOE_BLOCK
    blk_expert = jnp.sum((jnp.arange(nb, dtype=jnp.int32)[:, None] >= bend[None, :]).astype(jnp.int32), axis=1)
    blk_expert = jnp.minimum(blk_expert, N_EXPERTS - 1)
    return dict(dmap=dmap.reshape(-1).astype(jnp.int32), tile_chunks=jnp.sum(nch, axis=1).astype(jnp.int32),
                tail_start=(estart + seg).astype(jnp.int32), tail_n=(blocks * per_blk - seg).astype(jnp.int32),
                blk_expert=blk_expert.astype(jnp.int32), n_blocks=bend[-1:].astype(jnp.int32))


def _final_kernel(dmap_ref, nchk_ref, x1_ref, ri_ref, mod_ref, g_ref, y_hbm, o_ref, yloc, sem, *, nsteps):
    i = pl.program_id(0)
    slot = lax.rem(i, 2)

    def chunk_copy(t, u, c, sl):
        return pltpu.make_async_copy(y_hbm.at[_chunk_rows(dmap_ref[t * MOE_TILE_CHUNKS + c])],
                                     yloc.at[sl].at[u].at[_chunk_rows(c)], sem.at[sl])

    def gather(step, sl):
        for u in range(MOE_TPS):
            t = step * MOE_TPS + u
            lax.fori_loop(0, nchk_ref[t], lambda c, z, t=t, u=u: (chunk_copy(t, u, c, sl).start(), z)[1], 0)

    @pl.when(i == 0)
    def _():
        yloc[...] = jnp.zeros_like(yloc)
        gather(0, 0)

    @pl.when(i + 1 < nsteps)
    def _():
        gather(i + 1, 1 - slot)

    for u in range(MOE_TPS):
        _wait_chunks(nchk_ref[i * MOE_TPS + u],
                     lambda rows: pltpu.make_async_copy(y_hbm.at[rows], yloc.at[slot].at[0].at[rows], sem.at[slot]))

    l_io = lax.broadcasted_iota(jnp.int32, (MOE_TILE, MOE_SLAB), 1).astype(F32)
    tiles = [slice(u * MOE_TILE, (u + 1) * MOE_TILE) for u in range(MOE_TPS)]
    picks = [((l_io == ri_ref[ts, 4:5]) | (l_io == ri_ref[ts, 5:6])).astype(BF16) for ts in tiles]
    moes = [_mm(pick, yloc[slot, u]) for u, pick in enumerate(picks)]
    for ts, moe in zip(tiles, moes):
        x2 = x1_ref[ts, :] + mod_ref[0, 5:6, :] * moe
        o_ref[ts, :] = _rms(x2, g_ref[...])


def _final(x1, ri, y, mod3, final_g, tables, s):
    t, d = x1.shape
    rows = MOE_TPS * MOE_TILE
    nsteps = t // rows
    per_b = s // rows
    grid_spec = pltpu.PrefetchScalarGridSpec(
        num_scalar_prefetch=2,
        grid=(nsteps,),
        in_specs=[pl.BlockSpec((rows, d), lambda i, *_: (i, 0)),
                  pl.BlockSpec((rows, LANES), lambda i, *_: (i, 0)),
                  pl.BlockSpec((1, 6, d), lambda i, *_: (i // per_b, 0, 0)),
                  pl.BlockSpec((1, d), lambda i, *_: (0, 0)),
                  pl.BlockSpec(memory_space=pl.ANY)],
        out_specs=pl.BlockSpec((rows, d), lambda i, *_: (i, 0)),
        scratch_shapes=[pltpu.VMEM((2, MOE_TPS, MOE_SLAB, d), BF16), pltpu.SemaphoreType.DMA((2,))])
    return pl.pallas_call(
        functools.partial(_final_kernel, nsteps=nsteps),
        grid_spec=grid_spec,
        out_shape=jax.ShapeDtypeStruct((t, d), F32),
        compiler_params=_params("arbitrary"),
        name="moe_combine_final",
    )(tables["dmap"], tables["tile_chunks"], x1, ri, mod3, final_g.reshape(1, d), y)


def kernel(x, c, positions, w_ada, b_ada, norm1_g, w_in, q_norm_g, w_uq, kv_norm_g, w_ukv, w_o, norm2_g,
           w_gr, b_gr, w_er, b_er, w1, w3, w2, final_g):
    bsz, s, d = x.shape
    assert w_ada.shape[0] == 1, "one layer"
    tm = min(1024, s)
    tq = min(256, s)
    ts = min(1024, s)
    mod3 = _adaln(c, w_ada[0], b_ada[0]).reshape(bsz, 6, d)
    pos3 = positions.astype(F32).reshape(bsz, s, 1)
    q, k, vt, rq, rk, rv, rg = _pre(x, mod3, pos3, norm1_g[0], w_in[0], q_norm_g[0], w_uq[0], kv_norm_g[0], w_ukv[0],
                                    tm, tq)
    o_mla = _attention(q, k, vt, tq, ATTN_HEADS_PER_STEP)
    x1, h2, ri, rit, cnt = _post(x, o_mla, rq, rk, rv, rg, mod3, w_o[0], norm2_g[0], w_gr[0], b_gr[0], w_er[0],
                                 b_er[0], tm)
    t = bsz * s
    n_rows = _moe_rows(t // MOE_TILE)
    counts = cnt[:, 0, N_GROUPS:N_GROUPS + N_EXPERTS].astype(jnp.int32)
    tables = _moe_tables(counts, n_rows)
    ri2 = ri.reshape(t, LANES)
    xs = _dispatch(h2.reshape(t, d), ri2, rit, tables, n_rows)
    y = _experts(xs, tables, w1[0], w3[0], w2[0])
    out = _final(x1.reshape(t, d), ri2, y, mod3, final_g, tables, s)
    return out.reshape(bsz, s, d)
```

```python
import functools

import jax
import jax.numpy as jnp
from jax import lax
from jax.experimental import pallas as pl
from jax.experimental.pallas import tpu as pltpu

MLA_HEADS = 8
MLA_NOPE = 64
MLA_ROPE = 32
MLA_V = 64
Q_LORA = 256
KV_LORA = 128
RET_HEADS = 4
RET_DK = 64
RET_DV = 128
RET_CHUNK = 128
ROPE_BASE = 10000.0
NORM_EPS = 1e-6
N_GROUPS = 4
EXPERTS_PER_GROUP = 8
N_EXPERTS = N_GROUPS * EXPERTS_PER_GROUP
TOP_K = 2
D_EXPERT = 256
MOE_TILE = 256
MOE_CHUNK = 16
MOE_SLAB = 1024
MOE_TILE_CHUNKS = MOE_SLAB // MOE_CHUNK
MOE_TPS = 2
MOE_BLOCK = 1024

LANES = 128
VMEM_LIMIT = 56 * 1024 * 1024

F32 = jnp.float32
BF16 = jnp.bfloat16
NEG = float(jnp.finfo(jnp.float32).min)
LOG2_E = 1.4426950408889634
ATTN_HEADS_PER_STEP = 8
ATTN_PAIRS_PER_ITER = 2
MLA_VROWS = MLA_V + 16

_C_Q = 0
_C_KV = _C_Q + Q_LORA
_C_KPE = _C_KV + KV_LORA
_C_RQ = _C_KPE + LANES
_C_RK = _C_RQ + RET_HEADS * RET_DK
_C_RV = _C_RK + RET_HEADS * RET_DK
_C_RG = _C_RV + RET_HEADS * RET_DV
_IN_PERM = _C_RG + RET_HEADS * RET_DV


def _silu(v):
    return v / (1.0 + jnp.exp(-v))


def _mm(a, b):
    return jnp.dot(a, b, preferred_element_type=F32)


def _mm_nt(a, b):
    return lax.dot_general(a, b, (((1,), (1,)), ((), ())), preferred_element_type=F32)


def _params(*sem):
    return pltpu.CompilerParams(dimension_semantics=sem, vmem_limit_bytes=VMEM_LIMIT)


def _adaln_kernel(c_ref, w_ref, b_ref, o_ref):
    a = _silu(c_ref[...]).astype(BF16)
    o_ref[...] = _mm(a, w_ref[...].astype(BF16)) + b_ref[...]


def _adaln(c, w_ada, b_ada):
    bsz, d = c.shape
    n = w_ada.shape[1]
    tn = d
    return pl.pallas_call(
        _adaln_kernel,
        grid=(n // tn,),
        in_specs=[pl.BlockSpec((bsz, d), lambda j: (0, 0)),
                  pl.BlockSpec((d, tn), lambda j: (0, j)),
                  pl.BlockSpec((1, tn), lambda j: (0, j))],
        out_specs=pl.BlockSpec((bsz, tn), lambda j: (0, j)),
        out_shape=jax.ShapeDtypeStruct((bsz, n), F32),
        compiler_params=_params("arbitrary"),
        name="adaln",
    )(c, w_ada, b_ada.reshape(1, n))


def _rms(v, g):
    return v * lax.rsqrt(jnp.mean(v * v, axis=-1, keepdims=True) + NORM_EPS) * g


def _pre_kernel(x_ref, mod_ref, pos_ref, g1_ref, win_ref, qg_ref, wuq_ref, kvg_ref, wuk_ref,
                wuv_ref, invf_ref, q_ref, k_ref, v_ref, rq_ref, rk_ref, rv_ref, rg_ref):
    tq = v_ref.shape[-1]
    subs = [slice(i * tq, (i + 1) * tq) for i in range(v_ref.shape[1])]
    sh1 = mod_ref[0, 0:1, :]
    sc1 = mod_ref[0, 1:2, :]
    projs = [_mm((_rms(x_ref[0, rs, :], g1_ref[...]) * (1.0 + sc1) + sh1).astype(BF16), win_ref[...]) for rs in subs]

    cqs = [_rms(p[:, _C_Q:_C_Q + Q_LORA], qg_ref[...]).astype(BF16) for p in projs]
    ckvs = [_rms(p[:, _C_KV:_C_KV + KV_LORA], kvg_ref[...]).astype(BF16) for p in projs]
    qas = [_mm(cq, wuq_ref[...]) for cq in cqs]
    kns = [_mm(ckv, wuk_ref[...]) for ckv in ckvs]
    vvs = [_mm(ckv, wuv_ref[...]) for ckv in ckvs]

    lane = lax.broadcasted_iota(jnp.int32, (tq, LANES), 1)
    hi = lane >= RET_DK
    half_m, half_r = MLA_ROPE // 2, RET_DK // 2
    first_m = hi & (lane < RET_DK + half_m)
    first_r = (lane & half_r) == 0
    scale = (MLA_NOPE + MLA_ROPE) ** -0.5 * LOG2_E

    def rope(v, cos, sin, first, half):
        partner = jnp.where(first, pltpu.roll(v, LANES - half, 1), pltpu.roll(v, half, 1))
        return v * cos + partner * sin

    for i, rs in enumerate(subs):
        proj = projs[i]
        ang = pos_ref[0, rs, :] * invf_ref[...]
        cs = jnp.cos(ang)
        sn = jnp.sin(ang)
        c_mla = jnp.where(hi, cs, 1.0)
        s_mla = jnp.where(hi, jnp.where(first_m, -sn, sn), 0.0)
        c_ret = jnp.where(hi, pltpu.roll(cs, RET_DK, 1), cs)
        s_ret = jnp.where(hi, pltpu.roll(sn, RET_DK, 1), sn)
        s_ret = jnp.where(first_r, -s_ret, s_ret)

        kpe = rope(proj[:, _C_KPE:_C_KPE + LANES], c_mla, s_mla, first_m, half_m)
        cq_s = c_mla * scale
        sq_s = jnp.where(hi, sn, 0.0) * scale
        for hd in range(MLA_HEADS):
            sl = slice(hd * LANES, (hd + 1) * LANES)
            sr = slice((MLA_HEADS + hd) * LANES, (MLA_HEADS + hd + 1) * LANES)
            q_ref[0, rs, sl] = (qas[i][:, sl] * cq_s + qas[i][:, sr] * sq_s).astype(BF16)
            k_ref[0, rs, sl] = (kns[i][:, sl] + kpe).astype(BF16)
        vt = vvs[i].T
        tail = jnp.where(lax.broadcasted_iota(jnp.int32, (MLA_VROWS - MLA_V, tq), 0) == 0, 1.0, 0.0)
        slab = [piece for hd in range(MLA_HEADS) for piece in (vt[hd * MLA_V:(hd + 1) * MLA_V, :], tail)]
        v_ref[0, i] = jnp.concatenate(slab, axis=0).astype(BF16)

        for j in range(RET_HEADS * RET_DK // LANES):
            o = j * LANES
            rq = rope(proj[:, _C_RQ + o:_C_RQ + o + LANES], c_ret, s_ret, first_r, half_r)
            rk = rope(proj[:, _C_RK + o:_C_RK + o + LANES], c_ret, s_ret, first_r, half_r)
            rq_ref[0, rs, o:o + LANES] = rq.astype(BF16)
            rk_ref[0, rs, o:o + LANES] = (rk * (RET_DK ** -0.5)).astype(BF16)
        rv_ref[0, rs, :] = proj[:, _C_RV:_C_RV + RET_HEADS * RET_DV].astype(BF16)
        rg_ref[0, rs, :] = proj[:, _C_RG:_C_RG + RET_HEADS * RET_DV]


def _pad_heads(w, width, left):
    k = w.shape[0]
    w3 = w.reshape(k, -1, width)
    w3 = jnp.pad(w3, ((0, 0), (0, 0), (left, LANES - left - width)))
    return w3.reshape(k, -1)


def _pre(x, mod3, pos3, norm1_g, w_in, q_norm_g, w_uq, kv_norm_g, w_ukv, tm, tq):
    bsz, s, d = x.shape
    o = 0
    parts = {}
    for name, width in (("cq", Q_LORA), ("ckv", KV_LORA), ("kr", MLA_ROPE), ("rq", RET_HEADS * RET_DK),
                        ("rk", RET_HEADS * RET_DK), ("rv", RET_HEADS * RET_DV), ("rg", RET_HEADS * RET_DV)):
        parts[name] = w_in[:, o:o + width]
        o += width
    w_in_p = jnp.concatenate([
        parts["cq"], parts["ckv"], _pad_heads(parts["kr"], MLA_ROPE, MLA_NOPE),
        parts["rq"], parts["rk"], parts["rv"], parts["rg"]], axis=1).astype(BF16)
    assert w_in_p.shape[1] == _IN_PERM
    wq_rope = w_uq.reshape(Q_LORA, MLA_HEADS, MLA_NOPE + MLA_ROPE)[:, :, MLA_NOPE:].reshape(Q_LORA, MLA_HEADS, 2, -1)
    wq_rot = jnp.stack([-wq_rope[:, :, 1], wq_rope[:, :, 0]], axis=2).reshape(Q_LORA, -1)
    w_uq_p = jnp.concatenate([_pad_heads(w_uq, MLA_NOPE + MLA_ROPE, 0),
                              _pad_heads(wq_rot, MLA_ROPE, MLA_NOPE)], axis=1).astype(BF16)
    wkv3 = w_ukv.reshape(KV_LORA, MLA_HEADS, MLA_NOPE + MLA_V)
    w_uk_p = _pad_heads(wkv3[:, :, :MLA_NOPE].reshape(KV_LORA, -1), MLA_NOPE, 0).astype(BF16)
    w_uv = wkv3[:, :, MLA_NOPE:].reshape(KV_LORA, -1).astype(BF16)
    half_r, half_m = RET_DK // 2, MLA_ROPE // 2
    f_r = ROPE_BASE ** (-(jnp.arange(half_r, dtype=F32) / half_r))
    f_m = ROPE_BASE ** (-(jnp.arange(half_m, dtype=F32) / half_m))
    invf = jnp.concatenate([f_r, f_r, f_m, f_m, jnp.zeros((LANES - 2 * half_r - 2 * half_m,), F32)]).reshape(1, LANES)

    hq = MLA_HEADS * LANES
    const = lambda shape: pl.BlockSpec(shape, lambda b, i: (0,) * len(shape))
    tile = lambda w: pl.BlockSpec((1, tm, w), lambda b, i: (b, i, 0))
    return pl.pallas_call(
        _pre_kernel,
        grid=(bsz, s // tm),
        in_specs=[tile(d), pl.BlockSpec((1, 6, d), lambda b, i: (b, 0, 0)), tile(1), const((1, d)),
                  const((d, _IN_PERM)), const((1, Q_LORA)), const((Q_LORA, 2 * hq)),
                  const((1, KV_LORA)), const((KV_LORA, hq)), const((KV_LORA, MLA_HEADS * MLA_V)), const((1, LANES))],
        out_specs=[tile(hq), tile(hq),
                   pl.BlockSpec((1, tm // tq, MLA_HEADS * MLA_VROWS, tq), lambda b, i: (b, i, 0, 0)),
                   tile(RET_HEADS * RET_DK), tile(RET_HEADS * RET_DK),
                   tile(RET_HEADS * RET_DV), tile(RET_HEADS * RET_DV)],
        out_shape=[jax.ShapeDtypeStruct((bsz, s, hq), BF16), jax.ShapeDtypeStruct((bsz, s, hq), BF16),
                   jax.ShapeDtypeStruct((bsz, s // tq, MLA_HEADS * MLA_VROWS, tq), BF16),
                   jax.ShapeDtypeStruct((bsz, s, RET_HEADS * RET_DK), BF16),
                   jax.ShapeDtypeStruct((bsz, s, RET_HEADS * RET_DK), BF16),
                   jax.ShapeDtypeStruct((bsz, s, RET_HEADS * RET_DV), BF16),
                   jax.ShapeDtypeStruct((bsz, s, RET_HEADS * RET_DV), F32)],
        compiler_params=_params("parallel", "arbitrary"),
        name="pre_mixer",
    )(x, mod3, pos3, norm1_g.reshape(1, d), w_in_p, q_norm_g.reshape(1, -1), w_uq_p,
      kv_norm_g.reshape(1, -1), w_uk_p, w_uv, invf)


def _attn_kernel(q_ref, k_ref, vt_ref, o_ref, st_x, st_y, bm_x, bm_y, m_scr, acc_scr, *, tq, kb, hps):
    qi = pl.program_id(1)
    assert tq == 2 * kb, "a query tile spans two key blocks: the last two blocks of a tile are masked"
    key = lax.broadcasted_iota(jnp.int32, (kb, tq), 0)
    qry = lax.broadcasted_iota(jnp.int32, (kb, tq), 1)
    hsl = [slice(hh * LANES, (hh + 1) * LANES) for hh in range(hps)]
    vsl = [slice(hh * MLA_VROWS, (hh + 1) * MLA_VROWS) for hh in range(hps)]
    qs = [q_ref[0, :, hs] for hs in hsl]
    bufs = {"x": (st_x, bm_x), "y": (st_y, bm_y)}
    n_blk = 2 * qi + 2

    def scores(blk, buf, diag=None):
        st_ref, bm_ref = bufs[buf]
        start = pl.multiple_of(blk * kb, kb)
        for hh in range(hps):
            st = _mm_nt(k_ref[0, pl.ds(start, kb), hsl[hh]], qs[hh])
            if diag is not None:
                st = jnp.where(key + diag * kb <= qry, st, NEG)
            st_ref[hh] = st
            bm_ref[hh, 0:1, :] = jnp.max(st, axis=0, keepdims=True)

    def update(blk, buf):
        st_ref, bm_ref = bufs[buf]
        for hh in range(hps):
            m = m_scr[hh, 0:1, :]
            m_new = jnp.maximum(m, bm_ref[hh, 0:1, :])
            p = jnp.exp2(st_ref[hh] - m_new).astype(BF16)
            acc_scr[hh] = jnp.exp2(m - m_new) * acc_scr[hh] + _mm(vt_ref[0, blk, vsl[hh], :], p)
            m_scr[hh, 0:1, :] = m_new

    m_scr[...] = jnp.full(m_scr.shape, NEG, F32)
    acc_scr[...] = jnp.zeros(acc_scr.shape, F32)

    @pl.when(qi >= 1)
    def _():
        scores(0, "x")

    def steady(base, pairs):
        for r in range(pairs):
            scores(base + 2 * r + 1, "y")
            update(base + 2 * r, "x")
            scores(base + 2 * r + 2, "x")
            update(base + 2 * r + 1, "y")

    n_pairs = jnp.maximum(qi - 1, 0)
    n_long = n_pairs // ATTN_PAIRS_PER_ITER

    @pl.loop(0, n_long)
    def _(i):
        steady(2 * ATTN_PAIRS_PER_ITER * i, ATTN_PAIRS_PER_ITER)

    @pl.loop(n_long * ATTN_PAIRS_PER_ITER, n_pairs)
    def _(i):
        steady(2 * i, 1)

    @pl.when(qi >= 1)
    def _():
        scores(n_blk - 3, "y")
        update(n_blk - 4, "x")
        scores(n_blk - 2, "x", diag=0)
        update(n_blk - 3, "y")
        scores(n_blk - 1, "y", diag=1)
        update(n_blk - 2, "x")
        update(n_blk - 1, "y")

    @pl.when(qi == 0)
    def _():
        scores(0, "x", diag=0)
        scores(1, "y", diag=1)
        update(0, "x")
        update(1, "y")

    out_t = jnp.concatenate([acc_scr[hh, 0:MLA_V, :] * (1.0 / acc_scr[hh, MLA_V:MLA_V + 1, :]) for hh in range(hps)],
                            axis=0)
    o_ref[0] = out_t.T.astype(BF16)


def _attention(q, k, vt, kb, hps):
    bsz, s, _ = q.shape
    groups = MLA_HEADS // hps
    tq = 2 * kb
    assert vt.shape == (bsz, s // kb, MLA_HEADS * MLA_VROWS, kb)
    return pl.pallas_call(
        functools.partial(_attn_kernel, tq=tq, kb=kb, hps=hps),
        grid=(bsz * groups, s // tq),
        in_specs=[pl.BlockSpec((1, tq, hps * LANES), lambda g, i: (g // groups, i, g % groups)),
                  pl.BlockSpec((1, s, hps * LANES), lambda g, i: (g // groups, 0, g % groups)),
                  pl.BlockSpec((1, s // kb, hps * MLA_VROWS, kb), lambda g, i: (g // groups, 0, g % groups, 0))],
        out_specs=pl.BlockSpec((1, tq, hps * MLA_V), lambda g, i: (g // groups, i, g % groups)),
        out_shape=jax.ShapeDtypeStruct((bsz, s, MLA_HEADS * MLA_V), BF16),
        scratch_shapes=[pltpu.VMEM((hps, kb, tq), F32), pltpu.VMEM((hps, kb, tq), F32),
                        pltpu.VMEM((hps, 8, tq), F32), pltpu.VMEM((hps, 8, tq), F32),
                        pltpu.VMEM((hps, 8, tq), F32), pltpu.VMEM((hps, MLA_VROWS, tq), F32)],
        compiler_params=_params("parallel", "arbitrary"),
        name="mla_attention",
    )(q, k, vt)


def _retention_tile(rq_ref, rk_ref, rv_ref, rg_ref, dm_ref, xi_ref, zt_ref, dc_ref, o_ref, st_ref, *, ts):
    @pl.when(pl.program_id(1) == 0)
    def _():
        st_ref[...] = jnp.zeros_like(st_ref)

    lane = lax.broadcasted_iota(jnp.int32, (RET_CHUNK, LANES), 1)
    row = lax.broadcasted_iota(jnp.int32, (LANES, RET_DV), 0)
    hpt = LANES // RET_DK
    n_chunks = ts // RET_CHUNK
    tiles = [(c, p) for c in range(n_chunks) for p in range(RET_HEADS // hpt)]
    units = [(c, hd) for c in range(n_chunks) for hd in range(RET_HEADS)]
    rows = lambda c: slice(c * RET_CHUNK, (c + 1) * RET_CHUNK)
    vsl = lambda hd: slice(hd * RET_DV, (hd + 1) * RET_DV)
    mine = [(lane >= sub * RET_DK) & (lane < (sub + 1) * RET_DK) for sub in range(hpt)]
    mine_row = [(row >= sub * RET_DK) & (row < (sub + 1) * RET_DK) for sub in range(hpt)]

    q2 = {(c, p): rq_ref[0, rows(c), p * LANES:(p + 1) * LANES] for c, p in tiles}
    k2 = {(c, p): rk_ref[0, rows(c), p * LANES:(p + 1) * LANES] for c, p in tiles}
    vh = {(c, hd): rv_ref[0, rows(c), vsl(hd)] for c, hd in units}
    sc, un = {}, {}
    for c, p in tiles:
        heads = [p * hpt + sub for sub in range(hpt)]
        kcat = jnp.concatenate([jnp.where(mine[sub], k2[(c, p)], 0.0).astype(BF16) for sub in range(hpt)], axis=0)
        s_all = _mm_nt(q2[(c, p)], kcat)
        kz = jnp.zeros((RET_CHUNK, LANES), F32)
        for sub, hd in enumerate(heads):
            sc[(c, hd)] = s_all[:, sub * RET_CHUNK:(sub + 1) * RET_CHUNK]
            kz = kz + jnp.where(mine[sub], k2[(c, p)].astype(F32) * zt_ref[hd], 0.0)
        u_all = _mm(kz.astype(BF16).T, jnp.concatenate([vh[(c, hd)] for hd in heads], axis=1))
        for sub, hd in enumerate(heads):
            un[(c, hd)] = jnp.where(mine_row[sub], u_all[:, sub * RET_DV:(sub + 1) * RET_DV], 0.0)
    prev = {}
    for hd in range(RET_HEADS):
        st = st_ref[hd]
        for c in range(n_chunks):
            prev[(c, hd)] = st.astype(BF16)
            st = st * dc_ref[hd] + un[(c, hd)]
        st_ref[hd] = st
    for c, hd in units:
        u = (c, hd)
        lhs = jnp.concatenate([(sc[u] * dm_ref[hd]).astype(BF16),
                               (q2[(c, hd // hpt)].astype(F32) * xi_ref[hd]).astype(BF16)], axis=1)
        o = _mm(lhs, jnp.concatenate([vh[u], prev[u]], axis=0))
        mu = jnp.mean(o, axis=-1, keepdims=True)
        oc = o - mu
        on = oc * lax.rsqrt(jnp.mean(oc * oc, axis=-1, keepdims=True) + NORM_EPS)
        o_ref[rows(c), vsl(hd)] = (_silu(rg_ref[0, rows(c), vsl(hd)]) * on).astype(BF16)


def _retention_tables():
    c = RET_CHUNK
    gamma = 1.0 - jnp.power(2.0, -5.0 - jnp.arange(RET_HEADS, dtype=F32))
    log_g = jnp.log(gamma)
    idx = jnp.arange(c, dtype=F32)
    diff = idx[:, None] - idx[None, :]
    dmask = jnp.where(diff[None] >= 0, jnp.exp(jnp.maximum(diff, 0.0)[None] * log_g[:, None, None]), 0.0)
    zeta = jnp.exp((c - 1.0 - idx)[None, :] * log_g[:, None])
    xi = jnp.exp((idx + 1.0)[None, :] * log_g[:, None])
    decay = jnp.exp(c * log_g)
    xi_b = jnp.broadcast_to(xi[:, :, None], (RET_HEADS, c, LANES))
    zt_b = jnp.broadcast_to(zeta[:, :, None], (RET_HEADS, c, LANES))
    dc_b = jnp.broadcast_to(decay[:, None, None], (RET_HEADS, LANES, RET_DV))
    return dmask, xi_b, zt_b, dc_b


def _post_kernel(x_ref, om_ref, rq_ref, rk_ref, rv_ref, rg_ref, dm_ref, xi_ref, zt_ref, dc_ref,
                 mod_ref, wo_ref, g2_ref, wr_ref, br_ref,
                 x1_ref, h2_ref, ri_ref, rit_ref, cnt_ref, or_scr, st_ref):
    _retention_tile(rq_ref, rk_ref, rv_ref, rg_ref, dm_ref, xi_ref, zt_ref, dc_ref, or_scr, st_ref,
                    ts=x_ref.shape[1])
    half = om_ref.shape[-1]
    subs = [slice(i * MOE_TILE, (i + 1) * MOE_TILE) for i in range(x_ref.shape[1] // MOE_TILE)]
    mixes = [_mm(om_ref[0, rs, :], wo_ref[0:half, :]) + _mm(or_scr[rs, :], wo_ref[half:, :]) for rs in subs]

    lgs = []
    for rs, mix in zip(subs, mixes):
        x1 = x_ref[0, rs, :] + mod_ref[0, 2:3, :] * mix
        x1_ref[0, rs, :] = x1
        h2 = _rms(x1, g2_ref[...]) * (1.0 + mod_ref[0, 4:5, :]) + mod_ref[0, 3:4, :]
        hi = h2.astype(BF16)
        h2_ref[0, rs, :] = hi
        lo = (h2 - hi.astype(F32)).astype(BF16)
        both = _mm(hi, wr_ref[...])
        lgs.append(both[:, 0:LANES] + both[:, LANES:2 * LANES] + _mm(lo, wr_ref[:, 0:LANES]) + br_ref[...])

    lane = lax.broadcasted_iota(jnp.int32, (MOE_TILE, LANES), 1)
    big = jnp.int32(1 << 20)
    gmask = lane < N_GROUPS
    el = lane - N_GROUPS
    assert EXPERTS_PER_GROUP == 8
    routed = []
    for lg in lgs:
        gmax = jnp.max(jnp.where(gmask, lg, NEG), axis=-1, keepdims=True)
        ge = jnp.where(gmask, jnp.exp(lg - gmax), 0.0)
        pg = ge / jnp.sum(ge, axis=-1, keepdims=True)
        p_top = jnp.max(pg, axis=-1, keepdims=True)
        g_top = jnp.min(jnp.where(gmask & (pg == p_top), lane, big), axis=-1, keepdims=True)

        emask = (el >= 0) & (el < N_EXPERTS) & (lax.shift_right_arithmetic(el, 3) == g_top)
        ev = jnp.where(emask, lg, NEG)
        v1 = jnp.max(ev, axis=-1, keepdims=True)
        i1 = jnp.min(jnp.where(emask & (ev == v1), lane, big), axis=-1, keepdims=True)
        emask2 = emask & (lane != i1)
        ev2 = jnp.where(emask2, lg, NEG)
        v2 = jnp.max(ev2, axis=-1, keepdims=True)
        i2 = jnp.min(jnp.where(emask2 & (ev2 == v2), lane, big), axis=-1, keepdims=True)
        e = jnp.exp(v2 - v1)
        den = 1.0 + e
        routed.append((i1, i2, (1.0 / den) * p_top, (e / den) * p_top))

    r_io = lax.broadcasted_iota(jnp.int32, (MOE_TILE, MOE_TILE), 0)
    c_io = lax.broadcasted_iota(jnp.int32, (MOE_TILE, MOE_TILE), 1)
    earlier_tok = (c_io < r_io).astype(BF16)
    lr_io = lax.broadcasted_iota(jnp.int32, (LANES, LANES), 0)
    lc_io = lax.broadcasted_iota(jnp.int32, (LANES, LANES), 1)
    earlier_lane = (lr_io < lc_io).astype(BF16)
    for hf, (rs, (i1, i2, w1, w2)) in enumerate(zip(subs, routed)):
        oh = [lane == i1, lane == i2]
        cnt = (oh[0] | oh[1]).astype(BF16)
        excl = _mm(earlier_tok, cnt)
        n = jnp.sum(cnt.astype(F32), axis=0, keepdims=True)
        npad = jnp.floor((n + (MOE_CHUNK - 1)) * (1.0 / MOE_CHUNK)) * MOE_CHUNK
        loff = _mm(jnp.broadcast_to(npad, (8, LANES)).astype(BF16), earlier_lane)
        pos = excl + loff[0:1, :]
        cnt_ref[hf] = jnp.broadcast_to(n, (8, LANES))
        lp0, lp1 = [jnp.sum(jnp.where(o, pos, 0.0), axis=-1, keepdims=True) for o in oh]
        cols = [(i1 - N_GROUPS).astype(F32), (i2 - N_GROUPS).astype(F32), w1, w2, lp0, lp1]
        ri = jnp.zeros((MOE_TILE, LANES), F32)
        for j, col in enumerate(cols):
            ri = jnp.where(lane == j, col, ri)
        ri_ref[0, rs, :] = ri
        rit_ref[:, rs] = ri.T


def _post(x, o_mla, rq, rk, rv, rg, mod3, w_o, norm2_g, w_gr, b_gr, w_er, b_er, tm):
    bsz, s, d = x.shape
    ret_tables = _retention_tables()
    c = RET_CHUNK
    w_r = jnp.concatenate([w_gr, w_er.reshape(d, N_EXPERTS), jnp.zeros((d, LANES - N_GROUPS - N_EXPERTS), F32)], axis=1)
    w_rh = w_r.astype(BF16)
    w_rl = (w_r - w_rh.astype(F32)).astype(BF16)
    w_r2 = jnp.concatenate([w_rh, w_rl], axis=1)
    b_r =jnp.concatenate([b_gr, b_er.reshape(-1), jnp.zeros((LANES - N_GROUPS - N_EXPERTS,), F32)]).reshape(1, LANES)
    tile = lambda w: pl.BlockSpec((1, tm, w), lambda b, i: (b, i, 0))
    const = lambda shape: pl.BlockSpec(shape, lambda b, i: (0,) * len(shape))
    per_b = s // tm
    sub = tm // MOE_TILE
    return pl.pallas_call(
        _post_kernel,
        grid=(bsz, per_b),
        in_specs=[tile(d), tile(o_mla.shape[-1]),
                  tile(RET_HEADS * RET_DK), tile(RET_HEADS * RET_DK), tile(RET_HEADS * RET_DV), tile(RET_HEADS * RET_DV),
                  const((RET_HEADS, c, c)), const((RET_HEADS, c, LANES)), const((RET_HEADS, c, LANES)),
                  const((RET_HEADS, LANES, RET_DV)),
                  pl.BlockSpec((1, 6, d), lambda b, i: (b, 0, 0)),
                  const((d, d)), const((1, d)), const((d, 2 * LANES)), const((1, LANES))],
        out_specs=[tile(d), tile(d), tile(LANES),
                   pl.BlockSpec((LANES, tm), lambda b, i: (0, b * per_b + i)),
                   pl.BlockSpec((sub, 8, LANES), lambda b, i: (b * per_b + i, 0, 0))],
        out_shape=[jax.ShapeDtypeStruct((bsz, s, d), F32), jax.ShapeDtypeStruct((bsz, s, d), BF16),
                   jax.ShapeDtypeStruct((bsz, s, LANES), F32),
                   jax.ShapeDtypeStruct((LANES, bsz * s), F32),
                   jax.ShapeDtypeStruct((bsz * s // MOE_TILE, 8, LANES), F32)],
        scratch_shapes=[pltpu.VMEM((tm, RET_HEADS * RET_DV), BF16), pltpu.VMEM((RET_HEADS, LANES, RET_DV), F32)],
        compiler_params=_params("parallel", "arbitrary"),
        name="post_mixer",
    )(x, o_mla, rq, rk, rv, rg, *ret_tables, mod3, w_o.astype(BF16), norm2_g.reshape(1, d), w_r2, b_r)


def _chunk_rows(c):
    return pl.ds(pl.multiple_of(c * MOE_CHUNK, MOE_CHUNK), MOE_CHUNK)


def _dispatch_kernel(dmap_ref, nchk_ref, tstart_ref, tn_ref, nbr_ref, h2_ref, ri_ref, rit_ref, xs_hbm,
                     xloc, zblk, sem, zsem, bsem, *, nsteps, nb):
    i = pl.program_id(0)
    slot = lax.rem(i, 2)
    d = h2_ref.shape[-1]

    def zero_copy(e, c):
        return pltpu.make_async_copy(zblk.at[pl.ds(0, MOE_CHUNK)], xs_hbm.at[_chunk_rows(tstart_ref[e] + c)], zsem)

    def zero_block(j):
        rows = pl.ds(pl.multiple_of(j * MOE_BLOCK, MOE_BLOCK), MOE_BLOCK)
        return pltpu.make_async_copy(zblk, xs_hbm.at[rows], bsem)

    def chunk_copy(t, u, c, sl):
        return pltpu.make_async_copy(xloc.at[sl].at[u].at[_chunk_rows(c)],
                                     xs_hbm.at[_chunk_rows(dmap_ref[t * MOE_TILE_CHUNKS + c])], sem.at[sl])

    def wait_step(step, sl):
        for u in range(MOE_TPS):
            _wait_chunks(nchk_ref[step * MOE_TPS + u],
                         lambda rows: pltpu.make_async_copy(xloc.at[sl].at[0].at[rows], xs_hbm.at[rows], sem.at[sl]))

    @pl.when(i == 0)
    def _():
        zblk[...] = jnp.zeros_like(zblk)
        lax.fori_loop(nbr_ref[0], nb, lambda j, z: (zero_block(j).start(), z)[1], 0)
        for e in range(N_EXPERTS):
            lax.fori_loop(0, tn_ref[e], lambda c, z, e=e: (zero_copy(e, c).start(), z)[1], 0)
        for e in range(N_EXPERTS):
            lax.fori_loop(0, tn_ref[e], lambda c, z, e=e: (zero_copy(e, c).wait(), z)[1], 0)

    @pl.when(i >= 2)
    def _():
        wait_step(i - 2, slot)

    s_io = lax.broadcasted_iota(jnp.int32, (MOE_SLAB, MOE_TILE), 0).astype(F32)
    lane = lax.broadcasted_iota(jnp.int32, (MOE_TILE, LANES), 1)
    tiles = [slice(u * MOE_TILE, (u + 1) * MOE_TILE) for u in range(MOE_TPS)]
    pms = [[(s_io == rit_ref[4 + k:5 + k, ts]).astype(BF16) for k in range(TOP_K)] for ts in tiles]
    mains = [_mm(pm[0] + pm[1], h2_ref[ts, :]) for pm, ts in zip(pms, tiles)]
    for u, (pm, ts) in enumerate(zip(pms, tiles)):
        xloc[slot, u, :, 0:d] = mains[u].astype(BF16)
        wx = jnp.zeros((MOE_SLAB, LANES), F32)
        for k in range(TOP_K):
            w = ri_ref[ts, TOP_K + k:TOP_K + k + 1]
            hi = w.astype(BF16).astype(F32)
            wx = wx + _mm(pm[k], jnp.where(lane == 0, hi, jnp.where(lane == 1, w - hi, 0.0)).astype(BF16))
        xloc[slot, u, :, d:d + LANES] = wx.astype(BF16)

    for u in range(MOE_TPS):
        t = i * MOE_TPS + u
        lax.fori_loop(0, nchk_ref[t], lambda c, z, t=t, u=u: (chunk_copy(t, u, c, slot).start(), z)[1], 0)

    @pl.when(i == nsteps - 1)
    def _():
        wait_step(i, slot)
        if nsteps >= 2:
            wait_step(i - 1, 1 - slot)
        lax.fori_loop(nbr_ref[0], nb, lambda j, z: (zero_block(j).wait(), z)[1], 0)


def _wait_chunks(n, copy_of_rows):
    bit = MOE_TILE_CHUNKS
    while bit >= 1:
        @pl.when((n & bit) != 0)
        def _(bit=bit):
            copy_of_rows(pl.ds(0, bit * MOE_CHUNK)).wait()
        bit //= 2


def _dispatch(h2, ri, rit, tables, n_rows):
    t, d = h2.shape
    nt = t // MOE_TILE
    assert nt % MOE_TPS == 0 and MOE_SLAB >= TOP_K * MOE_TILE + N_EXPERTS * (MOE_CHUNK - 1)
    rows = MOE_TPS * MOE_TILE
    grid_spec = pltpu.PrefetchScalarGridSpec(
        num_scalar_prefetch=5,
        grid=(nt // MOE_TPS,),
        in_specs=[pl.BlockSpec((rows, d), lambda i, *_: (i, 0)),
                  pl.BlockSpec((rows, LANES), lambda i, *_: (i, 0)),
                  pl.BlockSpec((8, rows), lambda i, *_: (0, i))],
        out_specs=pl.BlockSpec(memory_space=pl.ANY),
        scratch_shapes=[pltpu.VMEM((2, MOE_TPS, MOE_SLAB, d + LANES), BF16), pltpu.VMEM((MOE_BLOCK, d + LANES), BF16),
                        pltpu.SemaphoreType.DMA((2,)), pltpu.SemaphoreType.DMA(()), pltpu.SemaphoreType.DMA(())])
    return pl.pallas_call(
        functools.partial(_dispatch_kernel, nsteps=nt // MOE_TPS, nb=n_rows // MOE_BLOCK),
        grid_spec=grid_spec,
        out_shape=jax.ShapeDtypeStruct((n_rows, d + LANES), BF16),
        compiler_params=_params("arbitrary"),
        name="moe_dispatch",
    )(tables["dmap"], tables["tile_chunks"], tables["tail_start"], tables["tail_n"], tables["n_blocks"],
      h2, ri, rit)


def _expert_kernel(be_ref, nbr_ref, xs_ref, w1_ref, w3_ref, w2_ref, y_ref, w1b, w3b, w2b):
    i = pl.program_id(0)
    used = i < nbr_ref[0]

    @pl.when(jnp.logical_not(used))
    def _():
        y_ref[...] = jnp.zeros_like(y_ref)

    @pl.when(used & ((i == 0) | (be_ref[i] != be_ref[jnp.maximum(i - 1, 0)])))
    def _():
        w1b[...] = w1_ref[0].astype(BF16)
        w3b[...] = w3_ref[0].astype(BF16)
        w2b[...] = w2_ref[0].astype(BF16)

    @pl.when(used)
    def _():
        d = y_ref.shape[-1]
        subs = [slice(j * MOE_TILE, (j + 1) * MOE_TILE) for j in range(y_ref.shape[0] // MOE_TILE)]
        ab = [(_mm(xs_ref[rs, 0:d], w1b[...]), _mm(xs_ref[rs, 0:d], w3b[...])) for rs in subs]
        hms = [(_silu(a) * b).astype(BF16) for a, b in ab]
        for rs, hm in zip(subs, hms):
            gw = xs_ref[rs, d:d + LANES].astype(F32)
            y_ref[rs, :] = (_mm(hm, w2b[...]) * (gw[:, 0:1] + gw[:, 1:2])).astype(BF16)


def _experts(xs, tables, w1, w3, w2):
    n_rows, dw = xs.shape
    d = dw - LANES
    nb = n_rows // MOE_BLOCK
    blk = lambda i, be, nbr: jnp.minimum(i, nbr[0] - 1)
    grid_spec = pltpu.PrefetchScalarGridSpec(
        num_scalar_prefetch=2,
        grid=(nb,),
        in_specs=[pl.BlockSpec((MOE_BLOCK, dw), lambda i, be, nbr: (blk(i, be, nbr), 0)),
                  pl.BlockSpec((1, d, D_EXPERT), lambda i, be, nbr: (be[blk(i, be, nbr)], 0, 0)),
                  pl.BlockSpec((1, d, D_EXPERT), lambda i, be, nbr: (be[blk(i, be, nbr)], 0, 0)),
                  pl.BlockSpec((1, D_EXPERT, d), lambda i, be, nbr: (be[blk(i, be, nbr)], 0, 0))],
        out_specs=pl.BlockSpec((MOE_BLOCK, d), lambda i, be, nbr: (i, 0)),
        scratch_shapes=[pltpu.VMEM((d, D_EXPERT), BF16), pltpu.VMEM((d, D_EXPERT), BF16),
                        pltpu.VMEM((D_EXPERT, d), BF16)])
    return pl.pallas_call(
        _expert_kernel,
        grid_spec=grid_spec,
        out_shape=jax.ShapeDtypeStruct((n_rows, d), BF16),
        compiler_params=_params("arbitrary"),
        name="moe_experts",
    )(tables["blk_expert"], tables["n_blocks"], xs, w1, w3, w2)


def _moe_rows(n_tiles):
    worst = n_tiles * (TOP_K * MOE_TILE + N_EXPERTS * (MOE_CHUNK - 1)) + N_EXPERTS * (MOE_BLOCK - MOE_CHUNK)
    return -(-worst // MOE_BLOCK) * MOE_BLOCK


def _moe_tables(cnt, n_rows):
    n_tiles = cnt.shape[0]
    per_blk = MOE_BLOCK // MOE_CHUNK

    def excl_cumsum(a, axis):
        n = a.shape[axis]
        lower = jnp.arange(n)[:, None] > jnp.arange(n)[None, :]
        if axis == 0:
            return jnp.sum(jnp.where(lower[:, :, None], a[None, :, :], 0), axis=1)
        return jnp.sum(jnp.where(lower[None, :, :], a[:, None, :], 0), axis=2)

    nch = (cnt + MOE_CHUNK - 1) // MOE_CHUNK
    loff = excl_cumsum(nch, 1)
    seg = jnp.sum(nch, axis=0)
    blocks = (seg + per_blk - 1) // per_blk
    bstart = excl_cumsum(blocks[None, :], 1)[0]
    bend = bstart + blocks
    estart = bstart * per_blk
    gbase = estart[None, :] + excl_cumsum(nch, 0)
    c = jnp.arange(MOE_TILE_CHUNKS, dtype=jnp.int32)
    owner = jnp.sum((c[None, :, None] >= (loff + nch)[:, None, :]).astype(jnp.int32), axis=-1)
    owner = jnp.minimum(owner, N_EXPERTS - 1)
    is_owner = owner[:, :, None] == jnp.arange(N_EXPERTS, dtype=jnp.int32)[None, None, :]
    dmap = c[None, :] + jnp.sum(jnp.where(is_owner, (gbase - loff)[:, None, :], 0), axis=-1)
    dmap = jnp.clip(dmap, 0, n_rows // MOE_CHUNK - 1)
    nb = n_rows // MOE_BLOCK
    blk_expert = jnp.sum((jnp.arange(nb, dtype=jnp.int32)[:, None] >= bend[None, :]).astype(jnp.int32), axis=1)
    blk_expert = jnp.minimum(blk_expert, N_EXPERTS - 1)
    return dict(dmap=dmap.reshape(-1).astype(jnp.int32), tile_chunks=jnp.sum(nch, axis=1).astype(jnp.int32),
                tail_start=(estart + seg).astype(jnp.int32), tail_n=(blocks * per_blk - seg).astype(jnp.int32),
                blk_expert=blk_expert.astype(jnp.int32), n_blocks=bend[-1:].astype(jnp.int32))


def _final_kernel(dmap_ref, nchk_ref, x1_ref, ri_ref, mod_ref, g_ref, y_hbm, o_ref, yloc, sem, *, nsteps):
    i = pl.program_id(0)
    slot = lax.rem(i, 2)

    def chunk_copy(t, u, c, sl):
        return pltpu.make_async_copy(y_hbm.at[_chunk_rows(dmap_ref[t * MOE_TILE_CHUNKS + c])],
                                     yloc.at[sl].at[u].at[_chunk_rows(c)], sem.at[sl])

    def gather(step, sl):
        for u in range(MOE_TPS):
            t = step * MOE_TPS + u
            lax.fori_loop(0, nchk_ref[t], lambda c, z, t=t, u=u: (chunk_copy(t, u, c, sl).start(), z)[1], 0)

    @pl.when(i == 0)
    def _():
        yloc[...] = jnp.zeros_like(yloc)
        gather(0, 0)

    @pl.when(i + 1 < nsteps)
    def _():
        gather(i + 1, 1 - slot)

    for u in range(MOE_TPS):
        _wait_chunks(nchk_ref[i * MOE_TPS + u],
                     lambda rows: pltpu.make_async_copy(y_hbm.at[rows], yloc.at[slot].at[0].at[rows], sem.at[slot]))

    l_io = lax.broadcasted_iota(jnp.int32, (MOE_TILE, MOE_SLAB), 1).astype(F32)
    tiles = [slice(u * MOE_TILE, (u + 1) * MOE_TILE) for u in range(MOE_TPS)]
    picks = [((l_io == ri_ref[ts, 4:5]) | (l_io == ri_ref[ts, 5:6])).astype(BF16) for ts in tiles]
    moes = [_mm(pick, yloc[slot, u]) for u, pick in enumerate(picks)]
    for ts, moe in zip(tiles, moes):
        x2 = x1_ref[ts, :] + mod_ref[0, 5:6, :] * moe
        o_ref[ts, :] = _rms(x2, g_ref[...])


def _final(x1, ri, y, mod3, final_g, tables, s):
    t, d = x1.shape
    rows = MOE_TPS * MOE_TILE
    nsteps = t // rows
    per_b = s // rows
    grid_spec = pltpu.PrefetchScalarGridSpec(
        num_scalar_prefetch=2,
        grid=(nsteps,),
        in_specs=[pl.BlockSpec((rows, d), lambda i, *_: (i, 0)),
                  pl.BlockSpec((rows, LANES), lambda i, *_: (i, 0)),
                  pl.BlockSpec((1, 6, d), lambda i, *_: (i // per_b, 0, 0)),
                  pl.BlockSpec((1, d), lambda i, *_: (0, 0)),
                  pl.BlockSpec(memory_space=pl.ANY)],
        out_specs=pl.BlockSpec((rows, d), lambda i, *_: (i, 0)),
        scratch_shapes=[pltpu.VMEM((2, MOE_TPS, MOE_SLAB, d), BF16), pltpu.SemaphoreType.DMA((2,))])
    return pl.pallas_call(
        functools.partial(_final_kernel, nsteps=nsteps),
        grid_spec=grid_spec,
        out_shape=jax.ShapeDtypeStruct((t, d), F32),
        compiler_params=_params("arbitrary"),
        name="moe_combine_final",
    )(tables["dmap"], tables["tile_chunks"], x1, ri, mod3, final_g.reshape(1, d), y)


def kernel(x, c, positions, w_ada, b_ada, norm1_g, w_in, q_norm_g, w_uq, kv_norm_g, w_ukv, w_o, norm2_g,
           w_gr, b_gr, w_er, b_er, w1, w3, w2, final_g):
    bsz, s, d = x.shape
    assert w_ada.shape[0] == 1, "one layer"
    tm = min(1024, s)
    tq = min(256, s)
    ts = min(1024, s)
    mod3 = _adaln(c, w_ada[0], b_ada[0]).reshape(bsz, 6, d)
    pos3 = positions.astype(F32).reshape(bsz, s, 1)
    q, k, vt, rq, rk, rv, rg = _pre(x, mod3, pos3, norm1_g[0], w_in[0], q_norm_g[0], w_uq[0], kv_norm_g[0], w_ukv[0],
                                    tm, tq)
    o_mla = _attention(q, k, vt, tq, ATTN_HEADS_PER_STEP)
    x1, h2, ri, rit, cnt = _post(x, o_mla, rq, rk, rv, rg, mod3, w_o[0], norm2_g[0], w_gr[0], b_gr[0], w_er[0],
                                 b_er[0], tm)
    t = bsz * s
    n_rows = _moe_rows(t // MOE_TILE)
    counts = cnt[:, 0, N_GROUPS:N_GROUPS + N_EXPERTS].astype(jnp.int32)
    tables = _moe_tables(counts, n_rows)
    ri2 = ri.reshape(t, LANES)
    xs = _dispatch(h2.reshape(t, d), ri2, rit, tables, n_rows)
    y = _experts(xs, tables, w1[0], w3[0], w2[0])
    out = _final(x1.reshape(t, d), ri2, y, mod3, final_g, tables, s)
    return out.reshape(bsz, s, d)
```

```python
import functools

import jax
import jax.numpy as jnp
from jax import lax
from jax.experimental import pallas as pl
from jax.experimental.pallas import tpu as pltpu

MLA_HEADS = 8
MLA_NOPE = 64
MLA_ROPE = 32
MLA_V = 64
Q_LORA = 256
KV_LORA = 128
RET_HEADS = 4
RET_DK = 64
RET_DV = 128
RET_CHUNK = 128
ROPE_BASE = 10000.0
NORM_EPS = 1e-6
N_GROUPS = 4
EXPERTS_PER_GROUP = 8
N_EXPERTS = N_GROUPS * EXPERTS_PER_GROUP
TOP_K = 2
D_EXPERT = 256
MOE_TILE = 256
MOE_CHUNK = 16
MOE_SLAB = 1024
MOE_TILE_CHUNKS = MOE_SLAB // MOE_CHUNK
MOE_TPS = 2
MOE_BLOCK = 1024

LANES = 128
VMEM_LIMIT = 56 * 1024 * 1024

F32 = jnp.float32
BF16 = jnp.bfloat16
NEG = float(jnp.finfo(jnp.float32).min)
LOG2_E = 1.4426950408889634
ATTN_HEADS_PER_STEP = 8
ATTN_PAIRS_PER_ITER = 2
MLA_VROWS = MLA_V + 16

_C_Q = 0
_C_KV = _C_Q + Q_LORA
_C_KPE = _C_KV + KV_LORA
_C_RQ = _C_KPE + LANES
_C_RK = _C_RQ + RET_HEADS * RET_DK
_C_RV = _C_RK + RET_HEADS * RET_DK
_C_RG = _C_RV + RET_HEADS * RET_DV
_IN_PERM = _C_RG + RET_HEADS * RET_DV


def _silu(v):
    return v / (1.0 + jnp.exp(-v))


def _mm(a, b):
    return jnp.dot(a, b, preferred_element_type=F32)


def _mm_nt(a, b):
    return lax.dot_general(a, b, (((1,), (1,)), ((), ())), preferred_element_type=F32)


def _params(*sem):
    return pltpu.CompilerParams(dimension_semantics=sem, vmem_limit_bytes=VMEM_LIMIT)


def _adaln_kernel(c_ref, w_ref, b_ref, o_ref):
    a = _silu(c_ref[...]).astype(BF16)
    o_ref[...] = _mm(a, w_ref[...].astype(BF16)) + b_ref[...]


def _adaln(c, w_ada, b_ada):
    bsz, d = c.shape
    n = w_ada.shape[1]
    tn = d
    return pl.pallas_call(
        _adaln_kernel,
        grid=(n // tn,),
        in_specs=[pl.BlockSpec((bsz, d), lambda j: (0, 0)),
                  pl.BlockSpec((d, tn), lambda j: (0, j)),
                  pl.BlockSpec((1, tn), lambda j: (0, j))],
        out_specs=pl.BlockSpec((bsz, tn), lambda j: (0, j)),
        out_shape=jax.ShapeDtypeStruct((bsz, n), F32),
        compiler_params=_params("arbitrary"),
        name="adaln",
    )(c, w_ada, b_ada.reshape(1, n))


def _rms(v, g):
    return v * lax.rsqrt(jnp.mean(v * v, axis=-1, keepdims=True) + NORM_EPS) * g


def _pre_kernel(x_ref, mod_ref, pos_ref, g1_ref, win_ref, qg_ref, wuq_ref, kvg_ref, wuk_ref,
                wuv_ref, invf_ref, q_ref, k_ref, v_ref, rq_ref, rk_ref, rv_ref, rg_ref):
    tq = v_ref.shape[-1]
    subs = [slice(i * tq, (i + 1) * tq) for i in range(v_ref.shape[1])]
    sh1 = mod_ref[0, 0:1, :]
    sc1 = mod_ref[0, 1:2, :]
    projs = [_mm((_rms(x_ref[0, rs, :], g1_ref[...]) * (1.0 + sc1) + sh1).astype(BF16), win_ref[...]) for rs in subs]

    cqs = [_rms(p[:, _C_Q:_C_Q + Q_LORA], qg_ref[...]).astype(BF16) for p in projs]
    ckvs = [_rms(p[:, _C_KV:_C_KV + KV_LORA], kvg_ref[...]).astype(BF16) for p in projs]
    qas = [_mm(cq, wuq_ref[...]) for cq in cqs]
    kns = [_mm(ckv, wuk_ref[...]) for ckv in ckvs]
    vvs = [_mm(ckv, wuv_ref[...]) for ckv in ckvs]

    lane = lax.broadcasted_iota(jnp.int32, (tq, LANES), 1)
    hi = lane >= RET_DK
    half_m, half_r = MLA_ROPE // 2, RET_DK // 2
    first_m = hi & (lane < RET_DK + half_m)
    first_r = (lane & half_r) == 0
    scale = (MLA_NOPE + MLA_ROPE) ** -0.5 * LOG2_E

    def rope(v, cos, sin, first, half):
        partner = jnp.where(first, pltpu.roll(v, LANES - half, 1), pltpu.roll(v, half, 1))
        return v * cos + partner * sin

    for i, rs in enumerate(subs):
        proj = projs[i]
        ang = pos_ref[0, rs, :] * invf_ref[...]
        cs = jnp.cos(ang)
        sn = jnp.sin(ang)
        c_mla = jnp.where(hi, cs, 1.0)
        s_mla = jnp.where(hi, jnp.where(first_m, -sn, sn), 0.0)
        c_ret = jnp.where(hi, pltpu.roll(cs, RET_DK, 1), cs)
        s_ret = jnp.where(hi, pltpu.roll(sn, RET_DK, 1), sn)
        s_ret = jnp.where(first_r, -s_ret, s_ret)

        kpe = rope(proj[:, _C_KPE:_C_KPE + LANES], c_mla, s_mla, first_m, half_m)
        cq_s = c_mla * scale
        sq_s = jnp.where(hi, sn, 0.0) * scale
        for hd in range(MLA_HEADS):
            sl = slice(hd * LANES, (hd + 1) * LANES)
            sr = slice((MLA_HEADS + hd) * LANES, (MLA_HEADS + hd + 1) * LANES)
            q_ref[0, rs, sl] = (qas[i][:, sl] * cq_s + qas[i][:, sr] * sq_s).astype(BF16)
            k_ref[0, rs, sl] = (kns[i][:, sl] + kpe).astype(BF16)
        vt = vvs[i].T
        tail = jnp.where(lax.broadcasted_iota(jnp.int32, (MLA_VROWS - MLA_V, tq), 0) == 0, 1.0, 0.0)
        slab = [piece for hd in range(MLA_HEADS) for piece in (vt[hd * MLA_V:(hd + 1) * MLA_V, :], tail)]
        v_ref[0, i] = jnp.concatenate(slab, axis=0).astype(BF16)

        for j in range(RET_HEADS * RET_DK // LANES):
            o = j * LANES
            rq = rope(proj[:, _C_RQ + o:_C_RQ + o + LANES], c_ret, s_ret, first_r, half_r)
            rk = rope(proj[:, _C_RK + o:_C_RK + o + LANES], c_ret, s_ret, first_r, half_r)
            rq_ref[0, rs, o:o + LANES] = rq.astype(BF16)
            rk_ref[0, rs, o:o + LANES] = (rk * (RET_DK ** -0.5)).astype(BF16)
        rv_ref[0, rs, :] = proj[:, _C_RV:_C_RV + RET_HEADS * RET_DV].astype(BF16)
        rg_ref[0, rs, :] = proj[:, _C_RG:_C_RG + RET_HEADS * RET_DV]


def _pad_heads(w, width, left):
    k = w.shape[0]
    w3 = w.reshape(k, -1, width)
    w3 = jnp.pad(w3, ((0, 0), (0, 0), (left, LANES - left - width)))
    return w3.reshape(k, -1)


def _pre(x, mod3, pos3, norm1_g, w_in, q_norm_g, w_uq, kv_norm_g, w_ukv, tm, tq):
    bsz, s, d = x.shape
    o = 0
    parts = {}
    for name, width in (("cq", Q_LORA), ("ckv", KV_LORA), ("kr", MLA_ROPE), ("rq", RET_HEADS * RET_DK),
                        ("rk", RET_HEADS * RET_DK), ("rv", RET_HEADS * RET_DV), ("rg", RET_HEADS * RET_DV)):
        parts[name] = w_in[:, o:o + width]
        o += width
    w_in_p = jnp.concatenate([
        parts["cq"], parts["ckv"], _pad_heads(parts["kr"], MLA_ROPE, MLA_NOPE),
        parts["rq"], parts["rk"], parts["rv"], parts["rg"]], axis=1).astype(BF16)
    assert w_in_p.shape[1] == _IN_PERM
    wq_rope = w_uq.reshape(Q_LORA, MLA_HEADS, MLA_NOPE + MLA_ROPE)[:, :, MLA_NOPE:].reshape(Q_LORA, MLA_HEADS, 2, -1)
    wq_rot = jnp.stack([-wq_rope[:, :, 1], wq_rope[:, :, 0]], axis=2).reshape(Q_LORA, -1)
    w_uq_p = jnp.concatenate([_pad_heads(w_uq, MLA_NOPE + MLA_ROPE, 0),
                              _pad_heads(wq_rot, MLA_ROPE, MLA_NOPE)], axis=1).astype(BF16)
    wkv3 = w_ukv.reshape(KV_LORA, MLA_HEADS, MLA_NOPE + MLA_V)
    w_uk_p = _pad_heads(wkv3[:, :, :MLA_NOPE].reshape(KV_LORA, -1), MLA_NOPE, 0).astype(BF16)
    w_uv = wkv3[:, :, MLA_NOPE:].reshape(KV_LORA, -1).astype(BF16)
    half_r, half_m = RET_DK // 2, MLA_ROPE // 2
    f_r = ROPE_BASE ** (-(jnp.arange(half_r, dtype=F32) / half_r))
    f_m = ROPE_BASE ** (-(jnp.arange(half_m, dtype=F32) / half_m))
    invf = jnp.concatenate([f_r, f_r, f_m, f_m, jnp.zeros((LANES - 2 * half_r - 2 * half_m,), F32)]).reshape(1, LANES)

    hq = MLA_HEADS * LANES
    const = lambda shape: pl.BlockSpec(shape, lambda b, i: (0,) * len(shape))
    tile = lambda w: pl.BlockSpec((1, tm, w), lambda b, i: (b, i, 0))
    return pl.pallas_call(
        _pre_kernel,
        grid=(bsz, s // tm),
        in_specs=[tile(d), pl.BlockSpec((1, 6, d), lambda b, i: (b, 0, 0)), tile(1), const((1, d)),
                  const((d, _IN_PERM)), const((1, Q_LORA)), const((Q_LORA, 2 * hq)),
                  const((1, KV_LORA)), const((KV_LORA, hq)), const((KV_LORA, MLA_HEADS * MLA_V)), const((1, LANES))],
        out_specs=[tile(hq), tile(hq),
                   pl.BlockSpec((1, tm // tq, MLA_HEADS * MLA_VROWS, tq), lambda b, i: (b, i, 0, 0)),
                   tile(RET_HEADS * RET_DK), tile(RET_HEADS * RET_DK),
                   tile(RET_HEADS * RET_DV), tile(RET_HEADS * RET_DV)],
        out_shape=[jax.ShapeDtypeStruct((bsz, s, hq), BF16), jax.ShapeDtypeStruct((bsz, s, hq), BF16),
                   jax.ShapeDtypeStruct((bsz, s // tq, MLA_HEADS * MLA_VROWS, tq), BF16),
                   jax.ShapeDtypeStruct((bsz, s, RET_HEADS * RET_DK), BF16),
                   jax.ShapeDtypeStruct((bsz, s, RET_HEADS * RET_DK), BF16),
                   jax.ShapeDtypeStruct((bsz, s, RET_HEADS * RET_DV), BF16),
                   jax.ShapeDtypeStruct((bsz, s, RET_HEADS * RET_DV), F32)],
        compiler_params=_params("parallel", "arbitrary"),
        name="pre_mixer",
    )(x, mod3, pos3, norm1_g.reshape(1, d), w_in_p, q_norm_g.reshape(1, -1), w_uq_p,
      kv_norm_g.reshape(1, -1), w_uk_p, w_uv, invf)


def _attn_kernel(q_ref, k_ref, vt_ref, o_ref, st_x, st_y, bm_x, bm_y, m_scr, acc_scr, *, tq, kb, hps):
    qi = pl.program_id(1)
    assert tq == 2 * kb, "a query tile spans two key blocks: the last two blocks of a tile are masked"
    key = lax.broadcasted_iota(jnp.int32, (kb, tq), 0)
    qry = lax.broadcasted_iota(jnp.int32, (kb, tq), 1)
    hsl = [slice(hh * LANES, (hh + 1) * LANES) for hh in range(hps)]
    vsl = [slice(hh * MLA_VROWS, (hh + 1) * MLA_VROWS) for hh in range(hps)]
    qs = [q_ref[0, :, hs] for hs in hsl]
    bufs = {"x": (st_x, bm_x), "y": (st_y, bm_y)}
    n_blk = 2 * qi + 2

    def scores(blk, buf, diag=None):
        st_ref, bm_ref = bufs[buf]
        start = pl.multiple_of(blk * kb, kb)
        for hh in range(hps):
            st = _mm_nt(k_ref[0, pl.ds(start, kb), hsl[hh]], qs[hh])
            if diag is not None:
                st = jnp.where(key + diag * kb <= qry, st, NEG)
            st_ref[hh] = st
            bm_ref[hh, 0:1, :] = jnp.max(st, axis=0, keepdims=True)

    def update(blk, buf):
        st_ref, bm_ref = bufs[buf]
        for hh in range(hps):
            m = m_scr[hh, 0:1, :]
            m_new = jnp.maximum(m, bm_ref[hh, 0:1, :])
            p = jnp.exp2(st_ref[hh] - m_new).astype(BF16)
            acc_scr[hh] = jnp.exp2(m - m_new) * acc_scr[hh] + _mm(vt_ref[0, blk, vsl[hh], :], p)
            m_scr[hh, 0:1, :] = m_new

    m_scr[...] = jnp.full(m_scr.shape, NEG, F32)
    acc_scr[...] = jnp.zeros(acc_scr.shape, F32)

    @pl.when(qi >= 1)
    def _():
        scores(0, "x")

    def steady(base, pairs):
        for r in range(pairs):
            scores(base + 2 * r + 1, "y")
            update(base + 2 * r, "x")
            scores(base + 2 * r + 2, "x")
            update(base + 2 * r + 1, "y")

    n_pairs = jnp.maximum(qi - 1, 0)
    n_long = n_pairs // ATTN_PAIRS_PER_ITER

    @pl.loop(0, n_long)
    def _(i):
        steady(2 * ATTN_PAIRS_PER_ITER * i, ATTN_PAIRS_PER_ITER)

    @pl.loop(n_long * ATTN_PAIRS_PER_ITER, n_pairs)
    def _(i):
        steady(2 * i, 1)

    @pl.when(qi >= 1)
    def _():
        scores(n_blk - 3, "y")
        update(n_blk - 4, "x")
        scores(n_blk - 2, "x", diag=0)
        update(n_blk - 3, "y")
        scores(n_blk - 1, "y", diag=1)
        update(n_blk - 2, "x")
        update(n_blk - 1, "y")

    @pl.when(qi == 0)
    def _():
        scores(0, "x", diag=0)
        scores(1, "y", diag=1)
        update(0, "x")
        update(1, "y")

    out_t = jnp.concatenate([acc_scr[hh, 0:MLA_V, :] * (1.0 / acc_scr[hh, MLA_V:MLA_V + 1, :]) for hh in range(hps)],
                            axis=0)
    o_ref[0] = out_t.T.astype(BF16)


def _attention(q, k, vt, kb, hps):
    bsz, s, _ = q.shape
    groups = MLA_HEADS // hps
    tq = 2 * kb
    assert vt.shape == (bsz, s // kb, MLA_HEADS * MLA_VROWS, kb)
    return pl.pallas_call(
        functools.partial(_attn_kernel, tq=tq, kb=kb, hps=hps),
        grid=(bsz * groups, s // tq),
        in_specs=[pl.BlockSpec((1, tq, hps * LANES), lambda g, i: (g // groups, i, g % groups)),
                  pl.BlockSpec((1, s, hps * LANES), lambda g, i: (g // groups, 0, g % groups)),
                  pl.BlockSpec((1, s // kb, hps * MLA_VROWS, kb), lambda g, i: (g // groups, 0, g % groups, 0))],
        out_specs=pl.BlockSpec((1, tq, hps * MLA_V), lambda g, i: (g // groups, i, g % groups)),
        out_shape=jax.ShapeDtypeStruct((bsz, s, MLA_HEADS * MLA_V), BF16),
        scratch_shapes=[pltpu.VMEM((hps, kb, tq), F32), pltpu.VMEM((hps, kb, tq), F32),
                        pltpu.VMEM((hps, 8, tq), F32), pltpu.VMEM((hps, 8, tq), F32),
                        pltpu.VMEM((hps, 8, tq), F32), pltpu.VMEM((hps, MLA_VROWS, tq), F32)],
        compiler_params=_params("parallel", "arbitrary"),
        name="mla_attention",
    )(q, k, vt)


def _retention_tile(rq_ref, rk_ref, rv_ref, rg_ref, dm_ref, xi_ref, zt_ref, dc_ref, o_ref, st_ref, *, ts):
    @pl.when(pl.program_id(1) == 0)
    def _():
        st_ref[...] = jnp.zeros_like(st_ref)

    lane = lax.broadcasted_iota(jnp.int32, (RET_CHUNK, LANES), 1)
    row = lax.broadcasted_iota(jnp.int32, (LANES, RET_DV), 0)
    hpt = LANES // RET_DK
    n_chunks = ts // RET_CHUNK
    tiles = [(c, p) for c in range(n_chunks) for p in range(RET_HEADS // hpt)]
    units = [(c, hd) for c in range(n_chunks) for hd in range(RET_HEADS)]
    rows = lambda c: slice(c * RET_CHUNK, (c + 1) * RET_CHUNK)
    vsl = lambda hd: slice(hd * RET_DV, (hd + 1) * RET_DV)
    mine = [(lane >= sub * RET_DK) & (lane < (sub + 1) * RET_DK) for sub in range(hpt)]
    mine_row = [(row >= sub * RET_DK) & (row < (sub + 1) * RET_DK) for sub in range(hpt)]

    q2 = {(c, p): rq_ref[0, rows(c), p * LANES:(p + 1) * LANES] for c, p in tiles}
    k2 = {(c, p): rk_ref[0, rows(c), p * LANES:(p + 1) * LANES] for c, p in tiles}
    vh = {(c, hd): rv_ref[0, rows(c), vsl(hd)] for c, hd in units}
    sc, un = {}, {}
    for c, p in tiles:
        heads = [p * hpt + sub for sub in range(hpt)]
        kcat = jnp.concatenate([jnp.where(mine[sub], k2[(c, p)], 0.0).astype(BF16) for sub in range(hpt)], axis=0)
        s_all = _mm_nt(q2[(c, p)], kcat)
        kz = jnp.zeros((RET_CHUNK, LANES), F32)
        for sub, hd in enumerate(heads):
            sc[(c, hd)] = s_all[:, sub * RET_CHUNK:(sub + 1) * RET_CHUNK]
            kz = kz + jnp.where(mine[sub], k2[(c, p)].astype(F32) * zt_ref[hd], 0.0)
        u_all = _mm(kz.astype(BF16).T, jnp.concatenate([vh[(c, hd)] for hd in heads], axis=1))
        for sub, hd in enumerate(heads):
            un[(c, hd)] = jnp.where(mine_row[sub], u_all[:, sub * RET_DV:(sub + 1) * RET_DV], 0.0)
    prev = {}
    for hd in range(RET_HEADS):
        st = st_ref[hd]
        for c in range(n_chunks):
            prev[(c, hd)] = st.astype(BF16)
            st = st * dc_ref[hd] + un[(c, hd)]
        st_ref[hd] = st
    for c, hd in units:
        u = (c, hd)
        lhs = jnp.concatenate([(sc[u] * dm_ref[hd]).astype(BF16),
                               (q2[(c, hd // hpt)].astype(F32) * xi_ref[hd]).astype(BF16)], axis=1)
        o = _mm(lhs, jnp.concatenate([vh[u], prev[u]], axis=0))
        mu = jnp.mean(o, axis=-1, keepdims=True)
        oc = o - mu
        on = oc * lax.rsqrt(jnp.mean(oc * oc, axis=-1, keepdims=True) + NORM_EPS)
        o_ref[rows(c), vsl(hd)] = (_silu(rg_ref[0, rows(c), vsl(hd)]) * on).astype(BF16)


def _retention_tables():
    c = RET_CHUNK
    gamma = 1.0 - jnp.power(2.0, -5.0 - jnp.arange(RET_HEADS, dtype=F32))
    log_g = jnp.log(gamma)
    idx = jnp.arange(c, dtype=F32)
    diff = idx[:, None] - idx[None, :]
    dmask = jnp.where(diff[None] >= 0, jnp.exp(jnp.maximum(diff, 0.0)[None] * log_g[:, None, None]), 0.0)
    zeta = jnp.exp((c - 1.0 - idx)[None, :] * log_g[:, None])
    xi = jnp.exp((idx + 1.0)[None, :] * log_g[:, None])
    decay = jnp.exp(c * log_g)
    xi_b = jnp.broadcast_to(xi[:, :, None], (RET_HEADS, c, LANES))
    zt_b = jnp.broadcast_to(zeta[:, :, None], (RET_HEADS, c, LANES))
    dc_b = jnp.broadcast_to(decay[:, None, None], (RET_HEADS, LANES, RET_DV))
    return dmask, xi_b, zt_b, dc_b


def _post_kernel(x_ref, om_ref, rq_ref, rk_ref, rv_ref, rg_ref, dm_ref, xi_ref, zt_ref, dc_ref,
                 mod_ref, wo_ref, g2_ref, wr_ref, br_ref,
                 x1_ref, h2_ref, ri_ref, rit_ref, cnt_ref, or_scr, st_ref):
    _retention_tile(rq_ref, rk_ref, rv_ref, rg_ref, dm_ref, xi_ref, zt_ref, dc_ref, or_scr, st_ref,
                    ts=x_ref.shape[1])
    half = om_ref.shape[-1]
    subs = [slice(i * MOE_TILE, (i + 1) * MOE_TILE) for i in range(x_ref.shape[1] // MOE_TILE)]
    mixes = [_mm(om_ref[0, rs, :], wo_ref[0:half, :]) + _mm(or_scr[rs, :], wo_ref[half:, :]) for rs in subs]

    lgs = []
    for rs, mix in zip(subs, mixes):
        x1 = x_ref[0, rs, :] + mod_ref[0, 2:3, :] * mix
        x1_ref[0, rs, :] = x1
        h2 = _rms(x1, g2_ref[...]) * (1.0 + mod_ref[0, 4:5, :]) + mod_ref[0, 3:4, :]
        hi = h2.astype(BF16)
        h2_ref[0, rs, :] = hi
        lo = (h2 - hi.astype(F32)).astype(BF16)
        both = _mm(hi, wr_ref[...])
        lgs.append(both[:, 0:LANES] + both[:, LANES:2 * LANES] + _mm(lo, wr_ref[:, 0:LANES]) + br_ref[...])

    lane = lax.broadcasted_iota(jnp.int32, (MOE_TILE, LANES), 1)
    big = jnp.int32(1 << 20)
    gmask = lane < N_GROUPS
    el = lane - N_GROUPS
    assert EXPERTS_PER_GROUP == 8
    routed = []
    for lg in lgs:
        gmax = jnp.max(jnp.where(gmask, lg, NEG), axis=-1, keepdims=True)
        ge = jnp.where(gmask, jnp.exp(lg - gmax), 0.0)
        pg = ge / jnp.sum(ge, axis=-1, keepdims=True)
        p_top = jnp.max(pg, axis=-1, keepdims=True)
        g_top = jnp.min(jnp.where(gmask & (pg == p_top), lane, big), axis=-1, keepdims=True)

        emask = (el >= 0) & (el < N_EXPERTS) & (lax.shift_right_arithmetic(el, 3) == g_top)
        ev = jnp.where(emask, lg, NEG)
        v1 = jnp.max(ev, axis=-1, keepdims=True)
        i1 = jnp.min(jnp.where(emask & (ev == v1), lane, big), axis=-1, keepdims=True)
        emask2 = emask & (lane != i1)
        ev2 = jnp.where(emask2, lg, NEG)
        v2 = jnp.max(ev2, axis=-1, keepdims=True)
        i2 = jnp.min(jnp.where(emask2 & (ev2 == v2), lane, big), axis=-1, keepdims=True)
        e = jnp.exp(v2 - v1)
        den = 1.0 + e
        routed.append((i1, i2, (1.0 / den) * p_top, (e / den) * p_top))

    r_io = lax.broadcasted_iota(jnp.int32, (MOE_TILE, MOE_TILE), 0)
    c_io = lax.broadcasted_iota(jnp.int32, (MOE_TILE, MOE_TILE), 1)
    earlier_tok = (c_io < r_io).astype(BF16)
    lr_io = lax.broadcasted_iota(jnp.int32, (LANES, LANES), 0)
    lc_io = lax.broadcasted_iota(jnp.int32, (LANES, LANES), 1)
    earlier_lane = (lr_io < lc_io).astype(BF16)
    for hf, (rs, (i1, i2, w1, w2)) in enumerate(zip(subs, routed)):
        oh = [lane == i1, lane == i2]
        cnt = (oh[0] | oh[1]).astype(BF16)
        excl = _mm(earlier_tok, cnt)
        n = jnp.sum(cnt.astype(F32), axis=0, keepdims=True)
        npad = jnp.floor((n + (MOE_CHUNK - 1)) * (1.0 / MOE_CHUNK)) * MOE_CHUNK
        loff = _mm(jnp.broadcast_to(npad, (8, LANES)).astype(BF16), earlier_lane)
        pos = excl + loff[0:1, :]
        cnt_ref[hf] = jnp.broadcast_to(n, (8, LANES))
        lp0, lp1 = [jnp.sum(jnp.where(o, pos, 0.0), axis=-1, keepdims=True) for o in oh]
        cols = [(i1 - N_GROUPS).astype(F32), (i2 - N_GROUPS).astype(F32), w1, w2, lp0, lp1]
        ri = jnp.zeros((MOE_TILE, LANES), F32)
        for j, col in enumerate(cols):
            ri = jnp.where(lane == j, col, ri)
        ri_ref[0, rs, :] = ri
        rit_ref[:, rs] = ri.T


def _post(x, o_mla, rq, rk, rv, rg, mod3, w_o, norm2_g, w_gr, b_gr, w_er, b_er, tm):
    bsz, s, d = x.shape
    ret_tables = _retention_tables()
    c = RET_CHUNK
    w_r = jnp.concatenate([w_gr, w_er.reshape(d, N_EXPERTS), jnp.zeros((d, LANES - N_GROUPS - N_EXPERTS), F32)], axis=1)
    w_rh = w_r.astype(BF16)
    w_rl = (w_r - w_rh.astype(F32)).astype(BF16)
    w_r2 = jnp.concatenate([w_rh, w_rl], axis=1)
    b_r =jnp.concatenate([b_gr, b_er.reshape(-1), jnp.zeros((LANES - N_GROUPS - N_EXPERTS,), F32)]).reshape(1, LANES)
    tile = lambda w: pl.BlockSpec((1, tm, w), lambda b, i: (b, i, 0))
    const = lambda shape: pl.BlockSpec(shape, lambda b, i: (0,) * len(shape))
    per_b = s // tm
    sub = tm // MOE_TILE
    return pl.pallas_call(
        _post_kernel,
        grid=(bsz, per_b),
        in_specs=[tile(d), tile(o_mla.shape[-1]),
                  tile(RET_HEADS * RET_DK), tile(RET_HEADS * RET_DK), tile(RET_HEADS * RET_DV), tile(RET_HEADS * RET_DV),
                  const((RET_HEADS, c, c)), const((RET_HEADS, c, LANES)), const((RET_HEADS, c, LANES)),
                  const((RET_HEADS, LANES, RET_DV)),
                  pl.BlockSpec((1, 6, d), lambda b, i: (b, 0, 0)),
                  const((d, d)), const((1, d)), const((d, 2 * LANES)), const((1, LANES))],
        out_specs=[tile(d), tile(d), tile(LANES),
                   pl.BlockSpec((LANES, tm), lambda b, i: (0, b * per_b + i)),
                   pl.BlockSpec((sub, 8, LANES), lambda b, i: (b * per_b + i, 0, 0))],
        out_shape=[jax.ShapeDtypeStruct((bsz, s, d), F32), jax.ShapeDtypeStruct((bsz, s, d), BF16),
                   jax.ShapeDtypeStruct((bsz, s, LANES), F32),
                   jax.ShapeDtypeStruct((LANES, bsz * s), F32),
                   jax.ShapeDtypeStruct((bsz * s // MOE_TILE, 8, LANES), F32)],
        scratch_shapes=[pltpu.VMEM((tm, RET_HEADS * RET_DV), BF16), pltpu.VMEM((RET_HEADS, LANES, RET_DV), F32)],
        compiler_params=_params("parallel", "arbitrary"),
        name="post_mixer",
    )(x, o_mla, rq, rk, rv, rg, *ret_tables, mod3, w_o.astype(BF16), norm2_g.reshape(1, d), w_r2, b_r)


def _chunk_rows(c):
    return pl.ds(pl.multiple_of(c * MOE_CHUNK, MOE_CHUNK), MOE_CHUNK)


def _dispatch_kernel(dmap_ref, nchk_ref, tstart_ref, tn_ref, nbr_ref, h2_ref, ri_ref, rit_ref, xs_hbm,
                     xloc, zblk, sem, zsem, bsem, *, nsteps, nb):
    i = pl.program_id(0)
    slot = lax.rem(i, 2)
    d = h2_ref.shape[-1]

    def zero_copy(e, c):
        return pltpu.make_async_copy(zblk.at[pl.ds(0, MOE_CHUNK)], xs_hbm.at[_chunk_rows(tstart_ref[e] + c)], zsem)

    def zero_block(j):
        rows = pl.ds(pl.multiple_of(j * MOE_BLOCK, MOE_BLOCK), MOE_BLOCK)
        return pltpu.make_async_copy(zblk, xs_hbm.at[rows], bsem)

    def chunk_copy(t, u, c, sl):
        return pltpu.make_async_copy(xloc.at[sl].at[u].at[_chunk_rows(c)],
                                     xs_hbm.at[_chunk_rows(dmap_ref[t * MOE_TILE_CHUNKS + c])], sem.at[sl])

    def wait_step(step, sl):
        for u in range(MOE_TPS):
            _wait_chunks(nchk_ref[step * MOE_TPS + u],
                         lambda rows: pltpu.make_async_copy(xloc.at[sl].at[0].at[rows], xs_hbm.at[rows], sem.at[sl]))

    @pl.when(i == 0)
    def _():
        zblk[...] = jnp.zeros_like(zblk)
        lax.fori_loop(nbr_ref[0], nb, lambda j, z: (zero_block(j).start(), z)[1], 0)
        for e in range(N_EXPERTS):
            lax.fori_loop(0, tn_ref[e], lambda c, z, e=e: (zero_copy(e, c).start(), z)[1], 0)
        for e in range(N_EXPERTS):
            lax.fori_loop(0, tn_ref[e], lambda c, z, e=e: (zero_copy(e, c).wait(), z)[1], 0)

    @pl.when(i >= 2)
    def _():
        wait_step(i - 2, slot)

    s_io = lax.broadcasted_iota(jnp.int32, (MOE_SLAB, MOE_TILE), 0).astype(F32)
    lane = lax.broadcasted_iota(jnp.int32, (MOE_TILE, LANES), 1)
    tiles = [slice(u * MOE_TILE, (u + 1) * MOE_TILE) for u in range(MOE_TPS)]
    pms = [((s_io == rit_ref[4:5, ts]) | (s_io == rit_ref[5:6, ts])).astype(BF16) for ts in tiles]
    exts = []
    for ts in tiles:
        ext = jnp.zeros((MOE_TILE, LANES), F32)
        for k in range(TOP_K):
            w = ri_ref[ts, TOP_K + k:TOP_K + k + 1]
            hi = w.astype(BF16).astype(F32)
            ext = jnp.where(lane == 2 * k, hi, jnp.where(lane == 2 * k + 1, w - hi, ext))
            ext = jnp.where(lane == 2 * TOP_K + k, ri_ref[ts, k:k + 1], ext)
        exts.append(ext.astype(BF16))
    mains = [_mm(pm, jnp.concatenate([h2_ref[ts, :], ext], axis=1)) for pm, ts, ext in zip(pms, tiles, exts)]
    for u in range(MOE_TPS):
        xloc[slot, u] = mains[u].astype(BF16)

    for u in range(MOE_TPS):
        t = i * MOE_TPS + u
        lax.fori_loop(0, nchk_ref[t], lambda c, z, t=t, u=u: (chunk_copy(t, u, c, slot).start(), z)[1], 0)

    @pl.when(i == nsteps - 1)
    def _():
        wait_step(i, slot)
        if nsteps >= 2:
            wait_step(i - 1, 1 - slot)
        lax.fori_loop(nbr_ref[0], nb, lambda j, z: (zero_block(j).wait(), z)[1], 0)


def _wait_chunks(n, copy_of_rows):
    bit = MOE_TILE_CHUNKS
    while bit >= 1:
        @pl.when((n & bit) != 0)
        def _(bit=bit):
            copy_of_rows(pl.ds(0, bit * MOE_CHUNK)).wait()
        bit //= 2


def _dispatch(h2, ri, rit, tables, n_rows):
    t, d = h2.shape
    nt = t // MOE_TILE
    assert nt % MOE_TPS == 0 and MOE_SLAB >= TOP_K * MOE_TILE + N_EXPERTS * (MOE_CHUNK - 1)
    rows = MOE_TPS * MOE_TILE
    grid_spec = pltpu.PrefetchScalarGridSpec(
        num_scalar_prefetch=5,
        grid=(nt // MOE_TPS,),
        in_specs=[pl.BlockSpec((rows, d), lambda i, *_: (i, 0)),
                  pl.BlockSpec((rows, LANES), lambda i, *_: (i, 0)),
                  pl.BlockSpec((8, rows), lambda i, *_: (0, i))],
        out_specs=pl.BlockSpec(memory_space=pl.ANY),
        scratch_shapes=[pltpu.VMEM((2, MOE_TPS, MOE_SLAB, d + LANES), BF16), pltpu.VMEM((MOE_BLOCK, d + LANES), BF16),
                        pltpu.SemaphoreType.DMA((2,)), pltpu.SemaphoreType.DMA(()), pltpu.SemaphoreType.DMA(())])
    return pl.pallas_call(
        functools.partial(_dispatch_kernel, nsteps=nt // MOE_TPS, nb=n_rows // MOE_BLOCK),
        grid_spec=grid_spec,
        out_shape=jax.ShapeDtypeStruct((n_rows, d + LANES), BF16),
        compiler_params=_params("arbitrary"),
        name="moe_dispatch",
    )(tables["dmap"], tables["tile_chunks"], tables["tail_start"], tables["tail_n"], tables["n_blocks"],
      h2, ri, rit)


def _expert_kernel(be_ref, nbr_ref, xs_ref, w1_ref, w3_ref, w2_ref, y_ref, w1b, w3b, w2b):
    i = pl.program_id(0)
    used = i < nbr_ref[0]

    @pl.when(jnp.logical_not(used))
    def _():
        y_ref[...] = jnp.zeros_like(y_ref)

    @pl.when(used & ((i == 0) | (be_ref[i] != be_ref[jnp.maximum(i - 1, 0)])))
    def _():
        w1b[...] = w1_ref[0].astype(BF16)
        w3b[...] = w3_ref[0].astype(BF16)
        w2b[...] = w2_ref[0].astype(BF16)

    @pl.when(used)
    def _():
        d = y_ref.shape[-1]
        subs = [slice(j * MOE_TILE, (j + 1) * MOE_TILE) for j in range(y_ref.shape[0] // MOE_TILE)]
        ab = [(_mm(xs_ref[rs, 0:d], w1b[...]), _mm(xs_ref[rs, 0:d], w3b[...])) for rs in subs]
        hms = [(_silu(a) * b).astype(BF16) for a, b in ab]
        expert = be_ref[i].astype(F32)
        for rs, hm in zip(subs, hms):
            gw = xs_ref[rs, d:d + LANES].astype(F32)
            gate = jnp.where(gw[:, 2 * TOP_K:2 * TOP_K + 1] == expert, gw[:, 0:1] + gw[:, 1:2], gw[:, 2:3] + gw[:, 3:4])
            y_ref[rs, :] = (_mm(hm, w2b[...]) * gate).astype(BF16)


def _experts(xs, tables, w1, w3, w2):
    n_rows, dw = xs.shape
    d = dw - LANES
    nb = n_rows // MOE_BLOCK
    blk = lambda i, be, nbr: jnp.minimum(i, nbr[0] - 1)
    grid_spec = pltpu.PrefetchScalarGridSpec(
        num_scalar_prefetch=2,
        grid=(nb,),
        in_specs=[pl.BlockSpec((MOE_BLOCK, dw), lambda i, be, nbr: (blk(i, be, nbr), 0)),
                  pl.BlockSpec((1, d, D_EXPERT), lambda i, be, nbr: (be[blk(i, be, nbr)], 0, 0)),
                  pl.BlockSpec((1, d, D_EXPERT), lambda i, be, nbr: (be[blk(i, be, nbr)], 0, 0)),
                  pl.BlockSpec((1, D_EXPERT, d), lambda i, be, nbr: (be[blk(i, be, nbr)], 0, 0))],
        out_specs=pl.BlockSpec((MOE_BLOCK, d), lambda i, be, nbr: (i, 0)),
        scratch_shapes=[pltpu.VMEM((d, D_EXPERT), BF16), pltpu.VMEM((d, D_EXPERT), BF16),
                        pltpu.VMEM((D_EXPERT, d), BF16)])
    return pl.pallas_call(
        _expert_kernel,
        grid_spec=grid_spec,
        out_shape=jax.ShapeDtypeStruct((n_rows, d), BF16),
        compiler_params=_params("arbitrary"),
        name="moe_experts",
    )(tables["blk_expert"], tables["n_blocks"], xs, w1, w3, w2)


def _moe_rows(n_tiles):
    worst = n_tiles * (TOP_K * MOE_TILE + N_EXPERTS * (MOE_CHUNK - 1)) + N_EXPERTS * (MOE_BLOCK - MOE_CHUNK)
    return -(-worst // MOE_BLOCK) * MOE_BLOCK


def _moe_tables(cnt, n_rows):
    n_tiles = cnt.shape[0]
    per_blk = MOE_BLOCK // MOE_CHUNK

    def excl_cumsum(a, axis):
        n = a.shape[axis]
        lower = jnp.arange(n)[:, None] > jnp.arange(n)[None, :]
        if axis == 0:
            return jnp.sum(jnp.where(lower[:, :, None], a[None, :, :], 0), axis=1)
        return jnp.sum(jnp.where(lower[None, :, :], a[:, None, :], 0), axis=2)

    nch = (cnt + MOE_CHUNK - 1) // MOE_CHUNK
    loff = excl_cumsum(nch, 1)
    seg = jnp.sum(nch, axis=0)
    blocks = (seg + per_blk - 1) // per_blk
    bstart = excl_cumsum(blocks[None, :], 1)[0]
    bend = bstart + blocks
    estart = bstart * per_blk
    gbase = estart[None, :] + excl_cumsum(nch, 0)
    c = jnp.arange(MOE_TILE_CHUNKS, dtype=jnp.int32)
    owner = jnp.sum((c[None, :, None] >= (loff + nch)[:, None, :]).astype(jnp.int32), axis=-1)
    owner = jnp.minimum(owner, N_EXPERTS - 1)
    is_owner = owner[:, :, None] == jnp.arange(N_EXPERTS, dtype=jnp.int32)[None, None, :]
    dmap = c[None, :] + jnp.sum(jnp.where(is_owner, (gbase - loff)[:, None, :], 0), axis=-1)
    dmap = jnp.clip(dmap, 0, n_rows // MOE_CHUNK - 1)
    nb = n_rows // MOE_BLOCK
    blk_expert = jnp.sum((jnp.arange(nb, dtype=jnp.int32)[:, None] >= bend[None, :]).astype(jnp.int32), axis=1)
    blk_expert = jnp.minimum(blk_expert, N_EXPERTS - 1)
    return dict(dmap=dmap.reshape(-1).astype(jnp.int32), tile_chunks=jnp.sum(nch, axis=1).astype(jnp.int32),
                tail_start=(estart + seg).astype(jnp.int32), tail_n=(blocks * per_blk - seg).astype(jnp.int32),
                blk_expert=blk_expert.astype(jnp.int32), n_blocks=bend[-1:].astype(jnp.int32))


def _final_kernel(dmap_ref, nchk_ref, x1_ref, ri_ref, mod_ref, g_ref, y_hbm, o_ref, yloc, sem, *, nsteps):
    i = pl.program_id(0)
    slot = lax.rem(i, 2)

    def chunk_copy(t, u, c, sl):
        return pltpu.make_async_copy(y_hbm.at[_chunk_rows(dmap_ref[t * MOE_TILE_CHUNKS + c])],
                                     yloc.at[sl].at[u].at[_chunk_rows(c)], sem.at[sl])

    def gather(step, sl):
        for u in range(MOE_TPS):
            t = step * MOE_TPS + u
            lax.fori_loop(0, nchk_ref[t], lambda c, z, t=t, u=u: (chunk_copy(t, u, c, sl).start(), z)[1], 0)

    @pl.when(i == 0)
    def _():
        yloc[...] = jnp.zeros_like(yloc)
        gather(0, 0)

    @pl.when(i + 1 < nsteps)
    def _():
        gather(i + 1, 1 - slot)

    for u in range(MOE_TPS):
        _wait_chunks(nchk_ref[i * MOE_TPS + u],
                     lambda rows: pltpu.make_async_copy(y_hbm.at[rows], yloc.at[slot].at[0].at[rows], sem.at[slot]))

    l_io = lax.broadcasted_iota(jnp.int32, (MOE_TILE, MOE_SLAB), 1).astype(F32)
    tiles = [slice(u * MOE_TILE, (u + 1) * MOE_TILE) for u in range(MOE_TPS)]
    picks = [((l_io == ri_ref[ts, 4:5]) | (l_io == ri_ref[ts, 5:6])).astype(BF16) for ts in tiles]
    moes = [_mm(pick, yloc[slot, u]) for u, pick in enumerate(picks)]
    for ts, moe in zip(tiles, moes):
        x2 = x1_ref[ts, :] + mod_ref[0, 5:6, :] * moe
        o_ref[ts, :] = _rms(x2, g_ref[...])


def _final(x1, ri, y, mod3, final_g, tables, s):
    t, d = x1.shape
    rows = MOE_TPS * MOE_TILE
    nsteps = t // rows
    per_b = s // rows
    grid_spec = pltpu.PrefetchScalarGridSpec(
        num_scalar_prefetch=2,
        grid=(nsteps,),
        in_specs=[pl.BlockSpec((rows, d), lambda i, *_: (i, 0)),
                  pl.BlockSpec((rows, LANES), lambda i, *_: (i, 0)),
                  pl.BlockSpec((1, 6, d), lambda i, *_: (i // per_b, 0, 0)),
                  pl.BlockSpec((1, d), lambda i, *_: (0, 0)),
                  pl.BlockSpec(memory_space=pl.ANY)],
        out_specs=pl.BlockSpec((rows, d), lambda i, *_: (i, 0)),
        scratch_shapes=[pltpu.VMEM((2, MOE_TPS, MOE_SLAB, d), BF16), pltpu.SemaphoreType.DMA((2,))])
    return pl.pallas_call(
        functools.partial(_final_kernel, nsteps=nsteps),
        grid_spec=grid_spec,
        out_shape=jax.ShapeDtypeStruct((t, d), F32),
        compiler_params=_params("arbitrary"),
        name="moe_combine_final",
    )(tables["dmap"], tables["tile_chunks"], x1, ri, mod3, final_g.reshape(1, d), y)


def kernel(x, c, positions, w_ada, b_ada, norm1_g, w_in, q_norm_g, w_uq, kv_norm_g, w_ukv, w_o, norm2_g,
           w_gr, b_gr, w_er, b_er, w1, w3, w2, final_g):
    bsz, s, d = x.shape
    assert w_ada.shape[0] == 1, "one layer"
    tm = min(1024, s)
    tq = min(256, s)
    ts = min(1024, s)
    mod3 = _adaln(c, w_ada[0], b_ada[0]).reshape(bsz, 6, d)
    pos3 = positions.astype(F32).reshape(bsz, s, 1)
    q, k, vt, rq, rk, rv, rg = _pre(x, mod3, pos3, norm1_g[0], w_in[0], q_norm_g[0], w_uq[0], kv_norm_g[0], w_ukv[0],
                                    tm, tq)
    o_mla = _attention(q, k, vt, tq, ATTN_HEADS_PER_STEP)
    x1, h2, ri, rit, cnt = _post(x, o_mla, rq, rk, rv, rg, mod3, w_o[0], norm2_g[0], w_gr[0], b_gr[0], w_er[0],
                                 b_er[0], tm)
    t = bsz * s
    n_rows = _moe_rows(t // MOE_TILE)
    counts = cnt[:, 0, N_GROUPS:N_GROUPS + N_EXPERTS].astype(jnp.int32)
    tables = _moe_tables(counts, n_rows)
    ri2 = ri.reshape(t, LANES)
    xs = _dispatch(h2.reshape(t, d), ri2, rit, tables, n_rows)
    y = _experts(xs, tables, w1[0], w3[0], w2[0])
    out = _final(x1.reshape(t, d), ri2, y, mod3, final_g, tables, s)
    return out.reshape(bsz, s, d)
```

```python
import functools

import jax
import jax.numpy as jnp
from jax import lax
from jax.experimental import pallas as pl
from jax.experimental.pallas import tpu as pltpu

MLA_HEADS = 8
MLA_NOPE = 64
MLA_ROPE = 32
MLA_V = 64
Q_LORA = 256
KV_LORA = 128
RET_HEADS = 4
RET_DK = 64
RET_DV = 128
RET_CHUNK = 128
ROPE_BASE = 10000.0
NORM_EPS = 1e-6
N_GROUPS = 4
EXPERTS_PER_GROUP = 8
N_EXPERTS = N_GROUPS * EXPERTS_PER_GROUP
TOP_K = 2
D_EXPERT = 256
MOE_TILE = 256
MOE_CHUNK = 16
MOE_SLAB = 1024
MOE_TILE_CHUNKS = MOE_SLAB // MOE_CHUNK
MOE_TPS = 2
MOE_BLOCK = 1024

LANES = 128
VMEM_LIMIT = 56 * 1024 * 1024

F32 = jnp.float32
BF16 = jnp.bfloat16
NEG = float(jnp.finfo(jnp.float32).min)
LOG2_E = 1.4426950408889634
ATTN_HEADS_PER_STEP = 8
ATTN_PAIRS_PER_ITER = 2
MLA_VROWS = MLA_V + 16

_C_Q = 0
_C_KV = _C_Q + Q_LORA
_C_KPE = _C_KV + KV_LORA
_C_RQ = _C_KPE + LANES
_C_RK = _C_RQ + RET_HEADS * RET_DK
_C_RV = _C_RK + RET_HEADS * RET_DK
_C_RG = _C_RV + RET_HEADS * RET_DV
_IN_PERM = _C_RG + RET_HEADS * RET_DV


def _silu(v):
    return v / (1.0 + jnp.exp(-v))


def _mm(a, b):
    return jnp.dot(a, b, preferred_element_type=F32)


def _mm_nt(a, b):
    return lax.dot_general(a, b, (((1,), (1,)), ((), ())), preferred_element_type=F32)


def _params(*sem):
    return pltpu.CompilerParams(dimension_semantics=sem, vmem_limit_bytes=VMEM_LIMIT)


def _adaln_kernel(c_ref, w_ref, b_ref, o_ref):
    a = _silu(c_ref[...]).astype(BF16)
    o_ref[...] = _mm(a, w_ref[...].astype(BF16)) + b_ref[...]


def _adaln(c, w_ada, b_ada):
    bsz, d = c.shape
    n = w_ada.shape[1]
    tn = d
    return pl.pallas_call(
        _adaln_kernel,
        grid=(n // tn,),
        in_specs=[pl.BlockSpec((bsz, d), lambda j: (0, 0)),
                  pl.BlockSpec((d, tn), lambda j: (0, j)),
                  pl.BlockSpec((1, tn), lambda j: (0, j))],
        out_specs=pl.BlockSpec((bsz, tn), lambda j: (0, j)),
        out_shape=jax.ShapeDtypeStruct((bsz, n), F32),
        compiler_params=_params("arbitrary"),
        name="adaln",
    )(c, w_ada, b_ada.reshape(1, n))


def _rms(v, g):
    return v * lax.rsqrt(jnp.mean(v * v, axis=-1, keepdims=True) + NORM_EPS) * g


def _pre_kernel(x_ref, mod_ref, pos_ref, g1_ref, win_ref, qg_ref, wuq_ref, kvg_ref, wuk_ref,
                wuv_ref, invf_ref, q_ref, k_ref, v_ref, rq_ref, rk_ref, rv_ref, rg_ref):
    tq = v_ref.shape[-1]
    subs = [slice(i * tq, (i + 1) * tq) for i in range(v_ref.shape[1])]
    sh1 = mod_ref[0, 0:1, :]
    sc1 = mod_ref[0, 1:2, :]
    projs = [_mm((_rms(x_ref[0, rs, :], g1_ref[...]) * (1.0 + sc1) + sh1).astype(BF16), win_ref[...]) for rs in subs]

    cqs = [_rms(p[:, _C_Q:_C_Q + Q_LORA], qg_ref[...]).astype(BF16) for p in projs]
    ckvs = [_rms(p[:, _C_KV:_C_KV + KV_LORA], kvg_ref[...]).astype(BF16) for p in projs]
    qas = [_mm(cq, wuq_ref[...]) for cq in cqs]
    kns = [_mm(ckv, wuk_ref[...]) for ckv in ckvs]
    vvs = [_mm(ckv, wuv_ref[...]) for ckv in ckvs]

    lane = lax.broadcasted_iota(jnp.int32, (tq, LANES), 1)
    hi = lane >= RET_DK
    half_m, half_r = MLA_ROPE // 2, RET_DK // 2
    first_m = hi & (lane < RET_DK + half_m)
    first_r = (lane & half_r) == 0
    scale = (MLA_NOPE + MLA_ROPE) ** -0.5 * LOG2_E

    def rope(v, cos, sin, first, half):
        partner = jnp.where(first, pltpu.roll(v, LANES - half, 1), pltpu.roll(v, half, 1))
        return v * cos + partner * sin

    for i, rs in enumerate(subs):
        proj = projs[i]
        ang = pos_ref[0, rs, :] * invf_ref[...]
        cs = jnp.cos(ang)
        sn = jnp.sin(ang)
        c_mla = jnp.where(hi, cs, 1.0)
        s_mla = jnp.where(hi, jnp.where(first_m, -sn, sn), 0.0)
        c_ret = jnp.where(hi, pltpu.roll(cs, RET_DK, 1), cs)
        s_ret = jnp.where(hi, pltpu.roll(sn, RET_DK, 1), sn)
        s_ret = jnp.where(first_r, -s_ret, s_ret)

        kpe = rope(proj[:, _C_KPE:_C_KPE + LANES], c_mla, s_mla, first_m, half_m)
        cq_s = c_mla * scale
        sq_s = jnp.where(hi, sn, 0.0) * scale
        for hd in range(MLA_HEADS):
            sl = slice(hd * LANES, (hd + 1) * LANES)
            sr = slice((MLA_HEADS + hd) * LANES, (MLA_HEADS + hd + 1) * LANES)
            q_ref[0, rs, sl] = (qas[i][:, sl] * cq_s + qas[i][:, sr] * sq_s).astype(BF16)
            k_ref[0, rs, sl] = (kns[i][:, sl] + kpe).astype(BF16)
        vt = vvs[i].T
        tail = jnp.where(lax.broadcasted_iota(jnp.int32, (MLA_VROWS - MLA_V, tq), 0) == 0, 1.0, 0.0)
        slab = [piece for hd in range(MLA_HEADS) for piece in (vt[hd * MLA_V:(hd + 1) * MLA_V, :], tail)]
        v_ref[0, i] = jnp.concatenate(slab, axis=0).astype(BF16)

        for j in range(RET_HEADS * RET_DK // LANES):
            o = j * LANES
            rq = rope(proj[:, _C_RQ + o:_C_RQ + o + LANES], c_ret, s_ret, first_r, half_r)
            rk = rope(proj[:, _C_RK + o:_C_RK + o + LANES], c_ret, s_ret, first_r, half_r)
            rq_ref[0, rs, o:o + LANES] = rq.astype(BF16)
            rk_ref[0, rs, o:o + LANES] = (rk * (RET_DK ** -0.5)).astype(BF16)
        rv_ref[0, rs, :] = proj[:, _C_RV:_C_RV + RET_HEADS * RET_DV].astype(BF16)
        rg_ref[0, rs, :] = proj[:, _C_RG:_C_RG + RET_HEADS * RET_DV]


def _pad_heads(w, width, left):
    k = w.shape[0]
    w3 = w.reshape(k, -1, width)
    w3 = jnp.pad(w3, ((0, 0), (0, 0), (left, LANES - left - width)))
    return w3.reshape(k, -1)


def _pre(x, mod3, pos3, norm1_g, w_in, q_norm_g, w_uq, kv_norm_g, w_ukv, tm, tq):
    bsz, s, d = x.shape
    o = 0
    parts = {}
    for name, width in (("cq", Q_LORA), ("ckv", KV_LORA), ("kr", MLA_ROPE), ("rq", RET_HEADS * RET_DK),
                        ("rk", RET_HEADS * RET_DK), ("rv", RET_HEADS * RET_DV), ("rg", RET_HEADS * RET_DV)):
        parts[name] = w_in[:, o:o + width]
        o += width
    w_in_p = jnp.concatenate([
        parts["cq"], parts["ckv"], _pad_heads(parts["kr"], MLA_ROPE, MLA_NOPE),
        parts["rq"], parts["rk"], parts["rv"], parts["rg"]], axis=1).astype(BF16)
    assert w_in_p.shape[1] == _IN_PERM
    wq_rope = w_uq.reshape(Q_LORA, MLA_HEADS, MLA_NOPE + MLA_ROPE)[:, :, MLA_NOPE:].reshape(Q_LORA, MLA_HEADS, 2, -1)
    wq_rot = jnp.stack([-wq_rope[:, :, 1], wq_rope[:, :, 0]], axis=2).reshape(Q_LORA, -1)
    w_uq_p = jnp.concatenate([_pad_heads(w_uq, MLA_NOPE + MLA_ROPE, 0),
                              _pad_heads(wq_rot, MLA_ROPE, MLA_NOPE)], axis=1).astype(BF16)
    wkv3 = w_ukv.reshape(KV_LORA, MLA_HEADS, MLA_NOPE + MLA_V)
    w_uk_p = _pad_heads(wkv3[:, :, :MLA_NOPE].reshape(KV_LORA, -1), MLA_NOPE, 0).astype(BF16)
    w_uv = wkv3[:, :, MLA_NOPE:].reshape(KV_LORA, -1).astype(BF16)
    half_r, half_m = RET_DK // 2, MLA_ROPE // 2
    f_r = ROPE_BASE ** (-(jnp.arange(half_r, dtype=F32) / half_r))
    f_m = ROPE_BASE ** (-(jnp.arange(half_m, dtype=F32) / half_m))
    invf = jnp.concatenate([f_r, f_r, f_m, f_m, jnp.zeros((LANES - 2 * half_r - 2 * half_m,), F32)]).reshape(1, LANES)

    hq = MLA_HEADS * LANES
    const = lambda shape: pl.BlockSpec(shape, lambda b, i: (0,) * len(shape))
    tile = lambda w: pl.BlockSpec((1, tm, w), lambda b, i: (b, i, 0))
    return pl.pallas_call(
        _pre_kernel,
        grid=(bsz, s // tm),
        in_specs=[tile(d), pl.BlockSpec((1, 6, d), lambda b, i: (b, 0, 0)), tile(1), const((1, d)),
                  const((d, _IN_PERM)), const((1, Q_LORA)), const((Q_LORA, 2 * hq)),
                  const((1, KV_LORA)), const((KV_LORA, hq)), const((KV_LORA, MLA_HEADS * MLA_V)), const((1, LANES))],
        out_specs=[tile(hq), tile(hq),
                   pl.BlockSpec((1, tm // tq, MLA_HEADS * MLA_VROWS, tq), lambda b, i: (b, i, 0, 0)),
                   tile(RET_HEADS * RET_DK), tile(RET_HEADS * RET_DK),
                   tile(RET_HEADS * RET_DV), tile(RET_HEADS * RET_DV)],
        out_shape=[jax.ShapeDtypeStruct((bsz, s, hq), BF16), jax.ShapeDtypeStruct((bsz, s, hq), BF16),
                   jax.ShapeDtypeStruct((bsz, s // tq, MLA_HEADS * MLA_VROWS, tq), BF16),
                   jax.ShapeDtypeStruct((bsz, s, RET_HEADS * RET_DK), BF16),
                   jax.ShapeDtypeStruct((bsz, s, RET_HEADS * RET_DK), BF16),
                   jax.ShapeDtypeStruct((bsz, s, RET_HEADS * RET_DV), BF16),
                   jax.ShapeDtypeStruct((bsz, s, RET_HEADS * RET_DV), F32)],
        compiler_params=_params("parallel", "arbitrary"),
        name="pre_mixer",
    )(x, mod3, pos3, norm1_g.reshape(1, d), w_in_p, q_norm_g.reshape(1, -1), w_uq_p,
      kv_norm_g.reshape(1, -1), w_uk_p, w_uv, invf)


def _attn_kernel(q_ref, k_ref, vt_ref, o_ref, st_x, st_y, bm_x, bm_y, m_scr, acc_scr, *, tq, kb, hps):
    qi = pl.program_id(1)
    assert tq == 2 * kb, "a query tile spans two key blocks: the last two blocks of a tile are masked"
    key = lax.broadcasted_iota(jnp.int32, (kb, tq), 0)
    qry = lax.broadcasted_iota(jnp.int32, (kb, tq), 1)
    hsl = [slice(hh * LANES, (hh + 1) * LANES) for hh in range(hps)]
    vsl = [slice(hh * MLA_VROWS, (hh + 1) * MLA_VROWS) for hh in range(hps)]
    qs = [q_ref[0, :, hs] for hs in hsl]
    bufs = {"x": (st_x, bm_x), "y": (st_y, bm_y)}
    n_blk = 2 * qi + 2

    def scores(blk, buf, diag=None):
        st_ref, bm_ref = bufs[buf]
        start = pl.multiple_of(blk * kb, kb)
        for hh in range(hps):
            if diag == 1:
                st = _mm_nt(k_ref[0, pl.ds(start, kb), hsl[hh]], q_ref[0, kb:, hsl[hh]])
                st = jnp.where(lax.broadcasted_iota(jnp.int32, (kb, kb), 0) <= lax.broadcasted_iota(jnp.int32, (kb, kb), 1),
                               st, NEG)
                st = jnp.concatenate([jnp.full((kb, kb), NEG, F32), st], axis=1)
                st_ref[hh] = st
                bm_ref[hh, 0:1, :] = jnp.max(st, axis=0, keepdims=True)
                continue
            st = _mm_nt(k_ref[0, pl.ds(start, kb), hsl[hh]], qs[hh])
            if diag is not None:
                st = jnp.where(key <= qry, st, NEG)
            st_ref[hh] = st
            bm_ref[hh, 0:1, :] = jnp.max(st, axis=0, keepdims=True)

    def update(blk, buf):
        st_ref, bm_ref = bufs[buf]
        for hh in range(hps):
            m = m_scr[hh, 0:1, :]
            m_new = jnp.maximum(m, bm_ref[hh, 0:1, :])
            p = jnp.exp2(st_ref[hh] - m_new).astype(BF16)
            acc_scr[hh] = jnp.exp2(m - m_new) * acc_scr[hh] + _mm(vt_ref[0, blk, vsl[hh], :], p)
            m_scr[hh, 0:1, :] = m_new

    m_scr[...] = jnp.full(m_scr.shape, NEG, F32)
    acc_scr[...] = jnp.zeros(acc_scr.shape, F32)

    @pl.when(qi >= 1)
    def _():
        scores(0, "x")

    def steady(base, pairs):
        for r in range(pairs):
            scores(base + 2 * r + 1, "y")
            update(base + 2 * r, "x")
            scores(base + 2 * r + 2, "x")
            update(base + 2 * r + 1, "y")

    n_pairs = jnp.maximum(qi - 1, 0)
    n_long = n_pairs // ATTN_PAIRS_PER_ITER

    @pl.loop(0, n_long)
    def _(i):
        steady(2 * ATTN_PAIRS_PER_ITER * i, ATTN_PAIRS_PER_ITER)

    @pl.loop(n_long * ATTN_PAIRS_PER_ITER, n_pairs)
    def _(i):
        steady(2 * i, 1)

    @pl.when(qi >= 1)
    def _():
        scores(n_blk - 3, "y")
        update(n_blk - 4, "x")
        scores(n_blk - 2, "x", diag=0)
        update(n_blk - 3, "y")
        scores(n_blk - 1, "y", diag=1)
        update(n_blk - 2, "x")
        update(n_blk - 1, "y")

    @pl.when(qi == 0)
    def _():
        scores(0, "x", diag=0)
        scores(1, "y", diag=1)
        update(0, "x")
        update(1, "y")

    out_t = jnp.concatenate([acc_scr[hh, 0:MLA_V, :] * (1.0 / acc_scr[hh, MLA_V:MLA_V + 1, :]) for hh in range(hps)],
                            axis=0)
    o_ref[0] = out_t.T.astype(BF16)


def _attention(q, k, vt, kb, hps):
    bsz, s, _ = q.shape
    groups = MLA_HEADS // hps
    tq = 2 * kb
    assert vt.shape == (bsz, s // kb, MLA_HEADS * MLA_VROWS, kb)
    return pl.pallas_call(
        functools.partial(_attn_kernel, tq=tq, kb=kb, hps=hps),
        grid=(bsz * groups, s // tq),
        in_specs=[pl.BlockSpec((1, tq, hps * LANES), lambda g, i: (g // groups, i, g % groups)),
                  pl.BlockSpec((1, s, hps * LANES), lambda g, i: (g // groups, 0, g % groups)),
                  pl.BlockSpec((1, s // kb, hps * MLA_VROWS, kb), lambda g, i: (g // groups, 0, g % groups, 0))],
        out_specs=pl.BlockSpec((1, tq, hps * MLA_V), lambda g, i: (g // groups, i, g % groups)),
        out_shape=jax.ShapeDtypeStruct((bsz, s, MLA_HEADS * MLA_V), BF16),
        scratch_shapes=[pltpu.VMEM((hps, kb, tq), F32), pltpu.VMEM((hps, kb, tq), F32),
                        pltpu.VMEM((hps, 8, tq), F32), pltpu.VMEM((hps, 8, tq), F32),
                        pltpu.VMEM((hps, 8, tq), F32), pltpu.VMEM((hps, MLA_VROWS, tq), F32)],
        compiler_params=_params("parallel", "arbitrary"),
        name="mla_attention",
    )(q, k, vt)


def _retention_tile(rq_ref, rk_ref, rv_ref, rg_ref, dm_ref, xi_ref, zt_ref, dc_ref, o_ref, st_ref, *, ts):
    @pl.when(pl.program_id(1) == 0)
    def _():
        st_ref[...] = jnp.zeros_like(st_ref)

    lane = lax.broadcasted_iota(jnp.int32, (RET_CHUNK, LANES), 1)
    row = lax.broadcasted_iota(jnp.int32, (LANES, RET_DV), 0)
    hpt = LANES // RET_DK
    n_chunks = ts // RET_CHUNK
    tiles = [(c, p) for c in range(n_chunks) for p in range(RET_HEADS // hpt)]
    units = [(c, hd) for c in range(n_chunks) for hd in range(RET_HEADS)]
    rows = lambda c: slice(c * RET_CHUNK, (c + 1) * RET_CHUNK)
    vsl = lambda hd: slice(hd * RET_DV, (hd + 1) * RET_DV)
    mine = [(lane >= sub * RET_DK) & (lane < (sub + 1) * RET_DK) for sub in range(hpt)]
    mine_row = [(row >= sub * RET_DK) & (row < (sub + 1) * RET_DK) for sub in range(hpt)]

    q2 = {(c, p): rq_ref[0, rows(c), p * LANES:(p + 1) * LANES] for c, p in tiles}
    k2 = {(c, p): rk_ref[0, rows(c), p * LANES:(p + 1) * LANES] for c, p in tiles}
    vh = {(c, hd): rv_ref[0, rows(c), vsl(hd)] for c, hd in units}
    sc, un = {}, {}
    for c, p in tiles:
        heads = [p * hpt + sub for sub in range(hpt)]
        kcat = jnp.concatenate([jnp.where(mine[sub], k2[(c, p)], 0.0).astype(BF16) for sub in range(hpt)], axis=0)
        s_all = _mm_nt(q2[(c, p)], kcat)
        kz = jnp.zeros((RET_CHUNK, LANES), F32)
        for sub, hd in enumerate(heads):
            sc[(c, hd)] = s_all[:, sub * RET_CHUNK:(sub + 1) * RET_CHUNK]
            kz = kz + jnp.where(mine[sub], k2[(c, p)].astype(F32) * zt_ref[hd], 0.0)
        u_all = _mm(kz.astype(BF16).T, jnp.concatenate([vh[(c, hd)] for hd in heads], axis=1))
        for sub, hd in enumerate(heads):
            un[(c, hd)] = jnp.where(mine_row[sub], u_all[:, sub * RET_DV:(sub + 1) * RET_DV], 0.0)
    prev = {}
    for hd in range(RET_HEADS):
        st = st_ref[hd]
        for c in range(n_chunks):
            prev[(c, hd)] = st.astype(BF16)
            st = st * dc_ref[hd] + un[(c, hd)]
        st_ref[hd] = st
    for c, hd in units:
        u = (c, hd)
        lhs = jnp.concatenate([(sc[u] * dm_ref[hd]).astype(BF16),
                               (q2[(c, hd // hpt)].astype(F32) * xi_ref[hd]).astype(BF16)], axis=1)
        o = _mm(lhs, jnp.concatenate([vh[u], prev[u]], axis=0))
        mu = jnp.mean(o, axis=-1, keepdims=True)
        oc = o - mu
        on = oc * lax.rsqrt(jnp.mean(oc * oc, axis=-1, keepdims=True) + NORM_EPS)
        o_ref[rows(c), vsl(hd)] = (_silu(rg_ref[0, rows(c), vsl(hd)]) * on).astype(BF16)


def _retention_tables():
    c = RET_CHUNK
    gamma = 1.0 - jnp.power(2.0, -5.0 - jnp.arange(RET_HEADS, dtype=F32))
    log_g = jnp.log(gamma)
    idx = jnp.arange(c, dtype=F32)
    diff = idx[:, None] - idx[None, :]
    dmask = jnp.where(diff[None] >= 0, jnp.exp(jnp.maximum(diff, 0.0)[None] * log_g[:, None, None]), 0.0)
    zeta = jnp.exp((c - 1.0 - idx)[None, :] * log_g[:, None])
    xi = jnp.exp((idx + 1.0)[None, :] * log_g[:, None])
    decay = jnp.exp(c * log_g)
    xi_b = jnp.broadcast_to(xi[:, :, None], (RET_HEADS, c, LANES))
    zt_b = jnp.broadcast_to(zeta[:, :, None], (RET_HEADS, c, LANES))
    dc_b = jnp.broadcast_to(decay[:, None, None], (RET_HEADS, LANES, RET_DV))
    return dmask, xi_b, zt_b, dc_b


def _post_kernel(x_ref, om_ref, rq_ref, rk_ref, rv_ref, rg_ref, dm_ref, xi_ref, zt_ref, dc_ref,
                 mod_ref, wo_ref, g2_ref, wr_ref, br_ref,
                 x1_ref, h2_ref, ri_ref, rit_ref, cnt_ref, or_scr, st_ref):
    _retention_tile(rq_ref, rk_ref, rv_ref, rg_ref, dm_ref, xi_ref, zt_ref, dc_ref, or_scr, st_ref,
                    ts=x_ref.shape[1])
    half = om_ref.shape[-1]
    subs = [slice(i * MOE_TILE, (i + 1) * MOE_TILE) for i in range(x_ref.shape[1] // MOE_TILE)]
    mixes = [_mm(om_ref[0, rs, :], wo_ref[0:half, :]) + _mm(or_scr[rs, :], wo_ref[half:, :]) for rs in subs]

    lgs = []
    for rs, mix in zip(subs, mixes):
        x1 = x_ref[0, rs, :] + mod_ref[0, 2:3, :] * mix
        x1_ref[0, rs, :] = x1
        h2 = _rms(x1, g2_ref[...]) * (1.0 + mod_ref[0, 4:5, :]) + mod_ref[0, 3:4, :]
        hi = h2.astype(BF16)
        h2_ref[0, rs, :] = hi
        lo = (h2 - hi.astype(F32)).astype(BF16)
        both = _mm(hi, wr_ref[...])
        lgs.append(both[:, 0:LANES] + both[:, LANES:2 * LANES] + _mm(lo, wr_ref[:, 0:LANES]) + br_ref[...])

    lane = lax.broadcasted_iota(jnp.int32, (MOE_TILE, LANES), 1)
    big = jnp.int32(1 << 20)
    gmask = lane < N_GROUPS
    el = lane - N_GROUPS
    assert EXPERTS_PER_GROUP == 8
    routed = []
    for lg in lgs:
        gmax = jnp.max(jnp.where(gmask, lg, NEG), axis=-1, keepdims=True)
        ge = jnp.where(gmask, jnp.exp(lg - gmax), 0.0)
        pg = ge / jnp.sum(ge, axis=-1, keepdims=True)
        p_top = jnp.max(pg, axis=-1, keepdims=True)
        g_top = jnp.min(jnp.where(gmask & (pg == p_top), lane, big), axis=-1, keepdims=True)

        emask = (el >= 0) & (el < N_EXPERTS) & (lax.shift_right_arithmetic(el, 3) == g_top)
        ev = jnp.where(emask, lg, NEG)
        v1 = jnp.max(ev, axis=-1, keepdims=True)
        i1 = jnp.min(jnp.where(emask & (ev == v1), lane, big), axis=-1, keepdims=True)
        emask2 = emask & (lane != i1)
        ev2 = jnp.where(emask2, lg, NEG)
        v2 = jnp.max(ev2, axis=-1, keepdims=True)
        i2 = jnp.min(jnp.where(emask2 & (ev2 == v2), lane, big), axis=-1, keepdims=True)
        e = jnp.exp(v2 - v1)
        den = 1.0 + e
        routed.append((i1, i2, (1.0 / den) * p_top, (e / den) * p_top))

    r_io = lax.broadcasted_iota(jnp.int32, (MOE_TILE, MOE_TILE), 0)
    c_io = lax.broadcasted_iota(jnp.int32, (MOE_TILE, MOE_TILE), 1)
    earlier_tok = (c_io < r_io).astype(BF16)
    lr_io = lax.broadcasted_iota(jnp.int32, (LANES, LANES), 0)
    lc_io = lax.broadcasted_iota(jnp.int32, (LANES, LANES), 1)
    earlier_lane = (lr_io < lc_io).astype(BF16)
    for hf, (rs, (i1, i2, w1, w2)) in enumerate(zip(subs, routed)):
        oh = [lane == i1, lane == i2]
        cnt = (oh[0] | oh[1]).astype(BF16)
        excl = _mm(earlier_tok, cnt)
        n = jnp.sum(cnt.astype(F32), axis=0, keepdims=True)
        npad = jnp.floor((n + (MOE_CHUNK - 1)) * (1.0 / MOE_CHUNK)) * MOE_CHUNK
        loff = _mm(jnp.broadcast_to(npad, (8, LANES)).astype(BF16), earlier_lane)
        pos = excl + loff[0:1, :]
        cnt_ref[hf] = jnp.broadcast_to(n, (8, LANES))
        lp0, lp1 = [jnp.sum(jnp.where(o, pos, 0.0), axis=-1, keepdims=True) for o in oh]
        cols = [(i1 - N_GROUPS).astype(F32), (i2 - N_GROUPS).astype(F32), w1, w2, lp0, lp1]
        ri = jnp.zeros((MOE_TILE, LANES), F32)
        for j, col in enumerate(cols):
            ri = jnp.where(lane == j, col, ri)
        ri_ref[0, rs, :] = ri
        rit_ref[:, rs] = ri.T


def _post(x, o_mla, rq, rk, rv, rg, mod3, w_o, norm2_g, w_gr, b_gr, w_er, b_er, tm):
    bsz, s, d = x.shape
    ret_tables = _retention_tables()
    c = RET_CHUNK
    w_r = jnp.concatenate([w_gr, w_er.reshape(d, N_EXPERTS), jnp.zeros((d, LANES - N_GROUPS - N_EXPERTS), F32)], axis=1)
    w_rh = w_r.astype(BF16)
    w_rl = (w_r - w_rh.astype(F32)).astype(BF16)
    w_r2 = jnp.concatenate([w_rh, w_rl], axis=1)
    b_r =jnp.concatenate([b_gr, b_er.reshape(-1), jnp.zeros((LANES - N_GROUPS - N_EXPERTS,), F32)]).reshape(1, LANES)
    tile = lambda w: pl.BlockSpec((1, tm, w), lambda b, i: (b, i, 0))
    const = lambda shape: pl.BlockSpec(shape, lambda b, i: (0,) * len(shape))
    per_b = s // tm
    sub = tm // MOE_TILE
    return pl.pallas_call(
        _post_kernel,
        grid=(bsz, per_b),
        in_specs=[tile(d), tile(o_mla.shape[-1]),
                  tile(RET_HEADS * RET_DK), tile(RET_HEADS * RET_DK), tile(RET_HEADS * RET_DV), tile(RET_HEADS * RET_DV),
                  const((RET_HEADS, c, c)), const((RET_HEADS, c, LANES)), const((RET_HEADS, c, LANES)),
                  const((RET_HEADS, LANES, RET_DV)),
                  pl.BlockSpec((1, 6, d), lambda b, i: (b, 0, 0)),
                  const((d, d)), const((1, d)), const((d, 2 * LANES)), const((1, LANES))],
        out_specs=[tile(d), tile(d), tile(LANES),
                   pl.BlockSpec((LANES, tm), lambda b, i: (0, b * per_b + i)),
                   pl.BlockSpec((sub, 8, LANES), lambda b, i: (b * per_b + i, 0, 0))],
        out_shape=[jax.ShapeDtypeStruct((bsz, s, d), F32), jax.ShapeDtypeStruct((bsz, s, d), BF16),
                   jax.ShapeDtypeStruct((bsz, s, LANES), F32),
                   jax.ShapeDtypeStruct((LANES, bsz * s), F32),
                   jax.ShapeDtypeStruct((bsz * s // MOE_TILE, 8, LANES), F32)],
        scratch_shapes=[pltpu.VMEM((tm, RET_HEADS * RET_DV), BF16), pltpu.VMEM((RET_HEADS, LANES, RET_DV), F32)],
        compiler_params=_params("parallel", "arbitrary"),
        name="post_mixer",
    )(x, o_mla, rq, rk, rv, rg, *ret_tables, mod3, w_o.astype(BF16), norm2_g.reshape(1, d), w_r2, b_r)


def _chunk_rows(c):
    return pl.ds(pl.multiple_of(c * MOE_CHUNK, MOE_CHUNK), MOE_CHUNK)


def _dispatch_kernel(dmap_ref, nchk_ref, tstart_ref, tn_ref, nbr_ref, h2_ref, ri_ref, rit_ref, xs_hbm,
                     xloc, zblk, sem, zsem, bsem, *, nsteps, nb):
    i = pl.program_id(0)
    slot = lax.rem(i, 2)
    d = h2_ref.shape[-1]

    def zero_copy(e, c):
        return pltpu.make_async_copy(zblk.at[pl.ds(0, MOE_CHUNK)], xs_hbm.at[_chunk_rows(tstart_ref[e] + c)], zsem)

    def zero_block(j):
        rows = pl.ds(pl.multiple_of(j * MOE_BLOCK, MOE_BLOCK), MOE_BLOCK)
        return pltpu.make_async_copy(zblk, xs_hbm.at[rows], bsem)

    def chunk_copy(t, u, c, sl):
        return pltpu.make_async_copy(xloc.at[sl].at[u].at[_chunk_rows(c)],
                                     xs_hbm.at[_chunk_rows(dmap_ref[t * MOE_TILE_CHUNKS + c])], sem.at[sl])

    def wait_step(step, sl):
        for u in range(MOE_TPS):
            _wait_chunks(nchk_ref[step * MOE_TPS + u],
                         lambda rows: pltpu.make_async_copy(xloc.at[sl].at[0].at[rows], xs_hbm.at[rows], sem.at[sl]))

    @pl.when(i == 0)
    def _():
        zblk[...] = jnp.zeros_like(zblk)
        lax.fori_loop(nbr_ref[0], nb, lambda j, z: (zero_block(j).start(), z)[1], 0)
        for e in range(N_EXPERTS):
            lax.fori_loop(0, tn_ref[e], lambda c, z, e=e: (zero_copy(e, c).start(), z)[1], 0)
        for e in range(N_EXPERTS):
            lax.fori_loop(0, tn_ref[e], lambda c, z, e=e: (zero_copy(e, c).wait(), z)[1], 0)

    @pl.when(i >= 2)
    def _():
        wait_step(i - 2, slot)

    s_io = lax.broadcasted_iota(jnp.int32, (MOE_SLAB, MOE_TILE), 0).astype(F32)
    lane = lax.broadcasted_iota(jnp.int32, (MOE_TILE, LANES), 1)
    tiles = [slice(u * MOE_TILE, (u + 1) * MOE_TILE) for u in range(MOE_TPS)]
    pms = [((s_io == rit_ref[4:5, ts]) | (s_io == rit_ref[5:6, ts])).astype(BF16) for ts in tiles]
    exts = []
    for ts in tiles:
        ext = jnp.zeros((MOE_TILE, LANES), F32)
        for k in range(TOP_K):
            w = ri_ref[ts, TOP_K + k:TOP_K + k + 1]
            hi = w.astype(BF16).astype(F32)
            ext = jnp.where(lane == 2 * k, hi, jnp.where(lane == 2 * k + 1, w - hi, ext))
            ext = jnp.where(lane == 2 * TOP_K + k, ri_ref[ts, k:k + 1], ext)
        exts.append(ext.astype(BF16))
    mains = [_mm(pm, jnp.concatenate([h2_ref[ts, :], ext], axis=1)) for pm, ts, ext in zip(pms, tiles, exts)]
    for u in range(MOE_TPS):
        xloc[slot, u] = mains[u].astype(BF16)

    for u in range(MOE_TPS):
        t = i * MOE_TPS + u
        lax.fori_loop(0, nchk_ref[t], lambda c, z, t=t, u=u: (chunk_copy(t, u, c, slot).start(), z)[1], 0)

    @pl.when(i == nsteps - 1)
    def _():
        wait_step(i, slot)
        if nsteps >= 2:
            wait_step(i - 1, 1 - slot)
        lax.fori_loop(nbr_ref[0], nb, lambda j, z: (zero_block(j).wait(), z)[1], 0)


def _wait_chunks(n, copy_of_rows):
    bit = MOE_TILE_CHUNKS
    while bit >= 1:
        @pl.when((n & bit) != 0)
        def _(bit=bit):
            copy_of_rows(pl.ds(0, bit * MOE_CHUNK)).wait()
        bit //= 2


def _dispatch(h2, ri, rit, tables, n_rows):
    t, d = h2.shape
    nt = t // MOE_TILE
    assert nt % MOE_TPS == 0 and MOE_SLAB >= TOP_K * MOE_TILE + N_EXPERTS * (MOE_CHUNK - 1)
    rows = MOE_TPS * MOE_TILE
    grid_spec = pltpu.PrefetchScalarGridSpec(
        num_scalar_prefetch=5,
        grid=(nt // MOE_TPS,),
        in_specs=[pl.BlockSpec((rows, d), lambda i, *_: (i, 0)),
                  pl.BlockSpec((rows, LANES), lambda i, *_: (i, 0)),
                  pl.BlockSpec((8, rows), lambda i, *_: (0, i))],
        out_specs=pl.BlockSpec(memory_space=pl.ANY),
        scratch_shapes=[pltpu.VMEM((2, MOE_TPS, MOE_SLAB, d + LANES), BF16), pltpu.VMEM((MOE_BLOCK, d + LANES), BF16),
                        pltpu.SemaphoreType.DMA((2,)), pltpu.SemaphoreType.DMA(()), pltpu.SemaphoreType.DMA(())])
    return pl.pallas_call(
        functools.partial(_dispatch_kernel, nsteps=nt // MOE_TPS, nb=n_rows // MOE_BLOCK),
        grid_spec=grid_spec,
        out_shape=jax.ShapeDtypeStruct((n_rows, d + LANES), BF16),
        compiler_params=_params("arbitrary"),
        name="moe_dispatch",
    )(tables["dmap"], tables["tile_chunks"], tables["tail_start"], tables["tail_n"], tables["n_blocks"],
      h2, ri, rit)


def _expert_kernel(be_ref, nbr_ref, xs_ref, w1_ref, w3_ref, w2_ref, y_ref, w1b, w3b, w2b):
    i = pl.program_id(0)
    used = i < nbr_ref[0]

    @pl.when(jnp.logical_not(used))
    def _():
        y_ref[...] = jnp.zeros_like(y_ref)

    @pl.when(used & ((i == 0) | (be_ref[i] != be_ref[jnp.maximum(i - 1, 0)])))
    def _():
        w1b[...] = w1_ref[0].astype(BF16)
        w3b[...] = w3_ref[0].astype(BF16)
        w2b[...] = w2_ref[0].astype(BF16)

    @pl.when(used)
    def _():
        d = y_ref.shape[-1]
        subs = [slice(j * MOE_TILE, (j + 1) * MOE_TILE) for j in range(y_ref.shape[0] // MOE_TILE)]
        ab = [(_mm(xs_ref[rs, 0:d], w1b[...]), _mm(xs_ref[rs, 0:d], w3b[...])) for rs in subs]
        hms = [(_silu(a) * b).astype(BF16) for a, b in ab]
        expert = be_ref[i].astype(F32)
        for rs, hm in zip(subs, hms):
            gw = xs_ref[rs, d:d + LANES].astype(F32)
            gate = jnp.where(gw[:, 2 * TOP_K:2 * TOP_K + 1] == expert, gw[:, 0:1] + gw[:, 1:2], gw[:, 2:3] + gw[:, 3:4])
            y_ref[rs, :] = (_mm(hm, w2b[...]) * gate).astype(BF16)


def _experts(xs, tables, w1, w3, w2):
    n_rows, dw = xs.shape
    d = dw - LANES
    nb = n_rows // MOE_BLOCK
    blk = lambda i, be, nbr: jnp.minimum(i, nbr[0] - 1)
    grid_spec = pltpu.PrefetchScalarGridSpec(
        num_scalar_prefetch=2,
        grid=(nb,),
        in_specs=[pl.BlockSpec((MOE_BLOCK, dw), lambda i, be, nbr: (blk(i, be, nbr), 0)),
                  pl.BlockSpec((1, d, D_EXPERT), lambda i, be, nbr: (be[blk(i, be, nbr)], 0, 0)),
                  pl.BlockSpec((1, d, D_EXPERT), lambda i, be, nbr: (be[blk(i, be, nbr)], 0, 0)),
                  pl.BlockSpec((1, D_EXPERT, d), lambda i, be, nbr: (be[blk(i, be, nbr)], 0, 0))],
        out_specs=pl.BlockSpec((MOE_BLOCK, d), lambda i, be, nbr: (i, 0)),
        scratch_shapes=[pltpu.VMEM((d, D_EXPERT), BF16), pltpu.VMEM((d, D_EXPERT), BF16),
                        pltpu.VMEM((D_EXPERT, d), BF16)])
    return pl.pallas_call(
        _expert_kernel,
        grid_spec=grid_spec,
        out_shape=jax.ShapeDtypeStruct((n_rows, d), BF16),
        compiler_params=_params("arbitrary"),
        name="moe_experts",
    )(tables["blk_expert"], tables["n_blocks"], xs, w1, w3, w2)


def _moe_rows(n_tiles):
    worst = n_tiles * (TOP_K * MOE_TILE + N_EXPERTS * (MOE_CHUNK - 1)) + N_EXPERTS * (MOE_BLOCK - MOE_CHUNK)
    return -(-worst // MOE_BLOCK) * MOE_BLOCK


def _moe_tables(cnt, n_rows):
    n_tiles = cnt.shape[0]
    per_blk = MOE_BLOCK // MOE_CHUNK

    def excl_cumsum(a, axis):
        n = a.shape[axis]
        lower = jnp.arange(n)[:, None] > jnp.arange(n)[None, :]
        if axis == 0:
            return jnp.sum(jnp.where(lower[:, :, None], a[None, :, :], 0), axis=1)
        return jnp.sum(jnp.where(lower[None, :, :], a[:, None, :], 0), axis=2)

    nch = (cnt + MOE_CHUNK - 1) // MOE_CHUNK
    loff = excl_cumsum(nch, 1)
    seg = jnp.sum(nch, axis=0)
    blocks = (seg + per_blk - 1) // per_blk
    bstart = excl_cumsum(blocks[None, :], 1)[0]
    bend = bstart + blocks
    estart = bstart * per_blk
    gbase = estart[None, :] + excl_cumsum(nch, 0)
    c = jnp.arange(MOE_TILE_CHUNKS, dtype=jnp.int32)
    owner = jnp.sum((c[None, :, None] >= (loff + nch)[:, None, :]).astype(jnp.int32), axis=-1)
    owner = jnp.minimum(owner, N_EXPERTS - 1)
    is_owner = owner[:, :, None] == jnp.arange(N_EXPERTS, dtype=jnp.int32)[None, None, :]
    dmap = c[None, :] + jnp.sum(jnp.where(is_owner, (gbase - loff)[:, None, :], 0), axis=-1)
    dmap = jnp.clip(dmap, 0, n_rows // MOE_CHUNK - 1)
    nb = n_rows // MOE_BLOCK
    blk_expert = jnp.sum((jnp.arange(nb, dtype=jnp.int32)[:, None] >= bend[None, :]).astype(jnp.int32), axis=1)
    blk_expert = jnp.minimum(blk_expert, N_EXPERTS - 1)
    return dict(dmap=dmap.reshape(-1).astype(jnp.int32), tile_chunks=jnp.sum(nch, axis=1).astype(jnp.int32),
                tail_start=(estart + seg).astype(jnp.int32), tail_n=(blocks * per_blk - seg).astype(jnp.int32),
                blk_expert=blk_expert.astype(jnp.int32), n_blocks=bend[-1:].astype(jnp.int32))


def _final_kernel(dmap_ref, nchk_ref, x1_ref, ri_ref, mod_ref, g_ref, y_hbm, o_ref, yloc, sem, *, nsteps):
    i = pl.program_id(0)
    slot = lax.rem(i, 2)

    def chunk_copy(t, u, c, sl):
        return pltpu.make_async_copy(y_hbm.at[_chunk_rows(dmap_ref[t * MOE_TILE_CHUNKS + c])],
                                     yloc.at[sl].at[u].at[_chunk_rows(c)], sem.at[sl])

    def gather(step, sl):
        for u in range(MOE_TPS):
            t = step * MOE_TPS + u
            lax.fori_loop(0, nchk_ref[t], lambda c, z, t=t, u=u: (chunk_copy(t, u, c, sl).start(), z)[1], 0)

    @pl.when(i == 0)
    def _():
        yloc[...] = jnp.zeros_like(yloc)
        gather(0, 0)

    @pl.when(i + 1 < nsteps)
    def _():
        gather(i + 1, 1 - slot)

    for u in range(MOE_TPS):
        _wait_chunks(nchk_ref[i * MOE_TPS + u],
                     lambda rows: pltpu.make_async_copy(y_hbm.at[rows], yloc.at[slot].at[0].at[rows], sem.at[slot]))

    l_io = lax.broadcasted_iota(jnp.int32, (MOE_TILE, MOE_SLAB), 1).astype(F32)
    tiles = [slice(u * MOE_TILE, (u + 1) * MOE_TILE) for u in range(MOE_TPS)]
    picks = [((l_io == ri_ref[ts, 4:5]) | (l_io == ri_ref[ts, 5:6])).astype(BF16) for ts in tiles]
    moes = [_mm(pick, yloc[slot, u]) for u, pick in enumerate(picks)]
    for ts, moe in zip(tiles, moes):
        x2 = x1_ref[ts, :] + mod_ref[0, 5:6, :] * moe
        o_ref[ts, :] = _rms(x2, g_ref[...])


def _final(x1, ri, y, mod3, final_g, tables, s):
    t, d = x1.shape
    rows = MOE_TPS * MOE_TILE
    nsteps = t // rows
    per_b = s // rows
    grid_spec = pltpu.PrefetchScalarGridSpec(
        num_scalar_prefetch=2,
        grid=(nsteps,),
        in_specs=[pl.BlockSpec((rows, d), lambda i, *_: (i, 0)),
                  pl.BlockSpec((rows, LANES), lambda i, *_: (i, 0)),
                  pl.BlockSpec((1, 6, d), lambda i, *_: (i // per_b, 0, 0)),
                  pl.BlockSpec((1, d), lambda i, *_: (0, 0)),
                  pl.BlockSpec(memory_space=pl.ANY)],
        out_specs=pl.BlockSpec((rows, d), lambda i, *_: (i, 0)),
        scratch_shapes=[pltpu.VMEM((2, MOE_TPS, MOE_SLAB, d), BF16), pltpu.SemaphoreType.DMA((2,))])
    return pl.pallas_call(
        functools.partial(_final_kernel, nsteps=nsteps),
        grid_spec=grid_spec,
        out_shape=jax.ShapeDtypeStruct((t, d), F32),
        compiler_params=_params("arbitrary"),
        name="moe_combine_final",
    )(tables["dmap"], tables["tile_chunks"], x1, ri, mod3, final_g.reshape(1, d), y)


def kernel(x, c, positions, w_ada, b_ada, norm1_g, w_in, q_norm_g, w_uq, kv_norm_g, w_ukv, w_o, norm2_g,
           w_gr, b_gr, w_er, b_er, w1, w3, w2, final_g):
    bsz, s, d = x.shape
    assert w_ada.shape[0] == 1, "one layer"
    tm = min(1024, s)
    tq = min(256, s)
    ts = min(1024, s)
    mod3 = _adaln(c, w_ada[0], b_ada[0]).reshape(bsz, 6, d)
    pos3 = positions.astype(F32).reshape(bsz, s, 1)
    q, k, vt, rq, rk, rv, rg = _pre(x, mod3, pos3, norm1_g[0], w_in[0], q_norm_g[0], w_uq[0], kv_norm_g[0], w_ukv[0],
                                    tm, tq)
    o_mla = _attention(q, k, vt, tq, ATTN_HEADS_PER_STEP)
    x1, h2, ri, rit, cnt = _post(x, o_mla, rq, rk, rv, rg, mod3, w_o[0], norm2_g[0], w_gr[0], b_gr[0], w_er[0],
                                 b_er[0], tm)
    t = bsz * s
    n_rows = _moe_rows(t // MOE_TILE)
    counts = cnt[:, 0, N_GROUPS:N_GROUPS + N_EXPERTS].astype(jnp.int32)
    tables = _moe_tables(counts, n_rows)
    ri2 = ri.reshape(t, LANES)
    xs = _dispatch(h2.reshape(t, d), ri2, rit, tables, n_rows)
    y = _experts(xs, tables, w1[0], w3[0], w2[0])
    out = _final(x1.reshape(t, d), ri2, y, mod3, final_g, tables, s)
    return out.reshape(bsz, s, d)
```

```python
import functools

import jax
import jax.numpy as jnp
from jax import lax
from jax.experimental import pallas as pl
from jax.experimental.pallas import tpu as pltpu

MLA_HEADS = 8
MLA_NOPE = 64
MLA_ROPE = 32
MLA_V = 64
Q_LORA = 256
KV_LORA = 128
RET_HEADS = 4
RET_DK = 64
RET_DV = 128
RET_CHUNK = 128
ROPE_BASE = 10000.0
NORM_EPS = 1e-6
N_GROUPS = 4
EXPERTS_PER_GROUP = 8
N_EXPERTS = N_GROUPS * EXPERTS_PER_GROUP
TOP_K = 2
D_EXPERT = 256
MOE_TILE = 256
MOE_CHUNK = 16
MOE_SLAB = 1024
MOE_TILE_CHUNKS = MOE_SLAB // MOE_CHUNK
MOE_TPS = 2
MOE_BLOCK = 1024

LANES = 128
VMEM_LIMIT = 56 * 1024 * 1024

F32 = jnp.float32
BF16 = jnp.bfloat16
NEG = float(jnp.finfo(jnp.float32).min)
LOG2_E = 1.4426950408889634
ATTN_HEADS_PER_STEP = 8
ATTN_PAIRS_PER_ITER = 2
MLA_VROWS = MLA_V + 16

_C_Q = 0
_C_KV = _C_Q + Q_LORA
_C_KPE = _C_KV + KV_LORA
_C_RQ = _C_KPE + LANES
_C_RK = _C_RQ + RET_HEADS * RET_DK
_C_RV = _C_RK + RET_HEADS * RET_DK
_C_RG = _C_RV + RET_HEADS * RET_DV
_IN_PERM = _C_RG + RET_HEADS * RET_DV


def _silu(v):
    return v / (1.0 + jnp.exp(-v))


def _mm(a, b):
    return jnp.dot(a, b, preferred_element_type=F32)


def _mm_nt(a, b):
    return lax.dot_general(a, b, (((1,), (1,)), ((), ())), preferred_element_type=F32)


def _params(*sem):
    return pltpu.CompilerParams(dimension_semantics=sem, vmem_limit_bytes=VMEM_LIMIT)


def _adaln_kernel(c_ref, w_ref, b_ref, o_ref):
    a = _silu(c_ref[...]).astype(BF16)
    o_ref[...] = _mm(a, w_ref[...].astype(BF16)) + b_ref[...]


def _adaln(c, w_ada, b_ada):
    bsz, d = c.shape
    n = w_ada.shape[1]
    tn = d
    return pl.pallas_call(
        _adaln_kernel,
        grid=(n // tn,),
        in_specs=[pl.BlockSpec((bsz, d), lambda j: (0, 0)),
                  pl.BlockSpec((d, tn), lambda j: (0, j)),
                  pl.BlockSpec((1, tn), lambda j: (0, j))],
        out_specs=pl.BlockSpec((bsz, tn), lambda j: (0, j)),
        out_shape=jax.ShapeDtypeStruct((bsz, n), F32),
        compiler_params=_params("arbitrary"),
        name="adaln",
    )(c, w_ada, b_ada.reshape(1, n))


def _rms(v, g):
    return v * lax.rsqrt(jnp.mean(v * v, axis=-1, keepdims=True) + NORM_EPS) * g


def _pre_kernel(x_ref, mod_ref, pos_ref, g1_ref, win_ref, qg_ref, wuq_ref, kvg_ref, wuk_ref,
                wuv_ref, invf_ref, q_ref, k_ref, v_ref, rq_ref, rk_ref, rv_ref, rg_ref):
    tq = v_ref.shape[-1]
    subs = [slice(i * tq, (i + 1) * tq) for i in range(v_ref.shape[1])]
    sh1 = mod_ref[0, 0:1, :]
    sc1 = mod_ref[0, 1:2, :]
    projs = [_mm((_rms(x_ref[0, rs, :], g1_ref[...]) * (1.0 + sc1) + sh1).astype(BF16), win_ref[...]) for rs in subs]

    cqs = [_rms(p[:, _C_Q:_C_Q + Q_LORA], qg_ref[...]).astype(BF16) for p in projs]
    ckvs = [_rms(p[:, _C_KV:_C_KV + KV_LORA], kvg_ref[...]).astype(BF16) for p in projs]
    qas = [_mm(cq, wuq_ref[...]) for cq in cqs]
    kns = [_mm(ckv, wuk_ref[...]) for ckv in ckvs]
    vvs = [_mm(ckv, wuv_ref[...]) for ckv in ckvs]

    lane = lax.broadcasted_iota(jnp.int32, (tq, LANES), 1)
    hi = lane >= RET_DK
    half_m, half_r = MLA_ROPE // 2, RET_DK // 2
    first_m = hi & (lane < RET_DK + half_m)
    first_r = (lane & half_r) == 0
    scale = (MLA_NOPE + MLA_ROPE) ** -0.5 * LOG2_E

    def rope(v, cos, sin, first, half):
        partner = jnp.where(first, pltpu.roll(v, LANES - half, 1), pltpu.roll(v, half, 1))
        return v * cos + partner * sin

    for i, rs in enumerate(subs):
        proj = projs[i]
        ang = pos_ref[0, rs, :] * invf_ref[...]
        cs = jnp.cos(ang)
        sn = jnp.sin(ang)
        c_mla = jnp.where(hi, cs, 1.0)
        s_mla = jnp.where(hi, jnp.where(first_m, -sn, sn), 0.0)
        c_ret = jnp.where(hi, pltpu.roll(cs, RET_DK, 1), cs)
        s_ret = jnp.where(hi, pltpu.roll(sn, RET_DK, 1), sn)
        s_ret = jnp.where(first_r, -s_ret, s_ret)

        kpe = rope(proj[:, _C_KPE:_C_KPE + LANES], c_mla, s_mla, first_m, half_m)
        cq_s = c_mla * scale
        sq_s = jnp.where(hi, sn, 0.0) * scale
        for hd in range(MLA_HEADS):
            sl = slice(hd * LANES, (hd + 1) * LANES)
            sr = slice((MLA_HEADS + hd) * LANES, (MLA_HEADS + hd + 1) * LANES)
            q_ref[0, rs, sl] = (qas[i][:, sl] * cq_s + qas[i][:, sr] * sq_s).astype(BF16)
            k_ref[0, rs, sl] = (kns[i][:, sl] + kpe).astype(BF16)
        vt = vvs[i].T
        tail = jnp.where(lax.broadcasted_iota(jnp.int32, (MLA_VROWS - MLA_V, tq), 0) == 0, 1.0, 0.0)
        slab = [piece for hd in range(MLA_HEADS) for piece in (vt[hd * MLA_V:(hd + 1) * MLA_V, :], tail)]
        v_ref[0, i] = jnp.concatenate(slab, axis=0).astype(BF16)

        for j in range(RET_HEADS * RET_DK // LANES):
            o = j * LANES
            rq = rope(proj[:, _C_RQ + o:_C_RQ + o + LANES], c_ret, s_ret, first_r, half_r)
            rk = rope(proj[:, _C_RK + o:_C_RK + o + LANES], c_ret, s_ret, first_r, half_r)
            rq_ref[0, rs, o:o + LANES] = rq.astype(BF16)
            rk_ref[0, rs, o:o + LANES] = (rk * (RET_DK ** -0.5)).astype(BF16)
        rv_ref[0, rs, :] = proj[:, _C_RV:_C_RV + RET_HEADS * RET_DV].astype(BF16)
        rg_ref[0, rs, :] = proj[:, _C_RG:_C_RG + RET_HEADS * RET_DV]


def _pad_heads(w, width, left):
    k = w.shape[0]
    w3 = w.reshape(k, -1, width)
    w3 = jnp.pad(w3, ((0, 0), (0, 0), (left, LANES - left - width)))
    return w3.reshape(k, -1)


def _pre(x, mod3, pos3, norm1_g, w_in, q_norm_g, w_uq, kv_norm_g, w_ukv, tm, tq):
    bsz, s, d = x.shape
    o = 0
    parts = {}
    for name, width in (("cq", Q_LORA), ("ckv", KV_LORA), ("kr", MLA_ROPE), ("rq", RET_HEADS * RET_DK),
                        ("rk", RET_HEADS * RET_DK), ("rv", RET_HEADS * RET_DV), ("rg", RET_HEADS * RET_DV)):
        parts[name] = w_in[:, o:o + width]
        o += width
    w_in_p = jnp.concatenate([
        parts["cq"], parts["ckv"], _pad_heads(parts["kr"], MLA_ROPE, MLA_NOPE),
        parts["rq"], parts["rk"], parts["rv"], parts["rg"]], axis=1).astype(BF16)
    assert w_in_p.shape[1] == _IN_PERM
    wq_rope = w_uq.reshape(Q_LORA, MLA_HEADS, MLA_NOPE + MLA_ROPE)[:, :, MLA_NOPE:].reshape(Q_LORA, MLA_HEADS, 2, -1)
    wq_rot = jnp.stack([-wq_rope[:, :, 1], wq_rope[:, :, 0]], axis=2).reshape(Q_LORA, -1)
    w_uq_p = jnp.concatenate([_pad_heads(w_uq, MLA_NOPE + MLA_ROPE, 0),
                              _pad_heads(wq_rot, MLA_ROPE, MLA_NOPE)], axis=1).astype(BF16)
    wkv3 = w_ukv.reshape(KV_LORA, MLA_HEADS, MLA_NOPE + MLA_V)
    w_uk_p = _pad_heads(wkv3[:, :, :MLA_NOPE].reshape(KV_LORA, -1), MLA_NOPE, 0).astype(BF16)
    w_uv = wkv3[:, :, MLA_NOPE:].reshape(KV_LORA, -1).astype(BF16)
    half_r, half_m = RET_DK // 2, MLA_ROPE // 2
    f_r = ROPE_BASE ** (-(jnp.arange(half_r, dtype=F32) / half_r))
    f_m = ROPE_BASE ** (-(jnp.arange(half_m, dtype=F32) / half_m))
    invf = jnp.concatenate([f_r, f_r, f_m, f_m, jnp.zeros((LANES - 2 * half_r - 2 * half_m,), F32)]).reshape(1, LANES)

    hq = MLA_HEADS * LANES
    const = lambda shape: pl.BlockSpec(shape, lambda b, i: (0,) * len(shape))
    tile = lambda w: pl.BlockSpec((1, tm, w), lambda b, i: (b, i, 0))
    return pl.pallas_call(
        _pre_kernel,
        grid=(bsz, s // tm),
        in_specs=[tile(d), pl.BlockSpec((1, 6, d), lambda b, i: (b, 0, 0)), tile(1), const((1, d)),
                  const((d, _IN_PERM)), const((1, Q_LORA)), const((Q_LORA, 2 * hq)),
                  const((1, KV_LORA)), const((KV_LORA, hq)), const((KV_LORA, MLA_HEADS * MLA_V)), const((1, LANES))],
        out_specs=[tile(hq), tile(hq),
                   pl.BlockSpec((1, tm // tq, MLA_HEADS * MLA_VROWS, tq), lambda b, i: (b, i, 0, 0)),
                   tile(RET_HEADS * RET_DK), tile(RET_HEADS * RET_DK),
                   tile(RET_HEADS * RET_DV), tile(RET_HEADS * RET_DV)],
        out_shape=[jax.ShapeDtypeStruct((bsz, s, hq), BF16), jax.ShapeDtypeStruct((bsz, s, hq), BF16),
                   jax.ShapeDtypeStruct((bsz, s // tq, MLA_HEADS * MLA_VROWS, tq), BF16),
                   jax.ShapeDtypeStruct((bsz, s, RET_HEADS * RET_DK), BF16),
                   jax.ShapeDtypeStruct((bsz, s, RET_HEADS * RET_DK), BF16),
                   jax.ShapeDtypeStruct((bsz, s, RET_HEADS * RET_DV), BF16),
                   jax.ShapeDtypeStruct((bsz, s, RET_HEADS * RET_DV), F32)],
        compiler_params=_params("parallel", "arbitrary"),
        name="pre_mixer",
    )(x, mod3, pos3, norm1_g.reshape(1, d), w_in_p, q_norm_g.reshape(1, -1), w_uq_p,
      kv_norm_g.reshape(1, -1), w_uk_p, w_uv, invf)


def _attn_kernel(q_ref, k_ref, vt_ref, o_ref, st_x, st_y, bm_x, bm_y, m_scr, acc_scr, *, tq, kb, hps):
    qi = pl.program_id(1)
    assert tq == 2 * kb, "a query tile spans two key blocks: the last two blocks of a tile are masked"
    key = lax.broadcasted_iota(jnp.int32, (kb, tq), 0)
    qry = lax.broadcasted_iota(jnp.int32, (kb, tq), 1)
    hsl = [slice(hh * LANES, (hh + 1) * LANES) for hh in range(hps)]
    vsl = [slice(hh * MLA_VROWS, (hh + 1) * MLA_VROWS) for hh in range(hps)]
    qs = [q_ref[0, :, hs] for hs in hsl]
    bufs = {"x": (st_x, bm_x), "y": (st_y, bm_y)}
    n_blk = 2 * qi + 2

    def scores(blk, buf, diag=None):
        st_ref, bm_ref = bufs[buf]
        start = pl.multiple_of(blk * kb, kb)
        for hh in range(hps):
            if diag == 1:
                st = _mm_nt(k_ref[0, pl.ds(start, kb), hsl[hh]], q_ref[0, kb:, hsl[hh]])
                st = jnp.where(lax.broadcasted_iota(jnp.int32, (kb, kb), 0) <= lax.broadcasted_iota(jnp.int32, (kb, kb), 1),
                               st, NEG)
                st = jnp.concatenate([jnp.full((kb, kb), NEG, F32), st], axis=1)
                st_ref[hh] = st
                bm_ref[hh, 0:1, :] = jnp.max(st, axis=0, keepdims=True)
                continue
            st = _mm_nt(k_ref[0, pl.ds(start, kb), hsl[hh]], qs[hh])
            if diag is not None:
                st = jnp.where(key <= qry, st, NEG)
            st_ref[hh] = st
            bm_ref[hh, 0:1, :] = jnp.max(st, axis=0, keepdims=True)

    def update(blk, buf):
        st_ref, bm_ref = bufs[buf]
        for hh in range(hps):
            m = m_scr[hh, 0:1, :]
            m_new = jnp.maximum(m, bm_ref[hh, 0:1, :])
            p = jnp.exp2(st_ref[hh] - m_new).astype(BF16)
            acc_scr[hh] = jnp.exp2(m - m_new) * acc_scr[hh] + _mm(vt_ref[0, blk, vsl[hh], :], p)
            m_scr[hh, 0:1, :] = m_new

    m_scr[...] = jnp.full(m_scr.shape, NEG, F32)
    acc_scr[...] = jnp.zeros(acc_scr.shape, F32)

    @pl.when(qi >= 1)
    def _():
        scores(0, "x")

    def steady(base, pairs):
        for r in range(pairs):
            scores(base + 2 * r + 1, "y")
            update(base + 2 * r, "x")
            scores(base + 2 * r + 2, "x")
            update(base + 2 * r + 1, "y")

    n_pairs = jnp.maximum(qi - 1, 0)
    n_long = n_pairs // ATTN_PAIRS_PER_ITER

    @pl.loop(0, n_long)
    def _(i):
        steady(2 * ATTN_PAIRS_PER_ITER * i, ATTN_PAIRS_PER_ITER)

    @pl.loop(n_long * ATTN_PAIRS_PER_ITER, n_pairs)
    def _(i):
        steady(2 * i, 1)

    @pl.when(qi >= 1)
    def _():
        scores(n_blk - 3, "y")
        update(n_blk - 4, "x")
        scores(n_blk - 2, "x", diag=0)
        update(n_blk - 3, "y")
        scores(n_blk - 1, "y", diag=1)
        update(n_blk - 2, "x")
        update(n_blk - 1, "y")

    @pl.when(qi == 0)
    def _():
        scores(0, "x", diag=0)
        scores(1, "y", diag=1)
        update(0, "x")
        update(1, "y")

    out_t = jnp.concatenate([acc_scr[hh, 0:MLA_V, :] * (1.0 / acc_scr[hh, MLA_V:MLA_V + 1, :]) for hh in range(hps)],
                            axis=0)
    o_ref[0] = out_t.T.astype(BF16)


def _attention(q, k, vt, kb, hps):
    bsz, s, _ = q.shape
    groups = MLA_HEADS // hps
    tq = 2 * kb
    assert vt.shape == (bsz, s // kb, MLA_HEADS * MLA_VROWS, kb)
    return pl.pallas_call(
        functools.partial(_attn_kernel, tq=tq, kb=kb, hps=hps),
        grid=(bsz * groups, s // tq),
        in_specs=[pl.BlockSpec((1, tq, hps * LANES), lambda g, i: (g // groups, i, g % groups)),
                  pl.BlockSpec((1, s, hps * LANES), lambda g, i: (g // groups, 0, g % groups)),
                  pl.BlockSpec((1, s // kb, hps * MLA_VROWS, kb), lambda g, i: (g // groups, 0, g % groups, 0))],
        out_specs=pl.BlockSpec((1, tq, hps * MLA_V), lambda g, i: (g // groups, i, g % groups)),
        out_shape=jax.ShapeDtypeStruct((bsz, s, MLA_HEADS * MLA_V), BF16),
        scratch_shapes=[pltpu.VMEM((hps, kb, tq), F32), pltpu.VMEM((hps, kb, tq), F32),
                        pltpu.VMEM((hps, 8, tq), F32), pltpu.VMEM((hps, 8, tq), F32),
                        pltpu.VMEM((hps, 8, tq), F32), pltpu.VMEM((hps, MLA_VROWS, tq), F32)],
        compiler_params=_params("parallel", "arbitrary"),
        name="mla_attention",
    )(q, k, vt)


def _retention_tile(rq_ref, rk_ref, rv_ref, rg_ref, dm_ref, xi_ref, zt_ref, dc_ref, o_ref, st_ref, *, ts):
    @pl.when(pl.program_id(1) == 0)
    def _():
        st_ref[...] = jnp.zeros_like(st_ref)

    lane = lax.broadcasted_iota(jnp.int32, (RET_CHUNK, LANES), 1)
    row = lax.broadcasted_iota(jnp.int32, (LANES, RET_DV), 0)
    hpt = LANES // RET_DK
    n_chunks = ts // RET_CHUNK
    tiles = [(c, p) for c in range(n_chunks) for p in range(RET_HEADS // hpt)]
    units = [(c, hd) for c in range(n_chunks) for hd in range(RET_HEADS)]
    rows = lambda c: slice(c * RET_CHUNK, (c + 1) * RET_CHUNK)
    vsl = lambda hd: slice(hd * RET_DV, (hd + 1) * RET_DV)
    mine = [(lane >= sub * RET_DK) & (lane < (sub + 1) * RET_DK) for sub in range(hpt)]
    mine_row = [(row >= sub * RET_DK) & (row < (sub + 1) * RET_DK) for sub in range(hpt)]

    q2 = {(c, p): rq_ref[0, rows(c), p * LANES:(p + 1) * LANES] for c, p in tiles}
    k2 = {(c, p): rk_ref[0, rows(c), p * LANES:(p + 1) * LANES] for c, p in tiles}
    vh = {(c, hd): rv_ref[0, rows(c), vsl(hd)] for c, hd in units}
    sc, un = {}, {}
    for c, p in tiles:
        heads = [p * hpt + sub for sub in range(hpt)]
        kcat = jnp.concatenate([jnp.where(mine[sub], k2[(c, p)], 0.0).astype(BF16) for sub in range(hpt)], axis=0)
        s_all = _mm_nt(q2[(c, p)], kcat)
        kz = jnp.zeros((RET_CHUNK, LANES), F32)
        for sub, hd in enumerate(heads):
            sc[(c, hd)] = s_all[:, sub * RET_CHUNK:(sub + 1) * RET_CHUNK]
            kz = kz + jnp.where(mine[sub], k2[(c, p)].astype(F32) * zt_ref[hd], 0.0)
        u_all = _mm(kz.astype(BF16).T, jnp.concatenate([vh[(c, hd)] for hd in heads], axis=1))
        for sub, hd in enumerate(heads):
            un[(c, hd)] = jnp.where(mine_row[sub], u_all[:, sub * RET_DV:(sub + 1) * RET_DV], 0.0)
    prev = {}
    for hd in range(RET_HEADS):
        st = st_ref[hd]
        for c in range(n_chunks):
            prev[(c, hd)] = st.astype(BF16)
            st = st * dc_ref[hd] + un[(c, hd)]
        st_ref[hd] = st
    for c, hd in units:
        u = (c, hd)
        lhs = jnp.concatenate([(sc[u] * dm_ref[hd]).astype(BF16),
                               (q2[(c, hd // hpt)].astype(F32) * xi_ref[hd]).astype(BF16)], axis=1)
        o = _mm(lhs, jnp.concatenate([vh[u], prev[u]], axis=0))
        mu = jnp.mean(o, axis=-1, keepdims=True)
        oc = o - mu
        on = oc * lax.rsqrt(jnp.mean(oc * oc, axis=-1, keepdims=True) + NORM_EPS)
        o_ref[rows(c), vsl(hd)] = (_silu(rg_ref[0, rows(c), vsl(hd)]) * on).astype(BF16)


def _retention_tables():
    c = RET_CHUNK
    gamma = 1.0 - jnp.power(2.0, -5.0 - jnp.arange(RET_HEADS, dtype=F32))
    log_g = jnp.log(gamma)
    idx = jnp.arange(c, dtype=F32)
    diff = idx[:, None] - idx[None, :]
    dmask = jnp.where(diff[None] >= 0, jnp.exp(jnp.maximum(diff, 0.0)[None] * log_g[:, None, None]), 0.0)
    zeta = jnp.exp((c - 1.0 - idx)[None, :] * log_g[:, None])
    xi = jnp.exp((idx + 1.0)[None, :] * log_g[:, None])
    decay = jnp.exp(c * log_g)
    xi_b = jnp.broadcast_to(xi[:, :, None], (RET_HEADS, c, LANES))
    zt_b = jnp.broadcast_to(zeta[:, :, None], (RET_HEADS, c, LANES))
    dc_b = jnp.broadcast_to(decay[:, None, None], (RET_HEADS, LANES, RET_DV))
    return dmask, xi_b, zt_b, dc_b


def _post_kernel(x_ref, om_ref, rq_ref, rk_ref, rv_ref, rg_ref, dm_ref, xi_ref, zt_ref, dc_ref,
                 mod_ref, wo_ref, g2_ref, wr_ref, br_ref,
                 x1_ref, h2_ref, ri_ref, rit_ref, cnt_ref, or_scr, st_ref):
    _retention_tile(rq_ref, rk_ref, rv_ref, rg_ref, dm_ref, xi_ref, zt_ref, dc_ref, or_scr, st_ref,
                    ts=x_ref.shape[1])
    half = om_ref.shape[-1]
    subs = [slice(i * MOE_TILE, (i + 1) * MOE_TILE) for i in range(x_ref.shape[1] // MOE_TILE)]
    mixes = [_mm(om_ref[0, rs, :], wo_ref[0:half, :]) + _mm(or_scr[rs, :], wo_ref[half:, :]) for rs in subs]

    lgs = []
    for rs, mix in zip(subs, mixes):
        x1 = x_ref[0, rs, :] + mod_ref[0, 2:3, :] * mix
        x1_ref[0, rs, :] = x1
        h2 = _rms(x1, g2_ref[...]) * (1.0 + mod_ref[0, 4:5, :]) + mod_ref[0, 3:4, :]
        hi = h2.astype(BF16)
        h2_ref[0, rs, :] = hi
        lo = (h2 - hi.astype(F32)).astype(BF16)
        both = _mm(hi, wr_ref[...])
        lgs.append(both[:, 0:LANES] + both[:, LANES:2 * LANES] + _mm(lo, wr_ref[:, 0:LANES]) + br_ref[...])

    lane = lax.broadcasted_iota(jnp.int32, (MOE_TILE, LANES), 1)
    big = jnp.int32(1 << 20)
    gmask = lane < N_GROUPS
    el = lane - N_GROUPS
    assert EXPERTS_PER_GROUP == 8
    routed = []
    for lg in lgs:
        gmax = jnp.max(jnp.where(gmask, lg, NEG), axis=-1, keepdims=True)
        ge = jnp.where(gmask, jnp.exp(lg - gmax), 0.0)
        pg = ge / jnp.sum(ge, axis=-1, keepdims=True)
        p_top = jnp.max(pg, axis=-1, keepdims=True)
        g_top = jnp.min(jnp.where(gmask & (pg == p_top), lane, big), axis=-1, keepdims=True)

        emask = (el >= 0) & (el < N_EXPERTS) & (lax.shift_right_arithmetic(el, 3) == g_top)
        ev = jnp.where(emask, lg, NEG)
        v1 = jnp.max(ev, axis=-1, keepdims=True)
        i1 = jnp.min(jnp.where(emask & (ev == v1), lane, big), axis=-1, keepdims=True)
        emask2 = emask & (lane != i1)
        ev2 = jnp.where(emask2, lg, NEG)
        v2 = jnp.max(ev2, axis=-1, keepdims=True)
        i2 = jnp.min(jnp.where(emask2 & (ev2 == v2), lane, big), axis=-1, keepdims=True)
        e = jnp.exp(v2 - v1)
        den = 1.0 + e
        routed.append((i1, i2, (1.0 / den) * p_top, (e / den) * p_top))

    r_io = lax.broadcasted_iota(jnp.int32, (MOE_TILE, MOE_TILE), 0)
    c_io = lax.broadcasted_iota(jnp.int32, (MOE_TILE, MOE_TILE), 1)
    earlier_tok = (c_io < r_io).astype(BF16)
    lr_io = lax.broadcasted_iota(jnp.int32, (LANES, LANES), 0)
    lc_io = lax.broadcasted_iota(jnp.int32, (LANES, LANES), 1)
    earlier_lane = (lr_io < lc_io).astype(BF16)
    for hf, (rs, (i1, i2, w1, w2)) in enumerate(zip(subs, routed)):
        oh = [lane == i1, lane == i2]
        cnt = (oh[0] | oh[1]).astype(BF16)
        excl = _mm(earlier_tok, cnt)
        n = jnp.sum(cnt.astype(F32), axis=0, keepdims=True)
        npad = jnp.floor((n + (MOE_CHUNK - 1)) * (1.0 / MOE_CHUNK)) * MOE_CHUNK
        loff = _mm(jnp.broadcast_to(npad, (8, LANES)).astype(BF16), earlier_lane)
        pos = excl + loff[0:1, :]
        cnt_ref[hf] = jnp.broadcast_to(n, (8, LANES))
        lp0, lp1 = [jnp.sum(jnp.where(o, pos, 0.0), axis=-1, keepdims=True) for o in oh]
        cols = [(i1 - N_GROUPS).astype(F32), (i2 - N_GROUPS).astype(F32), w1, w2, lp0, lp1]
        ri = jnp.zeros((MOE_TILE, LANES), F32)
        for j, col in enumerate(cols):
            ri = jnp.where(lane == j, col, ri)
        ri_ref[0, rs, :] = ri
        rit_ref[:, rs] = ri.T


def _post(x, o_mla, rq, rk, rv, rg, mod3, w_o, norm2_g, w_gr, b_gr, w_er, b_er, tm):
    bsz, s, d = x.shape
    ret_tables = _retention_tables()
    c = RET_CHUNK
    w_r = jnp.concatenate([w_gr, w_er.reshape(d, N_EXPERTS), jnp.zeros((d, LANES - N_GROUPS - N_EXPERTS), F32)], axis=1)
    w_rh = w_r.astype(BF16)
    w_rl = (w_r - w_rh.astype(F32)).astype(BF16)
    w_r2 = jnp.concatenate([w_rh, w_rl], axis=1)
    b_r =jnp.concatenate([b_gr, b_er.reshape(-1), jnp.zeros((LANES - N_GROUPS - N_EXPERTS,), F32)]).reshape(1, LANES)
    tile = lambda w: pl.BlockSpec((1, tm, w), lambda b, i: (b, i, 0))
    const = lambda shape: pl.BlockSpec(shape, lambda b, i: (0,) * len(shape))
    per_b = s // tm
    sub = tm // MOE_TILE
    return pl.pallas_call(
        _post_kernel,
        grid=(bsz, per_b),
        in_specs=[tile(d), tile(o_mla.shape[-1]),
                  tile(RET_HEADS * RET_DK), tile(RET_HEADS * RET_DK), tile(RET_HEADS * RET_DV), tile(RET_HEADS * RET_DV),
                  const((RET_HEADS, c, c)), const((RET_HEADS, c, LANES)), const((RET_HEADS, c, LANES)),
                  const((RET_HEADS, LANES, RET_DV)),
                  pl.BlockSpec((1, 6, d), lambda b, i: (b, 0, 0)),
                  const((d, d)), const((1, d)), const((d, 2 * LANES)), const((1, LANES))],
        out_specs=[tile(d), tile(d), tile(LANES),
                   pl.BlockSpec((LANES, tm), lambda b, i: (0, b * per_b + i)),
                   pl.BlockSpec((sub, 8, LANES), lambda b, i: (b * per_b + i, 0, 0))],
        out_shape=[jax.ShapeDtypeStruct((bsz, s, d), F32), jax.ShapeDtypeStruct((bsz, s, d), BF16),
                   jax.ShapeDtypeStruct((bsz, s, LANES), F32),
                   jax.ShapeDtypeStruct((LANES, bsz * s), F32),
                   jax.ShapeDtypeStruct((bsz * s // MOE_TILE, 8, LANES), F32)],
        scratch_shapes=[pltpu.VMEM((tm, RET_HEADS * RET_DV), BF16), pltpu.VMEM((RET_HEADS, LANES, RET_DV), F32)],
        compiler_params=_params("parallel", "arbitrary"),
        name="post_mixer",
    )(x, o_mla, rq, rk, rv, rg, *ret_tables, mod3, w_o.astype(BF16), norm2_g.reshape(1, d), w_r2, b_r)


def _chunk_rows(c):
    return pl.ds(pl.multiple_of(c * MOE_CHUNK, MOE_CHUNK), MOE_CHUNK)


def _dispatch_kernel(dmap_ref, nchk_ref, tstart_ref, tn_ref, nbr_ref, h2_ref, ri_ref, rit_ref, xs_hbm,
                     xloc, zblk, sem, zsem, bsem, *, nsteps, nb):
    i = pl.program_id(0)
    slot = lax.rem(i, 2)
    d = h2_ref.shape[-1]

    def zero_copy(e, c):
        return pltpu.make_async_copy(zblk.at[pl.ds(0, MOE_CHUNK)], xs_hbm.at[_chunk_rows(tstart_ref[e] + c)], zsem)

    def zero_block(j):
        rows = pl.ds(pl.multiple_of(j * MOE_BLOCK, MOE_BLOCK), MOE_BLOCK)
        return pltpu.make_async_copy(zblk, xs_hbm.at[rows], bsem)

    def chunk_copy(t, u, c, sl):
        return pltpu.make_async_copy(xloc.at[sl].at[u].at[_chunk_rows(c)],
                                     xs_hbm.at[_chunk_rows(dmap_ref[t * MOE_TILE_CHUNKS + c])], sem.at[sl])

    def wait_step(step, sl):
        for u in range(MOE_TPS):
            _wait_chunks(nchk_ref[step * MOE_TPS + u],
                         lambda rows: pltpu.make_async_copy(xloc.at[sl].at[0].at[rows], xs_hbm.at[rows], sem.at[sl]))

    @pl.when(i == 0)
    def _():
        zblk[...] = jnp.zeros_like(zblk)
        lax.fori_loop(nbr_ref[0], nb, lambda j, z: (zero_block(j).start(), z)[1], 0)
        for e in range(N_EXPERTS):
            lax.fori_loop(0, tn_ref[e], lambda c, z, e=e: (zero_copy(e, c).start(), z)[1], 0)
        for e in range(N_EXPERTS):
            lax.fori_loop(0, tn_ref[e], lambda c, z, e=e: (zero_copy(e, c).wait(), z)[1], 0)

    @pl.when(i >= 2)
    def _():
        wait_step(i - 2, slot)

    s_io = lax.broadcasted_iota(jnp.int32, (MOE_SLAB, MOE_TILE), 0).astype(F32)
    lane = lax.broadcasted_iota(jnp.int32, (MOE_TILE, LANES), 1)
    tiles = [slice(u * MOE_TILE, (u + 1) * MOE_TILE) for u in range(MOE_TPS)]
    pms = [((s_io == rit_ref[4:5, ts]) | (s_io == rit_ref[5:6, ts])).astype(BF16) for ts in tiles]
    exts = []
    for ts in tiles:
        ext = jnp.zeros((MOE_TILE, LANES), F32)
        for k in range(TOP_K):
            w = ri_ref[ts, TOP_K + k:TOP_K + k + 1]
            hi = w.astype(BF16).astype(F32)
            ext = jnp.where(lane == 2 * k, hi, jnp.where(lane == 2 * k + 1, w - hi, ext))
            ext = jnp.where(lane == 2 * TOP_K + k, ri_ref[ts, k:k + 1], ext)
        exts.append(ext.astype(BF16))
    mains = [_mm(pm, jnp.concatenate([h2_ref[ts, :], ext], axis=1)) for pm, ts, ext in zip(pms, tiles, exts)]
    for u in range(MOE_TPS):
        xloc[slot, u] = mains[u].astype(BF16)

    for u in range(MOE_TPS):
        t = i * MOE_TPS + u
        lax.fori_loop(0, nchk_ref[t], lambda c, z, t=t, u=u: (chunk_copy(t, u, c, slot).start(), z)[1], 0)

    @pl.when(i == nsteps - 1)
    def _():
        wait_step(i, slot)
        if nsteps >= 2:
            wait_step(i - 1, 1 - slot)
        lax.fori_loop(nbr_ref[0], nb, lambda j, z: (zero_block(j).wait(), z)[1], 0)


def _wait_chunks(n, copy_of_rows):
    bit = MOE_TILE_CHUNKS
    while bit >= 1:
        @pl.when((n & bit) != 0)
        def _(bit=bit):
            copy_of_rows(pl.ds(0, bit * MOE_CHUNK)).wait()
        bit //= 2


def _dispatch(h2, ri, rit, tables, n_rows):
    t, d = h2.shape
    nt = t // MOE_TILE
    assert nt % MOE_TPS == 0 and MOE_SLAB >= TOP_K * MOE_TILE + N_EXPERTS * (MOE_CHUNK - 1)
    rows = MOE_TPS * MOE_TILE
    grid_spec = pltpu.PrefetchScalarGridSpec(
        num_scalar_prefetch=5,
        grid=(nt // MOE_TPS,),
        in_specs=[pl.BlockSpec((rows, d), lambda i, *_: (i, 0)),
                  pl.BlockSpec((rows, LANES), lambda i, *_: (i, 0)),
                  pl.BlockSpec((8, rows), lambda i, *_: (0, i))],
        out_specs=pl.BlockSpec(memory_space=pl.ANY),
        scratch_shapes=[pltpu.VMEM((2, MOE_TPS, MOE_SLAB, d + LANES), BF16), pltpu.VMEM((MOE_BLOCK, d + LANES), BF16),
                        pltpu.SemaphoreType.DMA((2,)), pltpu.SemaphoreType.DMA(()), pltpu.SemaphoreType.DMA(())])
    return pl.pallas_call(
        functools.partial(_dispatch_kernel, nsteps=nt // MOE_TPS, nb=n_rows // MOE_BLOCK),
        grid_spec=grid_spec,
        out_shape=jax.ShapeDtypeStruct((n_rows, d + LANES), BF16),
        compiler_params=_params("arbitrary"),
        name="moe_dispatch",
    )(tables["dmap"], tables["tile_chunks"], tables["tail_start"], tables["tail_n"], tables["n_blocks"],
      h2, ri, rit)


def _expert_kernel(be_ref, nbr_ref, xs_ref, w1_ref, w3_ref, w2_ref, y_ref, w13b, w2b):
    i = pl.program_id(0)
    used = i < nbr_ref[0]

    @pl.when(jnp.logical_not(used))
    def _():
        y_ref[...] = jnp.zeros_like(y_ref)

    @pl.when(used & ((i == 0) | (be_ref[i] != be_ref[jnp.maximum(i - 1, 0)])))
    def _():
        w13b[:, 0:D_EXPERT] = w1_ref[0].astype(BF16)
        w13b[:, D_EXPERT:] = w3_ref[0].astype(BF16)
        w2b[...] = w2_ref[0].astype(BF16)

    @pl.when(used)
    def _():
        d = y_ref.shape[-1]
        subs = [slice(j * MOE_TILE, (j + 1) * MOE_TILE) for j in range(y_ref.shape[0] // MOE_TILE)]
        ab = [_mm(xs_ref[rs, 0:d], w13b[...]) for rs in subs]
        hms = [(_silu(v[:, 0:D_EXPERT]) * v[:, D_EXPERT:]).astype(BF16) for v in ab]
        expert = be_ref[i].astype(F32)
        for rs, hm in zip(subs, hms):
            gw = xs_ref[rs, d:d + LANES].astype(F32)
            gate = jnp.where(gw[:, 2 * TOP_K:2 * TOP_K + 1] == expert, gw[:, 0:1] + gw[:, 1:2], gw[:, 2:3] + gw[:, 3:4])
            y_ref[rs, :] = (_mm(hm, w2b[...]) * gate).astype(BF16)


def _experts(xs, tables, w1, w3, w2):
    n_rows, dw = xs.shape
    d = dw - LANES
    nb = n_rows // MOE_BLOCK
    blk = lambda i, be, nbr: jnp.minimum(i, nbr[0] - 1)
    grid_spec = pltpu.PrefetchScalarGridSpec(
        num_scalar_prefetch=2,
        grid=(nb,),
        in_specs=[pl.BlockSpec((MOE_BLOCK, dw), lambda i, be, nbr: (blk(i, be, nbr), 0)),
                  pl.BlockSpec((1, d, D_EXPERT), lambda i, be, nbr: (be[blk(i, be, nbr)], 0, 0)),
                  pl.BlockSpec((1, d, D_EXPERT), lambda i, be, nbr: (be[blk(i, be, nbr)], 0, 0)),
                  pl.BlockSpec((1, D_EXPERT, d), lambda i, be, nbr: (be[blk(i, be, nbr)], 0, 0))],
        out_specs=pl.BlockSpec((MOE_BLOCK, d), lambda i, be, nbr: (i, 0)),
        scratch_shapes=[pltpu.VMEM((d, 2 * D_EXPERT), BF16), pltpu.VMEM((D_EXPERT, d), BF16)])
    return pl.pallas_call(
        _expert_kernel,
        grid_spec=grid_spec,
        out_shape=jax.ShapeDtypeStruct((n_rows, d), BF16),
        compiler_params=_params("arbitrary"),
        name="moe_experts",
    )(tables["blk_expert"], tables["n_blocks"], xs, w1, w3, w2)


def _moe_rows(n_tiles):
    worst = n_tiles * (TOP_K * MOE_TILE + N_EXPERTS * (MOE_CHUNK - 1)) + N_EXPERTS * (MOE_BLOCK - MOE_CHUNK)
    return -(-worst // MOE_BLOCK) * MOE_BLOCK


def _moe_tables(cnt, n_rows):
    n_tiles = cnt.shape[0]
    per_blk = MOE_BLOCK // MOE_CHUNK

    def excl_cumsum(a, axis):
        n = a.shape[axis]
        lower = jnp.arange(n)[:, None] > jnp.arange(n)[None, :]
        if axis == 0:
            return jnp.sum(jnp.where(lower[:, :, None], a[None, :, :], 0), axis=1)
        return jnp.sum(jnp.where(lower[None, :, :], a[:, None, :], 0), axis=2)

    nch = (cnt + MOE_CHUNK - 1) // MOE_CHUNK
    loff = excl_cumsum(nch, 1)
    seg = jnp.sum(nch, axis=0)
    blocks = (seg + per_blk - 1) // per_blk
    bstart = excl_cumsum(blocks[None, :], 1)[0]
    bend = bstart + blocks
    estart = bstart * per_blk
    gbase = estart[None, :] + excl_cumsum(nch, 0)
    c = jnp.arange(MOE_TILE_CHUNKS, dtype=jnp.int32)
    owner = jnp.sum((c[None, :, None] >= (loff + nch)[:, None, :]).astype(jnp.int32), axis=-1)
    owner = jnp.minimum(owner, N_EXPERTS - 1)
    is_owner = owner[:, :, None] == jnp.arange(N_EXPERTS, dtype=jnp.int32)[None, None, :]
    dmap = c[None, :] + jnp.sum(jnp.where(is_owner, (gbase - loff)[:, None, :], 0), axis=-1)
    dmap = jnp.clip(dmap, 0, n_rows // MOE_CHUNK - 1)
    nb = n_rows // MOE_BLOCK
    blk_expert = jnp.sum((jnp.arange(nb, dtype=jnp.int32)[:, None] >= bend[None, :]).astype(jnp.int32), axis=1)
    blk_expert = jnp.minimum(blk_expert, N_EXPERTS - 1)
    return dict(dmap=dmap.reshape(-1).astype(jnp.int32), tile_chunks=jnp.sum(nch, axis=1).astype(jnp.int32),
                tail_start=(estart + seg).astype(jnp.int32), tail_n=(blocks * per_blk - seg).astype(jnp.int32),
                blk_expert=blk_expert.astype(jnp.int32), n_blocks=bend[-1:].astype(jnp.int32))


def _final_kernel(dmap_ref, nchk_ref, x1_ref, ri_ref, mod_ref, g_ref, y_hbm, o_ref, yloc, sem, *, nsteps):
    i = pl.program_id(0)
    slot = lax.rem(i, 2)

    def chunk_copy(t, u, c, sl):
        return pltpu.make_async_copy(y_hbm.at[_chunk_rows(dmap_ref[t * MOE_TILE_CHUNKS + c])],
                                     yloc.at[sl].at[u].at[_chunk_rows(c)], sem.at[sl])

    def gather(step, sl):
        for u in range(MOE_TPS):
            t = step * MOE_TPS + u
            lax.fori_loop(0, nchk_ref[t], lambda c, z, t=t, u=u: (chunk_copy(t, u, c, sl).start(), z)[1], 0)

    @pl.when(i == 0)
    def _():
        yloc[...] = jnp.zeros_like(yloc)
        gather(0, 0)

    @pl.when(i + 1 < nsteps)
    def _():
        gather(i + 1, 1 - slot)

    for u in range(MOE_TPS):
        _wait_chunks(nchk_ref[i * MOE_TPS + u],
                     lambda rows: pltpu.make_async_copy(y_hbm.at[rows], yloc.at[slot].at[0].at[rows], sem.at[slot]))

    l_io = lax.broadcasted_iota(jnp.int32, (MOE_TILE, MOE_SLAB), 1).astype(F32)
    tiles = [slice(u * MOE_TILE, (u + 1) * MOE_TILE) for u in range(MOE_TPS)]
    picks = [((l_io == ri_ref[ts, 4:5]) | (l_io == ri_ref[ts, 5:6])).astype(BF16) for ts in tiles]
    moes = [_mm(pick, yloc[slot, u]) for u, pick in enumerate(picks)]
    for ts, moe in zip(tiles, moes):
        x2 = x1_ref[ts, :] + mod_ref[0, 5:6, :] * moe
        o_ref[ts, :] = _rms(x2, g_ref[...])


def _final(x1, ri, y, mod3, final_g, tables, s):
    t, d = x1.shape
    rows = MOE_TPS * MOE_TILE
    nsteps = t // rows
    per_b = s // rows
    grid_spec = pltpu.PrefetchScalarGridSpec(
        num_scalar_prefetch=2,
        grid=(nsteps,),
        in_specs=[pl.BlockSpec((rows, d), lambda i, *_: (i, 0)),
                  pl.BlockSpec((rows, LANES), lambda i, *_: (i, 0)),
                  pl.BlockSpec((1, 6, d), lambda i, *_: (i // per_b, 0, 0)),
                  pl.BlockSpec((1, d), lambda i, *_: (0, 0)),
                  pl.BlockSpec(memory_space=pl.ANY)],
        out_specs=pl.BlockSpec((rows, d), lambda i, *_: (i, 0)),
        scratch_shapes=[pltpu.VMEM((2, MOE_TPS, MOE_SLAB, d), BF16), pltpu.SemaphoreType.DMA((2,))])
    return pl.pallas_call(
        functools.partial(_final_kernel, nsteps=nsteps),
        grid_spec=grid_spec,
        out_shape=jax.ShapeDtypeStruct((t, d), F32),
        compiler_params=_params("arbitrary"),
        name="moe_combine_final",
    )(tables["dmap"], tables["tile_chunks"], x1, ri, mod3, final_g.reshape(1, d), y)


def kernel(x, c, positions, w_ada, b_ada, norm1_g, w_in, q_norm_g, w_uq, kv_norm_g, w_ukv, w_o, norm2_g,
           w_gr, b_gr, w_er, b_er, w1, w3, w2, final_g):
    bsz, s, d = x.shape
    assert w_ada.shape[0] == 1, "one layer"
    tm = min(1024, s)
    tq = min(256, s)
    mod3 = _adaln(c, w_ada[0], b_ada[0]).reshape(bsz, 6, d)
    pos3 = positions.astype(F32).reshape(bsz, s, 1)
    q, k, vt, rq, rk, rv, rg = _pre(x, mod3, pos3, norm1_g[0], w_in[0], q_norm_g[0], w_uq[0], kv_norm_g[0], w_ukv[0],
                                    tm, tq)
    o_mla = _attention(q, k, vt, tq, ATTN_HEADS_PER_STEP)
    x1, h2, ri, rit, cnt = _post(x, o_mla, rq, rk, rv, rg, mod3, w_o[0], norm2_g[0], w_gr[0], b_gr[0], w_er[0],
                                 b_er[0], tm)
    t = bsz * s
    n_rows = _moe_rows(t // MOE_TILE)
    counts = cnt[:, 0, N_GROUPS:N_GROUPS + N_EXPERTS].astype(jnp.int32)
    tables = _moe_tables(counts, n_rows)
    ri2 = ri.reshape(t, LANES)
    xs = _dispatch(h2.reshape(t, d), ri2, rit, tables, n_rows)
    y = _experts(xs, tables, w1[0], w3[0], w2[0])
    out = _final(x1.reshape(t, d), ri2, y, mod3, final_g, tables, s)
    return out.reshape(bsz, s, d)
```

```python
import functools

import jax
import jax.numpy as jnp
from jax import lax
from jax.experimental import pallas as pl
from jax.experimental.pallas import tpu as pltpu

MLA_HEADS = 8
MLA_NOPE = 64
MLA_ROPE = 32
MLA_V = 64
Q_LORA = 256
KV_LORA = 128
RET_HEADS = 4
RET_DK = 64
RET_DV = 128
RET_CHUNK = 128
ROPE_BASE = 10000.0
NORM_EPS = 1e-6
N_GROUPS = 4
EXPERTS_PER_GROUP = 8
N_EXPERTS = N_GROUPS * EXPERTS_PER_GROUP
TOP_K = 2
D_EXPERT = 256
MOE_TILE = 256
MOE_CHUNK = 16
MOE_SLAB = 1024
MOE_TILE_CHUNKS = MOE_SLAB // MOE_CHUNK
MOE_LIST = max(MOE_TILE_CHUNKS // 2, N_EXPERTS)
MOE_TPS = 2
MOE_BLOCK = 1024

LANES = 128
VMEM_LIMIT = 56 * 1024 * 1024

F32 = jnp.float32
BF16 = jnp.bfloat16
NEG = float(jnp.finfo(jnp.float32).min)
LOG2_E = 1.4426950408889634
ATTN_HEADS_PER_STEP = 8
ATTN_PAIRS_PER_ITER = 2
MLA_VROWS = MLA_V + 16

_C_Q = 0
_C_KV = _C_Q + Q_LORA
_C_KPE = _C_KV + KV_LORA
_C_RQ = _C_KPE + LANES
_C_RK = _C_RQ + RET_HEADS * RET_DK
_C_RV = _C_RK + RET_HEADS * RET_DK
_C_RG = _C_RV + RET_HEADS * RET_DV
_IN_PERM = _C_RG + RET_HEADS * RET_DV


def _silu(v):
    return v / (1.0 + jnp.exp(-v))


def _mm(a, b):
    return jnp.dot(a, b, preferred_element_type=F32)


def _mm_nt(a, b):
    return lax.dot_general(a, b, (((1,), (1,)), ((), ())), preferred_element_type=F32)


def _params(*sem):
    return pltpu.CompilerParams(dimension_semantics=sem, vmem_limit_bytes=VMEM_LIMIT)


def _adaln_kernel(c_ref, w_ref, b_ref, o_ref):
    a = _silu(c_ref[...]).astype(BF16)
    o_ref[...] = _mm(a, w_ref[...].astype(BF16)) + b_ref[...]


def _adaln(c, w_ada, b_ada):
    bsz, d = c.shape
    n = w_ada.shape[1]
    tn = d
    return pl.pallas_call(
        _adaln_kernel,
        grid=(n // tn,),
        in_specs=[pl.BlockSpec((bsz, d), lambda j: (0, 0)),
                  pl.BlockSpec((d, tn), lambda j: (0, j)),
                  pl.BlockSpec((1, tn), lambda j: (0, j))],
        out_specs=pl.BlockSpec((bsz, tn), lambda j: (0, j)),
        out_shape=jax.ShapeDtypeStruct((bsz, n), F32),
        compiler_params=_params("arbitrary"),
        name="adaln",
    )(c, w_ada, b_ada.reshape(1, n))


def _rms(v, g):
    return v * lax.rsqrt(jnp.mean(v * v, axis=-1, keepdims=True) + NORM_EPS) * g


def _pre_kernel(x_ref, mod_ref, pos_ref, g1_ref, win_ref, qg_ref, wuq_ref, kvg_ref, wuk_ref,
                wuv_ref, invf_ref, q_ref, k_ref, v_ref, rq_ref, rk_ref, rv_ref, rg_ref):
    tq = v_ref.shape[-1]
    subs = [slice(i * tq, (i + 1) * tq) for i in range(v_ref.shape[1])]
    sh1 = mod_ref[0, 0:1, :]
    sc1 = mod_ref[0, 1:2, :]
    projs = [_mm((_rms(x_ref[0, rs, :], g1_ref[...]) * (1.0 + sc1) + sh1).astype(BF16), win_ref[...]) for rs in subs]

    cqs = [_rms(p[:, _C_Q:_C_Q + Q_LORA], qg_ref[...]).astype(BF16) for p in projs]
    ckvs = [_rms(p[:, _C_KV:_C_KV + KV_LORA], kvg_ref[...]).astype(BF16) for p in projs]
    qas = [_mm(cq, wuq_ref[...]) for cq in cqs]
    kns = [_mm(ckv, wuk_ref[...]) for ckv in ckvs]
    vvs = [_mm(ckv, wuv_ref[...]) for ckv in ckvs]

    lane = lax.broadcasted_iota(jnp.int32, (tq, LANES), 1)
    hi = lane >= RET_DK
    half_m, half_r = MLA_ROPE // 2, RET_DK // 2
    first_m = hi & (lane < RET_DK + half_m)
    first_r = (lane & half_r) == 0
    scale = (MLA_NOPE + MLA_ROPE) ** -0.5 * LOG2_E

    def rope(v, cos, sin, first, half):
        partner = jnp.where(first, pltpu.roll(v, LANES - half, 1), pltpu.roll(v, half, 1))
        return v * cos + partner * sin

    for i, rs in enumerate(subs):
        proj = projs[i]
        ang = pos_ref[0, rs, :] * invf_ref[...]
        cs = jnp.cos(ang)
        sn = jnp.sin(ang)
        c_mla = jnp.where(hi, cs, 1.0)
        s_mla = jnp.where(hi, jnp.where(first_m, -sn, sn), 0.0)
        c_ret = jnp.where(hi, pltpu.roll(cs, RET_DK, 1), cs)
        s_ret = jnp.where(hi, pltpu.roll(sn, RET_DK, 1), sn)
        s_ret = jnp.where(first_r, -s_ret, s_ret)

        kpe = rope(proj[:, _C_KPE:_C_KPE + LANES], c_mla, s_mla, first_m, half_m)
        cq_s = c_mla * scale
        sq_s = jnp.where(hi, sn, 0.0) * scale
        for hd in range(MLA_HEADS):
            sl = slice(hd * LANES, (hd + 1) * LANES)
            sr = slice((MLA_HEADS + hd) * LANES, (MLA_HEADS + hd + 1) * LANES)
            q_ref[0, rs, sl] = (qas[i][:, sl] * cq_s + qas[i][:, sr] * sq_s).astype(BF16)
            k_ref[0, rs, sl] = (kns[i][:, sl] + kpe).astype(BF16)
        vt = vvs[i].T
        tail = jnp.where(lax.broadcasted_iota(jnp.int32, (MLA_VROWS - MLA_V, tq), 0) == 0, 1.0, 0.0)
        slab = [piece for hd in range(MLA_HEADS) for piece in (vt[hd * MLA_V:(hd + 1) * MLA_V, :], tail)]
        v_ref[0, i] = jnp.concatenate(slab, axis=0).astype(BF16)

        for j in range(RET_HEADS * RET_DK // LANES):
            o = j * LANES
            rq = rope(proj[:, _C_RQ + o:_C_RQ + o + LANES], c_ret, s_ret, first_r, half_r)
            rk = rope(proj[:, _C_RK + o:_C_RK + o + LANES], c_ret, s_ret, first_r, half_r)
            rq_ref[0, rs, o:o + LANES] = rq.astype(BF16)
            rk_ref[0, rs, o:o + LANES] = (rk * (RET_DK ** -0.5)).astype(BF16)
        rv_ref[0, rs, :] = proj[:, _C_RV:_C_RV + RET_HEADS * RET_DV].astype(BF16)
        rg_ref[0, rs, :] = proj[:, _C_RG:_C_RG + RET_HEADS * RET_DV]


def _pad_heads(w, width, left):
    k = w.shape[0]
    w3 = w.reshape(k, -1, width)
    w3 = jnp.pad(w3, ((0, 0), (0, 0), (left, LANES - left - width)))
    return w3.reshape(k, -1)


def _pre(x, mod3, pos3, norm1_g, w_in, q_norm_g, w_uq, kv_norm_g, w_ukv, tm, tq):
    bsz, s, d = x.shape
    o = 0
    parts = {}
    for name, width in (("cq", Q_LORA), ("ckv", KV_LORA), ("kr", MLA_ROPE), ("rq", RET_HEADS * RET_DK),
                        ("rk", RET_HEADS * RET_DK), ("rv", RET_HEADS * RET_DV), ("rg", RET_HEADS * RET_DV)):
        parts[name] = w_in[:, o:o + width]
        o += width
    w_in_p = jnp.concatenate([
        parts["cq"], parts["ckv"], _pad_heads(parts["kr"], MLA_ROPE, MLA_NOPE),
        parts["rq"], parts["rk"], parts["rv"], parts["rg"]], axis=1).astype(BF16)
    assert w_in_p.shape[1] == _IN_PERM
    wq_rope = w_uq.reshape(Q_LORA, MLA_HEADS, MLA_NOPE + MLA_ROPE)[:, :, MLA_NOPE:].reshape(Q_LORA, MLA_HEADS, 2, -1)
    wq_rot = jnp.stack([-wq_rope[:, :, 1], wq_rope[:, :, 0]], axis=2).reshape(Q_LORA, -1)
    w_uq_p = jnp.concatenate([_pad_heads(w_uq, MLA_NOPE + MLA_ROPE, 0),
                              _pad_heads(wq_rot, MLA_ROPE, MLA_NOPE)], axis=1).astype(BF16)
    wkv3 = w_ukv.reshape(KV_LORA, MLA_HEADS, MLA_NOPE + MLA_V)
    w_uk_p = _pad_heads(wkv3[:, :, :MLA_NOPE].reshape(KV_LORA, -1), MLA_NOPE, 0).astype(BF16)
    w_uv = wkv3[:, :, MLA_NOPE:].reshape(KV_LORA, -1).astype(BF16)
    half_r, half_m = RET_DK // 2, MLA_ROPE // 2
    f_r = ROPE_BASE ** (-(jnp.arange(half_r, dtype=F32) / half_r))
    f_m = ROPE_BASE ** (-(jnp.arange(half_m, dtype=F32) / half_m))
    invf = jnp.concatenate([f_r, f_r, f_m, f_m, jnp.zeros((LANES - 2 * half_r - 2 * half_m,), F32)]).reshape(1, LANES)

    hq = MLA_HEADS * LANES
    const = lambda shape: pl.BlockSpec(shape, lambda b, i: (0,) * len(shape))
    tile = lambda w: pl.BlockSpec((1, tm, w), lambda b, i: (b, i, 0))
    return pl.pallas_call(
        _pre_kernel,
        grid=(bsz, s // tm),
        in_specs=[tile(d), pl.BlockSpec((1, 6, d), lambda b, i: (b, 0, 0)), tile(1), const((1, d)),
                  const((d, _IN_PERM)), const((1, Q_LORA)), const((Q_LORA, 2 * hq)),
                  const((1, KV_LORA)), const((KV_LORA, hq)), const((KV_LORA, MLA_HEADS * MLA_V)), const((1, LANES))],
        out_specs=[tile(hq), tile(hq),
                   pl.BlockSpec((1, tm // tq, MLA_HEADS * MLA_VROWS, tq), lambda b, i: (b, i, 0, 0)),
                   tile(RET_HEADS * RET_DK), tile(RET_HEADS * RET_DK),
                   tile(RET_HEADS * RET_DV), tile(RET_HEADS * RET_DV)],
        out_shape=[jax.ShapeDtypeStruct((bsz, s, hq), BF16), jax.ShapeDtypeStruct((bsz, s, hq), BF16),
                   jax.ShapeDtypeStruct((bsz, s // tq, MLA_HEADS * MLA_VROWS, tq), BF16),
                   jax.ShapeDtypeStruct((bsz, s, RET_HEADS * RET_DK), BF16),
                   jax.ShapeDtypeStruct((bsz, s, RET_HEADS * RET_DK), BF16),
                   jax.ShapeDtypeStruct((bsz, s, RET_HEADS * RET_DV), BF16),
                   jax.ShapeDtypeStruct((bsz, s, RET_HEADS * RET_DV), F32)],
        compiler_params=_params("parallel", "arbitrary"),
        name="pre_mixer",
    )(x, mod3, pos3, norm1_g.reshape(1, d), w_in_p, q_norm_g.reshape(1, -1), w_uq_p,
      kv_norm_g.reshape(1, -1), w_uk_p, w_uv, invf)


def _attn_kernel(q_ref, k_ref, vt_ref, o_ref, st_x, st_y, bm_x, bm_y, m_scr, acc_scr, *, tq, kb, hps):
    qi = pl.program_id(1)
    assert tq == 2 * kb, "a query tile spans two key blocks: the last two blocks of a tile are masked"
    key = lax.broadcasted_iota(jnp.int32, (kb, tq), 0)
    qry = lax.broadcasted_iota(jnp.int32, (kb, tq), 1)
    hsl = [slice(hh * LANES, (hh + 1) * LANES) for hh in range(hps)]
    vsl = [slice(hh * MLA_VROWS, (hh + 1) * MLA_VROWS) for hh in range(hps)]
    qs = [q_ref[0, :, hs] for hs in hsl]
    bufs = {"x": (st_x, bm_x), "y": (st_y, bm_y)}
    n_blk = 2 * qi + 2

    def scores(blk, buf, diag=None):
        st_ref, bm_ref = bufs[buf]
        start = pl.multiple_of(blk * kb, kb)
        for hh in range(hps):
            if diag == 1:
                st = _mm_nt(k_ref[0, pl.ds(start, kb), hsl[hh]], q_ref[0, kb:, hsl[hh]])
                st = jnp.where(lax.broadcasted_iota(jnp.int32, (kb, kb), 0) <= lax.broadcasted_iota(jnp.int32, (kb, kb), 1),
                               st, NEG)
                st = jnp.concatenate([jnp.full((kb, kb), NEG, F32), st], axis=1)
                st_ref[hh] = st
                bm_ref[hh, 0:1, :] = jnp.max(st, axis=0, keepdims=True)
                continue
            st = _mm_nt(k_ref[0, pl.ds(start, kb), hsl[hh]], qs[hh])
            if diag is not None:
                st = jnp.where(key <= qry, st, NEG)
            st_ref[hh] = st
            bm_ref[hh, 0:1, :] = jnp.max(st, axis=0, keepdims=True)

    def update(blk, buf):
        st_ref, bm_ref = bufs[buf]
        for hh in range(hps):
            m = m_scr[hh, 0:1, :]
            m_new = jnp.maximum(m, bm_ref[hh, 0:1, :])
            p = jnp.exp2(st_ref[hh] - m_new).astype(BF16)
            acc_scr[hh] = jnp.exp2(m - m_new) * acc_scr[hh] + _mm(vt_ref[0, blk, vsl[hh], :], p)
            m_scr[hh, 0:1, :] = m_new

    m_scr[...] = jnp.full(m_scr.shape, NEG, F32)
    acc_scr[...] = jnp.zeros(acc_scr.shape, F32)

    @pl.when(qi >= 1)
    def _():
        scores(0, "x")

    def steady(base, pairs):
        for r in range(pairs):
            scores(base + 2 * r + 1, "y")
            update(base + 2 * r, "x")
            scores(base + 2 * r + 2, "x")
            update(base + 2 * r + 1, "y")

    n_pairs = jnp.maximum(qi - 1, 0)
    n_long = n_pairs // ATTN_PAIRS_PER_ITER

    @pl.loop(0, n_long)
    def _(i):
        steady(2 * ATTN_PAIRS_PER_ITER * i, ATTN_PAIRS_PER_ITER)

    @pl.loop(n_long * ATTN_PAIRS_PER_ITER, n_pairs)
    def _(i):
        steady(2 * i, 1)

    @pl.when(qi >= 1)
    def _():
        scores(n_blk - 3, "y")
        update(n_blk - 4, "x")
        scores(n_blk - 2, "x", diag=0)
        update(n_blk - 3, "y")
        scores(n_blk - 1, "y", diag=1)
        update(n_blk - 2, "x")
        update(n_blk - 1, "y")

    @pl.when(qi == 0)
    def _():
        scores(0, "x", diag=0)
        scores(1, "y", diag=1)
        update(0, "x")
        update(1, "y")

    out_t = jnp.concatenate([acc_scr[hh, 0:MLA_V, :] * (1.0 / acc_scr[hh, MLA_V:MLA_V + 1, :]) for hh in range(hps)],
                            axis=0)
    o_ref[0] = out_t.T.astype(BF16)


def _attention(q, k, vt, kb, hps):
    bsz, s, _ = q.shape
    groups = MLA_HEADS // hps
    tq = 2 * kb
    assert vt.shape == (bsz, s // kb, MLA_HEADS * MLA_VROWS, kb)
    return pl.pallas_call(
        functools.partial(_attn_kernel, tq=tq, kb=kb, hps=hps),
        grid=(bsz * groups, s // tq),
        in_specs=[pl.BlockSpec((1, tq, hps * LANES), lambda g, i: (g // groups, i, g % groups)),
                  pl.BlockSpec((1, s, hps * LANES), lambda g, i: (g // groups, 0, g % groups)),
                  pl.BlockSpec((1, s // kb, hps * MLA_VROWS, kb), lambda g, i: (g // groups, 0, g % groups, 0))],
        out_specs=pl.BlockSpec((1, tq, hps * MLA_V), lambda g, i: (g // groups, i, g % groups)),
        out_shape=jax.ShapeDtypeStruct((bsz, s, MLA_HEADS * MLA_V), BF16),
        scratch_shapes=[pltpu.VMEM((hps, kb, tq), F32), pltpu.VMEM((hps, kb, tq), F32),
                        pltpu.VMEM((hps, 8, tq), F32), pltpu.VMEM((hps, 8, tq), F32),
                        pltpu.VMEM((hps, 8, tq), F32), pltpu.VMEM((hps, MLA_VROWS, tq), F32)],
        compiler_params=_params("parallel", "arbitrary"),
        name="mla_attention",
    )(q, k, vt)


def _retention_tile(rq_ref, rk_ref, rv_ref, rg_ref, dm_ref, xi_ref, zt_ref, dc_ref, o_ref, st_ref, *, ts):
    @pl.when(pl.program_id(1) == 0)
    def _():
        st_ref[...] = jnp.zeros_like(st_ref)

    lane = lax.broadcasted_iota(jnp.int32, (RET_CHUNK, LANES), 1)
    row = lax.broadcasted_iota(jnp.int32, (LANES, RET_DV), 0)
    hpt = LANES // RET_DK
    n_chunks = ts // RET_CHUNK
    tiles = [(c, p) for c in range(n_chunks) for p in range(RET_HEADS // hpt)]
    units = [(c, hd) for c in range(n_chunks) for hd in range(RET_HEADS)]
    rows = lambda c: slice(c * RET_CHUNK, (c + 1) * RET_CHUNK)
    vsl = lambda hd: slice(hd * RET_DV, (hd + 1) * RET_DV)
    mine = [(lane >= sub * RET_DK) & (lane < (sub + 1) * RET_DK) for sub in range(hpt)]
    mine_row = [(row >= sub * RET_DK) & (row < (sub + 1) * RET_DK) for sub in range(hpt)]

    q2 = {(c, p): rq_ref[0, rows(c), p * LANES:(p + 1) * LANES] for c, p in tiles}
    k2 = {(c, p): rk_ref[0, rows(c), p * LANES:(p + 1) * LANES] for c, p in tiles}
    vh = {(c, hd): rv_ref[0, rows(c), vsl(hd)] for c, hd in units}
    sc, un = {}, {}
    for c, p in tiles:
        heads = [p * hpt + sub for sub in range(hpt)]
        kcat = jnp.concatenate([jnp.where(mine[sub], k2[(c, p)], 0.0).astype(BF16) for sub in range(hpt)], axis=0)
        s_all = _mm_nt(q2[(c, p)], kcat)
        kz = jnp.zeros((RET_CHUNK, LANES), F32)
        for sub, hd in enumerate(heads):
            sc[(c, hd)] = s_all[:, sub * RET_CHUNK:(sub + 1) * RET_CHUNK]
            kz = kz + jnp.where(mine[sub], k2[(c, p)].astype(F32) * zt_ref[hd], 0.0)
        u_all = _mm(kz.astype(BF16).T, jnp.concatenate([vh[(c, hd)] for hd in heads], axis=1))
        for sub, hd in enumerate(heads):
            un[(c, hd)] = jnp.where(mine_row[sub], u_all[:, sub * RET_DV:(sub + 1) * RET_DV], 0.0)
    prev = {}
    for hd in range(RET_HEADS):
        st = st_ref[hd]
        for c in range(n_chunks):
            prev[(c, hd)] = st.astype(BF16)
            st = st * dc_ref[hd] + un[(c, hd)]
        st_ref[hd] = st
    for c, hd in units:
        u = (c, hd)
        lhs = jnp.concatenate([(sc[u] * dm_ref[hd]).astype(BF16),
                               (q2[(c, hd // hpt)].astype(F32) * xi_ref[hd]).astype(BF16)], axis=1)
        o = _mm(lhs, jnp.concatenate([vh[u], prev[u]], axis=0))
        mu = jnp.mean(o, axis=-1, keepdims=True)
        oc = o - mu
        on = oc * lax.rsqrt(jnp.mean(oc * oc, axis=-1, keepdims=True) + NORM_EPS)
        o_ref[rows(c), vsl(hd)] = (_silu(rg_ref[0, rows(c), vsl(hd)]) * on).astype(BF16)


def _retention_tables():
    c = RET_CHUNK
    gamma = 1.0 - jnp.power(2.0, -5.0 - jnp.arange(RET_HEADS, dtype=F32))
    log_g = jnp.log(gamma)
    idx = jnp.arange(c, dtype=F32)
    diff = idx[:, None] - idx[None, :]
    dmask = jnp.where(diff[None] >= 0, jnp.exp(jnp.maximum(diff, 0.0)[None] * log_g[:, None, None]), 0.0)
    zeta = jnp.exp((c - 1.0 - idx)[None, :] * log_g[:, None])
    xi = jnp.exp((idx + 1.0)[None, :] * log_g[:, None])
    decay = jnp.exp(c * log_g)
    xi_b = jnp.broadcast_to(xi[:, :, None], (RET_HEADS, c, LANES))
    zt_b = jnp.broadcast_to(zeta[:, :, None], (RET_HEADS, c, LANES))
    dc_b = jnp.broadcast_to(decay[:, None, None], (RET_HEADS, LANES, RET_DV))
    return dmask, xi_b, zt_b, dc_b


def _post_kernel(x_ref, om_ref, rq_ref, rk_ref, rv_ref, rg_ref, dm_ref, xi_ref, zt_ref, dc_ref,
                 mod_ref, wo_ref, g2_ref, wr_ref, br_ref,
                 x1_ref, h2_ref, ri_ref, rit_ref, cnt_ref, or_scr, st_ref):
    _retention_tile(rq_ref, rk_ref, rv_ref, rg_ref, dm_ref, xi_ref, zt_ref, dc_ref, or_scr, st_ref,
                    ts=x_ref.shape[1])
    half = om_ref.shape[-1]
    subs = [slice(i * MOE_TILE, (i + 1) * MOE_TILE) for i in range(x_ref.shape[1] // MOE_TILE)]
    mixes = [_mm(om_ref[0, rs, :], wo_ref[0:half, :]) + _mm(or_scr[rs, :], wo_ref[half:, :]) for rs in subs]

    lgs = []
    for rs, mix in zip(subs, mixes):
        x1 = x_ref[0, rs, :] + mod_ref[0, 2:3, :] * mix
        x1_ref[0, rs, :] = x1
        h2 = _rms(x1, g2_ref[...]) * (1.0 + mod_ref[0, 4:5, :]) + mod_ref[0, 3:4, :]
        hi = h2.astype(BF16)
        h2_ref[0, rs, :] = hi
        lo = (h2 - hi.astype(F32)).astype(BF16)
        both = _mm(hi, wr_ref[...])
        lgs.append(both[:, 0:LANES] + both[:, LANES:2 * LANES] + _mm(lo, wr_ref[:, 0:LANES]) + br_ref[...])

    lane = lax.broadcasted_iota(jnp.int32, (MOE_TILE, LANES), 1)
    big = jnp.int32(1 << 20)
    gmask = lane < N_GROUPS
    el = lane - N_GROUPS
    assert EXPERTS_PER_GROUP == 8
    routed = []
    for lg in lgs:
        gmax = jnp.max(jnp.where(gmask, lg, NEG), axis=-1, keepdims=True)
        ge = jnp.where(gmask, jnp.exp(lg - gmax), 0.0)
        pg = ge / jnp.sum(ge, axis=-1, keepdims=True)
        p_top = jnp.max(pg, axis=-1, keepdims=True)
        g_top = jnp.min(jnp.where(gmask & (pg == p_top), lane, big), axis=-1, keepdims=True)

        emask = (el >= 0) & (el < N_EXPERTS) & (lax.shift_right_arithmetic(el, 3) == g_top)
        ev = jnp.where(emask, lg, NEG)
        v1 = jnp.max(ev, axis=-1, keepdims=True)
        i1 = jnp.min(jnp.where(emask & (ev == v1), lane, big), axis=-1, keepdims=True)
        emask2 = emask & (lane != i1)
        ev2 = jnp.where(emask2, lg, NEG)
        v2 = jnp.max(ev2, axis=-1, keepdims=True)
        i2 = jnp.min(jnp.where(emask2 & (ev2 == v2), lane, big), axis=-1, keepdims=True)
        e = jnp.exp(v2 - v1)
        den = 1.0 + e
        routed.append((i1, i2, (1.0 / den) * p_top, (e / den) * p_top))

    r_io = lax.broadcasted_iota(jnp.int32, (MOE_TILE, MOE_TILE), 0)
    c_io = lax.broadcasted_iota(jnp.int32, (MOE_TILE, MOE_TILE), 1)
    earlier_tok = (c_io < r_io).astype(BF16)
    lr_io = lax.broadcasted_iota(jnp.int32, (LANES, LANES), 0)
    lc_io = lax.broadcasted_iota(jnp.int32, (LANES, LANES), 1)
    earlier_lane = (lr_io < lc_io).astype(BF16)
    for hf, (rs, (i1, i2, w1, w2)) in enumerate(zip(subs, routed)):
        oh = [lane == i1, lane == i2]
        cnt = (oh[0] | oh[1]).astype(BF16)
        excl = _mm(earlier_tok, cnt)
        n = jnp.sum(cnt.astype(F32), axis=0, keepdims=True)
        npad = jnp.floor((n + (MOE_CHUNK - 1)) * (1.0 / MOE_CHUNK)) * MOE_CHUNK
        loff = _mm(jnp.broadcast_to(npad, (8, LANES)).astype(BF16), earlier_lane)
        pos = excl + loff[0:1, :]
        cnt_ref[hf] = jnp.broadcast_to(n, (8, LANES))
        lp0, lp1 = [jnp.sum(jnp.where(o, pos, 0.0), axis=-1, keepdims=True) for o in oh]
        cols = [(i1 - N_GROUPS).astype(F32), (i2 - N_GROUPS).astype(F32), w1, w2, lp0, lp1]
        ri = jnp.zeros((MOE_TILE, LANES), F32)
        for j, col in enumerate(cols):
            ri = jnp.where(lane == j, col, ri)
        ri_ref[0, rs, :] = ri
        rit_ref[:, rs] = ri.T


def _post(x, o_mla, rq, rk, rv, rg, mod3, w_o, norm2_g, w_gr, b_gr, w_er, b_er, tm):
    bsz, s, d = x.shape
    ret_tables = _retention_tables()
    c = RET_CHUNK
    w_r = jnp.concatenate([w_gr, w_er.reshape(d, N_EXPERTS), jnp.zeros((d, LANES - N_GROUPS - N_EXPERTS), F32)], axis=1)
    w_rh = w_r.astype(BF16)
    w_rl = (w_r - w_rh.astype(F32)).astype(BF16)
    w_r2 = jnp.concatenate([w_rh, w_rl], axis=1)
    b_r =jnp.concatenate([b_gr, b_er.reshape(-1), jnp.zeros((LANES - N_GROUPS - N_EXPERTS,), F32)]).reshape(1, LANES)
    tile = lambda w: pl.BlockSpec((1, tm, w), lambda b, i: (b, i, 0))
    const = lambda shape: pl.BlockSpec(shape, lambda b, i: (0,) * len(shape))
    per_b = s // tm
    sub = tm // MOE_TILE
    return pl.pallas_call(
        _post_kernel,
        grid=(bsz, per_b),
        in_specs=[tile(d), tile(o_mla.shape[-1]),
                  tile(RET_HEADS * RET_DK), tile(RET_HEADS * RET_DK), tile(RET_HEADS * RET_DV), tile(RET_HEADS * RET_DV),
                  const((RET_HEADS, c, c)), const((RET_HEADS, c, LANES)), const((RET_HEADS, c, LANES)),
                  const((RET_HEADS, LANES, RET_DV)),
                  pl.BlockSpec((1, 6, d), lambda b, i: (b, 0, 0)),
                  const((d, d)), const((1, d)), const((d, 2 * LANES)), const((1, LANES))],
        out_specs=[tile(d), tile(d), tile(LANES),
                   pl.BlockSpec((LANES, tm), lambda b, i: (0, b * per_b + i)),
                   pl.BlockSpec((sub, 8, LANES), lambda b, i: (b * per_b + i, 0, 0))],
        out_shape=[jax.ShapeDtypeStruct((bsz, s, d), F32), jax.ShapeDtypeStruct((bsz, s, d), BF16),
                   jax.ShapeDtypeStruct((bsz, s, LANES), F32),
                   jax.ShapeDtypeStruct((LANES, bsz * s), F32),
                   jax.ShapeDtypeStruct((bsz * s // MOE_TILE, 8, LANES), F32)],
        scratch_shapes=[pltpu.VMEM((tm, RET_HEADS * RET_DV), BF16), pltpu.VMEM((RET_HEADS, LANES, RET_DV), F32)],
        compiler_params=_params("parallel", "arbitrary"),
        name="post_mixer",
    )(x, o_mla, rq, rk, rv, rg, *ret_tables, mod3, w_o.astype(BF16), norm2_g.reshape(1, d), w_r2, b_r)


def _chunk_rows(c, n=1):
    return pl.ds(pl.multiple_of(c * MOE_CHUNK, MOE_CHUNK), n * MOE_CHUNK)


def _start_tile_copies(copies, t, make):
    psrc, pdst, osrc, odst, npair, nodd = copies
    base = t * MOE_LIST
    lax.fori_loop(0, npair[t], lambda j, z: (make(psrc[base + j], pdst[base + j], 2).start(), z)[1], 0)
    lax.fori_loop(0, nodd[t], lambda j, z: (make(osrc[base + j], odst[base + j], 1).start(), z)[1], 0)


def _dispatch_kernel(psrc_ref, pdst_ref, osrc_ref, odst_ref, npair_ref, nodd_ref, nchk_ref, tstart_ref, tn_ref,
                     nbr_ref, h2_ref, ri_ref, rit_ref, xs_hbm, xloc, zblk, sem, zsem, bsem, *, nsteps, nb):
    i = pl.program_id(0)
    slot = lax.rem(i, 2)
    d = h2_ref.shape[-1]

    def zero_copy(e, c):
        return pltpu.make_async_copy(zblk.at[pl.ds(0, MOE_CHUNK)], xs_hbm.at[_chunk_rows(tstart_ref[e] + c)], zsem)

    def zero_block(j):
        rows = pl.ds(pl.multiple_of(j * MOE_BLOCK, MOE_BLOCK), MOE_BLOCK)
        return pltpu.make_async_copy(zblk, xs_hbm.at[rows], bsem)

    copies = (psrc_ref, pdst_ref, osrc_ref, odst_ref, npair_ref, nodd_ref)

    def wait_step(step, sl):
        for u in range(MOE_TPS):
            _wait_chunks(nchk_ref[step * MOE_TPS + u],
                         lambda rows: pltpu.make_async_copy(xloc.at[sl].at[0].at[rows], xs_hbm.at[rows], sem.at[sl]))

    @pl.when(i == 0)
    def _():
        zblk[...] = jnp.zeros_like(zblk)
        lax.fori_loop(nbr_ref[0], nb, lambda j, z: (zero_block(j).start(), z)[1], 0)
        for e in range(N_EXPERTS):
            lax.fori_loop(0, tn_ref[e], lambda c, z, e=e: (zero_copy(e, c).start(), z)[1], 0)
        for e in range(N_EXPERTS):
            lax.fori_loop(0, tn_ref[e], lambda c, z, e=e: (zero_copy(e, c).wait(), z)[1], 0)

    @pl.when(i >= 2)
    def _():
        wait_step(i - 2, slot)

    s_io = lax.broadcasted_iota(jnp.int32, (MOE_SLAB, MOE_TILE), 0).astype(F32)
    lane = lax.broadcasted_iota(jnp.int32, (MOE_TILE, LANES), 1)
    tiles = [slice(u * MOE_TILE, (u + 1) * MOE_TILE) for u in range(MOE_TPS)]
    pms = [((s_io == rit_ref[4:5, ts]) | (s_io == rit_ref[5:6, ts])).astype(BF16) for ts in tiles]
    exts = []
    for ts in tiles:
        ext = jnp.zeros((MOE_TILE, LANES), F32)
        for k in range(TOP_K):
            w = ri_ref[ts, TOP_K + k:TOP_K + k + 1]
            hi = w.astype(BF16).astype(F32)
            ext = jnp.where(lane == 2 * k, hi, jnp.where(lane == 2 * k + 1, w - hi, ext))
            ext = jnp.where(lane == 2 * TOP_K + k, ri_ref[ts, k:k + 1], ext)
        exts.append(ext.astype(BF16))
    mains = [_mm(pm, jnp.concatenate([h2_ref[ts, :], ext], axis=1)) for pm, ts, ext in zip(pms, tiles, exts)]
    for u in range(MOE_TPS):
        xloc[slot, u] = mains[u].astype(BF16)

    for u in range(MOE_TPS):
        t = i * MOE_TPS + u
        _start_tile_copies(copies, t, lambda src, dst, n, u=u: pltpu.make_async_copy(
            xloc.at[slot].at[u].at[_chunk_rows(src, n)], xs_hbm.at[_chunk_rows(dst, n)], sem.at[slot]))

    @pl.when(i == nsteps - 1)
    def _():
        wait_step(i, slot)
        if nsteps >= 2:
            wait_step(i - 1, 1 - slot)
        lax.fori_loop(nbr_ref[0], nb, lambda j, z: (zero_block(j).wait(), z)[1], 0)


def _wait_chunks(n, copy_of_rows):
    bit = MOE_TILE_CHUNKS
    while bit >= 1:
        @pl.when((n & bit) != 0)
        def _(bit=bit):
            copy_of_rows(pl.ds(0, bit * MOE_CHUNK)).wait()
        bit //= 2


def _dispatch(h2, ri, rit, tables, n_rows):
    t, d = h2.shape
    nt = t // MOE_TILE
    assert nt % MOE_TPS == 0 and MOE_SLAB >= TOP_K * MOE_TILE + N_EXPERTS * (MOE_CHUNK - 1)
    rows = MOE_TPS * MOE_TILE
    grid_spec = pltpu.PrefetchScalarGridSpec(
        num_scalar_prefetch=10,
        grid=(nt // MOE_TPS,),
        in_specs=[pl.BlockSpec((rows, d), lambda i, *_: (i, 0)),
                  pl.BlockSpec((rows, LANES), lambda i, *_: (i, 0)),
                  pl.BlockSpec((8, rows), lambda i, *_: (0, i))],
        out_specs=pl.BlockSpec(memory_space=pl.ANY),
        scratch_shapes=[pltpu.VMEM((2, MOE_TPS, MOE_SLAB, d + LANES), BF16), pltpu.VMEM((MOE_BLOCK, d + LANES), BF16),
                        pltpu.SemaphoreType.DMA((2,)), pltpu.SemaphoreType.DMA(()), pltpu.SemaphoreType.DMA(())])
    return pl.pallas_call(
        functools.partial(_dispatch_kernel, nsteps=nt // MOE_TPS, nb=n_rows // MOE_BLOCK),
        grid_spec=grid_spec,
        out_shape=jax.ShapeDtypeStruct((n_rows, d + LANES), BF16),
        compiler_params=_params("arbitrary"),
        name="moe_dispatch",
    )(*tables["copies"], tables["tail_start"], tables["tail_n"], tables["n_blocks"],
      h2, ri, rit)


def _expert_kernel(be_ref, nbr_ref, xs_ref, w1_ref, w3_ref, w2_ref, y_ref, w13b, w2b):
    i = pl.program_id(0)
    used = i < nbr_ref[0]

    @pl.when(jnp.logical_not(used))
    def _():
        y_ref[...] = jnp.zeros_like(y_ref)

    @pl.when(used & ((i == 0) | (be_ref[i] != be_ref[jnp.maximum(i - 1, 0)])))
    def _():
        w13b[:, 0:D_EXPERT] = w1_ref[0].astype(BF16)
        w13b[:, D_EXPERT:] = w3_ref[0].astype(BF16)
        w2b[...] = w2_ref[0].astype(BF16)

    @pl.when(used)
    def _():
        d = y_ref.shape[-1]
        subs = [slice(j * MOE_TILE, (j + 1) * MOE_TILE) for j in range(y_ref.shape[0] // MOE_TILE)]
        ab = [_mm(xs_ref[rs, 0:d], w13b[...]) for rs in subs]
        hms = [(_silu(v[:, 0:D_EXPERT]) * v[:, D_EXPERT:]).astype(BF16) for v in ab]
        expert = be_ref[i].astype(F32)
        for rs, hm in zip(subs, hms):
            gw = xs_ref[rs, d:d + LANES].astype(F32)
            gate = jnp.where(gw[:, 2 * TOP_K:2 * TOP_K + 1] == expert, gw[:, 0:1] + gw[:, 1:2], gw[:, 2:3] + gw[:, 3:4])
            y_ref[rs, :] = (_mm(hm, w2b[...]) * gate).astype(BF16)


def _experts(xs, tables, w1, w3, w2):
    n_rows, dw = xs.shape
    d = dw - LANES
    nb = n_rows // MOE_BLOCK
    blk = lambda i, be, nbr: jnp.minimum(i, nbr[0] - 1)
    grid_spec = pltpu.PrefetchScalarGridSpec(
        num_scalar_prefetch=2,
        grid=(nb,),
        in_specs=[pl.BlockSpec((MOE_BLOCK, dw), lambda i, be, nbr: (blk(i, be, nbr), 0)),
                  pl.BlockSpec((1, d, D_EXPERT), lambda i, be, nbr: (be[blk(i, be, nbr)], 0, 0)),
                  pl.BlockSpec((1, d, D_EXPERT), lambda i, be, nbr: (be[blk(i, be, nbr)], 0, 0)),
                  pl.BlockSpec((1, D_EXPERT, d), lambda i, be, nbr: (be[blk(i, be, nbr)], 0, 0))],
        out_specs=pl.BlockSpec((MOE_BLOCK, d), lambda i, be, nbr: (i, 0)),
        scratch_shapes=[pltpu.VMEM((d, 2 * D_EXPERT), BF16), pltpu.VMEM((D_EXPERT, d), BF16)])
    return pl.pallas_call(
        _expert_kernel,
        grid_spec=grid_spec,
        out_shape=jax.ShapeDtypeStruct((n_rows, d), BF16),
        compiler_params=_params("arbitrary"),
        name="moe_experts",
    )(tables["blk_expert"], tables["n_blocks"], xs, w1, w3, w2)


def _moe_rows(n_tiles):
    worst = n_tiles * (TOP_K * MOE_TILE + N_EXPERTS * (MOE_CHUNK - 1)) + N_EXPERTS * (MOE_BLOCK - MOE_CHUNK)
    return -(-worst // MOE_BLOCK) * MOE_BLOCK


def _moe_tables(cnt, n_rows):
    n_tiles = cnt.shape[0]
    per_blk = MOE_BLOCK // MOE_CHUNK

    def excl_cumsum(a, axis):
        n = a.shape[axis]
        lower = jnp.arange(n)[:, None] > jnp.arange(n)[None, :]
        if axis == 0:
            return jnp.sum(jnp.where(lower[:, :, None], a[None, :, :], 0), axis=1)
        return jnp.sum(jnp.where(lower[None, :, :], a[:, None, :], 0), axis=2)

    nch = (cnt + MOE_CHUNK - 1) // MOE_CHUNK
    loff = excl_cumsum(nch, 1)
    seg = jnp.sum(nch, axis=0)
    blocks = (seg + per_blk - 1) // per_blk
    bstart = excl_cumsum(blocks[None, :], 1)[0]
    bend = bstart + blocks
    estart = bstart * per_blk
    gbase = estart[None, :] + excl_cumsum(nch, 0)
    pos = jnp.arange(MOE_LIST, dtype=jnp.int32)
    experts = jnp.arange(N_EXPERTS, dtype=jnp.int32)

    def copy_list(n_te, first, stride):
        start = excl_cumsum(n_te, 1)
        owner = jnp.sum((pos[None, :, None] >= (start + n_te)[:, None, :]).astype(jnp.int32), axis=-1)
        mine = jnp.minimum(owner, N_EXPERTS - 1)[:, :, None] == experts[None, None, :]
        pick = lambda a: jnp.sum(jnp.where(mine, a[:, None, :], 0), axis=-1)
        off = pick(first) + stride * (pos[None, :] - pick(start))
        dst = jnp.clip(pick(gbase) + off, 0, n_rows // MOE_CHUNK - 2)
        return (pick(loff) + off).reshape(-1).astype(jnp.int32), dst.reshape(-1).astype(jnp.int32)

    pairs = nch // 2
    odd = nch - 2 * pairs
    psrc, pdst = copy_list(pairs, jnp.zeros_like(nch), 2)
    osrc, odst = copy_list(odd, 2 * pairs, 1)
    nb = n_rows // MOE_BLOCK
    blk_expert = jnp.sum((jnp.arange(nb, dtype=jnp.int32)[:, None] >= bend[None, :]).astype(jnp.int32), axis=1)
    blk_expert = jnp.minimum(blk_expert, N_EXPERTS - 1)
    return dict(copies=(psrc, pdst, osrc, odst, jnp.sum(pairs, axis=1).astype(jnp.int32),
                        jnp.sum(odd, axis=1).astype(jnp.int32), jnp.sum(nch, axis=1).astype(jnp.int32)),
                tail_start=(estart + seg).astype(jnp.int32), tail_n=(blocks * per_blk - seg).astype(jnp.int32),
                blk_expert=blk_expert.astype(jnp.int32), n_blocks=bend[-1:].astype(jnp.int32))


def _final_kernel(psrc_ref, pdst_ref, osrc_ref, odst_ref, npair_ref, nodd_ref, nchk_ref,
                  x1_ref, ri_ref, mod_ref, g_ref, y_hbm, o_ref, yloc, sem, *, nsteps):
    i = pl.program_id(0)
    slot = lax.rem(i, 2)

    copies = (psrc_ref, pdst_ref, osrc_ref, odst_ref, npair_ref, nodd_ref)

    def gather(step, sl):
        for u in range(MOE_TPS):
            _start_tile_copies(copies, step * MOE_TPS + u, lambda src, dst, n, u=u: pltpu.make_async_copy(
                y_hbm.at[_chunk_rows(dst, n)], yloc.at[sl].at[u].at[_chunk_rows(src, n)], sem.at[sl]))

    @pl.when(i == 0)
    def _():
        yloc[...] = jnp.zeros_like(yloc)
        gather(0, 0)

    @pl.when(i + 1 < nsteps)
    def _():
        gather(i + 1, 1 - slot)

    for u in range(MOE_TPS):
        _wait_chunks(nchk_ref[i * MOE_TPS + u],
                     lambda rows: pltpu.make_async_copy(y_hbm.at[rows], yloc.at[slot].at[0].at[rows], sem.at[slot]))

    l_io = lax.broadcasted_iota(jnp.int32, (MOE_TILE, MOE_SLAB), 1).astype(F32)
    tiles = [slice(u * MOE_TILE, (u + 1) * MOE_TILE) for u in range(MOE_TPS)]
    picks = [((l_io == ri_ref[ts, 4:5]) | (l_io == ri_ref[ts, 5:6])).astype(BF16) for ts in tiles]
    moes = [_mm(pick, yloc[slot, u]) for u, pick in enumerate(picks)]
    for ts, moe in zip(tiles, moes):
        x2 = x1_ref[ts, :] + mod_ref[0, 5:6, :] * moe
        o_ref[ts, :] = _rms(x2, g_ref[...])


def _final(x1, ri, y, mod3, final_g, tables, s):
    t, d = x1.shape
    rows = MOE_TPS * MOE_TILE
    nsteps = t // rows
    per_b = s // rows
    grid_spec = pltpu.PrefetchScalarGridSpec(
        num_scalar_prefetch=7,
        grid=(nsteps,),
        in_specs=[pl.BlockSpec((rows, d), lambda i, *_: (i, 0)),
                  pl.BlockSpec((rows, LANES), lambda i, *_: (i, 0)),
                  pl.BlockSpec((1, 6, d), lambda i, *_: (i // per_b, 0, 0)),
                  pl.BlockSpec((1, d), lambda i, *_: (0, 0)),
                  pl.BlockSpec(memory_space=pl.ANY)],
        out_specs=pl.BlockSpec((rows, d), lambda i, *_: (i, 0)),
        scratch_shapes=[pltpu.VMEM((2, MOE_TPS, MOE_SLAB, d), BF16), pltpu.SemaphoreType.DMA((2,))])
    return pl.pallas_call(
        functools.partial(_final_kernel, nsteps=nsteps),
        grid_spec=grid_spec,
        out_shape=jax.ShapeDtypeStruct((t, d), F32),
        compiler_params=_params("arbitrary"),
        name="moe_combine_final",
    )(*tables["copies"], x1, ri, mod3, final_g.reshape(1, d), y)


def kernel(x, c, positions, w_ada, b_ada, norm1_g, w_in, q_norm_g, w_uq, kv_norm_g, w_ukv, w_o, norm2_g,
           w_gr, b_gr, w_er, b_er, w1, w3, w2, final_g):
    bsz, s, d = x.shape
    assert w_ada.shape[0] == 1, "one layer"
    tm = min(1024, s)
    tq = min(256, s)
    mod3 = _adaln(c, w_ada[0], b_ada[0]).reshape(bsz, 6, d)
    pos3 = positions.astype(F32).reshape(bsz, s, 1)
    q, k, vt, rq, rk, rv, rg = _pre(x, mod3, pos3, norm1_g[0], w_in[0], q_norm_g[0], w_uq[0], kv_norm_g[0], w_ukv[0],
                                    tm, tq)
    o_mla = _attention(q, k, vt, tq, ATTN_HEADS_PER_STEP)
    x1, h2, ri, rit, cnt = _post(x, o_mla, rq, rk, rv, rg, mod3, w_o[0], norm2_g[0], w_gr[0], b_gr[0], w_er[0],
                                 b_er[0], tm)
    t = bsz * s
    n_rows = _moe_rows(t // MOE_TILE)
    counts = cnt[:, 0, N_GROUPS:N_GROUPS + N_EXPERTS].astype(jnp.int32)
    tables = _moe_tables(counts, n_rows)
    ri2 = ri.reshape(t, LANES)
    xs = _dispatch(h2.reshape(t, d), ri2, rit, tables, n_rows)
    y = _experts(xs, tables, w1[0], w3[0], w2[0])
    out = _final(x1.reshape(t, d), ri2, y, mod3, final_g, tables, s)
    return out.reshape(bsz, s, d)
```

```python
import functools

import jax
import jax.numpy as jnp
from jax import lax
from jax.experimental import pallas as pl
from jax.experimental.pallas import tpu as pltpu

MLA_HEADS = 8
MLA_NOPE = 64
MLA_ROPE = 32
MLA_V = 64
Q_LORA = 256
KV_LORA = 128
RET_HEADS = 4
RET_DK = 64
RET_DV = 128
RET_CHUNK = 128
ROPE_BASE = 10000.0
NORM_EPS = 1e-6
N_GROUPS = 4
EXPERTS_PER_GROUP = 8
N_EXPERTS = N_GROUPS * EXPERTS_PER_GROUP
TOP_K = 2
D_EXPERT = 256
MOE_TILE = 512
MOE_CHUNK = 16
MOE_SLAB = 1536
MOE_TILE_CHUNKS = MOE_SLAB // MOE_CHUNK
MOE_LIST = max(MOE_TILE_CHUNKS // 2, N_EXPERTS)
MOE_TPS = 2
MOE_BLOCK = 1024

LANES = 128
VMEM_LIMIT = 56 * 1024 * 1024

F32 = jnp.float32
BF16 = jnp.bfloat16
NEG = float(jnp.finfo(jnp.float32).min)
LOG2_E = 1.4426950408889634
ATTN_HEADS_PER_STEP = 8
ATTN_PAIRS_PER_ITER = 2
MLA_VROWS = MLA_V + 16

_C_Q = 0
_C_KV = _C_Q + Q_LORA
_C_KPE = _C_KV + KV_LORA
_C_RQ = _C_KPE + LANES
_C_RK = _C_RQ + RET_HEADS * RET_DK
_C_RV = _C_RK + RET_HEADS * RET_DK
_C_RG = _C_RV + RET_HEADS * RET_DV
_IN_PERM = _C_RG + RET_HEADS * RET_DV


def _silu(v):
    return v / (1.0 + jnp.exp(-v))


def _mm(a, b):
    return jnp.dot(a, b, preferred_element_type=F32)


def _mm_nt(a, b):
    return lax.dot_general(a, b, (((1,), (1,)), ((), ())), preferred_element_type=F32)


def _params(*sem):
    return pltpu.CompilerParams(dimension_semantics=sem, vmem_limit_bytes=VMEM_LIMIT)


def _adaln_kernel(c_ref, w_ref, b_ref, o_ref):
    a = _silu(c_ref[...]).astype(BF16)
    o_ref[...] = _mm(a, w_ref[...].astype(BF16)) + b_ref[...]


def _adaln(c, w_ada, b_ada):
    bsz, d = c.shape
    n = w_ada.shape[1]
    tn = d
    return pl.pallas_call(
        _adaln_kernel,
        grid=(n // tn,),
        in_specs=[pl.BlockSpec((bsz, d), lambda j: (0, 0)),
                  pl.BlockSpec((d, tn), lambda j: (0, j)),
                  pl.BlockSpec((1, tn), lambda j: (0, j))],
        out_specs=pl.BlockSpec((bsz, tn), lambda j: (0, j)),
        out_shape=jax.ShapeDtypeStruct((bsz, n), F32),
        compiler_params=_params("arbitrary"),
        name="adaln",
    )(c, w_ada, b_ada.reshape(1, n))


def _rms(v, g):
    return v * lax.rsqrt(jnp.mean(v * v, axis=-1, keepdims=True) + NORM_EPS) * g


def _pre_kernel(x_ref, mod_ref, pos_ref, g1_ref, win_ref, qg_ref, wuq_ref, kvg_ref, wuk_ref,
                wuv_ref, invf_ref, q_ref, k_ref, v_ref, rq_ref, rk_ref, rv_ref, rg_ref):
    tq = v_ref.shape[-1]
    subs = [slice(i * tq, (i + 1) * tq) for i in range(v_ref.shape[1])]
    sh1 = mod_ref[0, 0:1, :]
    sc1 = mod_ref[0, 1:2, :]
    projs = [_mm((_rms(x_ref[0, rs, :], g1_ref[...]) * (1.0 + sc1) + sh1).astype(BF16), win_ref[...]) for rs in subs]

    cqs = [_rms(p[:, _C_Q:_C_Q + Q_LORA], qg_ref[...]).astype(BF16) for p in projs]
    ckvs = [_rms(p[:, _C_KV:_C_KV + KV_LORA], kvg_ref[...]).astype(BF16) for p in projs]
    qas = [_mm(cq, wuq_ref[...]) for cq in cqs]
    kns = [_mm(ckv, wuk_ref[...]) for ckv in ckvs]
    vvs = [_mm(ckv, wuv_ref[...]) for ckv in ckvs]

    lane = lax.broadcasted_iota(jnp.int32, (tq, LANES), 1)
    hi = lane >= RET_DK
    half_m, half_r = MLA_ROPE // 2, RET_DK // 2
    first_m = hi & (lane < RET_DK + half_m)
    first_r = (lane & half_r) == 0
    scale = (MLA_NOPE + MLA_ROPE) ** -0.5 * LOG2_E

    def rope(v, cos, sin, first, half):
        partner = jnp.where(first, pltpu.roll(v, LANES - half, 1), pltpu.roll(v, half, 1))
        return v * cos + partner * sin

    for i, rs in enumerate(subs):
        proj = projs[i]
        ang = pos_ref[0, rs, :] * invf_ref[...]
        cs = jnp.cos(ang)
        sn = jnp.sin(ang)
        c_mla = jnp.where(hi, cs, 1.0)
        s_mla = jnp.where(hi, jnp.where(first_m, -sn, sn), 0.0)
        c_ret = jnp.where(hi, pltpu.roll(cs, RET_DK, 1), cs)
        s_ret = jnp.where(hi, pltpu.roll(sn, RET_DK, 1), sn)
        s_ret = jnp.where(first_r, -s_ret, s_ret)

        kpe = rope(proj[:, _C_KPE:_C_KPE + LANES], c_mla, s_mla, first_m, half_m)
        cq_s = c_mla * scale
        sq_s = jnp.where(hi, sn, 0.0) * scale
        for hd in range(MLA_HEADS):
            sl = slice(hd * LANES, (hd + 1) * LANES)
            sr = slice((MLA_HEADS + hd) * LANES, (MLA_HEADS + hd + 1) * LANES)
            q_ref[0, rs, sl] = (qas[i][:, sl] * cq_s + qas[i][:, sr] * sq_s).astype(BF16)
            k_ref[0, rs, sl] = (kns[i][:, sl] + kpe).astype(BF16)
        vt = vvs[i].T
        tail = jnp.where(lax.broadcasted_iota(jnp.int32, (MLA_VROWS - MLA_V, tq), 0) == 0, 1.0, 0.0)
        slab = [piece for hd in range(MLA_HEADS) for piece in (vt[hd * MLA_V:(hd + 1) * MLA_V, :], tail)]
        v_ref[0, i] = jnp.concatenate(slab, axis=0).astype(BF16)

        for j in range(RET_HEADS * RET_DK // LANES):
            o = j * LANES
            rq = rope(proj[:, _C_RQ + o:_C_RQ + o + LANES], c_ret, s_ret, first_r, half_r)
            rk = rope(proj[:, _C_RK + o:_C_RK + o + LANES], c_ret, s_ret, first_r, half_r)
            rq_ref[0, rs, o:o + LANES] = rq.astype(BF16)
            rk_ref[0, rs, o:o + LANES] = (rk * (RET_DK ** -0.5)).astype(BF16)
        rv_ref[0, rs, :] = proj[:, _C_RV:_C_RV + RET_HEADS * RET_DV].astype(BF16)
        rg_ref[0, rs, :] = proj[:, _C_RG:_C_RG + RET_HEADS * RET_DV]


def _pad_heads(w, width, left):
    k = w.shape[0]
    w3 = w.reshape(k, -1, width)
    w3 = jnp.pad(w3, ((0, 0), (0, 0), (left, LANES - left - width)))
    return w3.reshape(k, -1)


def _pre(x, mod3, pos3, norm1_g, w_in, q_norm_g, w_uq, kv_norm_g, w_ukv, tm, tq):
    bsz, s, d = x.shape
    o = 0
    parts = {}
    for name, width in (("cq", Q_LORA), ("ckv", KV_LORA), ("kr", MLA_ROPE), ("rq", RET_HEADS * RET_DK),
                        ("rk", RET_HEADS * RET_DK), ("rv", RET_HEADS * RET_DV), ("rg", RET_HEADS * RET_DV)):
        parts[name] = w_in[:, o:o + width]
        o += width
    w_in_p = jnp.concatenate([
        parts["cq"], parts["ckv"], _pad_heads(parts["kr"], MLA_ROPE, MLA_NOPE),
        parts["rq"], parts["rk"], parts["rv"], parts["rg"]], axis=1).astype(BF16)
    assert w_in_p.shape[1] == _IN_PERM
    wq_rope = w_uq.reshape(Q_LORA, MLA_HEADS, MLA_NOPE + MLA_ROPE)[:, :, MLA_NOPE:].reshape(Q_LORA, MLA_HEADS, 2, -1)
    wq_rot = jnp.stack([-wq_rope[:, :, 1], wq_rope[:, :, 0]], axis=2).reshape(Q_LORA, -1)
    w_uq_p = jnp.concatenate([_pad_heads(w_uq, MLA_NOPE + MLA_ROPE, 0),
                              _pad_heads(wq_rot, MLA_ROPE, MLA_NOPE)], axis=1).astype(BF16)
    wkv3 = w_ukv.reshape(KV_LORA, MLA_HEADS, MLA_NOPE + MLA_V)
    w_uk_p = _pad_heads(wkv3[:, :, :MLA_NOPE].reshape(KV_LORA, -1), MLA_NOPE, 0).astype(BF16)
    w_uv = wkv3[:, :, MLA_NOPE:].reshape(KV_LORA, -1).astype(BF16)
    half_r, half_m = RET_DK // 2, MLA_ROPE // 2
    f_r = ROPE_BASE ** (-(jnp.arange(half_r, dtype=F32) / half_r))
    f_m = ROPE_BASE ** (-(jnp.arange(half_m, dtype=F32) / half_m))
    invf = jnp.concatenate([f_r, f_r, f_m, f_m, jnp.zeros((LANES - 2 * half_r - 2 * half_m,), F32)]).reshape(1, LANES)

    hq = MLA_HEADS * LANES
    const = lambda shape: pl.BlockSpec(shape, lambda b, i: (0,) * len(shape))
    tile = lambda w: pl.BlockSpec((1, tm, w), lambda b, i: (b, i, 0))
    return pl.pallas_call(
        _pre_kernel,
        grid=(bsz, s // tm),
        in_specs=[tile(d), pl.BlockSpec((1, 6, d), lambda b, i: (b, 0, 0)), tile(1), const((1, d)),
                  const((d, _IN_PERM)), const((1, Q_LORA)), const((Q_LORA, 2 * hq)),
                  const((1, KV_LORA)), const((KV_LORA, hq)), const((KV_LORA, MLA_HEADS * MLA_V)), const((1, LANES))],
        out_specs=[tile(hq), tile(hq),
                   pl.BlockSpec((1, tm // tq, MLA_HEADS * MLA_VROWS, tq), lambda b, i: (b, i, 0, 0)),
                   tile(RET_HEADS * RET_DK), tile(RET_HEADS * RET_DK),
                   tile(RET_HEADS * RET_DV), tile(RET_HEADS * RET_DV)],
        out_shape=[jax.ShapeDtypeStruct((bsz, s, hq), BF16), jax.ShapeDtypeStruct((bsz, s, hq), BF16),
                   jax.ShapeDtypeStruct((bsz, s // tq, MLA_HEADS * MLA_VROWS, tq), BF16),
                   jax.ShapeDtypeStruct((bsz, s, RET_HEADS * RET_DK), BF16),
                   jax.ShapeDtypeStruct((bsz, s, RET_HEADS * RET_DK), BF16),
                   jax.ShapeDtypeStruct((bsz, s, RET_HEADS * RET_DV), BF16),
                   jax.ShapeDtypeStruct((bsz, s, RET_HEADS * RET_DV), F32)],
        compiler_params=_params("parallel", "arbitrary"),
        name="pre_mixer",
    )(x, mod3, pos3, norm1_g.reshape(1, d), w_in_p, q_norm_g.reshape(1, -1), w_uq_p,
      kv_norm_g.reshape(1, -1), w_uk_p, w_uv, invf)


def _attn_kernel(q_ref, k_ref, vt_ref, o_ref, st_x, st_y, bm_x, bm_y, m_scr, acc_scr, *, tq, kb, hps):
    qi = pl.program_id(1)
    assert tq == 2 * kb, "a query tile spans two key blocks: the last two blocks of a tile are masked"
    key = lax.broadcasted_iota(jnp.int32, (kb, tq), 0)
    qry = lax.broadcasted_iota(jnp.int32, (kb, tq), 1)
    hsl = [slice(hh * LANES, (hh + 1) * LANES) for hh in range(hps)]
    vsl = [slice(hh * MLA_VROWS, (hh + 1) * MLA_VROWS) for hh in range(hps)]
    qs = [q_ref[0, :, hs] for hs in hsl]
    bufs = {"x": (st_x, bm_x), "y": (st_y, bm_y)}
    n_blk = 2 * qi + 2

    def scores(blk, buf, diag=None):
        st_ref, bm_ref = bufs[buf]
        start = pl.multiple_of(blk * kb, kb)
        for hh in range(hps):
            if diag == 1:
                st = _mm_nt(k_ref[0, pl.ds(start, kb), hsl[hh]], q_ref[0, kb:, hsl[hh]])
                st = jnp.where(lax.broadcasted_iota(jnp.int32, (kb, kb), 0) <= lax.broadcasted_iota(jnp.int32, (kb, kb), 1),
                               st, NEG)
                st = jnp.concatenate([jnp.full((kb, kb), NEG, F32), st], axis=1)
                st_ref[hh] = st
                bm_ref[hh, 0:1, :] = jnp.max(st, axis=0, keepdims=True)
                continue
            st = _mm_nt(k_ref[0, pl.ds(start, kb), hsl[hh]], qs[hh])
            if diag is not None:
                st = jnp.where(key <= qry, st, NEG)
            st_ref[hh] = st
            bm_ref[hh, 0:1, :] = jnp.max(st, axis=0, keepdims=True)

    def update(blk, buf):
        st_ref, bm_ref = bufs[buf]
        for hh in range(hps):
            m = m_scr[hh, 0:1, :]
            m_new = jnp.maximum(m, bm_ref[hh, 0:1, :])
            p = jnp.exp2(st_ref[hh] - m_new).astype(BF16)
            acc_scr[hh] = jnp.exp2(m - m_new) * acc_scr[hh] + _mm(vt_ref[0, blk, vsl[hh], :], p)
            m_scr[hh, 0:1, :] = m_new

    m_scr[...] = jnp.full(m_scr.shape, NEG, F32)
    acc_scr[...] = jnp.zeros(acc_scr.shape, F32)

    @pl.when(qi >= 1)
    def _():
        scores(0, "x")

    def steady(base, pairs):
        for r in range(pairs):
            scores(base + 2 * r + 1, "y")
            update(base + 2 * r, "x")
            scores(base + 2 * r + 2, "x")
            update(base + 2 * r + 1, "y")

    n_pairs = jnp.maximum(qi - 1, 0)
    n_long = n_pairs // ATTN_PAIRS_PER_ITER

    @pl.loop(0, n_long)
    def _(i):
        steady(2 * ATTN_PAIRS_PER_ITER * i, ATTN_PAIRS_PER_ITER)

    @pl.loop(n_long * ATTN_PAIRS_PER_ITER, n_pairs)
    def _(i):
        steady(2 * i, 1)

    @pl.when(qi >= 1)
    def _():
        scores(n_blk - 3, "y")
        update(n_blk - 4, "x")
        scores(n_blk - 2, "x", diag=0)
        update(n_blk - 3, "y")
        scores(n_blk - 1, "y", diag=1)
        update(n_blk - 2, "x")
        update(n_blk - 1, "y")

    @pl.when(qi == 0)
    def _():
        scores(0, "x", diag=0)
        scores(1, "y", diag=1)
        update(0, "x")
        update(1, "y")

    out_t = jnp.concatenate([acc_scr[hh, 0:MLA_V, :] * (1.0 / acc_scr[hh, MLA_V:MLA_V + 1, :]) for hh in range(hps)],
                            axis=0)
    o_ref[0] = out_t.T.astype(BF16)


def _attention(q, k, vt, kb, hps):
    bsz, s, _ = q.shape
    groups = MLA_HEADS // hps
    tq = 2 * kb
    assert vt.shape == (bsz, s // kb, MLA_HEADS * MLA_VROWS, kb)
    return pl.pallas_call(
        functools.partial(_attn_kernel, tq=tq, kb=kb, hps=hps),
        grid=(bsz * groups, s // tq),
        in_specs=[pl.BlockSpec((1, tq, hps * LANES), lambda g, i: (g // groups, i, g % groups)),
                  pl.BlockSpec((1, s, hps * LANES), lambda g, i: (g // groups, 0, g % groups)),
                  pl.BlockSpec((1, s // kb, hps * MLA_VROWS, kb), lambda g, i: (g // groups, 0, g % groups, 0))],
        out_specs=pl.BlockSpec((1, tq, hps * MLA_V), lambda g, i: (g // groups, i, g % groups)),
        out_shape=jax.ShapeDtypeStruct((bsz, s, MLA_HEADS * MLA_V), BF16),
        scratch_shapes=[pltpu.VMEM((hps, kb, tq), F32), pltpu.VMEM((hps, kb, tq), F32),
                        pltpu.VMEM((hps, 8, tq), F32), pltpu.VMEM((hps, 8, tq), F32),
                        pltpu.VMEM((hps, 8, tq), F32), pltpu.VMEM((hps, MLA_VROWS, tq), F32)],
        compiler_params=_params("parallel", "arbitrary"),
        name="mla_attention",
    )(q, k, vt)


def _retention_tile(rq_ref, rk_ref, rv_ref, rg_ref, dm_ref, xi_ref, zt_ref, dc_ref, o_ref, st_ref, *, ts):
    @pl.when(pl.program_id(1) == 0)
    def _():
        st_ref[...] = jnp.zeros_like(st_ref)

    lane = lax.broadcasted_iota(jnp.int32, (RET_CHUNK, LANES), 1)
    row = lax.broadcasted_iota(jnp.int32, (LANES, RET_DV), 0)
    hpt = LANES // RET_DK
    n_chunks = ts // RET_CHUNK
    tiles = [(c, p) for c in range(n_chunks) for p in range(RET_HEADS // hpt)]
    units = [(c, hd) for c in range(n_chunks) for hd in range(RET_HEADS)]
    rows = lambda c: slice(c * RET_CHUNK, (c + 1) * RET_CHUNK)
    vsl = lambda hd: slice(hd * RET_DV, (hd + 1) * RET_DV)
    mine = [(lane >= sub * RET_DK) & (lane < (sub + 1) * RET_DK) for sub in range(hpt)]
    mine_row = [(row >= sub * RET_DK) & (row < (sub + 1) * RET_DK) for sub in range(hpt)]

    q2 = {(c, p): rq_ref[0, rows(c), p * LANES:(p + 1) * LANES] for c, p in tiles}
    k2 = {(c, p): rk_ref[0, rows(c), p * LANES:(p + 1) * LANES] for c, p in tiles}
    vh = {(c, hd): rv_ref[0, rows(c), vsl(hd)] for c, hd in units}
    sc, un = {}, {}
    for c, p in tiles:
        heads = [p * hpt + sub for sub in range(hpt)]
        kcat = jnp.concatenate([jnp.where(mine[sub], k2[(c, p)], 0.0).astype(BF16) for sub in range(hpt)], axis=0)
        s_all = _mm_nt(q2[(c, p)], kcat)
        kz = jnp.zeros((RET_CHUNK, LANES), F32)
        for sub, hd in enumerate(heads):
            sc[(c, hd)] = s_all[:, sub * RET_CHUNK:(sub + 1) * RET_CHUNK]
            kz = kz + jnp.where(mine[sub], k2[(c, p)].astype(F32) * zt_ref[hd], 0.0)
        u_all = _mm(kz.astype(BF16).T, jnp.concatenate([vh[(c, hd)] for hd in heads], axis=1))
        for sub, hd in enumerate(heads):
            un[(c, hd)] = jnp.where(mine_row[sub], u_all[:, sub * RET_DV:(sub + 1) * RET_DV], 0.0)
    prev = {}
    for hd in range(RET_HEADS):
        st = st_ref[hd]
        for c in range(n_chunks):
            prev[(c, hd)] = st.astype(BF16)
            st = st * dc_ref[hd] + un[(c, hd)]
        st_ref[hd] = st
    for c, hd in units:
        u = (c, hd)
        lhs = jnp.concatenate([(sc[u] * dm_ref[hd]).astype(BF16),
                               (q2[(c, hd // hpt)].astype(F32) * xi_ref[hd]).astype(BF16)], axis=1)
        o = _mm(lhs, jnp.concatenate([vh[u], prev[u]], axis=0))
        mu = jnp.mean(o, axis=-1, keepdims=True)
        oc = o - mu
        on = oc * lax.rsqrt(jnp.mean(oc * oc, axis=-1, keepdims=True) + NORM_EPS)
        o_ref[rows(c), vsl(hd)] = (_silu(rg_ref[0, rows(c), vsl(hd)]) * on).astype(BF16)


def _retention_tables():
    c = RET_CHUNK
    gamma = 1.0 - jnp.power(2.0, -5.0 - jnp.arange(RET_HEADS, dtype=F32))
    log_g = jnp.log(gamma)
    idx = jnp.arange(c, dtype=F32)
    diff = idx[:, None] - idx[None, :]
    dmask = jnp.where(diff[None] >= 0, jnp.exp(jnp.maximum(diff, 0.0)[None] * log_g[:, None, None]), 0.0)
    zeta = jnp.exp((c - 1.0 - idx)[None, :] * log_g[:, None])
    xi = jnp.exp((idx + 1.0)[None, :] * log_g[:, None])
    decay = jnp.exp(c * log_g)
    xi_b = jnp.broadcast_to(xi[:, :, None], (RET_HEADS, c, LANES))
    zt_b = jnp.broadcast_to(zeta[:, :, None], (RET_HEADS, c, LANES))
    dc_b = jnp.broadcast_to(decay[:, None, None], (RET_HEADS, LANES, RET_DV))
    return dmask, xi_b, zt_b, dc_b


def _post_kernel(x_ref, om_ref, rq_ref, rk_ref, rv_ref, rg_ref, dm_ref, xi_ref, zt_ref, dc_ref,
                 mod_ref, wo_ref, g2_ref, wr_ref, br_ref,
                 x1_ref, h2_ref, ri_ref, rit_ref, cnt_ref, or_scr, st_ref):
    _retention_tile(rq_ref, rk_ref, rv_ref, rg_ref, dm_ref, xi_ref, zt_ref, dc_ref, or_scr, st_ref,
                    ts=x_ref.shape[1])
    half = om_ref.shape[-1]
    subs = [slice(i * MOE_TILE, (i + 1) * MOE_TILE) for i in range(x_ref.shape[1] // MOE_TILE)]
    mixes = [_mm(om_ref[0, rs, :], wo_ref[0:half, :]) + _mm(or_scr[rs, :], wo_ref[half:, :]) for rs in subs]

    lgs = []
    for rs, mix in zip(subs, mixes):
        x1 = x_ref[0, rs, :] + mod_ref[0, 2:3, :] * mix
        x1_ref[0, rs, :] = x1
        h2 = _rms(x1, g2_ref[...]) * (1.0 + mod_ref[0, 4:5, :]) + mod_ref[0, 3:4, :]
        hi = h2.astype(BF16)
        h2_ref[0, rs, :] = hi
        lo = (h2 - hi.astype(F32)).astype(BF16)
        both = _mm(hi, wr_ref[...])
        lgs.append(both[:, 0:LANES] + both[:, LANES:2 * LANES] + _mm(lo, wr_ref[:, 0:LANES]) + br_ref[...])

    lane = lax.broadcasted_iota(jnp.int32, (MOE_TILE, LANES), 1)
    big = jnp.int32(1 << 20)
    gmask = lane < N_GROUPS
    el = lane - N_GROUPS
    assert EXPERTS_PER_GROUP == 8
    routed = []
    for lg in lgs:
        gmax = jnp.max(jnp.where(gmask, lg, NEG), axis=-1, keepdims=True)
        ge = jnp.where(gmask, jnp.exp(lg - gmax), 0.0)
        pg = ge / jnp.sum(ge, axis=-1, keepdims=True)
        p_top = jnp.max(pg, axis=-1, keepdims=True)
        g_top = jnp.min(jnp.where(gmask & (pg == p_top), lane, big), axis=-1, keepdims=True)

        emask = (el >= 0) & (el < N_EXPERTS) & (lax.shift_right_arithmetic(el, 3) == g_top)
        ev = jnp.where(emask, lg, NEG)
        v1 = jnp.max(ev, axis=-1, keepdims=True)
        i1 = jnp.min(jnp.where(emask & (ev == v1), lane, big), axis=-1, keepdims=True)
        emask2 = emask & (lane != i1)
        ev2 = jnp.where(emask2, lg, NEG)
        v2 = jnp.max(ev2, axis=-1, keepdims=True)
        i2 = jnp.min(jnp.where(emask2 & (ev2 == v2), lane, big), axis=-1, keepdims=True)
        e = jnp.exp(v2 - v1)
        den = 1.0 + e
        routed.append((i1, i2, (1.0 / den) * p_top, (e / den) * p_top))

    r_io = lax.broadcasted_iota(jnp.int32, (MOE_TILE, MOE_TILE), 0)
    c_io = lax.broadcasted_iota(jnp.int32, (MOE_TILE, MOE_TILE), 1)
    earlier_tok = (c_io < r_io).astype(BF16)
    lr_io = lax.broadcasted_iota(jnp.int32, (LANES, LANES), 0)
    lc_io = lax.broadcasted_iota(jnp.int32, (LANES, LANES), 1)
    earlier_lane = (lr_io < lc_io).astype(BF16)
    for hf, (rs, (i1, i2, w1, w2)) in enumerate(zip(subs, routed)):
        oh = [lane == i1, lane == i2]
        cnt = (oh[0] | oh[1]).astype(BF16)
        excl = _mm(earlier_tok, cnt)
        n = jnp.sum(cnt.astype(F32), axis=0, keepdims=True)
        npad = jnp.floor((n + (MOE_CHUNK - 1)) * (1.0 / MOE_CHUNK)) * MOE_CHUNK
        loff = _mm(jnp.broadcast_to(npad, (8, LANES)).astype(BF16), earlier_lane)
        pos = excl + loff[0:1, :]
        cnt_ref[hf] = jnp.broadcast_to(n, (8, LANES))
        lp0, lp1 = [jnp.sum(jnp.where(o, pos, 0.0), axis=-1, keepdims=True) for o in oh]
        cols = [(i1 - N_GROUPS).astype(F32), (i2 - N_GROUPS).astype(F32), w1, w2, lp0, lp1]
        ri = jnp.zeros((MOE_TILE, LANES), F32)
        for j, col in enumerate(cols):
            ri = jnp.where(lane == j, col, ri)
        ri_ref[0, rs, :] = ri
        rit_ref[:, rs] = ri.T


def _post(x, o_mla, rq, rk, rv, rg, mod3, w_o, norm2_g, w_gr, b_gr, w_er, b_er, tm):
    bsz, s, d = x.shape
    ret_tables = _retention_tables()
    c = RET_CHUNK
    w_r = jnp.concatenate([w_gr, w_er.reshape(d, N_EXPERTS), jnp.zeros((d, LANES - N_GROUPS - N_EXPERTS), F32)], axis=1)
    w_rh = w_r.astype(BF16)
    w_rl = (w_r - w_rh.astype(F32)).astype(BF16)
    w_r2 = jnp.concatenate([w_rh, w_rl], axis=1)
    b_r =jnp.concatenate([b_gr, b_er.reshape(-1), jnp.zeros((LANES - N_GROUPS - N_EXPERTS,), F32)]).reshape(1, LANES)
    tile = lambda w: pl.BlockSpec((1, tm, w), lambda b, i: (b, i, 0))
    const = lambda shape: pl.BlockSpec(shape, lambda b, i: (0,) * len(shape))
    per_b = s // tm
    sub = tm // MOE_TILE
    return pl.pallas_call(
        _post_kernel,
        grid=(bsz, per_b),
        in_specs=[tile(d), tile(o_mla.shape[-1]),
                  tile(RET_HEADS * RET_DK), tile(RET_HEADS * RET_DK), tile(RET_HEADS * RET_DV), tile(RET_HEADS * RET_DV),
                  const((RET_HEADS, c, c)), const((RET_HEADS, c, LANES)), const((RET_HEADS, c, LANES)),
                  const((RET_HEADS, LANES, RET_DV)),
                  pl.BlockSpec((1, 6, d), lambda b, i: (b, 0, 0)),
                  const((d, d)), const((1, d)), const((d, 2 * LANES)), const((1, LANES))],
        out_specs=[tile(d), tile(d), tile(LANES),
                   pl.BlockSpec((LANES, tm), lambda b, i: (0, b * per_b + i)),
                   pl.BlockSpec((sub, 8, LANES), lambda b, i: (b * per_b + i, 0, 0))],
        out_shape=[jax.ShapeDtypeStruct((bsz, s, d), F32), jax.ShapeDtypeStruct((bsz, s, d), BF16),
                   jax.ShapeDtypeStruct((bsz, s, LANES), F32),
                   jax.ShapeDtypeStruct((LANES, bsz * s), F32),
                   jax.ShapeDtypeStruct((bsz * s // MOE_TILE, 8, LANES), F32)],
        scratch_shapes=[pltpu.VMEM((tm, RET_HEADS * RET_DV), BF16), pltpu.VMEM((RET_HEADS, LANES, RET_DV), F32)],
        compiler_params=_params("parallel", "arbitrary"),
        name="post_mixer",
    )(x, o_mla, rq, rk, rv, rg, *ret_tables, mod3, w_o.astype(BF16), norm2_g.reshape(1, d), w_r2, b_r)


def _chunk_rows(c, n=1):
    return pl.ds(pl.multiple_of(c * MOE_CHUNK, MOE_CHUNK), n * MOE_CHUNK)


def _start_tile_copies(copies, t, make):
    psrc, pdst, osrc, odst, npair, nodd = copies
    base = t * MOE_LIST
    lax.fori_loop(0, npair[t], lambda j, z: (make(psrc[base + j], pdst[base + j], 2).start(), z)[1], 0)
    lax.fori_loop(0, nodd[t], lambda j, z: (make(osrc[base + j], odst[base + j], 1).start(), z)[1], 0)


def _dispatch_kernel(psrc_ref, pdst_ref, osrc_ref, odst_ref, npair_ref, nodd_ref, nchk_ref, tstart_ref, tn_ref,
                     nbr_ref, h2_ref, ri_ref, rit_ref, xs_hbm, xloc, zblk, sem, zsem, bsem, *, nsteps, nb):
    i = pl.program_id(0)
    slot = lax.rem(i, 2)
    d = h2_ref.shape[-1]

    def zero_copy(e, c):
        return pltpu.make_async_copy(zblk.at[pl.ds(0, MOE_CHUNK)], xs_hbm.at[_chunk_rows(tstart_ref[e] + c)], zsem)

    def zero_block(j):
        rows = pl.ds(pl.multiple_of(j * MOE_BLOCK, MOE_BLOCK), MOE_BLOCK)
        return pltpu.make_async_copy(zblk, xs_hbm.at[rows], bsem)

    copies = (psrc_ref, pdst_ref, osrc_ref, odst_ref, npair_ref, nodd_ref)

    def wait_step(step, sl):
        for u in range(MOE_TPS):
            _wait_chunks(nchk_ref[step * MOE_TPS + u],
                         lambda rows: pltpu.make_async_copy(xloc.at[sl].at[0].at[rows], xs_hbm.at[rows], sem.at[sl]))

    @pl.when(i == 0)
    def _():
        zblk[...] = jnp.zeros_like(zblk)
        lax.fori_loop(nbr_ref[0], nb, lambda j, z: (zero_block(j).start(), z)[1], 0)
        for e in range(N_EXPERTS):
            lax.fori_loop(0, tn_ref[e], lambda c, z, e=e: (zero_copy(e, c).start(), z)[1], 0)
        for e in range(N_EXPERTS):
            lax.fori_loop(0, tn_ref[e], lambda c, z, e=e: (zero_copy(e, c).wait(), z)[1], 0)

    @pl.when(i >= 2)
    def _():
        wait_step(i - 2, slot)

    s_io = lax.broadcasted_iota(jnp.int32, (MOE_SLAB, MOE_TILE), 0).astype(F32)
    lane = lax.broadcasted_iota(jnp.int32, (MOE_TILE, LANES), 1)
    tiles = [slice(u * MOE_TILE, (u + 1) * MOE_TILE) for u in range(MOE_TPS)]
    pms = [((s_io == rit_ref[4:5, ts]) | (s_io == rit_ref[5:6, ts])).astype(BF16) for ts in tiles]
    exts = []
    for ts in tiles:
        ext = jnp.zeros((MOE_TILE, LANES), F32)
        for k in range(TOP_K):
            w = ri_ref[ts, TOP_K + k:TOP_K + k + 1]
            hi = w.astype(BF16).astype(F32)
            ext = jnp.where(lane == 2 * k, hi, jnp.where(lane == 2 * k + 1, w - hi, ext))
            ext = jnp.where(lane == 2 * TOP_K + k, ri_ref[ts, k:k + 1], ext)
        exts.append(ext.astype(BF16))
    mains = [_mm(pm, jnp.concatenate([h2_ref[ts, :], ext], axis=1)) for pm, ts, ext in zip(pms, tiles, exts)]
    for u in range(MOE_TPS):
        xloc[slot, u] = mains[u].astype(BF16)

    for u in range(MOE_TPS):
        t = i * MOE_TPS + u
        _start_tile_copies(copies, t, lambda src, dst, n, u=u: pltpu.make_async_copy(
            xloc.at[slot].at[u].at[_chunk_rows(src, n)], xs_hbm.at[_chunk_rows(dst, n)], sem.at[slot]))

    @pl.when(i == nsteps - 1)
    def _():
        wait_step(i, slot)
        if nsteps >= 2:
            wait_step(i - 1, 1 - slot)
        lax.fori_loop(nbr_ref[0], nb, lambda j, z: (zero_block(j).wait(), z)[1], 0)


def _wait_chunks(n, copy_of_rows):
    bit = 1 << (MOE_TILE_CHUNKS.bit_length() - 1)
    while bit >= 1:
        @pl.when((n & bit) != 0)
        def _(bit=bit):
            copy_of_rows(pl.ds(0, bit * MOE_CHUNK)).wait()
        bit //= 2


def _dispatch(h2, ri, rit, tables, n_rows):
    t, d = h2.shape
    nt = t // MOE_TILE
    assert nt % MOE_TPS == 0 and MOE_SLAB >= TOP_K * MOE_TILE + N_EXPERTS * (MOE_CHUNK - 1)
    rows = MOE_TPS * MOE_TILE
    grid_spec = pltpu.PrefetchScalarGridSpec(
        num_scalar_prefetch=10,
        grid=(nt // MOE_TPS,),
        in_specs=[pl.BlockSpec((rows, d), lambda i, *_: (i, 0)),
                  pl.BlockSpec((rows, LANES), lambda i, *_: (i, 0)),
                  pl.BlockSpec((8, rows), lambda i, *_: (0, i))],
        out_specs=pl.BlockSpec(memory_space=pl.ANY),
        scratch_shapes=[pltpu.VMEM((2, MOE_TPS, MOE_SLAB, d + LANES), BF16), pltpu.VMEM((MOE_BLOCK, d + LANES), BF16),
                        pltpu.SemaphoreType.DMA((2,)), pltpu.SemaphoreType.DMA(()), pltpu.SemaphoreType.DMA(())])
    return pl.pallas_call(
        functools.partial(_dispatch_kernel, nsteps=nt // MOE_TPS, nb=n_rows // MOE_BLOCK),
        grid_spec=grid_spec,
        out_shape=jax.ShapeDtypeStruct((n_rows, d + LANES), BF16),
        compiler_params=_params("arbitrary"),
        name="moe_dispatch",
    )(*tables["copies"], tables["tail_start"], tables["tail_n"], tables["n_blocks"],
      h2, ri, rit)


def _expert_kernel(be_ref, nbr_ref, xs_ref, w1_ref, w3_ref, w2_ref, y_ref, w13b, w2b):
    i = pl.program_id(0)
    used = i < nbr_ref[0]

    @pl.when(jnp.logical_not(used))
    def _():
        y_ref[...] = jnp.zeros_like(y_ref)

    @pl.when(used & ((i == 0) | (be_ref[i] != be_ref[jnp.maximum(i - 1, 0)])))
    def _():
        w13b[:, 0:D_EXPERT] = w1_ref[0].astype(BF16)
        w13b[:, D_EXPERT:] = w3_ref[0].astype(BF16)
        w2b[...] = w2_ref[0].astype(BF16)

    @pl.when(used)
    def _():
        d = y_ref.shape[-1]
        subs = [slice(j * MOE_TILE, (j + 1) * MOE_TILE) for j in range(y_ref.shape[0] // MOE_TILE)]
        ab = [_mm(xs_ref[rs, 0:d], w13b[...]) for rs in subs]
        hms = [(_silu(v[:, 0:D_EXPERT]) * v[:, D_EXPERT:]).astype(BF16) for v in ab]
        expert = be_ref[i].astype(F32)
        for rs, hm in zip(subs, hms):
            gw = xs_ref[rs, d:d + LANES].astype(F32)
            gate = jnp.where(gw[:, 2 * TOP_K:2 * TOP_K + 1] == expert, gw[:, 0:1] + gw[:, 1:2], gw[:, 2:3] + gw[:, 3:4])
            y_ref[rs, :] = (_mm(hm, w2b[...]) * gate).astype(BF16)


def _experts(xs, tables, w1, w3, w2):
    n_rows, dw = xs.shape
    d = dw - LANES
    nb = n_rows // MOE_BLOCK
    blk = lambda i, be, nbr: jnp.minimum(i, nbr[0] - 1)
    grid_spec = pltpu.PrefetchScalarGridSpec(
        num_scalar_prefetch=2,
        grid=(nb,),
        in_specs=[pl.BlockSpec((MOE_BLOCK, dw), lambda i, be, nbr: (blk(i, be, nbr), 0)),
                  pl.BlockSpec((1, d, D_EXPERT), lambda i, be, nbr: (be[blk(i, be, nbr)], 0, 0)),
                  pl.BlockSpec((1, d, D_EXPERT), lambda i, be, nbr: (be[blk(i, be, nbr)], 0, 0)),
                  pl.BlockSpec((1, D_EXPERT, d), lambda i, be, nbr: (be[blk(i, be, nbr)], 0, 0))],
        out_specs=pl.BlockSpec((MOE_BLOCK, d), lambda i, be, nbr: (i, 0)),
        scratch_shapes=[pltpu.VMEM((d, 2 * D_EXPERT), BF16), pltpu.VMEM((D_EXPERT, d), BF16)])
    return pl.pallas_call(
        _expert_kernel,
        grid_spec=grid_spec,
        out_shape=jax.ShapeDtypeStruct((n_rows, d), BF16),
        compiler_params=_params("arbitrary"),
        name="moe_experts",
    )(tables["blk_expert"], tables["n_blocks"], xs, w1, w3, w2)


def _moe_rows(n_tiles):
    worst = n_tiles * (TOP_K * MOE_TILE + N_EXPERTS * (MOE_CHUNK - 1)) + N_EXPERTS * (MOE_BLOCK - MOE_CHUNK)
    return -(-worst // MOE_BLOCK) * MOE_BLOCK


def _moe_tables(cnt, n_rows):
    n_tiles = cnt.shape[0]
    per_blk = MOE_BLOCK // MOE_CHUNK

    def excl_cumsum(a, axis):
        n = a.shape[axis]
        lower = jnp.arange(n)[:, None] > jnp.arange(n)[None, :]
        if axis == 0:
            return jnp.sum(jnp.where(lower[:, :, None], a[None, :, :], 0), axis=1)
        return jnp.sum(jnp.where(lower[None, :, :], a[:, None, :], 0), axis=2)

    nch = (cnt + MOE_CHUNK - 1) // MOE_CHUNK
    loff = excl_cumsum(nch, 1)
    seg = jnp.sum(nch, axis=0)
    blocks = (seg + per_blk - 1) // per_blk
    bstart = excl_cumsum(blocks[None, :], 1)[0]
    bend = bstart + blocks
    estart = bstart * per_blk
    gbase = estart[None, :] + excl_cumsum(nch, 0)
    pos = jnp.arange(MOE_LIST, dtype=jnp.int32)
    experts = jnp.arange(N_EXPERTS, dtype=jnp.int32)

    def copy_list(n_te, first, stride):
        start = excl_cumsum(n_te, 1)
        owner = jnp.sum((pos[None, :, None] >= (start + n_te)[:, None, :]).astype(jnp.int32), axis=-1)
        mine = jnp.minimum(owner, N_EXPERTS - 1)[:, :, None] == experts[None, None, :]
        pick = lambda a: jnp.sum(jnp.where(mine, a[:, None, :], 0), axis=-1)
        off = pick(first) + stride * (pos[None, :] - pick(start))
        dst = jnp.clip(pick(gbase) + off, 0, n_rows // MOE_CHUNK - 2)
        return (pick(loff) + off).reshape(-1).astype(jnp.int32), dst.reshape(-1).astype(jnp.int32)

    pairs = nch // 2
    odd = nch - 2 * pairs
    psrc, pdst = copy_list(pairs, jnp.zeros_like(nch), 2)
    osrc, odst = copy_list(odd, 2 * pairs, 1)
    nb = n_rows // MOE_BLOCK
    blk_expert = jnp.sum((jnp.arange(nb, dtype=jnp.int32)[:, None] >= bend[None, :]).astype(jnp.int32), axis=1)
    blk_expert = jnp.minimum(blk_expert, N_EXPERTS - 1)
    return dict(copies=(psrc, pdst, osrc, odst, jnp.sum(pairs, axis=1).astype(jnp.int32),
                        jnp.sum(odd, axis=1).astype(jnp.int32), jnp.sum(nch, axis=1).astype(jnp.int32)),
                tail_start=(estart + seg).astype(jnp.int32), tail_n=(blocks * per_blk - seg).astype(jnp.int32),
                blk_expert=blk_expert.astype(jnp.int32), n_blocks=bend[-1:].astype(jnp.int32))


def _final_kernel(psrc_ref, pdst_ref, osrc_ref, odst_ref, npair_ref, nodd_ref, nchk_ref,
                  x1_ref, ri_ref, mod_ref, g_ref, y_hbm, o_ref, yloc, sem, *, nsteps):
    i = pl.program_id(0)
    slot = lax.rem(i, 2)

    copies = (psrc_ref, pdst_ref, osrc_ref, odst_ref, npair_ref, nodd_ref)

    def gather(step, sl):
        for u in range(MOE_TPS):
            _start_tile_copies(copies, step * MOE_TPS + u, lambda src, dst, n, u=u: pltpu.make_async_copy(
                y_hbm.at[_chunk_rows(dst, n)], yloc.at[sl].at[u].at[_chunk_rows(src, n)], sem.at[sl]))

    @pl.when(i == 0)
    def _():
        yloc[...] = jnp.zeros_like(yloc)
        gather(0, 0)

    @pl.when(i + 1 < nsteps)
    def _():
        gather(i + 1, 1 - slot)

    for u in range(MOE_TPS):
        _wait_chunks(nchk_ref[i * MOE_TPS + u],
                     lambda rows: pltpu.make_async_copy(y_hbm.at[rows], yloc.at[slot].at[0].at[rows], sem.at[slot]))

    l_io = lax.broadcasted_iota(jnp.int32, (MOE_TILE, MOE_SLAB), 1).astype(F32)
    tiles = [slice(u * MOE_TILE, (u + 1) * MOE_TILE) for u in range(MOE_TPS)]
    picks = [((l_io == ri_ref[ts, 4:5]) | (l_io == ri_ref[ts, 5:6])).astype(BF16) for ts in tiles]
    moes = [_mm(pick, yloc[slot, u]) for u, pick in enumerate(picks)]
    for ts, moe in zip(tiles, moes):
        x2 = x1_ref[ts, :] + mod_ref[0, 5:6, :] * moe
        o_ref[ts, :] = _rms(x2, g_ref[...])


def _final(x1, ri, y, mod3, final_g, tables, s):
    t, d = x1.shape
    rows = MOE_TPS * MOE_TILE
    nsteps = t // rows
    per_b = s // rows
    grid_spec = pltpu.PrefetchScalarGridSpec(
        num_scalar_prefetch=7,
        grid=(nsteps,),
        in_specs=[pl.BlockSpec((rows, d), lambda i, *_: (i, 0)),
                  pl.BlockSpec((rows, LANES), lambda i, *_: (i, 0)),
                  pl.BlockSpec((1, 6, d), lambda i, *_: (i // per_b, 0, 0)),
                  pl.BlockSpec((1, d), lambda i, *_: (0, 0)),
                  pl.BlockSpec(memory_space=pl.ANY)],
        out_specs=pl.BlockSpec((rows, d), lambda i, *_: (i, 0)),
        scratch_shapes=[pltpu.VMEM((2, MOE_TPS, MOE_SLAB, d), BF16), pltpu.SemaphoreType.DMA((2,))])
    return pl.pallas_call(
        functools.partial(_final_kernel, nsteps=nsteps),
        grid_spec=grid_spec,
        out_shape=jax.ShapeDtypeStruct((t, d), F32),
        compiler_params=_params("arbitrary"),
        name="moe_combine_final",
    )(*tables["copies"], x1, ri, mod3, final_g.reshape(1, d), y)


def kernel(x, c, positions, w_ada, b_ada, norm1_g, w_in, q_norm_g, w_uq, kv_norm_g, w_ukv, w_o, norm2_g,
           w_gr, b_gr, w_er, b_er, w1, w3, w2, final_g):
    bsz, s, d = x.shape
    assert w_ada.shape[0] == 1, "one layer"
    tm = min(1024, s)
    tq = min(256, s)
    mod3 = _adaln(c, w_ada[0], b_ada[0]).reshape(bsz, 6, d)
    pos3 = positions.astype(F32).reshape(bsz, s, 1)
    q, k, vt, rq, rk, rv, rg = _pre(x, mod3, pos3, norm1_g[0], w_in[0], q_norm_g[0], w_uq[0], kv_norm_g[0], w_ukv[0],
                                    tm, tq)
    o_mla = _attention(q, k, vt, tq, ATTN_HEADS_PER_STEP)
    x1, h2, ri, rit, cnt = _post(x, o_mla, rq, rk, rv, rg, mod3, w_o[0], norm2_g[0], w_gr[0], b_gr[0], w_er[0],
                                 b_er[0], tm)
    t = bsz * s
    n_rows = _moe_rows(t // MOE_TILE)
    counts = cnt[:, 0, N_GROUPS:N_GROUPS + N_EXPERTS].astype(jnp.int32)
    tables = _moe_tables(counts, n_rows)
    ri2 = ri.reshape(t, LANES)
    xs = _dispatch(h2.reshape(t, d), ri2, rit, tables, n_rows)
    y = _experts(xs, tables, w1[0], w3[0], w2[0])
    out = _final(x1.reshape(t, d), ri2, y, mod3, final_g, tables, s)
    return out.reshape(bsz, s, d)
```

```python
import functools

import jax
import jax.numpy as jnp
from jax import lax
from jax.experimental import pallas as pl
from jax.experimental.pallas import tpu as pltpu

MLA_HEADS = 8
MLA_NOPE = 64
MLA_ROPE = 32
MLA_V = 64
Q_LORA = 256
KV_LORA = 128
RET_HEADS = 4
RET_DK = 64
RET_DV = 128
RET_CHUNK = 128
ROPE_BASE = 10000.0
NORM_EPS = 1e-6
N_GROUPS = 4
EXPERTS_PER_GROUP = 8
N_EXPERTS = N_GROUPS * EXPERTS_PER_GROUP
TOP_K = 2
D_EXPERT = 256
MOE_TILE = 512
MOE_CHUNK = 16
MOE_SLAB = 1536
MOE_TILE_CHUNKS = MOE_SLAB // MOE_CHUNK
MOE_LIST = max(MOE_TILE_CHUNKS // 2, N_EXPERTS)
MOE_TPS = 2
MOE_BLOCK = 1024

LANES = 128
VMEM_LIMIT = 56 * 1024 * 1024

F32 = jnp.float32
BF16 = jnp.bfloat16
NEG = float(jnp.finfo(jnp.float32).min)
LOG2_E = 1.4426950408889634
ATTN_HEADS_PER_STEP = 8
ATTN_PAIRS_PER_ITER = 2
MLA_VROWS = MLA_V + 16

_C_Q = 0
_C_KV = _C_Q + Q_LORA
_C_KPE = _C_KV + KV_LORA
_C_RQ = _C_KPE + LANES
_C_RK = _C_RQ + RET_HEADS * RET_DK
_C_RV = _C_RK + RET_HEADS * RET_DK
_C_RG = _C_RV + RET_HEADS * RET_DV
_IN_PERM = _C_RG + RET_HEADS * RET_DV


def _silu(v):
    return v / (1.0 + jnp.exp(-v))


def _mm(a, b):
    return jnp.dot(a, b, preferred_element_type=F32)


def _mm_nt(a, b):
    return lax.dot_general(a, b, (((1,), (1,)), ((), ())), preferred_element_type=F32)


def _params(*sem):
    return pltpu.CompilerParams(dimension_semantics=sem, vmem_limit_bytes=VMEM_LIMIT)


def _adaln_kernel(c_ref, w_ref, b_ref, o_ref):
    a = _silu(c_ref[...]).astype(BF16)
    o_ref[...] = _mm(a, w_ref[...].astype(BF16)) + b_ref[...]


def _adaln(c, w_ada, b_ada):
    bsz, d = c.shape
    n = w_ada.shape[1]
    tn = d
    return pl.pallas_call(
        _adaln_kernel,
        grid=(n // tn,),
        in_specs=[pl.BlockSpec((bsz, d), lambda j: (0, 0)),
                  pl.BlockSpec((d, tn), lambda j: (0, j)),
                  pl.BlockSpec((1, tn), lambda j: (0, j))],
        out_specs=pl.BlockSpec((bsz, tn), lambda j: (0, j)),
        out_shape=jax.ShapeDtypeStruct((bsz, n), F32),
        compiler_params=_params("arbitrary"),
        name="adaln",
    )(c, w_ada, b_ada.reshape(1, n))


def _rms(v, g):
    return v * lax.rsqrt(jnp.mean(v * v, axis=-1, keepdims=True) + NORM_EPS) * g


def _pre_kernel(x_ref, mod_ref, pos_ref, g1_ref, win_ref, qg_ref, wuq_ref, kvg_ref, wuk_ref,
                wuv_ref, invf_ref, q_ref, k_ref, v_ref, rq_ref, rk_ref, rv_ref, rg_ref):
    tq = v_ref.shape[-1]
    subs = [slice(i * tq, (i + 1) * tq) for i in range(v_ref.shape[1])]
    sh1 = mod_ref[0, 0:1, :]
    sc1 = mod_ref[0, 1:2, :]
    projs = [_mm((_rms(x_ref[0, rs, :], g1_ref[...]) * (1.0 + sc1) + sh1).astype(BF16), win_ref[...]) for rs in subs]

    cqs = [_rms(p[:, _C_Q:_C_Q + Q_LORA], qg_ref[...]).astype(BF16) for p in projs]
    ckvs = [_rms(p[:, _C_KV:_C_KV + KV_LORA], kvg_ref[...]).astype(BF16) for p in projs]
    qas = [_mm(cq, wuq_ref[...]) for cq in cqs]
    kns = [_mm(ckv, wuk_ref[...]) for ckv in ckvs]
    vvs = [_mm(ckv, wuv_ref[...]) for ckv in ckvs]

    lane = lax.broadcasted_iota(jnp.int32, (tq, LANES), 1)
    hi = lane >= RET_DK
    half_m, half_r = MLA_ROPE // 2, RET_DK // 2
    first_m = hi & (lane < RET_DK + half_m)
    first_r = (lane & half_r) == 0
    scale = (MLA_NOPE + MLA_ROPE) ** -0.5 * LOG2_E

    def rope(v, cos, sin, first, half):
        partner = jnp.where(first, pltpu.roll(v, LANES - half, 1), pltpu.roll(v, half, 1))
        return v * cos + partner * sin

    for i, rs in enumerate(subs):
        proj = projs[i]
        ang = pos_ref[0, rs, :] * invf_ref[...]
        cs = jnp.cos(ang)
        sn = jnp.sin(ang)
        c_mla = jnp.where(hi, cs, 1.0)
        s_mla = jnp.where(hi, jnp.where(first_m, -sn, sn), 0.0)
        c_ret = jnp.where(hi, pltpu.roll(cs, RET_DK, 1), cs)
        s_ret = jnp.where(hi, pltpu.roll(sn, RET_DK, 1), sn)
        s_ret = jnp.where(first_r, -s_ret, s_ret)

        kpe = rope(proj[:, _C_KPE:_C_KPE + LANES], c_mla, s_mla, first_m, half_m)
        cq_s = c_mla * scale
        sq_s = jnp.where(hi, sn, 0.0) * scale
        for hd in range(MLA_HEADS):
            sl = slice(hd * LANES, (hd + 1) * LANES)
            sr = slice((MLA_HEADS + hd) * LANES, (MLA_HEADS + hd + 1) * LANES)
            q_ref[0, rs, sl] = (qas[i][:, sl] * cq_s + qas[i][:, sr] * sq_s).astype(BF16)
            k_ref[0, rs, sl] = (kns[i][:, sl] + kpe).astype(BF16)
        vt = vvs[i].T
        tail = jnp.where(lax.broadcasted_iota(jnp.int32, (MLA_VROWS - MLA_V, tq), 0) == 0, 1.0, 0.0)
        slab = [piece for hd in range(MLA_HEADS) for piece in (vt[hd * MLA_V:(hd + 1) * MLA_V, :], tail)]
        v_ref[0, i] = jnp.concatenate(slab, axis=0).astype(BF16)

        for j in range(RET_HEADS * RET_DK // LANES):
            o = j * LANES
            rq = rope(proj[:, _C_RQ + o:_C_RQ + o + LANES], c_ret, s_ret, first_r, half_r)
            rk = rope(proj[:, _C_RK + o:_C_RK + o + LANES], c_ret, s_ret, first_r, half_r)
            rq_ref[0, rs, o:o + LANES] = rq.astype(BF16)
            rk_ref[0, rs, o:o + LANES] = (rk * (RET_DK ** -0.5)).astype(BF16)
        rv_ref[0, rs, :] = proj[:, _C_RV:_C_RV + RET_HEADS * RET_DV].astype(BF16)
        rg_ref[0, rs, :] = proj[:, _C_RG:_C_RG + RET_HEADS * RET_DV]


def _pad_heads(w, width, left):
    k = w.shape[0]
    w3 = w.reshape(k, -1, width)
    w3 = jnp.pad(w3, ((0, 0), (0, 0), (left, LANES - left - width)))
    return w3.reshape(k, -1)


def _pre(x, mod3, pos3, norm1_g, w_in, q_norm_g, w_uq, kv_norm_g, w_ukv, tm, tq):
    bsz, s, d = x.shape
    o = 0
    parts = {}
    for name, width in (("cq", Q_LORA), ("ckv", KV_LORA), ("kr", MLA_ROPE), ("rq", RET_HEADS * RET_DK),
                        ("rk", RET_HEADS * RET_DK), ("rv", RET_HEADS * RET_DV), ("rg", RET_HEADS * RET_DV)):
        parts[name] = w_in[:, o:o + width]
        o += width
    w_in_p = jnp.concatenate([
        parts["cq"], parts["ckv"], _pad_heads(parts["kr"], MLA_ROPE, MLA_NOPE),
        parts["rq"], parts["rk"], parts["rv"], parts["rg"]], axis=1).astype(BF16)
    assert w_in_p.shape[1] == _IN_PERM
    wq_rope = w_uq.reshape(Q_LORA, MLA_HEADS, MLA_NOPE + MLA_ROPE)[:, :, MLA_NOPE:].reshape(Q_LORA, MLA_HEADS, 2, -1)
    wq_rot = jnp.stack([-wq_rope[:, :, 1], wq_rope[:, :, 0]], axis=2).reshape(Q_LORA, -1)
    w_uq_p = jnp.concatenate([_pad_heads(w_uq, MLA_NOPE + MLA_ROPE, 0),
                              _pad_heads(wq_rot, MLA_ROPE, MLA_NOPE)], axis=1).astype(BF16)
    wkv3 = w_ukv.reshape(KV_LORA, MLA_HEADS, MLA_NOPE + MLA_V)
    w_uk_p = _pad_heads(wkv3[:, :, :MLA_NOPE].reshape(KV_LORA, -1), MLA_NOPE, 0).astype(BF16)
    w_uv = wkv3[:, :, MLA_NOPE:].reshape(KV_LORA, -1).astype(BF16)
    half_r, half_m = RET_DK // 2, MLA_ROPE // 2
    f_r = ROPE_BASE ** (-(jnp.arange(half_r, dtype=F32) / half_r))
    f_m = ROPE_BASE ** (-(jnp.arange(half_m, dtype=F32) / half_m))
    invf = jnp.concatenate([f_r, f_r, f_m, f_m, jnp.zeros((LANES - 2 * half_r - 2 * half_m,), F32)]).reshape(1, LANES)

    hq = MLA_HEADS * LANES
    const = lambda shape: pl.BlockSpec(shape, lambda b, i: (0,) * len(shape))
    tile = lambda w: pl.BlockSpec((1, tm, w), lambda b, i: (b, i, 0))
    return pl.pallas_call(
        _pre_kernel,
        grid=(bsz, s // tm),
        in_specs=[tile(d), pl.BlockSpec((1, 6, d), lambda b, i: (b, 0, 0)), tile(1), const((1, d)),
                  const((d, _IN_PERM)), const((1, Q_LORA)), const((Q_LORA, 2 * hq)),
                  const((1, KV_LORA)), const((KV_LORA, hq)), const((KV_LORA, MLA_HEADS * MLA_V)), const((1, LANES))],
        out_specs=[tile(hq), tile(hq),
                   pl.BlockSpec((1, tm // tq, MLA_HEADS * MLA_VROWS, tq), lambda b, i: (b, i, 0, 0)),
                   tile(RET_HEADS * RET_DK), tile(RET_HEADS * RET_DK),
                   tile(RET_HEADS * RET_DV), tile(RET_HEADS * RET_DV)],
        out_shape=[jax.ShapeDtypeStruct((bsz, s, hq), BF16), jax.ShapeDtypeStruct((bsz, s, hq), BF16),
                   jax.ShapeDtypeStruct((bsz, s // tq, MLA_HEADS * MLA_VROWS, tq), BF16),
                   jax.ShapeDtypeStruct((bsz, s, RET_HEADS * RET_DK), BF16),
                   jax.ShapeDtypeStruct((bsz, s, RET_HEADS * RET_DK), BF16),
                   jax.ShapeDtypeStruct((bsz, s, RET_HEADS * RET_DV), BF16),
                   jax.ShapeDtypeStruct((bsz, s, RET_HEADS * RET_DV), F32)],
        compiler_params=_params("parallel", "arbitrary"),
        name="pre_mixer",
    )(x, mod3, pos3, norm1_g.reshape(1, d), w_in_p, q_norm_g.reshape(1, -1), w_uq_p,
      kv_norm_g.reshape(1, -1), w_uk_p, w_uv, invf)


def _attn_kernel(q_ref, k_ref, vt_ref, o_ref, st_x, st_y, bm_x, bm_y, m_scr, acc_scr, *, tq, kb, hps):
    qi = pl.program_id(1)
    assert tq == 2 * kb, "a query tile spans two key blocks: the last two blocks of a tile are masked"
    key = lax.broadcasted_iota(jnp.int32, (kb, tq), 0)
    qry = lax.broadcasted_iota(jnp.int32, (kb, tq), 1)
    hsl = [slice(hh * LANES, (hh + 1) * LANES) for hh in range(hps)]
    vsl = [slice(hh * MLA_VROWS, (hh + 1) * MLA_VROWS) for hh in range(hps)]
    qs = [q_ref[0, :, hs] for hs in hsl]
    bufs = {"x": (st_x, bm_x), "y": (st_y, bm_y)}
    n_blk = 2 * qi + 2

    def scores(blk, buf, diag=None):
        st_ref, bm_ref = bufs[buf]
        start = pl.multiple_of(blk * kb, kb)
        for hh in range(hps):
            if diag == 1:
                st = _mm_nt(k_ref[0, pl.ds(start, kb), hsl[hh]], q_ref[0, kb:, hsl[hh]])
                st = jnp.where(lax.broadcasted_iota(jnp.int32, (kb, kb), 0) <= lax.broadcasted_iota(jnp.int32, (kb, kb), 1),
                               st, NEG)
                st = jnp.concatenate([jnp.full((kb, kb), NEG, F32), st], axis=1)
                st_ref[hh] = st
                bm_ref[hh, 0:1, :] = jnp.max(st, axis=0, keepdims=True)
                continue
            st = _mm_nt(k_ref[0, pl.ds(start, kb), hsl[hh]], qs[hh])
            if diag is not None:
                st = jnp.where(key <= qry, st, NEG)
            st_ref[hh] = st
            bm_ref[hh, 0:1, :] = jnp.max(st, axis=0, keepdims=True)

    def update(blk, buf):
        st_ref, bm_ref = bufs[buf]
        for hh in range(hps):
            m = m_scr[hh, 0:1, :]
            m_new = jnp.maximum(m, bm_ref[hh, 0:1, :])
            p = jnp.exp2(st_ref[hh] - m_new).astype(BF16)
            acc_scr[hh] = jnp.exp2(m - m_new) * acc_scr[hh] + _mm(vt_ref[0, blk, vsl[hh], :], p)
            m_scr[hh, 0:1, :] = m_new

    m_scr[...] = jnp.full(m_scr.shape, NEG, F32)
    acc_scr[...] = jnp.zeros(acc_scr.shape, F32)

    @pl.when(qi >= 1)
    def _():
        scores(0, "x")

    def steady(base, pairs):
        for r in range(pairs):
            scores(base + 2 * r + 1, "y")
            update(base + 2 * r, "x")
            scores(base + 2 * r + 2, "x")
            update(base + 2 * r + 1, "y")

    n_pairs = jnp.maximum(qi - 1, 0)
    n_long = n_pairs // ATTN_PAIRS_PER_ITER

    @pl.loop(0, n_long)
    def _(i):
        steady(2 * ATTN_PAIRS_PER_ITER * i, ATTN_PAIRS_PER_ITER)

    @pl.loop(n_long * ATTN_PAIRS_PER_ITER, n_pairs)
    def _(i):
        steady(2 * i, 1)

    @pl.when(qi >= 1)
    def _():
        scores(n_blk - 3, "y")
        update(n_blk - 4, "x")
        scores(n_blk - 2, "x", diag=0)
        update(n_blk - 3, "y")
        scores(n_blk - 1, "y", diag=1)
        update(n_blk - 2, "x")
        update(n_blk - 1, "y")

    @pl.when(qi == 0)
    def _():
        scores(0, "x", diag=0)
        scores(1, "y", diag=1)
        update(0, "x")
        update(1, "y")

    out_t = jnp.concatenate([acc_scr[hh, 0:MLA_V, :] * (1.0 / acc_scr[hh, MLA_V:MLA_V + 1, :]) for hh in range(hps)],
                            axis=0)
    o_ref[0] = out_t.T.astype(BF16)


def _attention(q, k, vt, kb, hps):
    bsz, s, _ = q.shape
    groups = MLA_HEADS // hps
    tq = 2 * kb
    assert vt.shape == (bsz, s // kb, MLA_HEADS * MLA_VROWS, kb)
    return pl.pallas_call(
        functools.partial(_attn_kernel, tq=tq, kb=kb, hps=hps),
        grid=(bsz * groups, s // tq),
        in_specs=[pl.BlockSpec((1, tq, hps * LANES), lambda g, i: (g // groups, i, g % groups)),
                  pl.BlockSpec((1, s, hps * LANES), lambda g, i: (g // groups, 0, g % groups)),
                  pl.BlockSpec((1, s // kb, hps * MLA_VROWS, kb), lambda g, i: (g // groups, 0, g % groups, 0))],
        out_specs=pl.BlockSpec((1, tq, hps * MLA_V), lambda g, i: (g // groups, i, g % groups)),
        out_shape=jax.ShapeDtypeStruct((bsz, s, MLA_HEADS * MLA_V), BF16),
        scratch_shapes=[pltpu.VMEM((hps, kb, tq), F32), pltpu.VMEM((hps, kb, tq), F32),
                        pltpu.VMEM((hps, 8, tq), F32), pltpu.VMEM((hps, 8, tq), F32),
                        pltpu.VMEM((hps, 8, tq), F32), pltpu.VMEM((hps, MLA_VROWS, tq), F32)],
        compiler_params=_params("parallel", "arbitrary"),
        name="mla_attention",
    )(q, k, vt)


def _retention_tile(rq_ref, rk_ref, rv_ref, rg_ref, dm_ref, xi_ref, zt_ref, dc_ref, o_ref, st_ref, *, ts):
    @pl.when(pl.program_id(1) == 0)
    def _():
        st_ref[...] = jnp.zeros_like(st_ref)

    lane = lax.broadcasted_iota(jnp.int32, (RET_CHUNK, LANES), 1)
    row = lax.broadcasted_iota(jnp.int32, (LANES, RET_DV), 0)
    hpt = LANES // RET_DK
    n_chunks = ts // RET_CHUNK
    tiles = [(c, p) for c in range(n_chunks) for p in range(RET_HEADS // hpt)]
    units = [(c, hd) for c in range(n_chunks) for hd in range(RET_HEADS)]
    rows = lambda c: slice(c * RET_CHUNK, (c + 1) * RET_CHUNK)
    vsl = lambda hd: slice(hd * RET_DV, (hd + 1) * RET_DV)
    mine = [(lane >= sub * RET_DK) & (lane < (sub + 1) * RET_DK) for sub in range(hpt)]
    mine_row = [(row >= sub * RET_DK) & (row < (sub + 1) * RET_DK) for sub in range(hpt)]

    q2 = {(c, p): rq_ref[0, rows(c), p * LANES:(p + 1) * LANES] for c, p in tiles}
    k2 = {(c, p): rk_ref[0, rows(c), p * LANES:(p + 1) * LANES] for c, p in tiles}
    vh = {(c, hd): rv_ref[0, rows(c), vsl(hd)] for c, hd in units}
    sc, un = {}, {}
    for c, p in tiles:
        heads = [p * hpt + sub for sub in range(hpt)]
        kcat = jnp.concatenate([jnp.where(mine[sub], k2[(c, p)], 0.0).astype(BF16) for sub in range(hpt)], axis=0)
        s_all = _mm_nt(q2[(c, p)], kcat)
        kz = jnp.zeros((RET_CHUNK, LANES), F32)
        for sub, hd in enumerate(heads):
            sc[(c, hd)] = s_all[:, sub * RET_CHUNK:(sub + 1) * RET_CHUNK]
            kz = kz + jnp.where(mine[sub], k2[(c, p)].astype(F32) * zt_ref[hd], 0.0)
        u_all = _mm(kz.astype(BF16).T, jnp.concatenate([vh[(c, hd)] for hd in heads], axis=1))
        for sub, hd in enumerate(heads):
            un[(c, hd)] = jnp.where(mine_row[sub], u_all[:, sub * RET_DV:(sub + 1) * RET_DV], 0.0)
    prev = {}
    for hd in range(RET_HEADS):
        st = st_ref[hd]
        for c in range(n_chunks):
            prev[(c, hd)] = st.astype(BF16)
            st = st * dc_ref[hd] + un[(c, hd)]
        st_ref[hd] = st
    for c, hd in units:
        u = (c, hd)
        lhs = jnp.concatenate([(sc[u] * dm_ref[hd]).astype(BF16),
                               (q2[(c, hd // hpt)].astype(F32) * xi_ref[hd]).astype(BF16)], axis=1)
        o = _mm(lhs, jnp.concatenate([vh[u], prev[u]], axis=0))
        mu = jnp.mean(o, axis=-1, keepdims=True)
        oc = o - mu
        on = oc * lax.rsqrt(jnp.mean(oc * oc, axis=-1, keepdims=True) + NORM_EPS)
        o_ref[rows(c), vsl(hd)] = (_silu(rg_ref[0, rows(c), vsl(hd)]) * on).astype(BF16)


def _retention_tables():
    c = RET_CHUNK
    gamma = 1.0 - jnp.power(2.0, -5.0 - jnp.arange(RET_HEADS, dtype=F32))
    log_g = jnp.log(gamma)
    idx = jnp.arange(c, dtype=F32)
    diff = idx[:, None] - idx[None, :]
    dmask = jnp.where(diff[None] >= 0, jnp.exp(jnp.maximum(diff, 0.0)[None] * log_g[:, None, None]), 0.0)
    zeta = jnp.exp((c - 1.0 - idx)[None, :] * log_g[:, None])
    xi = jnp.exp((idx + 1.0)[None, :] * log_g[:, None])
    decay = jnp.exp(c * log_g)
    xi_b = jnp.broadcast_to(xi[:, :, None], (RET_HEADS, c, LANES))
    zt_b = jnp.broadcast_to(zeta[:, :, None], (RET_HEADS, c, LANES))
    dc_b = jnp.broadcast_to(decay[:, None, None], (RET_HEADS, LANES, RET_DV))
    return dmask, xi_b, zt_b, dc_b


def _post_kernel(x_ref, om_ref, rq_ref, rk_ref, rv_ref, rg_ref, dm_ref, xi_ref, zt_ref, dc_ref,
                 mod_ref, wo_ref, g2_ref, wr_ref, br_ref,
                 x1_ref, h2_ref, ri_ref, rit_ref, cnt_ref, or_scr, st_ref):
    _retention_tile(rq_ref, rk_ref, rv_ref, rg_ref, dm_ref, xi_ref, zt_ref, dc_ref, or_scr, st_ref,
                    ts=x_ref.shape[1])
    half = om_ref.shape[-1]
    subs = [slice(i * MOE_TILE, (i + 1) * MOE_TILE) for i in range(x_ref.shape[1] // MOE_TILE)]
    mixes = [_mm(om_ref[0, rs, :], wo_ref[0:half, :]) + _mm(or_scr[rs, :], wo_ref[half:, :]) for rs in subs]

    lgs = []
    for rs, mix in zip(subs, mixes):
        x1 = x_ref[0, rs, :] + mod_ref[0, 2:3, :] * mix
        x1_ref[0, rs, :] = x1
        h2 = _rms(x1, g2_ref[...]) * (1.0 + mod_ref[0, 4:5, :]) + mod_ref[0, 3:4, :]
        hi = h2.astype(BF16)
        h2_ref[0, rs, :] = hi
        lo = (h2 - hi.astype(F32)).astype(BF16)
        both = _mm(hi, wr_ref[...])
        lgs.append(both[:, 0:LANES] + both[:, LANES:2 * LANES] + _mm(lo, wr_ref[:, 0:LANES]) + br_ref[...])

    lane = lax.broadcasted_iota(jnp.int32, (MOE_TILE, LANES), 1)
    big = jnp.int32(1 << 20)
    gmask = lane < N_GROUPS
    el = lane - N_GROUPS
    assert EXPERTS_PER_GROUP == 8
    routed = []
    for lg in lgs:
        gmax = jnp.max(jnp.where(gmask, lg, NEG), axis=-1, keepdims=True)
        ge = jnp.where(gmask, jnp.exp(lg - gmax), 0.0)
        pg = ge / jnp.sum(ge, axis=-1, keepdims=True)
        p_top = jnp.max(pg, axis=-1, keepdims=True)
        g_top = jnp.min(jnp.where(gmask & (pg == p_top), lane, big), axis=-1, keepdims=True)

        emask = (el >= 0) & (el < N_EXPERTS) & (lax.shift_right_arithmetic(el, 3) == g_top)
        ev = jnp.where(emask, lg, NEG)
        v1 = jnp.max(ev, axis=-1, keepdims=True)
        i1 = jnp.min(jnp.where(emask & (ev == v1), lane, big), axis=-1, keepdims=True)
        emask2 = emask & (lane != i1)
        ev2 = jnp.where(emask2, lg, NEG)
        v2 = jnp.max(ev2, axis=-1, keepdims=True)
        i2 = jnp.min(jnp.where(emask2 & (ev2 == v2), lane, big), axis=-1, keepdims=True)
        e = jnp.exp(v2 - v1)
        den = 1.0 + e
        routed.append((i1, i2, (1.0 / den) * p_top, (e / den) * p_top))

    r_io = lax.broadcasted_iota(jnp.int32, (MOE_TILE, MOE_TILE), 0)
    c_io = lax.broadcasted_iota(jnp.int32, (MOE_TILE, MOE_TILE), 1)
    earlier_tok = (c_io < r_io).astype(BF16)
    lr_io = lax.broadcasted_iota(jnp.int32, (LANES, LANES), 0)
    lc_io = lax.broadcasted_iota(jnp.int32, (LANES, LANES), 1)
    earlier_lane = (lr_io < lc_io).astype(BF16)
    for hf, (rs, (i1, i2, w1, w2)) in enumerate(zip(subs, routed)):
        oh = [lane == i1, lane == i2]
        cnt = (oh[0] | oh[1]).astype(BF16)
        excl = _mm(earlier_tok, cnt)
        n = jnp.sum(cnt.astype(F32), axis=0, keepdims=True)
        npad = jnp.floor((n + (MOE_CHUNK - 1)) * (1.0 / MOE_CHUNK)) * MOE_CHUNK
        loff = _mm(jnp.broadcast_to(npad, (8, LANES)).astype(BF16), earlier_lane)
        pos = excl + loff[0:1, :]
        cnt_ref[hf] = jnp.broadcast_to(n, (8, LANES))
        lp0, lp1 = [jnp.sum(jnp.where(o, pos, 0.0), axis=-1, keepdims=True) for o in oh]
        cols = [(i1 - N_GROUPS).astype(F32), (i2 - N_GROUPS).astype(F32), w1, w2, lp0, lp1]
        ri = jnp.zeros((MOE_TILE, LANES), F32)
        for j, col in enumerate(cols):
            ri = jnp.where(lane == j, col, ri)
        ri_ref[0, rs, :] = ri
        rit_ref[:, rs] = ri.T


def _post(x, o_mla, rq, rk, rv, rg, mod3, w_o, norm2_g, w_gr, b_gr, w_er, b_er, tm):
    bsz, s, d = x.shape
    ret_tables = _retention_tables()
    c = RET_CHUNK
    w_r = jnp.concatenate([w_gr, w_er.reshape(d, N_EXPERTS), jnp.zeros((d, LANES - N_GROUPS - N_EXPERTS), F32)], axis=1)
    w_rh = w_r.astype(BF16)
    w_rl = (w_r - w_rh.astype(F32)).astype(BF16)
    w_r2 = jnp.concatenate([w_rh, w_rl], axis=1)
    b_r =jnp.concatenate([b_gr, b_er.reshape(-1), jnp.zeros((LANES - N_GROUPS - N_EXPERTS,), F32)]).reshape(1, LANES)
    tile = lambda w: pl.BlockSpec((1, tm, w), lambda b, i: (b, i, 0))
    const = lambda shape: pl.BlockSpec(shape, lambda b, i: (0,) * len(shape))
    per_b = s // tm
    sub = tm // MOE_TILE
    return pl.pallas_call(
        _post_kernel,
        grid=(bsz, per_b),
        in_specs=[tile(d), tile(o_mla.shape[-1]),
                  tile(RET_HEADS * RET_DK), tile(RET_HEADS * RET_DK), tile(RET_HEADS * RET_DV), tile(RET_HEADS * RET_DV),
                  const((RET_HEADS, c, c)), const((RET_HEADS, c, LANES)), const((RET_HEADS, c, LANES)),
                  const((RET_HEADS, LANES, RET_DV)),
                  pl.BlockSpec((1, 6, d), lambda b, i: (b, 0, 0)),
                  const((d, d)), const((1, d)), const((d, 2 * LANES)), const((1, LANES))],
        out_specs=[tile(d), tile(d), tile(LANES),
                   pl.BlockSpec((LANES, tm), lambda b, i: (0, b * per_b + i)),
                   pl.BlockSpec((sub, 8, LANES), lambda b, i: (b * per_b + i, 0, 0))],
        out_shape=[jax.ShapeDtypeStruct((bsz, s, d), F32), jax.ShapeDtypeStruct((bsz, s, d), BF16),
                   jax.ShapeDtypeStruct((bsz, s, LANES), F32),
                   jax.ShapeDtypeStruct((LANES, bsz * s), F32),
                   jax.ShapeDtypeStruct((bsz * s // MOE_TILE, 8, LANES), F32)],
        scratch_shapes=[pltpu.VMEM((tm, RET_HEADS * RET_DV), BF16), pltpu.VMEM((RET_HEADS, LANES, RET_DV), F32)],
        compiler_params=_params("parallel", "arbitrary"),
        name="post_mixer",
    )(x, o_mla, rq, rk, rv, rg, *ret_tables, mod3, w_o.astype(BF16), norm2_g.reshape(1, d), w_r2, b_r)


def _chunk_rows(c, n=1):
    return pl.ds(pl.multiple_of(c * MOE_CHUNK, MOE_CHUNK), n * MOE_CHUNK)


def _start_tile_copies(copies, t, make):
    psrc, pdst, osrc, odst, npair, nodd = copies
    base = t * MOE_LIST
    lax.fori_loop(0, npair[t], lambda j, z: (make(psrc[base + j], pdst[base + j], 2).start(), z)[1], 0)
    lax.fori_loop(0, nodd[t], lambda j, z: (make(osrc[base + j], odst[base + j], 1).start(priority=1), z)[1], 0)


def _dispatch_kernel(psrc_ref, pdst_ref, osrc_ref, odst_ref, npair_ref, nodd_ref, nchk_ref, tstart_ref, tn_ref,
                     nbr_ref, h2_ref, ri_ref, rit_ref, xs_hbm, xloc, zblk, sem, zsem, bsem, *, nsteps, nb):
    i = pl.program_id(0)
    slot = lax.rem(i, 2)
    d = h2_ref.shape[-1]

    def zero_copy(e, c):
        return pltpu.make_async_copy(zblk.at[pl.ds(0, MOE_CHUNK)], xs_hbm.at[_chunk_rows(tstart_ref[e] + c)], zsem)

    def zero_block(j):
        rows = pl.ds(pl.multiple_of(j * MOE_BLOCK, MOE_BLOCK), MOE_BLOCK)
        return pltpu.make_async_copy(zblk, xs_hbm.at[rows], bsem)

    copies = (psrc_ref, pdst_ref, osrc_ref, odst_ref, npair_ref, nodd_ref)

    def wait_step(step, sl):
        for u in range(MOE_TPS):
            _wait_chunks(nchk_ref[step * MOE_TPS + u],
                         lambda rows: pltpu.make_async_copy(xloc.at[sl].at[0].at[rows], xs_hbm.at[rows], sem.at[sl]))

    @pl.when(i == 0)
    def _():
        zblk[...] = jnp.zeros_like(zblk)
        lax.fori_loop(nbr_ref[0], nb, lambda j, z: (zero_block(j).start(), z)[1], 0)
        for e in range(N_EXPERTS):
            lax.fori_loop(0, tn_ref[e], lambda c, z, e=e: (zero_copy(e, c).start(), z)[1], 0)
        for e in range(N_EXPERTS):
            lax.fori_loop(0, tn_ref[e], lambda c, z, e=e: (zero_copy(e, c).wait(), z)[1], 0)

    @pl.when(i >= 2)
    def _():
        wait_step(i - 2, slot)

    s_io = lax.broadcasted_iota(jnp.int32, (MOE_SLAB, MOE_TILE), 0).astype(F32)
    lane = lax.broadcasted_iota(jnp.int32, (MOE_TILE, LANES), 1)
    tiles = [slice(u * MOE_TILE, (u + 1) * MOE_TILE) for u in range(MOE_TPS)]
    pms = [((s_io == rit_ref[4:5, ts]) | (s_io == rit_ref[5:6, ts])).astype(BF16) for ts in tiles]
    exts = []
    for ts in tiles:
        ext = jnp.zeros((MOE_TILE, LANES), F32)
        for k in range(TOP_K):
            w = ri_ref[ts, TOP_K + k:TOP_K + k + 1]
            hi = w.astype(BF16).astype(F32)
            ext = jnp.where(lane == 2 * k, hi, jnp.where(lane == 2 * k + 1, w - hi, ext))
            ext = jnp.where(lane == 2 * TOP_K + k, ri_ref[ts, k:k + 1], ext)
        exts.append(ext.astype(BF16))
    mains = [_mm(pm, jnp.concatenate([h2_ref[ts, :], ext], axis=1)) for pm, ts, ext in zip(pms, tiles, exts)]
    for u in range(MOE_TPS):
        xloc[slot, u] = mains[u].astype(BF16)

    for u in range(MOE_TPS):
        t = i * MOE_TPS + u
        _start_tile_copies(copies, t, lambda src, dst, n, u=u: pltpu.make_async_copy(
            xloc.at[slot].at[u].at[_chunk_rows(src, n)], xs_hbm.at[_chunk_rows(dst, n)], sem.at[slot]))

    @pl.when(i == nsteps - 1)
    def _():
        wait_step(i, slot)
        if nsteps >= 2:
            wait_step(i - 1, 1 - slot)
        lax.fori_loop(nbr_ref[0], nb, lambda j, z: (zero_block(j).wait(), z)[1], 0)


def _wait_chunks(n, copy_of_rows):
    bit = 1 << (MOE_TILE_CHUNKS.bit_length() - 1)
    while bit >= 1:
        @pl.when((n & bit) != 0)
        def _(bit=bit):
            copy_of_rows(pl.ds(0, bit * MOE_CHUNK)).wait()
        bit //= 2


def _dispatch(h2, ri, rit, tables, n_rows):
    t, d = h2.shape
    nt = t // MOE_TILE
    assert nt % MOE_TPS == 0 and MOE_SLAB >= TOP_K * MOE_TILE + N_EXPERTS * (MOE_CHUNK - 1)
    rows = MOE_TPS * MOE_TILE
    grid_spec = pltpu.PrefetchScalarGridSpec(
        num_scalar_prefetch=10,
        grid=(nt // MOE_TPS,),
        in_specs=[pl.BlockSpec((rows, d), lambda i, *_: (i, 0)),
                  pl.BlockSpec((rows, LANES), lambda i, *_: (i, 0)),
                  pl.BlockSpec((8, rows), lambda i, *_: (0, i))],
        out_specs=pl.BlockSpec(memory_space=pl.ANY),
        scratch_shapes=[pltpu.VMEM((2, MOE_TPS, MOE_SLAB, d + LANES), BF16), pltpu.VMEM((MOE_BLOCK, d + LANES), BF16),
                        pltpu.SemaphoreType.DMA((2,)), pltpu.SemaphoreType.DMA(()), pltpu.SemaphoreType.DMA(())])
    return pl.pallas_call(
        functools.partial(_dispatch_kernel, nsteps=nt // MOE_TPS, nb=n_rows // MOE_BLOCK),
        grid_spec=grid_spec,
        out_shape=jax.ShapeDtypeStruct((n_rows, d + LANES), BF16),
        compiler_params=_params("arbitrary"),
        name="moe_dispatch",
    )(*tables["copies"], tables["tail_start"], tables["tail_n"], tables["n_blocks"],
      h2, ri, rit)


def _expert_kernel(be_ref, nbr_ref, xs_ref, w1_ref, w3_ref, w2_ref, y_ref, w13b, w2b):
    i = pl.program_id(0)
    used = i < nbr_ref[0]

    @pl.when(jnp.logical_not(used))
    def _():
        y_ref[...] = jnp.zeros_like(y_ref)

    @pl.when(used & ((i == 0) | (be_ref[i] != be_ref[jnp.maximum(i - 1, 0)])))
    def _():
        w13b[:, 0:D_EXPERT] = w1_ref[0].astype(BF16)
        w13b[:, D_EXPERT:] = w3_ref[0].astype(BF16)
        w2b[...] = w2_ref[0].astype(BF16)

    @pl.when(used)
    def _():
        d = y_ref.shape[-1]
        subs = [slice(j * MOE_TILE, (j + 1) * MOE_TILE) for j in range(y_ref.shape[0] // MOE_TILE)]
        ab = [_mm(xs_ref[rs, 0:d], w13b[...]) for rs in subs]
        hms = [(_silu(v[:, 0:D_EXPERT]) * v[:, D_EXPERT:]).astype(BF16) for v in ab]
        expert = be_ref[i].astype(F32)
        for rs, hm in zip(subs, hms):
            gw = xs_ref[rs, d:d + LANES].astype(F32)
            gate = jnp.where(gw[:, 2 * TOP_K:2 * TOP_K + 1] == expert, gw[:, 0:1] + gw[:, 1:2], gw[:, 2:3] + gw[:, 3:4])
            y_ref[rs, :] = (_mm(hm, w2b[...]) * gate).astype(BF16)


def _experts(xs, tables, w1, w3, w2):
    n_rows, dw = xs.shape
    d = dw - LANES
    nb = n_rows // MOE_BLOCK
    blk = lambda i, be, nbr: jnp.minimum(i, nbr[0] - 1)
    grid_spec = pltpu.PrefetchScalarGridSpec(
        num_scalar_prefetch=2,
        grid=(nb,),
        in_specs=[pl.BlockSpec((MOE_BLOCK, dw), lambda i, be, nbr: (blk(i, be, nbr), 0)),
                  pl.BlockSpec((1, d, D_EXPERT), lambda i, be, nbr: (be[blk(i, be, nbr)], 0, 0)),
                  pl.BlockSpec((1, d, D_EXPERT), lambda i, be, nbr: (be[blk(i, be, nbr)], 0, 0)),
                  pl.BlockSpec((1, D_EXPERT, d), lambda i, be, nbr: (be[blk(i, be, nbr)], 0, 0))],
        out_specs=pl.BlockSpec((MOE_BLOCK, d), lambda i, be, nbr: (i, 0)),
        scratch_shapes=[pltpu.VMEM((d, 2 * D_EXPERT), BF16), pltpu.VMEM((D_EXPERT, d), BF16)])
    return pl.pallas_call(
        _expert_kernel,
        grid_spec=grid_spec,
        out_shape=jax.ShapeDtypeStruct((n_rows, d), BF16),
        compiler_params=_params("arbitrary"),
        name="moe_experts",
    )(tables["blk_expert"], tables["n_blocks"], xs, w1, w3, w2)


def _moe_rows(n_tiles):
    worst = n_tiles * (TOP_K * MOE_TILE + N_EXPERTS * (MOE_CHUNK - 1)) + N_EXPERTS * (MOE_BLOCK - MOE_CHUNK)
    return -(-worst // MOE_BLOCK) * MOE_BLOCK


def _moe_tables(cnt, n_rows):
    n_tiles = cnt.shape[0]
    per_blk = MOE_BLOCK // MOE_CHUNK

    def excl_cumsum(a, axis):
        n = a.shape[axis]
        lower = jnp.arange(n)[:, None] > jnp.arange(n)[None, :]
        if axis == 0:
            return jnp.sum(jnp.where(lower[:, :, None], a[None, :, :], 0), axis=1)
        return jnp.sum(jnp.where(lower[None, :, :], a[:, None, :], 0), axis=2)

    nch = (cnt + MOE_CHUNK - 1) // MOE_CHUNK
    loff = excl_cumsum(nch, 1)
    seg = jnp.sum(nch, axis=0)
    blocks = (seg + per_blk - 1) // per_blk
    bstart = excl_cumsum(blocks[None, :], 1)[0]
    bend = bstart + blocks
    estart = bstart * per_blk
    gbase = estart[None, :] + excl_cumsum(nch, 0)
    pos = jnp.arange(MOE_LIST, dtype=jnp.int32)
    experts = jnp.arange(N_EXPERTS, dtype=jnp.int32)

    def copy_list(n_te, first, stride):
        start = excl_cumsum(n_te, 1)
        owner = jnp.sum((pos[None, :, None] >= (start + n_te)[:, None, :]).astype(jnp.int32), axis=-1)
        mine = jnp.minimum(owner, N_EXPERTS - 1)[:, :, None] == experts[None, None, :]
        pick = lambda a: jnp.sum(jnp.where(mine, a[:, None, :], 0), axis=-1)
        off = pick(first) + stride * (pos[None, :] - pick(start))
        dst = jnp.clip(pick(gbase) + off, 0, n_rows // MOE_CHUNK - 2)
        return (pick(loff) + off).reshape(-1).astype(jnp.int32), dst.reshape(-1).astype(jnp.int32)

    pairs = nch // 2
    odd = nch - 2 * pairs
    psrc, pdst = copy_list(pairs, jnp.zeros_like(nch), 2)
    osrc, odst = copy_list(odd, 2 * pairs, 1)
    nb = n_rows // MOE_BLOCK
    blk_expert = jnp.sum((jnp.arange(nb, dtype=jnp.int32)[:, None] >= bend[None, :]).astype(jnp.int32), axis=1)
    blk_expert = jnp.minimum(blk_expert, N_EXPERTS - 1)
    return dict(copies=(psrc, pdst, osrc, odst, jnp.sum(pairs, axis=1).astype(jnp.int32),
                        jnp.sum(odd, axis=1).astype(jnp.int32), jnp.sum(nch, axis=1).astype(jnp.int32)),
                tail_start=(estart + seg).astype(jnp.int32), tail_n=(blocks * per_blk - seg).astype(jnp.int32),
                blk_expert=blk_expert.astype(jnp.int32), n_blocks=bend[-1:].astype(jnp.int32))


def _final_kernel(psrc_ref, pdst_ref, osrc_ref, odst_ref, npair_ref, nodd_ref, nchk_ref,
                  x1_ref, ri_ref, mod_ref, g_ref, y_hbm, o_ref, yloc, sem, *, nsteps):
    i = pl.program_id(0)
    slot = lax.rem(i, 2)

    copies = (psrc_ref, pdst_ref, osrc_ref, odst_ref, npair_ref, nodd_ref)

    def gather(step, sl):
        for u in range(MOE_TPS):
            _start_tile_copies(copies, step * MOE_TPS + u, lambda src, dst, n, u=u: pltpu.make_async_copy(
                y_hbm.at[_chunk_rows(dst, n)], yloc.at[sl].at[u].at[_chunk_rows(src, n)], sem.at[sl]))

    @pl.when(i == 0)
    def _():
        yloc[...] = jnp.zeros_like(yloc)
        gather(0, 0)

    @pl.when(i + 1 < nsteps)
    def _():
        gather(i + 1, 1 - slot)

    for u in range(MOE_TPS):
        _wait_chunks(nchk_ref[i * MOE_TPS + u],
                     lambda rows: pltpu.make_async_copy(y_hbm.at[rows], yloc.at[slot].at[0].at[rows], sem.at[slot]))

    l_io = lax.broadcasted_iota(jnp.int32, (MOE_TILE, MOE_SLAB), 1).astype(F32)
    tiles = [slice(u * MOE_TILE, (u + 1) * MOE_TILE) for u in range(MOE_TPS)]
    picks = [((l_io == ri_ref[ts, 4:5]) | (l_io == ri_ref[ts, 5:6])).astype(BF16) for ts in tiles]
    moes = [_mm(pick, yloc[slot, u]) for u, pick in enumerate(picks)]
    for ts, moe in zip(tiles, moes):
        x2 = x1_ref[ts, :] + mod_ref[0, 5:6, :] * moe
        o_ref[ts, :] = _rms(x2, g_ref[...])


def _final(x1, ri, y, mod3, final_g, tables, s):
    t, d = x1.shape
    rows = MOE_TPS * MOE_TILE
    nsteps = t // rows
    per_b = s // rows
    grid_spec = pltpu.PrefetchScalarGridSpec(
        num_scalar_prefetch=7,
        grid=(nsteps,),
        in_specs=[pl.BlockSpec((rows, d), lambda i, *_: (i, 0)),
                  pl.BlockSpec((rows, LANES), lambda i, *_: (i, 0)),
                  pl.BlockSpec((1, 6, d), lambda i, *_: (i // per_b, 0, 0)),
                  pl.BlockSpec((1, d), lambda i, *_: (0, 0)),
                  pl.BlockSpec(memory_space=pl.ANY)],
        out_specs=pl.BlockSpec((rows, d), lambda i, *_: (i, 0)),
        scratch_shapes=[pltpu.VMEM((2, MOE_TPS, MOE_SLAB, d), BF16), pltpu.SemaphoreType.DMA((2,))])
    return pl.pallas_call(
        functools.partial(_final_kernel, nsteps=nsteps),
        grid_spec=grid_spec,
        out_shape=jax.ShapeDtypeStruct((t, d), F32),
        compiler_params=_params("arbitrary"),
        name="moe_combine_final",
    )(*tables["copies"], x1, ri, mod3, final_g.reshape(1, d), y)


def kernel(x, c, positions, w_ada, b_ada, norm1_g, w_in, q_norm_g, w_uq, kv_norm_g, w_ukv, w_o, norm2_g,
           w_gr, b_gr, w_er, b_er, w1, w3, w2, final_g):
    bsz, s, d = x.shape
    assert w_ada.shape[0] == 1, "one layer"
    tm = min(1024, s)
    tq = min(256, s)
    mod3 = _adaln(c, w_ada[0], b_ada[0]).reshape(bsz, 6, d)
    pos3 = positions.astype(F32).reshape(bsz, s, 1)
    q, k, vt, rq, rk, rv, rg = _pre(x, mod3, pos3, norm1_g[0], w_in[0], q_norm_g[0], w_uq[0], kv_norm_g[0], w_ukv[0],
                                    tm, tq)
    o_mla = _attention(q, k, vt, tq, ATTN_HEADS_PER_STEP)
    x1, h2, ri, rit, cnt = _post(x, o_mla, rq, rk, rv, rg, mod3, w_o[0], norm2_g[0], w_gr[0], b_gr[0], w_er[0],
                                 b_er[0], tm)
    t = bsz * s
    n_rows = _moe_rows(t // MOE_TILE)
    counts = cnt[:, 0, N_GROUPS:N_GROUPS + N_EXPERTS].astype(jnp.int32)
    tables = _moe_tables(counts, n_rows)
    ri2 = ri.reshape(t, LANES)
    xs = _dispatch(h2.reshape(t, d), ri2, rit, tables, n_rows)
    y = _experts(xs, tables, w1[0], w3[0], w2[0])
    out = _final(x1.reshape(t, d), ri2, y, mod3, final_g, tables, s)
    return out.reshape(bsz, s, d)
```

```python
import functools

import jax
import jax.numpy as jnp
from jax import lax
from jax.experimental import pallas as pl
from jax.experimental.pallas import tpu as pltpu

MLA_HEADS = 8
MLA_NOPE = 64
MLA_ROPE = 32
MLA_V = 64
Q_LORA = 256
KV_LORA = 128
RET_HEADS = 4
RET_DK = 64
RET_DV = 128
RET_CHUNK = 128
ROPE_BASE = 10000.0
NORM_EPS = 1e-6
N_GROUPS = 4
EXPERTS_PER_GROUP = 8
N_EXPERTS = N_GROUPS * EXPERTS_PER_GROUP
TOP_K = 2
D_EXPERT = 256
MOE_TILE = 512
MOE_CHUNK = 16
MOE_SLAB = 1536
MOE_TILE_CHUNKS = MOE_SLAB // MOE_CHUNK
MOE_LIST = max(MOE_TILE_CHUNKS // 2, N_EXPERTS)
MOE_TPS = 2
MOE_BLOCK = 512

LANES = 128
VMEM_LIMIT = 56 * 1024 * 1024

F32 = jnp.float32
BF16 = jnp.bfloat16
NEG = float(jnp.finfo(jnp.float32).min)
LOG2_E = 1.4426950408889634
ATTN_HEADS_PER_STEP = 8
ATTN_PAIRS_PER_ITER = 2
MLA_VROWS = MLA_V + 16

_C_Q = 0
_C_KV = _C_Q + Q_LORA
_C_KPE = _C_KV + KV_LORA
_C_RQ = _C_KPE + LANES
_C_RK = _C_RQ + RET_HEADS * RET_DK
_C_RV = _C_RK + RET_HEADS * RET_DK
_C_RG = _C_RV + RET_HEADS * RET_DV
_IN_PERM = _C_RG + RET_HEADS * RET_DV


def _silu(v):
    return v / (1.0 + jnp.exp(-v))


def _mm(a, b):
    return jnp.dot(a, b, preferred_element_type=F32)


def _mm_nt(a, b):
    return lax.dot_general(a, b, (((1,), (1,)), ((), ())), preferred_element_type=F32)


def _params(*sem):
    return pltpu.CompilerParams(dimension_semantics=sem, vmem_limit_bytes=VMEM_LIMIT)


def _adaln_kernel(c_ref, w_ref, b_ref, o_ref):
    a = _silu(c_ref[...]).astype(BF16)
    o_ref[...] = _mm(a, w_ref[...].astype(BF16)) + b_ref[...]


def _adaln(c, w_ada, b_ada):
    bsz, d = c.shape
    n = w_ada.shape[1]
    tn = d
    return pl.pallas_call(
        _adaln_kernel,
        grid=(n // tn,),
        in_specs=[pl.BlockSpec((bsz, d), lambda j: (0, 0)),
                  pl.BlockSpec((d, tn), lambda j: (0, j)),
                  pl.BlockSpec((1, tn), lambda j: (0, j))],
        out_specs=pl.BlockSpec((bsz, tn), lambda j: (0, j)),
        out_shape=jax.ShapeDtypeStruct((bsz, n), F32),
        compiler_params=_params("arbitrary"),
        name="adaln",
    )(c, w_ada, b_ada.reshape(1, n))


def _rms(v, g):
    return v * lax.rsqrt(jnp.mean(v * v, axis=-1, keepdims=True) + NORM_EPS) * g


def _pre_kernel(x_ref, mod_ref, pos_ref, g1_ref, win_ref, qg_ref, wuq_ref, kvg_ref, wuk_ref,
                wuv_ref, invf_ref, q_ref, k_ref, v_ref, rq_ref, rk_ref, rv_ref, rg_ref):
    tq = v_ref.shape[-1]
    subs = [slice(i * tq, (i + 1) * tq) for i in range(v_ref.shape[1])]
    sh1 = mod_ref[0, 0:1, :]
    sc1 = mod_ref[0, 1:2, :]
    projs = [_mm((_rms(x_ref[0, rs, :], g1_ref[...]) * (1.0 + sc1) + sh1).astype(BF16), win_ref[...]) for rs in subs]

    cqs = [_rms(p[:, _C_Q:_C_Q + Q_LORA], qg_ref[...]).astype(BF16) for p in projs]
    ckvs = [_rms(p[:, _C_KV:_C_KV + KV_LORA], kvg_ref[...]).astype(BF16) for p in projs]
    qas = [_mm(cq, wuq_ref[...]) for cq in cqs]
    kns = [_mm(ckv, wuk_ref[...]) for ckv in ckvs]
    vvs = [_mm(ckv, wuv_ref[...]) for ckv in ckvs]

    lane = lax.broadcasted_iota(jnp.int32, (tq, LANES), 1)
    hi = lane >= RET_DK
    half_m, half_r = MLA_ROPE // 2, RET_DK // 2
    first_m = hi & (lane < RET_DK + half_m)
    first_r = (lane & half_r) == 0
    scale = (MLA_NOPE + MLA_ROPE) ** -0.5 * LOG2_E

    def rope(v, cos, sin, first, half):
        partner = jnp.where(first, pltpu.roll(v, LANES - half, 1), pltpu.roll(v, half, 1))
        return v * cos + partner * sin

    for i, rs in enumerate(subs):
        proj = projs[i]
        ang = pos_ref[0, rs, :] * invf_ref[...]
        cs = jnp.cos(ang)
        sn = jnp.sin(ang)
        c_mla = jnp.where(hi, cs, 1.0)
        s_mla = jnp.where(hi, jnp.where(first_m, -sn, sn), 0.0)
        c_ret = jnp.where(hi, pltpu.roll(cs, RET_DK, 1), cs)
        s_ret = jnp.where(hi, pltpu.roll(sn, RET_DK, 1), sn)
        s_ret = jnp.where(first_r, -s_ret, s_ret)

        kpe = rope(proj[:, _C_KPE:_C_KPE + LANES], c_mla, s_mla, first_m, half_m)
        cq_s = c_mla * scale
        sq_s = jnp.where(hi, sn, 0.0) * scale
        for hd in range(MLA_HEADS):
            sl = slice(hd * LANES, (hd + 1) * LANES)
            sr = slice((MLA_HEADS + hd) * LANES, (MLA_HEADS + hd + 1) * LANES)
            q_ref[0, rs, sl] = (qas[i][:, sl] * cq_s + qas[i][:, sr] * sq_s).astype(BF16)
            k_ref[0, rs, sl] = (kns[i][:, sl] + kpe).astype(BF16)
        vt = vvs[i].T
        tail = jnp.where(lax.broadcasted_iota(jnp.int32, (MLA_VROWS - MLA_V, tq), 0) == 0, 1.0, 0.0)
        slab = [piece for hd in range(MLA_HEADS) for piece in (vt[hd * MLA_V:(hd + 1) * MLA_V, :], tail)]
        v_ref[0, i] = jnp.concatenate(slab, axis=0).astype(BF16)

        for j in range(RET_HEADS * RET_DK // LANES):
            o = j * LANES
            rq = rope(proj[:, _C_RQ + o:_C_RQ + o + LANES], c_ret, s_ret, first_r, half_r)
            rk = rope(proj[:, _C_RK + o:_C_RK + o + LANES], c_ret, s_ret, first_r, half_r)
            rq_ref[0, rs, o:o + LANES] = rq.astype(BF16)
            rk_ref[0, rs, o:o + LANES] = (rk * (RET_DK ** -0.5)).astype(BF16)
        rv_ref[0, rs, :] = proj[:, _C_RV:_C_RV + RET_HEADS * RET_DV].astype(BF16)
        rg_ref[0, rs, :] = proj[:, _C_RG:_C_RG + RET_HEADS * RET_DV]


def _pad_heads(w, width, left):
    k = w.shape[0]
    w3 = w.reshape(k, -1, width)
    w3 = jnp.pad(w3, ((0, 0), (0, 0), (left, LANES - left - width)))
    return w3.reshape(k, -1)


def _pre(x, mod3, pos3, norm1_g, w_in, q_norm_g, w_uq, kv_norm_g, w_ukv, tm, tq):
    bsz, s, d = x.shape
    o = 0
    parts = {}
    for name, width in (("cq", Q_LORA), ("ckv", KV_LORA), ("kr", MLA_ROPE), ("rq", RET_HEADS * RET_DK),
                        ("rk", RET_HEADS * RET_DK), ("rv", RET_HEADS * RET_DV), ("rg", RET_HEADS * RET_DV)):
        parts[name] = w_in[:, o:o + width]
        o += width
    w_in_p = jnp.concatenate([
        parts["cq"], parts["ckv"], _pad_heads(parts["kr"], MLA_ROPE, MLA_NOPE),
        parts["rq"], parts["rk"], parts["rv"], parts["rg"]], axis=1).astype(BF16)
    assert w_in_p.shape[1] == _IN_PERM
    wq_rope = w_uq.reshape(Q_LORA, MLA_HEADS, MLA_NOPE + MLA_ROPE)[:, :, MLA_NOPE:].reshape(Q_LORA, MLA_HEADS, 2, -1)
    wq_rot = jnp.stack([-wq_rope[:, :, 1], wq_rope[:, :, 0]], axis=2).reshape(Q_LORA, -1)
    w_uq_p = jnp.concatenate([_pad_heads(w_uq, MLA_NOPE + MLA_ROPE, 0),
                              _pad_heads(wq_rot, MLA_ROPE, MLA_NOPE)], axis=1).astype(BF16)
    wkv3 = w_ukv.reshape(KV_LORA, MLA_HEADS, MLA_NOPE + MLA_V)
    w_uk_p = _pad_heads(wkv3[:, :, :MLA_NOPE].reshape(KV_LORA, -1), MLA_NOPE, 0).astype(BF16)
    w_uv = wkv3[:, :, MLA_NOPE:].reshape(KV_LORA, -1).astype(BF16)
    half_r, half_m = RET_DK // 2, MLA_ROPE // 2
    f_r = ROPE_BASE ** (-(jnp.arange(half_r, dtype=F32) / half_r))
    f_m = ROPE_BASE ** (-(jnp.arange(half_m, dtype=F32) / half_m))
    invf = jnp.concatenate([f_r, f_r, f_m, f_m, jnp.zeros((LANES - 2 * half_r - 2 * half_m,), F32)]).reshape(1, LANES)

    hq = MLA_HEADS * LANES
    const = lambda shape: pl.BlockSpec(shape, lambda b, i: (0,) * len(shape))
    tile = lambda w: pl.BlockSpec((1, tm, w), lambda b, i: (b, i, 0))
    return pl.pallas_call(
        _pre_kernel,
        grid=(bsz, s // tm),
        in_specs=[tile(d), pl.BlockSpec((1, 6, d), lambda b, i: (b, 0, 0)), tile(1), const((1, d)),
                  const((d, _IN_PERM)), const((1, Q_LORA)), const((Q_LORA, 2 * hq)),
                  const((1, KV_LORA)), const((KV_LORA, hq)), const((KV_LORA, MLA_HEADS * MLA_V)), const((1, LANES))],
        out_specs=[tile(hq), tile(hq),
                   pl.BlockSpec((1, tm // tq, MLA_HEADS * MLA_VROWS, tq), lambda b, i: (b, i, 0, 0)),
                   tile(RET_HEADS * RET_DK), tile(RET_HEADS * RET_DK),
                   tile(RET_HEADS * RET_DV), tile(RET_HEADS * RET_DV)],
        out_shape=[jax.ShapeDtypeStruct((bsz, s, hq), BF16), jax.ShapeDtypeStruct((bsz, s, hq), BF16),
                   jax.ShapeDtypeStruct((bsz, s // tq, MLA_HEADS * MLA_VROWS, tq), BF16),
                   jax.ShapeDtypeStruct((bsz, s, RET_HEADS * RET_DK), BF16),
                   jax.ShapeDtypeStruct((bsz, s, RET_HEADS * RET_DK), BF16),
                   jax.ShapeDtypeStruct((bsz, s, RET_HEADS * RET_DV), BF16),
                   jax.ShapeDtypeStruct((bsz, s, RET_HEADS * RET_DV), F32)],
        compiler_params=_params("parallel", "arbitrary"),
        name="pre_mixer",
    )(x, mod3, pos3, norm1_g.reshape(1, d), w_in_p, q_norm_g.reshape(1, -1), w_uq_p,
      kv_norm_g.reshape(1, -1), w_uk_p, w_uv, invf)


def _attn_kernel(q_ref, k_ref, vt_ref, o_ref, st_x, st_y, bm_x, bm_y, m_scr, acc_scr, *, tq, kb, hps):
    qi = pl.program_id(1)
    assert tq == 2 * kb, "a query tile spans two key blocks: the last two blocks of a tile are masked"
    key = lax.broadcasted_iota(jnp.int32, (kb, tq), 0)
    qry = lax.broadcasted_iota(jnp.int32, (kb, tq), 1)
    hsl = [slice(hh * LANES, (hh + 1) * LANES) for hh in range(hps)]
    vsl = [slice(hh * MLA_VROWS, (hh + 1) * MLA_VROWS) for hh in range(hps)]
    qs = [q_ref[0, :, hs] for hs in hsl]
    bufs = {"x": (st_x, bm_x), "y": (st_y, bm_y)}
    n_blk = 2 * qi + 2

    def scores(blk, buf, diag=None):
        st_ref, bm_ref = bufs[buf]
        start = pl.multiple_of(blk * kb, kb)
        for hh in range(hps):
            if diag == 1:
                st = _mm_nt(k_ref[0, pl.ds(start, kb), hsl[hh]], q_ref[0, kb:, hsl[hh]])
                st = jnp.where(lax.broadcasted_iota(jnp.int32, (kb, kb), 0) <= lax.broadcasted_iota(jnp.int32, (kb, kb), 1),
                               st, NEG)
                st = jnp.concatenate([jnp.full((kb, kb), NEG, F32), st], axis=1)
                st_ref[hh] = st
                bm_ref[hh, 0:1, :] = jnp.max(st, axis=0, keepdims=True)
                continue
            st = _mm_nt(k_ref[0, pl.ds(start, kb), hsl[hh]], qs[hh])
            if diag is not None:
                st = jnp.where(key <= qry, st, NEG)
            st_ref[hh] = st
            bm_ref[hh, 0:1, :] = jnp.max(st, axis=0, keepdims=True)

    def update(blk, buf):
        st_ref, bm_ref = bufs[buf]
        for hh in range(hps):
            m = m_scr[hh, 0:1, :]
            m_new = jnp.maximum(m, bm_ref[hh, 0:1, :])
            p = jnp.exp2(st_ref[hh] - m_new).astype(BF16)
            acc_scr[hh] = jnp.exp2(m - m_new) * acc_scr[hh] + _mm(vt_ref[0, blk, vsl[hh], :], p)
            m_scr[hh, 0:1, :] = m_new

    m_scr[...] = jnp.full(m_scr.shape, NEG, F32)
    acc_scr[...] = jnp.zeros(acc_scr.shape, F32)

    @pl.when(qi >= 1)
    def _():
        scores(0, "x")

    def steady(base, pairs):
        for r in range(pairs):
            scores(base + 2 * r + 1, "y")
            update(base + 2 * r, "x")
            scores(base + 2 * r + 2, "x")
            update(base + 2 * r + 1, "y")

    n_pairs = jnp.maximum(qi - 1, 0)
    n_long = n_pairs // ATTN_PAIRS_PER_ITER

    @pl.loop(0, n_long)
    def _(i):
        steady(2 * ATTN_PAIRS_PER_ITER * i, ATTN_PAIRS_PER_ITER)

    @pl.loop(n_long * ATTN_PAIRS_PER_ITER, n_pairs)
    def _(i):
        steady(2 * i, 1)

    @pl.when(qi >= 1)
    def _():
        scores(n_blk - 3, "y")
        update(n_blk - 4, "x")
        scores(n_blk - 2, "x", diag=0)
        update(n_blk - 3, "y")
        scores(n_blk - 1, "y", diag=1)
        update(n_blk - 2, "x")
        update(n_blk - 1, "y")

    @pl.when(qi == 0)
    def _():
        scores(0, "x", diag=0)
        scores(1, "y", diag=1)
        update(0, "x")
        update(1, "y")

    out_t = jnp.concatenate([acc_scr[hh, 0:MLA_V, :] * (1.0 / acc_scr[hh, MLA_V:MLA_V + 1, :]) for hh in range(hps)],
                            axis=0)
    o_ref[0] = out_t.T.astype(BF16)


def _attention(q, k, vt, kb, hps):
    bsz, s, _ = q.shape
    groups = MLA_HEADS // hps
    tq = 2 * kb
    assert vt.shape == (bsz, s // kb, MLA_HEADS * MLA_VROWS, kb)
    return pl.pallas_call(
        functools.partial(_attn_kernel, tq=tq, kb=kb, hps=hps),
        grid=(bsz * groups, s // tq),
        in_specs=[pl.BlockSpec((1, tq, hps * LANES), lambda g, i: (g // groups, i, g % groups)),
                  pl.BlockSpec((1, s, hps * LANES), lambda g, i: (g // groups, 0, g % groups)),
                  pl.BlockSpec((1, s // kb, hps * MLA_VROWS, kb), lambda g, i: (g // groups, 0, g % groups, 0))],
        out_specs=pl.BlockSpec((1, tq, hps * MLA_V), lambda g, i: (g // groups, i, g % groups)),
        out_shape=jax.ShapeDtypeStruct((bsz, s, MLA_HEADS * MLA_V), BF16),
        scratch_shapes=[pltpu.VMEM((hps, kb, tq), F32), pltpu.VMEM((hps, kb, tq), F32),
                        pltpu.VMEM((hps, 8, tq), F32), pltpu.VMEM((hps, 8, tq), F32),
                        pltpu.VMEM((hps, 8, tq), F32), pltpu.VMEM((hps, MLA_VROWS, tq), F32)],
        compiler_params=_params("parallel", "arbitrary"),
        name="mla_attention",
    )(q, k, vt)


def _retention_tile(rq_ref, rk_ref, rv_ref, rg_ref, dm_ref, xi_ref, zt_ref, dc_ref, o_ref, st_ref, *, ts):
    @pl.when(pl.program_id(1) == 0)
    def _():
        st_ref[...] = jnp.zeros_like(st_ref)

    lane = lax.broadcasted_iota(jnp.int32, (RET_CHUNK, LANES), 1)
    row = lax.broadcasted_iota(jnp.int32, (LANES, RET_DV), 0)
    hpt = LANES // RET_DK
    n_chunks = ts // RET_CHUNK
    tiles = [(c, p) for c in range(n_chunks) for p in range(RET_HEADS // hpt)]
    units = [(c, hd) for c in range(n_chunks) for hd in range(RET_HEADS)]
    rows = lambda c: slice(c * RET_CHUNK, (c + 1) * RET_CHUNK)
    vsl = lambda hd: slice(hd * RET_DV, (hd + 1) * RET_DV)
    mine = [(lane >= sub * RET_DK) & (lane < (sub + 1) * RET_DK) for sub in range(hpt)]
    mine_row = [(row >= sub * RET_DK) & (row < (sub + 1) * RET_DK) for sub in range(hpt)]

    q2 = {(c, p): rq_ref[0, rows(c), p * LANES:(p + 1) * LANES] for c, p in tiles}
    k2 = {(c, p): rk_ref[0, rows(c), p * LANES:(p + 1) * LANES] for c, p in tiles}
    vh = {(c, hd): rv_ref[0, rows(c), vsl(hd)] for c, hd in units}
    sc, un = {}, {}
    for c, p in tiles:
        heads = [p * hpt + sub for sub in range(hpt)]
        kcat = jnp.concatenate([jnp.where(mine[sub], k2[(c, p)], 0.0).astype(BF16) for sub in range(hpt)], axis=0)
        s_all = _mm_nt(q2[(c, p)], kcat)
        kz = jnp.zeros((RET_CHUNK, LANES), F32)
        for sub, hd in enumerate(heads):
            sc[(c, hd)] = s_all[:, sub * RET_CHUNK:(sub + 1) * RET_CHUNK]
            kz = kz + jnp.where(mine[sub], k2[(c, p)].astype(F32) * zt_ref[hd], 0.0)
        u_all = _mm(kz.astype(BF16).T, jnp.concatenate([vh[(c, hd)] for hd in heads], axis=1))
        for sub, hd in enumerate(heads):
            un[(c, hd)] = jnp.where(mine_row[sub], u_all[:, sub * RET_DV:(sub + 1) * RET_DV], 0.0)
    prev = {}
    for hd in range(RET_HEADS):
        st = st_ref[hd]
        for c in range(n_chunks):
            prev[(c, hd)] = st.astype(BF16)
            st = st * dc_ref[hd] + un[(c, hd)]
        st_ref[hd] = st
    for c, hd in units:
        u = (c, hd)
        lhs = jnp.concatenate([(sc[u] * dm_ref[hd]).astype(BF16),
                               (q2[(c, hd // hpt)].astype(F32) * xi_ref[hd]).astype(BF16)], axis=1)
        o = _mm(lhs, jnp.concatenate([vh[u], prev[u]], axis=0))
        mu = jnp.mean(o, axis=-1, keepdims=True)
        oc = o - mu
        on = oc * lax.rsqrt(jnp.mean(oc * oc, axis=-1, keepdims=True) + NORM_EPS)
        o_ref[rows(c), vsl(hd)] = (_silu(rg_ref[0, rows(c), vsl(hd)]) * on).astype(BF16)


def _retention_tables():
    c = RET_CHUNK
    gamma = 1.0 - jnp.power(2.0, -5.0 - jnp.arange(RET_HEADS, dtype=F32))
    log_g = jnp.log(gamma)
    idx = jnp.arange(c, dtype=F32)
    diff = idx[:, None] - idx[None, :]
    dmask = jnp.where(diff[None] >= 0, jnp.exp(jnp.maximum(diff, 0.0)[None] * log_g[:, None, None]), 0.0)
    zeta = jnp.exp((c - 1.0 - idx)[None, :] * log_g[:, None])
    xi = jnp.exp((idx + 1.0)[None, :] * log_g[:, None])
    decay = jnp.exp(c * log_g)
    xi_b = jnp.broadcast_to(xi[:, :, None], (RET_HEADS, c, LANES))
    zt_b = jnp.broadcast_to(zeta[:, :, None], (RET_HEADS, c, LANES))
    dc_b = jnp.broadcast_to(decay[:, None, None], (RET_HEADS, LANES, RET_DV))
    return dmask, xi_b, zt_b, dc_b


def _post_kernel(x_ref, om_ref, rq_ref, rk_ref, rv_ref, rg_ref, dm_ref, xi_ref, zt_ref, dc_ref,
                 mod_ref, wo_ref, g2_ref, wr_ref, br_ref,
                 x1_ref, h2_ref, ri_ref, rit_ref, cnt_ref, or_scr, st_ref):
    _retention_tile(rq_ref, rk_ref, rv_ref, rg_ref, dm_ref, xi_ref, zt_ref, dc_ref, or_scr, st_ref,
                    ts=x_ref.shape[1])
    half = om_ref.shape[-1]
    subs = [slice(i * MOE_TILE, (i + 1) * MOE_TILE) for i in range(x_ref.shape[1] // MOE_TILE)]
    mixes = [_mm(om_ref[0, rs, :], wo_ref[0:half, :]) + _mm(or_scr[rs, :], wo_ref[half:, :]) for rs in subs]

    lgs = []
    for rs, mix in zip(subs, mixes):
        x1 = x_ref[0, rs, :] + mod_ref[0, 2:3, :] * mix
        x1_ref[0, rs, :] = x1
        h2 = _rms(x1, g2_ref[...]) * (1.0 + mod_ref[0, 4:5, :]) + mod_ref[0, 3:4, :]
        hi = h2.astype(BF16)
        h2_ref[0, rs, :] = hi
        lo = (h2 - hi.astype(F32)).astype(BF16)
        both = _mm(hi, wr_ref[...])
        lgs.append(both[:, 0:LANES] + both[:, LANES:2 * LANES] + _mm(lo, wr_ref[:, 0:LANES]) + br_ref[...])

    lane = lax.broadcasted_iota(jnp.int32, (MOE_TILE, LANES), 1)
    big = jnp.int32(1 << 20)
    gmask = lane < N_GROUPS
    el = lane - N_GROUPS
    assert EXPERTS_PER_GROUP == 8
    routed = []
    for lg in lgs:
        gmax = jnp.max(jnp.where(gmask, lg, NEG), axis=-1, keepdims=True)
        ge = jnp.where(gmask, jnp.exp(lg - gmax), 0.0)
        pg = ge / jnp.sum(ge, axis=-1, keepdims=True)
        p_top = jnp.max(pg, axis=-1, keepdims=True)
        g_top = jnp.min(jnp.where(gmask & (pg == p_top), lane, big), axis=-1, keepdims=True)

        emask = (el >= 0) & (el < N_EXPERTS) & (lax.shift_right_arithmetic(el, 3) == g_top)
        ev = jnp.where(emask, lg, NEG)
        v1 = jnp.max(ev, axis=-1, keepdims=True)
        i1 = jnp.min(jnp.where(emask & (ev == v1), lane, big), axis=-1, keepdims=True)
        emask2 = emask & (lane != i1)
        ev2 = jnp.where(emask2, lg, NEG)
        v2 = jnp.max(ev2, axis=-1, keepdims=True)
        i2 = jnp.min(jnp.where(emask2 & (ev2 == v2), lane, big), axis=-1, keepdims=True)
        e = jnp.exp(v2 - v1)
        den = 1.0 + e
        routed.append((i1, i2, (1.0 / den) * p_top, (e / den) * p_top))

    r_io = lax.broadcasted_iota(jnp.int32, (MOE_TILE, MOE_TILE), 0)
    c_io = lax.broadcasted_iota(jnp.int32, (MOE_TILE, MOE_TILE), 1)
    earlier_tok = (c_io < r_io).astype(BF16)
    lr_io = lax.broadcasted_iota(jnp.int32, (LANES, LANES), 0)
    lc_io = lax.broadcasted_iota(jnp.int32, (LANES, LANES), 1)
    earlier_lane = (lr_io < lc_io).astype(BF16)
    for hf, (rs, (i1, i2, w1, w2)) in enumerate(zip(subs, routed)):
        oh = [lane == i1, lane == i2]
        cnt = (oh[0] | oh[1]).astype(BF16)
        excl = _mm(earlier_tok, cnt)
        n = jnp.sum(cnt.astype(F32), axis=0, keepdims=True)
        npad = jnp.floor((n + (MOE_CHUNK - 1)) * (1.0 / MOE_CHUNK)) * MOE_CHUNK
        loff = _mm(jnp.broadcast_to(npad, (8, LANES)).astype(BF16), earlier_lane)
        pos = excl + loff[0:1, :]
        cnt_ref[hf] = jnp.broadcast_to(n, (8, LANES))
        lp0, lp1 = [jnp.sum(jnp.where(o, pos, 0.0), axis=-1, keepdims=True) for o in oh]
        cols = [(i1 - N_GROUPS).astype(F32), (i2 - N_GROUPS).astype(F32), w1, w2, lp0, lp1]
        ri = jnp.zeros((MOE_TILE, LANES), F32)
        for j, col in enumerate(cols):
            ri = jnp.where(lane == j, col, ri)
        ri_ref[0, rs, :] = ri
        rit_ref[:, rs] = ri.T


def _post(x, o_mla, rq, rk, rv, rg, mod3, w_o, norm2_g, w_gr, b_gr, w_er, b_er, tm):
    bsz, s, d = x.shape
    ret_tables = _retention_tables()
    c = RET_CHUNK
    w_r = jnp.concatenate([w_gr, w_er.reshape(d, N_EXPERTS), jnp.zeros((d, LANES - N_GROUPS - N_EXPERTS), F32)], axis=1)
    w_rh = w_r.astype(BF16)
    w_rl = (w_r - w_rh.astype(F32)).astype(BF16)
    w_r2 = jnp.concatenate([w_rh, w_rl], axis=1)
    b_r =jnp.concatenate([b_gr, b_er.reshape(-1), jnp.zeros((LANES - N_GROUPS - N_EXPERTS,), F32)]).reshape(1, LANES)
    tile = lambda w: pl.BlockSpec((1, tm, w), lambda b, i: (b, i, 0))
    const = lambda shape: pl.BlockSpec(shape, lambda b, i: (0,) * len(shape))
    per_b = s // tm
    sub = tm // MOE_TILE
    return pl.pallas_call(
        _post_kernel,
        grid=(bsz, per_b),
        in_specs=[tile(d), tile(o_mla.shape[-1]),
                  tile(RET_HEADS * RET_DK), tile(RET_HEADS * RET_DK), tile(RET_HEADS * RET_DV), tile(RET_HEADS * RET_DV),
                  const((RET_HEADS, c, c)), const((RET_HEADS, c, LANES)), const((RET_HEADS, c, LANES)),
                  const((RET_HEADS, LANES, RET_DV)),
                  pl.BlockSpec((1, 6, d), lambda b, i: (b, 0, 0)),
                  const((d, d)), const((1, d)), const((d, 2 * LANES)), const((1, LANES))],
        out_specs=[tile(d), tile(d), tile(LANES),
                   pl.BlockSpec((LANES, tm), lambda b, i: (0, b * per_b + i)),
                   pl.BlockSpec((sub, 8, LANES), lambda b, i: (b * per_b + i, 0, 0))],
        out_shape=[jax.ShapeDtypeStruct((bsz, s, d), F32), jax.ShapeDtypeStruct((bsz, s, d), BF16),
                   jax.ShapeDtypeStruct((bsz, s, LANES), F32),
                   jax.ShapeDtypeStruct((LANES, bsz * s), F32),
                   jax.ShapeDtypeStruct((bsz * s // MOE_TILE, 8, LANES), F32)],
        scratch_shapes=[pltpu.VMEM((tm, RET_HEADS * RET_DV), BF16), pltpu.VMEM((RET_HEADS, LANES, RET_DV), F32)],
        compiler_params=_params("parallel", "arbitrary"),
        name="post_mixer",
    )(x, o_mla, rq, rk, rv, rg, *ret_tables, mod3, w_o.astype(BF16), norm2_g.reshape(1, d), w_r2, b_r)


def _chunk_rows(c, n=1):
    return pl.ds(pl.multiple_of(c * MOE_CHUNK, MOE_CHUNK), n * MOE_CHUNK)


def _start_tile_copies(copies, t, make):
    psrc, pdst, osrc, odst, npair, nodd = copies
    base = t * MOE_LIST
    lax.fori_loop(0, npair[t], lambda j, z: (make(psrc[base + j], pdst[base + j], 2).start(), z)[1], 0)
    lax.fori_loop(0, nodd[t], lambda j, z: (make(osrc[base + j], odst[base + j], 1).start(), z)[1], 0)


def _dispatch_kernel(psrc_ref, pdst_ref, osrc_ref, odst_ref, npair_ref, nodd_ref, nchk_ref, tstart_ref, tn_ref,
                     nbr_ref, h2_ref, ri_ref, rit_ref, xs_hbm, xloc, zblk, sem, zsem, bsem, *, nsteps, nb):
    i = pl.program_id(0)
    slot = lax.rem(i, 2)
    d = h2_ref.shape[-1]

    def zero_copy(e, c):
        return pltpu.make_async_copy(zblk.at[pl.ds(0, MOE_CHUNK)], xs_hbm.at[_chunk_rows(tstart_ref[e] + c)], zsem)

    def zero_block(j):
        rows = pl.ds(pl.multiple_of(j * MOE_BLOCK, MOE_BLOCK), MOE_BLOCK)
        return pltpu.make_async_copy(zblk, xs_hbm.at[rows], bsem)

    copies = (psrc_ref, pdst_ref, osrc_ref, odst_ref, npair_ref, nodd_ref)

    def wait_step(step, sl):
        for u in range(MOE_TPS):
            _wait_chunks(nchk_ref[step * MOE_TPS + u],
                         lambda rows: pltpu.make_async_copy(xloc.at[sl].at[0].at[rows], xs_hbm.at[rows], sem.at[sl]))

    @pl.when(i == 0)
    def _():
        zblk[...] = jnp.zeros_like(zblk)
        lax.fori_loop(nbr_ref[0], nb, lambda j, z: (zero_block(j).start(), z)[1], 0)
        for e in range(N_EXPERTS):
            lax.fori_loop(0, tn_ref[e], lambda c, z, e=e: (zero_copy(e, c).start(), z)[1], 0)
        for e in range(N_EXPERTS):
            lax.fori_loop(0, tn_ref[e], lambda c, z, e=e: (zero_copy(e, c).wait(), z)[1], 0)

    @pl.when(i >= 2)
    def _():
        wait_step(i - 2, slot)

    s_io = lax.broadcasted_iota(jnp.int32, (MOE_SLAB, MOE_TILE), 0).astype(F32)
    lane = lax.broadcasted_iota(jnp.int32, (MOE_TILE, LANES), 1)
    tiles = [slice(u * MOE_TILE, (u + 1) * MOE_TILE) for u in range(MOE_TPS)]
    pms = [((s_io == rit_ref[4:5, ts]) | (s_io == rit_ref[5:6, ts])).astype(BF16) for ts in tiles]
    exts = []
    for ts in tiles:
        ext = jnp.zeros((MOE_TILE, LANES), F32)
        for k in range(TOP_K):
            w = ri_ref[ts, TOP_K + k:TOP_K + k + 1]
            hi = w.astype(BF16).astype(F32)
            ext = jnp.where(lane == 2 * k, hi, jnp.where(lane == 2 * k + 1, w - hi, ext))
            ext = jnp.where(lane == 2 * TOP_K + k, ri_ref[ts, k:k + 1], ext)
        exts.append(ext.astype(BF16))
    mains = [_mm(pm, jnp.concatenate([h2_ref[ts, :], ext], axis=1)) for pm, ts, ext in zip(pms, tiles, exts)]
    for u in range(MOE_TPS):
        xloc[slot, u] = mains[u].astype(BF16)

    for u in range(MOE_TPS):
        t = i * MOE_TPS + u
        _start_tile_copies(copies, t, lambda src, dst, n, u=u: pltpu.make_async_copy(
            xloc.at[slot].at[u].at[_chunk_rows(src, n)], xs_hbm.at[_chunk_rows(dst, n)], sem.at[slot]))

    @pl.when(i == nsteps - 1)
    def _():
        wait_step(i, slot)
        if nsteps >= 2:
            wait_step(i - 1, 1 - slot)
        lax.fori_loop(nbr_ref[0], nb, lambda j, z: (zero_block(j).wait(), z)[1], 0)


def _wait_chunks(n, copy_of_rows):
    bit = 1 << (MOE_TILE_CHUNKS.bit_length() - 1)
    while bit >= 1:
        @pl.when((n & bit) != 0)
        def _(bit=bit):
            copy_of_rows(pl.ds(0, bit * MOE_CHUNK)).wait()
        bit //= 2


def _dispatch(h2, ri, rit, tables, n_rows):
    t, d = h2.shape
    nt = t // MOE_TILE
    assert nt % MOE_TPS == 0 and MOE_SLAB >= TOP_K * MOE_TILE + N_EXPERTS * (MOE_CHUNK - 1)
    rows = MOE_TPS * MOE_TILE
    grid_spec = pltpu.PrefetchScalarGridSpec(
        num_scalar_prefetch=10,
        grid=(nt // MOE_TPS,),
        in_specs=[pl.BlockSpec((rows, d), lambda i, *_: (i, 0)),
                  pl.BlockSpec((rows, LANES), lambda i, *_: (i, 0)),
                  pl.BlockSpec((8, rows), lambda i, *_: (0, i))],
        out_specs=pl.BlockSpec(memory_space=pl.ANY),
        scratch_shapes=[pltpu.VMEM((2, MOE_TPS, MOE_SLAB, d + LANES), BF16), pltpu.VMEM((MOE_BLOCK, d + LANES), BF16),
                        pltpu.SemaphoreType.DMA((2,)), pltpu.SemaphoreType.DMA(()), pltpu.SemaphoreType.DMA(())])
    return pl.pallas_call(
        functools.partial(_dispatch_kernel, nsteps=nt // MOE_TPS, nb=n_rows // MOE_BLOCK),
        grid_spec=grid_spec,
        out_shape=jax.ShapeDtypeStruct((n_rows, d + LANES), BF16),
        compiler_params=_params("arbitrary"),
        name="moe_dispatch",
    )(*tables["copies"], tables["tail_start"], tables["tail_n"], tables["n_blocks"],
      h2, ri, rit)


def _expert_kernel(be_ref, nbr_ref, xs_ref, w1_ref, w3_ref, w2_ref, y_ref, w13b, w2b):
    i = pl.program_id(0)
    used = i < nbr_ref[0]

    @pl.when(jnp.logical_not(used))
    def _():
        y_ref[...] = jnp.zeros_like(y_ref)

    @pl.when(used & ((i == 0) | (be_ref[i] != be_ref[jnp.maximum(i - 1, 0)])))
    def _():
        w13b[:, 0:D_EXPERT] = w1_ref[0].astype(BF16)
        w13b[:, D_EXPERT:] = w3_ref[0].astype(BF16)
        w2b[...] = w2_ref[0].astype(BF16)

    @pl.when(used)
    def _():
        d = y_ref.shape[-1]
        subs = [slice(j * MOE_TILE, (j + 1) * MOE_TILE) for j in range(y_ref.shape[0] // MOE_TILE)]
        ab = [_mm(xs_ref[rs, 0:d], w13b[...]) for rs in subs]
        hms = [(_silu(v[:, 0:D_EXPERT]) * v[:, D_EXPERT:]).astype(BF16) for v in ab]
        expert = be_ref[i].astype(F32)
        for rs, hm in zip(subs, hms):
            gw = xs_ref[rs, d:d + LANES].astype(F32)
            gate = jnp.where(gw[:, 2 * TOP_K:2 * TOP_K + 1] == expert, gw[:, 0:1] + gw[:, 1:2], gw[:, 2:3] + gw[:, 3:4])
            y_ref[rs, :] = (_mm(hm, w2b[...]) * gate).astype(BF16)


def _experts(xs, tables, w1, w3, w2):
    n_rows, dw = xs.shape
    d = dw - LANES
    nb = n_rows // MOE_BLOCK
    blk = lambda i, be, nbr: jnp.minimum(i, nbr[0] - 1)
    grid_spec = pltpu.PrefetchScalarGridSpec(
        num_scalar_prefetch=2,
        grid=(nb,),
        in_specs=[pl.BlockSpec((MOE_BLOCK, dw), lambda i, be, nbr: (blk(i, be, nbr), 0)),
                  pl.BlockSpec((1, d, D_EXPERT), lambda i, be, nbr: (be[blk(i, be, nbr)], 0, 0)),
                  pl.BlockSpec((1, d, D_EXPERT), lambda i, be, nbr: (be[blk(i, be, nbr)], 0, 0)),
                  pl.BlockSpec((1, D_EXPERT, d), lambda i, be, nbr: (be[blk(i, be, nbr)], 0, 0))],
        out_specs=pl.BlockSpec((MOE_BLOCK, d), lambda i, be, nbr: (i, 0)),
        scratch_shapes=[pltpu.VMEM((d, 2 * D_EXPERT), BF16), pltpu.VMEM((D_EXPERT, d), BF16)])
    return pl.pallas_call(
        _expert_kernel,
        grid_spec=grid_spec,
        out_shape=jax.ShapeDtypeStruct((n_rows, d), BF16),
        compiler_params=_params("arbitrary"),
        name="moe_experts",
    )(tables["blk_expert"], tables["n_blocks"], xs, w1, w3, w2)


def _moe_rows(n_tiles):
    worst = n_tiles * (TOP_K * MOE_TILE + N_EXPERTS * (MOE_CHUNK - 1)) + N_EXPERTS * (MOE_BLOCK - MOE_CHUNK)
    return -(-worst // MOE_BLOCK) * MOE_BLOCK


def _moe_tables(cnt, n_rows):
    n_tiles = cnt.shape[0]
    per_blk = MOE_BLOCK // MOE_CHUNK

    def excl_cumsum(a, axis):
        n = a.shape[axis]
        lower = jnp.arange(n)[:, None] > jnp.arange(n)[None, :]
        if axis == 0:
            return jnp.sum(jnp.where(lower[:, :, None], a[None, :, :], 0), axis=1)
        return jnp.sum(jnp.where(lower[None, :, :], a[:, None, :], 0), axis=2)

    nch = (cnt + MOE_CHUNK - 1) // MOE_CHUNK
    loff = excl_cumsum(nch, 1)
    seg = jnp.sum(nch, axis=0)
    blocks = (seg + per_blk - 1) // per_blk
    bstart = excl_cumsum(blocks[None, :], 1)[0]
    bend = bstart + blocks
    estart = bstart * per_blk
    gbase = estart[None, :] + excl_cumsum(nch, 0)
    pos = jnp.arange(MOE_LIST, dtype=jnp.int32)
    experts = jnp.arange(N_EXPERTS, dtype=jnp.int32)

    def copy_list(n_te, first, stride):
        start = excl_cumsum(n_te, 1)
        owner = jnp.sum((pos[None, :, None] >= (start + n_te)[:, None, :]).astype(jnp.int32), axis=-1)
        mine = jnp.minimum(owner, N_EXPERTS - 1)[:, :, None] == experts[None, None, :]
        pick = lambda a: jnp.sum(jnp.where(mine, a[:, None, :], 0), axis=-1)
        off = pick(first) + stride * (pos[None, :] - pick(start))
        dst = jnp.clip(pick(gbase) + off, 0, n_rows // MOE_CHUNK - 2)
        return (pick(loff) + off).reshape(-1).astype(jnp.int32), dst.reshape(-1).astype(jnp.int32)

    pairs = nch // 2
    odd = nch - 2 * pairs
    psrc, pdst = copy_list(pairs, jnp.zeros_like(nch), 2)
    osrc, odst = copy_list(odd, 2 * pairs, 1)
    nb = n_rows // MOE_BLOCK
    blk_expert = jnp.sum((jnp.arange(nb, dtype=jnp.int32)[:, None] >= bend[None, :]).astype(jnp.int32), axis=1)
    blk_expert = jnp.minimum(blk_expert, N_EXPERTS - 1)
    return dict(copies=(psrc, pdst, osrc, odst, jnp.sum(pairs, axis=1).astype(jnp.int32),
                        jnp.sum(odd, axis=1).astype(jnp.int32), jnp.sum(nch, axis=1).astype(jnp.int32)),
                tail_start=(estart + seg).astype(jnp.int32), tail_n=(blocks * per_blk - seg).astype(jnp.int32),
                blk_expert=blk_expert.astype(jnp.int32), n_blocks=bend[-1:].astype(jnp.int32))


def _final_kernel(psrc_ref, pdst_ref, osrc_ref, odst_ref, npair_ref, nodd_ref, nchk_ref,
                  x1_ref, ri_ref, mod_ref, g_ref, y_hbm, o_ref, yloc, sem, *, nsteps):
    i = pl.program_id(0)
    slot = lax.rem(i, 2)

    copies = (psrc_ref, pdst_ref, osrc_ref, odst_ref, npair_ref, nodd_ref)

    def gather(step, sl):
        for u in range(MOE_TPS):
            _start_tile_copies(copies, step * MOE_TPS + u, lambda src, dst, n, u=u: pltpu.make_async_copy(
                y_hbm.at[_chunk_rows(dst, n)], yloc.at[sl].at[u].at[_chunk_rows(src, n)], sem.at[sl]))

    @pl.when(i == 0)
    def _():
        yloc[...] = jnp.zeros_like(yloc)
        gather(0, 0)

    @pl.when(i + 1 < nsteps)
    def _():
        gather(i + 1, 1 - slot)

    for u in range(MOE_TPS):
        _wait_chunks(nchk_ref[i * MOE_TPS + u],
                     lambda rows: pltpu.make_async_copy(y_hbm.at[rows], yloc.at[slot].at[0].at[rows], sem.at[slot]))

    l_io = lax.broadcasted_iota(jnp.int32, (MOE_TILE, MOE_SLAB), 1).astype(F32)
    tiles = [slice(u * MOE_TILE, (u + 1) * MOE_TILE) for u in range(MOE_TPS)]
    picks = [((l_io == ri_ref[ts, 4:5]) | (l_io == ri_ref[ts, 5:6])).astype(BF16) for ts in tiles]
    moes = [_mm(pick, yloc[slot, u]) for u, pick in enumerate(picks)]
    for ts, moe in zip(tiles, moes):
        x2 = x1_ref[ts, :] + mod_ref[0, 5:6, :] * moe
        o_ref[ts, :] = _rms(x2, g_ref[...])


def _final(x1, ri, y, mod3, final_g, tables, s):
    t, d = x1.shape
    rows = MOE_TPS * MOE_TILE
    nsteps = t // rows
    per_b = s // rows
    grid_spec = pltpu.PrefetchScalarGridSpec(
        num_scalar_prefetch=7,
        grid=(nsteps,),
        in_specs=[pl.BlockSpec((rows, d), lambda i, *_: (i, 0)),
                  pl.BlockSpec((rows, LANES), lambda i, *_: (i, 0)),
                  pl.BlockSpec((1, 6, d), lambda i, *_: (i // per_b, 0, 0)),
                  pl.BlockSpec((1, d), lambda i, *_: (0, 0)),
                  pl.BlockSpec(memory_space=pl.ANY)],
        out_specs=pl.BlockSpec((rows, d), lambda i, *_: (i, 0)),
        scratch_shapes=[pltpu.VMEM((2, MOE_TPS, MOE_SLAB, d), BF16), pltpu.SemaphoreType.DMA((2,))])
    return pl.pallas_call(
        functools.partial(_final_kernel, nsteps=nsteps),
        grid_spec=grid_spec,
        out_shape=jax.ShapeDtypeStruct((t, d), F32),
        compiler_params=_params("arbitrary"),
        name="moe_combine_final",
    )(*tables["copies"], x1, ri, mod3, final_g.reshape(1, d), y)


def kernel(x, c, positions, w_ada, b_ada, norm1_g, w_in, q_norm_g, w_uq, kv_norm_g, w_ukv, w_o, norm2_g,
           w_gr, b_gr, w_er, b_er, w1, w3, w2, final_g):
    bsz, s, d = x.shape
    assert w_ada.shape[0] == 1, "one layer"
    tm = min(1024, s)
    tq = min(256, s)
    mod3 = _adaln(c, w_ada[0], b_ada[0]).reshape(bsz, 6, d)
    pos3 = positions.astype(F32).reshape(bsz, s, 1)
    q, k, vt, rq, rk, rv, rg = _pre(x, mod3, pos3, norm1_g[0], w_in[0], q_norm_g[0], w_uq[0], kv_norm_g[0], w_ukv[0],
                                    tm, tq)
    o_mla = _attention(q, k, vt, tq, ATTN_HEADS_PER_STEP)
    x1, h2, ri, rit, cnt = _post(x, o_mla, rq, rk, rv, rg, mod3, w_o[0], norm2_g[0], w_gr[0], b_gr[0], w_er[0],
                                 b_er[0], tm)
    t = bsz * s
    n_rows = _moe_rows(t // MOE_TILE)
    counts = cnt[:, 0, N_GROUPS:N_GROUPS + N_EXPERTS].astype(jnp.int32)
    tables = _moe_tables(counts, n_rows)
    ri2 = ri.reshape(t, LANES)
    xs = _dispatch(h2.reshape(t, d), ri2, rit, tables, n_rows)
    y = _experts(xs, tables, w1[0], w3[0], w2[0])
    out = _final(x1.reshape(t, d), ri2, y, mod3, final_g, tables, s)
    return out.reshape(bsz, s, d)
```
